```python
import math
import jax
import jax.numpy as jnp
from jax import lax
import numpy as np

D_MODEL = 2048
BATCH = 2
SEQ = 8192
DEPTH = 2

GRID_W = 64
CTX_LEN = 256
N_EVEN = (DEPTH + 1) // 2
N_ODD = DEPTH // 2
MIX_WIDTH = D_MODEL
GROUP_WIDTH = MIX_WIDTH // 2
EPS = 1e-6
F32 = jnp.float32

GLA_HEADS = 4
GLA_DK = GROUP_WIDTH // (2 * GLA_HEADS)
GLA_DV = GROUP_WIDTH // GLA_HEADS
GLA_LOWRANK = 16
GLA_GATE_NORMALIZER = 16.0
GLA_CHUNK = 64

HY_CH = GROUP_WIDTH
HY_ORDER = 2
HY_BANDS = 16
HY_EMB = 1 + 2 * HY_BANDS
HY_FFN = 64
HY_SHIFT = 0.05
HY_MIN_DECAY = math.log(1e-2) / 1.5
HY_MAX_DECAY = math.log(1e-2) / 0.3

RET_HEADS = 4
RET_DK = GROUP_WIDTH // (2 * RET_HEADS)
RET_DV = GROUP_WIDTH // RET_HEADS
RET_CHUNK = 64

ATT_HD = 128
ATT_HEADS = GROUP_WIDTH // ATT_HD
ATT_KV_HEADS = 2
ATT_GROUP = ATT_HEADS // ATT_KV_HEADS
ATT_WINDOW = 128
ATT_BLOCK = 128
ROPE_BASE = 10000.0

N_EXPERTS = 16
N_GROUPS = 4
EXPERTS_PER_GROUP = N_EXPERTS // N_GROUPS
TOP_K = 2
D_FF_EXPERT = D_MODEL // 2

EVEN_SPLITS = (GLA_HEADS * GLA_DK, GLA_HEADS * GLA_DK, GLA_HEADS * GLA_DV, GLA_HEADS * GLA_DV,
               2 * GLA_LOWRANK, (HY_ORDER + 1) * HY_CH)
EVEN_COLS = sum(EVEN_SPLITS)
ODD_SPLITS = (RET_HEADS * RET_DK, RET_HEADS * RET_DK, RET_HEADS * RET_DV, RET_HEADS * RET_DV,
              ATT_HEADS * ATT_HD, ATT_KV_HEADS * ATT_HD, ATT_KV_HEADS * ATT_HD)
ODD_COLS = sum(ODD_SPLITS)

kernel_name = "hybrid_gla_hyena_retention_swa_moe_dit_block"


def rms_norm(x, gain):
    xf = x.astype(F32)
    y = xf * lax.rsqrt(jnp.mean(xf * xf, axis=-1, keepdims=True) + EPS)
    return (y * gain.astype(F32)).astype(x.dtype)


def split_cols(z, sizes):
    offs = np.cumsum([0] + list(sizes))
    return [z[..., int(offs[i]):int(offs[i + 1])] for i in range(len(sizes))]


def to_heads(t, n_heads):
    b, l, _ = t.shape
    return t.reshape(b, l, n_heads, -1).transpose(0, 2, 1, 3)


def from_heads(t):
    b, h, l, d = t.shape
    return t.transpose(0, 2, 1, 3).reshape(b, l, h * d)


def flip(t):
    return t[:, :, ::-1]


def rope(x, pos, inv_freq):
    half = x.shape[-1] // 2
    ang = pos.astype(F32)[:, None] * inv_freq[None, :]
    cos = jnp.cos(ang)[None, :, None, :]
    sin = jnp.sin(ang)[None, :, None, :]
    xf = x.astype(F32)
    x1, x2 = xf[..., :half], xf[..., half:]
    return jnp.concatenate([x1 * cos - x2 * sin, x2 * cos + x1 * sin], axis=-1).astype(x.dtype)


def axial_rope(x):
    L = x.shape[1]
    rows = L // GRID_W
    row = jnp.repeat(jnp.arange(rows), GRID_W)
    col = jnp.tile(jnp.arange(GRID_W), rows)
    h = x.shape[-1] // 2
    quarter = h // 2
    inv = ROPE_BASE ** (-jnp.arange(quarter, dtype=F32) / quarter)
    return jnp.concatenate([rope(x[..., :h], row, inv), rope(x[..., h:], col, inv)], axis=-1)


def decayed_state(k, v, log_a):
    cum = jnp.cumsum(log_a.astype(F32), axis=2)
    w = jnp.exp(cum[:, :, -1:] - cum)
    return jnp.einsum('bhld,bhle->bhde', k.astype(F32) * w, v.astype(F32))


def gla_chunked(q, k, v, log_a, s0):
    b, h, L, dk = q.shape
    dv = v.shape[-1]
    n = L // GLA_CHUNK

    def chunks(t):
        return jnp.moveaxis(t.astype(F32).reshape(b, h, n, GLA_CHUNK, t.shape[-1]), 2, 0)

    mask = jnp.tril(jnp.ones((GLA_CHUNK, GLA_CHUNK), dtype=bool))

    def step(S, inp):
        qc, kc, vc, ac = inp
        cum = jnp.cumsum(ac, axis=2)
        qe = qc * jnp.exp(cum)
        ke = kc * jnp.exp(-cum)
        att = jnp.where(mask, jnp.einsum('bhid,bhjd->bhij', qe, ke), 0.0)
        o = jnp.einsum('bhij,bhje->bhie', att, vc) + jnp.einsum('bhid,bhde->bhie', qe, S)
        last = cum[:, :, -1:, :]
        S = jnp.exp(last[:, :, 0, :])[..., None] * S + jnp.einsum('bhjd,bhje->bhde', kc * jnp.exp(last - cum), vc)
        return S, o

    S, o = lax.scan(step, s0, (chunks(q), chunks(k), chunks(v), chunks(log_a)))
    o = jnp.moveaxis(o, 0, 2).reshape(b, h, L, dv)
    return o.astype(v.dtype), S


def retention_chunked(q, k, v, log_g, s0):
    b, h, L, dk = q.shape
    dv = v.shape[-1]
    n = L // RET_CHUNK
    idx = jnp.arange(RET_CHUNK, dtype=F32)
    diff = idx[:, None] - idx[None, :]
    dmat = jnp.where(diff >= 0, jnp.exp(jnp.maximum(diff, 0.0)[None] * log_g[:, None, None]), 0.0)
    q_dec = jnp.exp((idx + 1.0)[None, :] * log_g[:, None])[..., None]
    k_dec = jnp.exp((RET_CHUNK - 1.0 - idx)[None, :] * log_g[:, None])[..., None]
    c_dec = jnp.exp(RET_CHUNK * log_g)[:, None, None]

    def chunks(t):
        return jnp.moveaxis(t.astype(F32).reshape(b, h, n, RET_CHUNK, t.shape[-1]), 2, 0)

    def step(S, inp):
        qc, kc, vc = inp
        att = jnp.einsum('bhid,bhjd->bhij', qc, kc) * dmat
        o = jnp.einsum('bhij,bhje->bhie', att, vc) + jnp.einsum('bhid,bhde->bhie', qc * q_dec, S)
        S = c_dec * S + jnp.einsum('bhjd,bhje->bhde', kc * k_dec, vc)
        return S, o

    S, o = lax.scan(step, s0, (chunks(q), chunks(k), chunks(v)))
    o = jnp.moveaxis(o, 0, 2).reshape(b, h, L, dv)
    return o.astype(v.dtype), S


def short_conv(u, w, bias):
    L = u.shape[1]
    up = jnp.pad(u, ((0, 0), (1, 1), (0, 0)))
    return up[:, :L] * w[0] + up[:, 1:L + 1] * w[1] + up[:, 2:] * w[2] + bias


def hyena_filters(L, w1, b1, w2, b2, w3):
    t = jnp.linspace(0.0, 1.0, L, dtype=F32)[:, None]
    w = (2.0 * math.pi / L) * jnp.arange(L, dtype=F32)[:, None]
    bands = jnp.linspace(1e-4, HY_BANDS - 1.0, HY_BANDS, dtype=F32)[None, :]
    feats = jnp.concatenate([t, jnp.cos(bands * w), -jnp.sin(bands * w)], axis=-1)
    hdn = jnp.sin(feats @ w1.astype(F32) + b1.astype(F32))
    hdn = jnp.sin(hdn @ w2.astype(F32) + b2.astype(F32))
    filt = (hdn @ w3.astype(F32)).reshape(L, HY_ORDER, 2, HY_CH)
    deltas = jnp.abs(jnp.linspace(HY_MIN_DECAY, HY_MAX_DECAY, HY_CH, dtype=F32))
    filt = filt * (jnp.exp(-t * deltas[None, :]) + HY_SHIFT)[:, None, None, :]
    kern = jnp.concatenate([filt[:, :, 0], jnp.zeros((1, HY_ORDER, HY_CH), F32), filt[:0:-1, :, 1]], axis=0)
    kern = kern / jnp.sum(jnp.abs(kern), axis=0, keepdims=True)
    return jnp.fft.rfft(kern, axis=0)


def fft_conv(u, kf, bias):
    L = u.shape[1]
    uf = jnp.fft.rfft(u.astype(F32), n=2 * L, axis=1)
    y = jnp.fft.irfft(uf * kf[None], n=2 * L, axis=1)[:, :L]
    return (y + u.astype(F32) * bias.astype(F32)).astype(u.dtype)


def context_attention(q, k, v, sink):
    b, lc = q.shape[:2]
    s = jnp.einsum('bqhgd,bkhd->bhgqk', q, k).astype(F32)
    sk = jnp.broadcast_to(sink.astype(F32)[None, :, :, None, None], s.shape[:-1] + (1,))
    p = jax.nn.softmax(jnp.concatenate([s, sk], axis=-1), axis=-1)[..., :-1]
    o = jnp.einsum('bhgqk,bkhd->bqhgd', p.astype(v.dtype), v)
    return o.reshape(b, lc, -1)


def window_attention(q, k, v, kc, vc, sink):
    b, L = q.shape[:2]
    W = ATT_BLOCK
    nb = L // W
    qb = q.reshape(b, nb, W, ATT_KV_HEADS, ATT_GROUP, ATT_HD)
    pad = ((0, 0), (W, W), (0, 0), (0, 0))
    kp = jnp.pad(k, pad).reshape(b, nb + 2, W, ATT_KV_HEADS, ATT_HD)
    vp = jnp.pad(v, pad).reshape(b, nb + 2, W, ATT_KV_HEADS, ATT_HD)
    kw = jnp.concatenate([kp[:, :-2], kp[:, 1:-1], kp[:, 2:]], axis=2)
    vw = jnp.concatenate([vp[:, :-2], vp[:, 1:-1], vp[:, 2:]], axis=2)
    s_win = jnp.einsum('bnqhgd,bnkhd->bhgnqk', qb, kw).astype(F32)
    s_ctx = jnp.einsum('bnqhgd,bkhd->bhgnqk', qb, kc).astype(F32)
    blk = jnp.arange(nb)[:, None, None] * W
    qpos = blk + jnp.arange(W)[None, :, None]
    kpos = blk - W + jnp.arange(3 * W)[None, None, :]
    valid = (jnp.abs(qpos - kpos) <= ATT_WINDOW) & (kpos >= 0) & (kpos < L)
    s_win = jnp.where(valid, s_win, -jnp.inf)
    sk = jnp.broadcast_to(sink.astype(F32)[None, :, :, None, None, None], s_win.shape[:-1] + (1,))
    p = jax.nn.softmax(jnp.concatenate([s_win, s_ctx, sk], axis=-1), axis=-1)
    n_ctx = kc.shape[1]
    p_win = p[..., :3 * W].astype(v.dtype)
    p_ctx = p[..., 3 * W:3 * W + n_ctx].astype(v.dtype)
    o = jnp.einsum('bhgnqk,bnkhd->bnqhgd', p_win, vw) + jnp.einsum('bhgnqk,bkhd->bnqhgd', p_ctx, vc)
    return o.reshape(b, L, -1)


def even_mixer(hc, hl, w_in, w_out, gla_w_gate, gla_b_gate, gla_norm, hy_conv_w, hy_conv_b,
               hy_w1, hy_b1, hy_w2, hy_b2, hy_w3, hy_bias, need_ctx):
    b = hl.shape[0]

    def prep(h):
        q, k, v, g, lr, u = split_cols(h @ w_in, EVEN_SPLITS)
        q = to_heads(q, GLA_HEADS) * (GLA_DK ** -0.5)
        k = to_heads(k, GLA_HEADS)
        v = to_heads(v, GLA_HEADS)
        la = [to_heads(jax.nn.log_sigmoid((lr[..., d * GLA_LOWRANK:(d + 1) * GLA_LOWRANK] @ gla_w_gate[d]
                                          + gla_b_gate[d]).astype(F32)) / GLA_GATE_NORMALIZER, GLA_HEADS)
              for d in range(2)]
        return q, k, v, g, la[0], la[1], u

    def gla_merge(o, g):
        return from_heads(rms_norm(o, gla_norm)) * jax.nn.silu(g)

    def hyena(u):
        L = u.shape[1]
        u = short_conv(u, hy_conv_w, hy_conv_b)
        z, *gates = split_cols(u, (HY_CH,) * (HY_ORDER + 1))
        kf = hyena_filters(L, hy_w1, hy_b1, hy_w2, hy_b2, hy_w3)
        for n, gate in enumerate(gates):
            z = gate * fft_conv(z, kf[:, n], hy_bias[n])
        return z

    cq, ck, cv, cg, ca_f, ca_b, cu = prep(hc)
    lq, lk, lv, lg, la_f, la_b, lu = prep(hl)
    s_zero = jnp.zeros((b, GLA_HEADS, GLA_DK, GLA_DV), F32)
    yc = None
    if need_ctx:
        oc_f, sc_f = gla_chunked(cq, ck, cv, ca_f, s_zero)
        oc_b, sc_b = gla_chunked(flip(cq), flip(ck), flip(cv), flip(ca_b), s_zero)
        yc = jnp.concatenate([gla_merge(oc_f + flip(oc_b), cg), hyena(cu)], axis=-1) @ w_out
    else:
        sc_f = decayed_state(ck, cv, ca_f)
        sc_b = decayed_state(flip(ck), flip(cv), flip(ca_b))
    ol_f, _ = gla_chunked(lq, lk, lv, la_f, sc_f)
    ol_b, _ = gla_chunked(flip(lq), flip(lk), flip(lv), flip(la_b), sc_b)
    yl = jnp.concatenate([gla_merge(ol_f + flip(ol_b), lg), hyena(lu)], axis=-1) @ w_out
    return yc, yl


def odd_mixer(hc, hl, w_in, w_out, ret_norm, att_q_norm, att_k_norm, att_sink, need_ctx):
    b = hl.shape[0]
    ret_inv = ROPE_BASE ** (-jnp.linspace(0.0, 1.0, RET_DK // 2, dtype=F32))
    log_g_f = jnp.log(1.0 - 2.0 ** (-5.0 - jnp.arange(RET_HEADS, dtype=F32)))
    log_g_b = log_g_f[::-1]
    sink = att_sink.reshape(ATT_KV_HEADS, ATT_GROUP)

    def prep(h, latent):
        z = h @ w_in
        L = z.shape[1]
        rq, rk, rv, rg, aq, ak, av = split_cols(z, ODD_SPLITS)
        rq = rq.reshape(b, L, RET_HEADS, RET_DK)
        rk = rk.reshape(b, L, RET_HEADS, RET_DK)
        aq = rms_norm(aq.reshape(b, L, ATT_HEADS, ATT_HD), att_q_norm)
        ak = rms_norm(ak.reshape(b, L, ATT_KV_HEADS, ATT_HD), att_k_norm)
        if latent:
            pos = jnp.arange(L)
            rq, rk = rope(rq, pos, ret_inv), rope(rk, pos, ret_inv)
            aq, ak = axial_rope(aq), axial_rope(ak)
        rq = rq.transpose(0, 2, 1, 3)
        rk = (rk * (RET_DK ** -0.5)).transpose(0, 2, 1, 3)
        rv = to_heads(rv, RET_HEADS)
        aq = (aq * (ATT_HD ** -0.5)).reshape(b, L, ATT_KV_HEADS, ATT_GROUP, ATT_HD)
        av = av.reshape(b, L, ATT_KV_HEADS, ATT_HD)
        return rq, rk, rv, rg, aq, ak, av

    def ret_merge(o, g):
        return from_heads(rms_norm(o, ret_norm)) * jax.nn.silu(g)

    cq, ck, cv, cg, caq, cak, cav = prep(hc, False)
    lq, lk, lv, lg, laq, lak, lav = prep(hl, True)
    s_zero = jnp.zeros((b, RET_HEADS, RET_DK, RET_DV), F32)
    yc = None
    if need_ctx:
        oc_f, sc_f = retention_chunked(cq, ck, cv, log_g_f, s_zero)
        oc_b, sc_b = retention_chunked(flip(cq), flip(ck), flip(cv), log_g_b, s_zero)
        yc = jnp.concatenate([ret_merge(oc_f + flip(oc_b), cg),
                              context_attention(caq, cak, cav, sink)], axis=-1) @ w_out
    else:
        lc = ck.shape[2]
        sc_f = decayed_state(ck, cv, jnp.broadcast_to(log_g_f[None, :, None, None], (1, RET_HEADS, lc, 1)))
        sc_b = decayed_state(flip(ck), flip(cv), jnp.broadcast_to(log_g_b[None, :, None, None], (1, RET_HEADS, lc, 1)))
    ol_f, _ = retention_chunked(lq, lk, lv, log_g_f, sc_f)
    ol_b, _ = retention_chunked(flip(lq), flip(lk), flip(lv), log_g_b, sc_b)
    yl = jnp.concatenate([ret_merge(ol_f + flip(ol_b), lg),
                          window_attention(laq, lak, lav, cak, cav, sink)], axis=-1) @ w_out
    return yc, yl


def moe(h, router_w, router_b, w1, w3, w2):
    n = h.shape[0]
    scores = jax.nn.softmax((h @ router_w).astype(F32), axis=-1)
    sel = scores + router_b.astype(F32)
    gscore = lax.top_k(sel.reshape(n, N_GROUPS, EXPERTS_PER_GROUP), TOP_K)[0].sum(-1)
    g_idx = jnp.argmax(gscore, axis=-1)
    in_group = (jnp.arange(N_EXPERTS) // EXPERTS_PER_GROUP)[None, :] == g_idx[:, None]
    _, e_idx = lax.top_k(jnp.where(in_group, sel, -jnp.inf), TOP_K)
    wts = jnp.take_along_axis(scores, e_idx, axis=-1)
    wts = wts / jnp.sum(wts, axis=-1, keepdims=True)
    comb = jnp.sum(jax.nn.one_hot(e_idx, N_EXPERTS, dtype=F32) * wts[..., None], axis=1).astype(h.dtype)
    out = jnp.zeros_like(h)
    for e in range(N_EXPERTS):
        y = (jax.nn.silu(h @ w1[e]) * (h @ w3[e])) @ w2[e]
        out = out + comb[:, e:e + 1] * y
    return out


def setup_inputs(seed: int = 0) -> dict:
    key = jax.random.key(seed)
    ks = iter(jax.random.split(key, 40))
    D = D_MODEL

    def nrm(shape, scale):
        return scale * jax.random.normal(next(ks), shape, F32)

    def gain(shape):
        return 1.0 + 0.02 * jax.random.normal(next(ks), shape, F32)

    return {
        "x": nrm((BATCH, SEQ, D), 1.0),
        "c": nrm((BATCH, D), 1.0),
        "ctx": nrm((BATCH, CTX_LEN, D), 1.0),
        "c_ctx": nrm((D,), 1.0),
        "ada_w": nrm((DEPTH, D, 6 * D), 0.5 * D ** -0.5),
        "ada_b": nrm((DEPTH, 6 * D), 0.01),
        "norm_mix": gain((DEPTH, D)),
        "norm_ffn": gain((DEPTH, D)),
        "ev_w_in": nrm((N_EVEN, D, EVEN_COLS), D ** -0.5),
        "ev_w_out": nrm((N_EVEN, MIX_WIDTH, D), MIX_WIDTH ** -0.5),
        "gla_w_gate": nrm((N_EVEN, 2, GLA_LOWRANK, GLA_HEADS * GLA_DK), GLA_LOWRANK ** -0.5),
        "gla_b_gate": nrm((N_EVEN, 2, GLA_HEADS * GLA_DK), 0.1),
        "gla_norm": gain((N_EVEN, GLA_DV)),
        "hy_conv_w": nrm((N_EVEN, 3, (HY_ORDER + 1) * HY_CH), 3 ** -0.5),
        "hy_conv_b": nrm((N_EVEN, (HY_ORDER + 1) * HY_CH), 0.01),
        "hy_w1": nrm((N_EVEN, HY_EMB, HY_FFN), 1.0),
        "hy_b1": nrm((N_EVEN, HY_FFN), 0.1),
        "hy_w2": nrm((N_EVEN, HY_FFN, HY_FFN), HY_FFN ** -0.5),
        "hy_b2": nrm((N_EVEN, HY_FFN), 0.1),
        "hy_w3": nrm((N_EVEN, HY_FFN, HY_ORDER * 2 * HY_CH), HY_FFN ** -0.5),
        "hy_bias": nrm((N_EVEN, HY_ORDER, HY_CH), 1.0),
        "od_w_in": nrm((N_ODD, D, ODD_COLS), D ** -0.5),
        "od_w_out": nrm((N_ODD, MIX_WIDTH, D), MIX_WIDTH ** -0.5),
        "ret_norm": gain((N_ODD, RET_DV)),
        "att_q_norm": gain((N_ODD, ATT_HD)),
        "att_k_norm": gain((N_ODD, ATT_HD)),
        "att_sink": nrm((N_ODD, ATT_HEADS), 0.5),
        "router_w": nrm((D, N_EXPERTS), D ** -0.5),
        "router_b": nrm((N_EXPERTS,), 0.01),
        "moe_w1": nrm((DEPTH, N_EXPERTS, D, D_FF_EXPERT), D ** -0.5),
        "moe_w3": nrm((DEPTH, N_EXPERTS, D, D_FF_EXPERT), D ** -0.5),
        "moe_w2": nrm((DEPTH, N_EXPERTS, D_FF_EXPERT, D), D_FF_EXPERT ** -0.5),
    }


def reference(x, c, ctx, c_ctx, ada_w, ada_b, norm_mix, norm_ffn, ev_w_in, ev_w_out, gla_w_gate,
              gla_b_gate, gla_norm, hy_conv_w, hy_conv_b, hy_w1, hy_b1, hy_w2, hy_b2, hy_w3, hy_bias,
              od_w_in, od_w_out, ret_norm, att_q_norm, att_k_norm, att_sink, router_w, router_b,
              moe_w1, moe_w3, moe_w2):
    xl, xc = x, ctx
    n_ctx = ctx.shape[1]
    for layer in range(DEPTH):
        last = layer == DEPTH - 1
        ml = (jax.nn.silu(c) @ ada_w[layer] + ada_b[layer])[:, None, :]
        mc = (jax.nn.silu(c_ctx) @ ada_w[layer] + ada_b[layer])[None, None, :]
        sh1, sc1, g1, sh2, sc2, g2 = jnp.split(ml, 6, axis=-1)
        csh1, csc1, cg1, csh2, csc2, cg2 = jnp.split(mc, 6, axis=-1)
        hl = rms_norm(xl, norm_mix[layer]) * (1.0 + sc1) + sh1
        hc = rms_norm(xc, norm_mix[layer]) * (1.0 + csc1) + csh1
        if layer % 2 == 0:
            i = layer // 2
            oc, ol = even_mixer(hc, hl, ev_w_in[i], ev_w_out[i], gla_w_gate[i], gla_b_gate[i], gla_norm[i],
                                hy_conv_w[i], hy_conv_b[i], hy_w1[i], hy_b1[i], hy_w2[i], hy_b2[i], hy_w3[i],
                                hy_bias[i], not last)
        else:
            i = layer // 2
            oc, ol = odd_mixer(hc, hl, od_w_in[i], od_w_out[i], ret_norm[i], att_q_norm[i], att_k_norm[i],
                               att_sink[i], not last)
        xl = xl + g1 * ol
        hl = rms_norm(xl, norm_ffn[layer]) * (1.0 + sc2) + sh2
        if not last:
            xc = xc + cg1 * oc
            hc = rms_norm(xc, norm_ffn[layer]) * (1.0 + csc2) + csh2
            h = jnp.concatenate([hc, hl], axis=1)
            y = moe(h.reshape(-1, D_MODEL), router_w, router_b, moe_w1[layer], moe_w3[layer],
                    moe_w2[layer]).reshape(h.shape)
            xc = xc + cg2 * y[:, :n_ctx]
            xl = xl + g2 * y[:, n_ctx:]
        else:
            y = moe(hl.reshape(-1, D_MODEL), router_w, router_b, moe_w1[layer], moe_w3[layer],
                    moe_w2[layer]).reshape(hl.shape)
            xl = xl + g2 * y
    return xl
```

```python
import functools
import math

import jax
import jax.numpy as jnp
import numpy as np
from jax import lax
from jax.experimental import pallas as pl
from jax.experimental.pallas import tpu as pltpu

D_MODEL = 2048
BATCH = 2
SEQ = 8192
DEPTH = 2
GRID_W = 64
CTX_LEN = 256
MIX_WIDTH = D_MODEL
GROUP_WIDTH = MIX_WIDTH // 2
EPS = 1e-6
F32 = jnp.float32
BF16 = jnp.bfloat16

GLA_HEADS = 4
GLA_DK = GROUP_WIDTH // (2 * GLA_HEADS)
GLA_DV = GROUP_WIDTH // GLA_HEADS
GLA_LOWRANK = 16
GLA_GATE_NORMALIZER = 16.0
GLA_CHUNK = 64

HY_CH = GROUP_WIDTH
HY_ORDER = 2
HY_BANDS = 16
HY_EMB = 1 + 2 * HY_BANDS
HY_FFN = 64
HY_SHIFT = 0.05
HY_MIN_DECAY = math.log(1e-2) / 1.5
HY_MAX_DECAY = math.log(1e-2) / 0.3

RET_HEADS = 4
RET_DK = GROUP_WIDTH // (2 * RET_HEADS)
RET_DV = GROUP_WIDTH // RET_HEADS
RET_CHUNK = 64

ATT_HD = 128
ATT_HEADS = GROUP_WIDTH // ATT_HD
ATT_KV_HEADS = 2
ATT_GROUP = ATT_HEADS // ATT_KV_HEADS
ATT_WINDOW = 128
ATT_BLOCK = 128
ROPE_BASE = 10000.0

N_EXPERTS = 16
N_GROUPS = 4
EXPERTS_PER_GROUP = N_EXPERTS // N_GROUPS
TOP_K = 2
D_FF_EXPERT = D_MODEL // 2

EVEN_SPLITS = (GLA_HEADS * GLA_DK, GLA_HEADS * GLA_DK, GLA_HEADS * GLA_DV, GLA_HEADS * GLA_DV,
               2 * GLA_LOWRANK, (HY_ORDER + 1) * HY_CH)
ODD_SPLITS = (RET_HEADS * RET_DK, RET_HEADS * RET_DK, RET_HEADS * RET_DV, RET_HEADS * RET_DV,
              ATT_HEADS * ATT_HD, ATT_KV_HEADS * ATT_HD, ATT_KV_HEADS * ATT_HD)

LANE = 128
EVEN_COLS_PAD = -(-sum(EVEN_SPLITS) // (7 * LANE)) * (7 * LANE)
VMEM_LIMIT = 56 * 1024 * 1024
MOE_TM = 512
LOGIT_PAD = LANE


def _cparams(sem):
    return pltpu.CompilerParams(dimension_semantics=sem, vmem_limit_bytes=VMEM_LIMIT)


def _ada_kernel(c_ref, w_ref, b_ref, o_ref):
    c = c_ref[...]
    s = (c * jax.nn.sigmoid(c)).astype(BF16)
    o_ref[0] = jnp.dot(s, w_ref[0].astype(BF16), preferred_element_type=F32) + b_ref[0]


def ada_modulation(c_pad, ada_w, ada_b):
    depth, d, n = ada_w.shape
    tn = 1024
    return pl.pallas_call(
        _ada_kernel,
        grid=(depth, n // tn),
        in_specs=[
            pl.BlockSpec((8, d), lambda l, j: (0, 0)),
            pl.BlockSpec((1, d, tn), lambda l, j: (l, 0, j)),
            pl.BlockSpec((1, 1, tn), lambda l, j: (l, 0, j)),
        ],
        out_specs=pl.BlockSpec((1, 8, tn), lambda l, j: (l, 0, j)),
        out_shape=jax.ShapeDtypeStruct((depth, 8, n), F32),
        compiler_params=_cparams(("parallel", "arbitrary")),
        name="ada_modulation",
    )(c_pad, ada_w, ada_b.reshape(depth, 1, n))


def _norm_mod(x, gain, sc, sh):
    ms = jnp.mean(x * x, axis=-1, keepdims=True)
    return (x * lax.rsqrt(ms + EPS) * gain) * (1.0 + sc) + sh


def _inproj_kernel(x_ref, gain_ref, sc_ref, sh_ref, w_ref, o_ref, h_ref):
    @pl.when(pl.program_id(2) == 0)
    def _():
        h_ref[...] = _norm_mod(x_ref[0], gain_ref[...], sc_ref[0], sh_ref[0]).astype(BF16)

    o_ref[0] = jnp.dot(h_ref[...], w_ref[...], preferred_element_type=F32).astype(o_ref.dtype)


def norm_mod_matmul(x, gain, sc, sh, w, *, tm, tn, out_dtype):
    b, l, d = x.shape
    n = w.shape[1]
    return pl.pallas_call(
        _inproj_kernel,
        grid=(b, l // tm, n // tn),
        in_specs=[
            pl.BlockSpec((1, tm, d), lambda bi, i, j: (bi, i, 0)),
            pl.BlockSpec((1, d), lambda bi, i, j: (0, 0)),
            pl.BlockSpec((1, 1, d), lambda bi, i, j: (bi, 0, 0)),
            pl.BlockSpec((1, 1, d), lambda bi, i, j: (bi, 0, 0)),
            pl.BlockSpec((d, tn), lambda bi, i, j: (0, j)),
        ],
        out_specs=pl.BlockSpec((1, tm, tn), lambda bi, i, j: (bi, i, j)),
        out_shape=jax.ShapeDtypeStruct((b, l, n), out_dtype),
        scratch_shapes=[pltpu.VMEM((tm, d), BF16)],
        compiler_params=_cparams(("parallel", "parallel", "arbitrary")),
        name="norm_mod_matmul",
    )(x, gain.reshape(1, d), sc, sh, w)


def _outproj_kernel(m_ref, w_ref, x_ref, g_ref, gain_ref, sc_ref, sh_ref, rw_ref,
                    xo_ref, h_ref, lg_ref):
    o = jnp.dot(m_ref[0].astype(BF16), w_ref[...], preferred_element_type=F32)
    xn = x_ref[0] + g_ref[0] * o
    xo_ref[0] = xn
    h = _norm_mod(xn, gain_ref[...], sc_ref[0], sh_ref[0])
    h_ref[0] = h.astype(BF16)
    lg_ref[0] = jnp.dot(h, rw_ref[...], preferred_element_type=F32,
                        precision=lax.Precision.HIGHEST)


def outproj_residual_norm(m, w, x, g, gain, sc, sh, rw_pad, *, tm):
    b, l, d = x.shape
    k = m.shape[-1]
    row = lambda bi, i: (bi, i, 0)
    mod = lambda bi, i: (bi, 0, 0)
    const = lambda bi, i: (0, 0)
    return pl.pallas_call(
        _outproj_kernel,
        grid=(b, l // tm),
        in_specs=[
            pl.BlockSpec((1, tm, k), row),
            pl.BlockSpec((k, d), const, pipeline_mode=pl.Buffered(1)),
            pl.BlockSpec((1, tm, d), row),
            pl.BlockSpec((1, 1, d), mod),
            pl.BlockSpec((1, d), const),
            pl.BlockSpec((1, 1, d), mod),
            pl.BlockSpec((1, 1, d), mod),
            pl.BlockSpec((d, LOGIT_PAD), const),
        ],
        out_specs=[
            pl.BlockSpec((1, tm, d), row),
            pl.BlockSpec((1, tm, d), row),
            pl.BlockSpec((1, tm, LOGIT_PAD), row),
        ],
        out_shape=[
            jax.ShapeDtypeStruct((b, l, d), F32),
            jax.ShapeDtypeStruct((b, l, d), BF16),
            jax.ShapeDtypeStruct((b, l, LOGIT_PAD), F32),
        ],
        compiler_params=_cparams(("parallel", "parallel")),
        name="outproj_residual_norm",
    )(m, w, x, g, gain.reshape(1, d), sc, sh, rw_pad)


def _moe_kernel(te_ref, tsrc_ref, tvalid_ref, x_ref, w1_ref, w3_ref, w2_ref, rw_ref, o_ref):
    i = pl.program_id(0)

    @pl.when(tvalid_ref[i] > 0)
    def _():
        x = x_ref[...]
        a = jnp.dot(x, w1_ref[0], preferred_element_type=F32)
        b = jnp.dot(x, w3_ref[0], preferred_element_type=F32)
        mid = (a * jax.nn.sigmoid(a) * b).astype(BF16)
        y = jnp.dot(mid, w2_ref[0], preferred_element_type=F32)
        o_ref[...] = (y * rw_ref[...]).astype(o_ref.dtype)

    @pl.when(tvalid_ref[i] == 0)
    def _():
        o_ref[...] = jnp.zeros_like(o_ref)


def moe_grouped(x_sorted, w1, w3, w2, row_w, tile_expert, tile_src, tile_valid):
    p, d = x_sorted.shape
    ff = w1.shape[-1]
    tm = MOE_TM
    grid_spec = pltpu.PrefetchScalarGridSpec(
        num_scalar_prefetch=3,
        grid=(p // tm,),
        in_specs=[
            pl.BlockSpec((tm, d), lambda i, te, ts, tv: (ts[i], 0)),
            pl.BlockSpec((1, d, ff), lambda i, te, ts, tv: (te[i], 0, 0)),
            pl.BlockSpec((1, d, ff), lambda i, te, ts, tv: (te[i], 0, 0)),
            pl.BlockSpec((1, ff, d), lambda i, te, ts, tv: (te[i], 0, 0)),
            pl.BlockSpec((tm, 1), lambda i, te, ts, tv: (ts[i], 0)),
        ],
        out_specs=pl.BlockSpec((tm, d), lambda i, te, ts, tv: (i, 0)),
    )
    return pl.pallas_call(
        _moe_kernel,
        grid_spec=grid_spec,
        out_shape=jax.ShapeDtypeStruct((p, d), BF16),
        compiler_params=_cparams(("arbitrary",)),
        name="moe_grouped",
    )(tile_expert, tile_src, tile_valid, x_sorted, w1, w3, w2, row_w)


def _route(logits, router_b):
    n = logits.shape[0]
    scores = jax.nn.softmax(logits, axis=-1)
    sel = scores + router_b.astype(F32)
    gscore = lax.top_k(sel.reshape(n, N_GROUPS, EXPERTS_PER_GROUP), TOP_K)[0].sum(-1)
    g_idx = jnp.argmax(gscore, axis=-1)
    in_group = (jnp.arange(N_EXPERTS) // EXPERTS_PER_GROUP)[None, :] == g_idx[:, None]
    _, e_idx = lax.top_k(jnp.where(in_group, sel, -jnp.inf), TOP_K)
    wts = jnp.take_along_axis(scores, e_idx, axis=-1)
    wts = wts / jnp.sum(wts, axis=-1, keepdims=True)
    return e_idx, wts


def moe_apply(h_bf16, logits, router_b, w1, w3, w2):
    n, d = h_bf16.shape
    tm = MOE_TM
    e_idx, wts = _route(logits[:, :N_EXPERTS], router_b)
    flat_e = e_idx.reshape(-1)
    flat_w = wts.reshape(-1)
    onehot = (flat_e[:, None] == jnp.arange(N_EXPERTS)[None, :]).astype(jnp.int32)
    rank = jnp.sum((jnp.cumsum(onehot, axis=0) - onehot) * onehot, axis=-1)
    counts = jnp.sum(onehot, axis=0)
    padded = ((counts + tm - 1) // tm) * tm
    ends = jnp.cumsum(padded)
    offs = ends - padded
    pos = offs[flat_e] + rank
    n_tiles = (2 * n + tm - 1) // tm + N_EXPERTS
    p = n_tiles * tm
    used = ends[-1] // tm
    tile_ids = jnp.arange(n_tiles, dtype=jnp.int32)
    tile_valid = (tile_ids < used).astype(jnp.int32)
    tile_src = jnp.minimum(tile_ids, used - 1).astype(jnp.int32)
    tile_expert = jnp.minimum(
        jnp.searchsorted(ends, tile_src * tm, side="right"), N_EXPERTS - 1).astype(jnp.int32)
    src = jnp.zeros((p,), jnp.int32).at[pos].set(jnp.arange(2 * n, dtype=jnp.int32) // 2)
    row_w = jnp.zeros((p,), F32).at[pos].set(flat_w)
    x_sorted = jnp.take(h_bf16, src, axis=0)
    y_sorted = moe_grouped(x_sorted, w1, w3, w2, row_w.reshape(p, 1), tile_expert, tile_src,
                           tile_valid)
    pos2 = pos.reshape(n, 2)
    return (jnp.take(y_sorted, pos2[:, 0], axis=0).astype(F32)
            + jnp.take(y_sorted, pos2[:, 1], axis=0).astype(F32))


def rms_norm(x, gain):
    xf = x.astype(F32)
    y = xf * lax.rsqrt(jnp.mean(xf * xf, axis=-1, keepdims=True) + EPS)
    return (y * gain.astype(F32)).astype(x.dtype)


def split_cols(z, sizes):
    offs = np.cumsum([0] + list(sizes))
    return [z[..., int(offs[i]):int(offs[i + 1])] for i in range(len(sizes))]


def to_heads(t, n_heads):
    b, l, _ = t.shape
    return t.reshape(b, l, n_heads, -1).transpose(0, 2, 1, 3)


def from_heads(t):
    b, h, l, d = t.shape
    return t.transpose(0, 2, 1, 3).reshape(b, l, h * d)


def flip(t):
    return t[:, :, ::-1]


def rope(x, pos, inv_freq):
    half = x.shape[-1] // 2
    ang = pos.astype(F32)[:, None] * inv_freq[None, :]
    cos = jnp.cos(ang)[None, :, None, :]
    sin = jnp.sin(ang)[None, :, None, :]
    xf = x.astype(F32)
    x1, x2 = xf[..., :half], xf[..., half:]
    return jnp.concatenate([x1 * cos - x2 * sin, x2 * cos + x1 * sin], axis=-1).astype(x.dtype)


def axial_rope(x):
    L = x.shape[1]
    rows = L // GRID_W
    row = jnp.repeat(jnp.arange(rows), GRID_W)
    col = jnp.tile(jnp.arange(GRID_W), rows)
    h = x.shape[-1] // 2
    quarter = h // 2
    inv = ROPE_BASE ** (-jnp.arange(quarter, dtype=F32) / quarter)
    return jnp.concatenate([rope(x[..., :h], row, inv), rope(x[..., h:], col, inv)], axis=-1)


def decayed_state(k, v, log_a):
    cum = jnp.cumsum(log_a.astype(F32), axis=2)
    w = jnp.exp(cum[:, :, -1:] - cum)
    return jnp.einsum('bhld,bhle->bhde', k.astype(F32) * w, v.astype(F32))


def gla_chunked(q, k, v, log_a, s0):
    b, h, L, dk = q.shape
    dv = v.shape[-1]
    n = L // GLA_CHUNK

    def chunks(t):
        return jnp.moveaxis(t.astype(F32).reshape(b, h, n, GLA_CHUNK, t.shape[-1]), 2, 0)

    mask = jnp.tril(jnp.ones((GLA_CHUNK, GLA_CHUNK), dtype=bool))

    def step(S, inp):
        qc, kc, vc, ac = inp
        cum = jnp.cumsum(ac, axis=2)
        qe = qc * jnp.exp(cum)
        ke = kc * jnp.exp(-cum)
        att = jnp.where(mask, jnp.einsum('bhid,bhjd->bhij', qe, ke), 0.0)
        o = jnp.einsum('bhij,bhje->bhie', att, vc) + jnp.einsum('bhid,bhde->bhie', qe, S)
        last = cum[:, :, -1:, :]
        S = jnp.exp(last[:, :, 0, :])[..., None] * S + jnp.einsum('bhjd,bhje->bhde', kc * jnp.exp(last - cum), vc)
        return S, o

    S, o = lax.scan(step, s0, (chunks(q), chunks(k), chunks(v), chunks(log_a)))
    o = jnp.moveaxis(o, 0, 2).reshape(b, h, L, dv)
    return o.astype(v.dtype), S


def retention_chunked(q, k, v, log_g, s0):
    b, h, L, dk = q.shape
    dv = v.shape[-1]
    n = L // RET_CHUNK
    idx = jnp.arange(RET_CHUNK, dtype=F32)
    diff = idx[:, None] - idx[None, :]
    dmat = jnp.where(diff >= 0, jnp.exp(jnp.maximum(diff, 0.0)[None] * log_g[:, None, None]), 0.0)
    q_dec = jnp.exp((idx + 1.0)[None, :] * log_g[:, None])[..., None]
    k_dec = jnp.exp((RET_CHUNK - 1.0 - idx)[None, :] * log_g[:, None])[..., None]
    c_dec = jnp.exp(RET_CHUNK * log_g)[:, None, None]

    def chunks(t):
        return jnp.moveaxis(t.astype(F32).reshape(b, h, n, RET_CHUNK, t.shape[-1]), 2, 0)

    def step(S, inp):
        qc, kc, vc = inp
        att = jnp.einsum('bhid,bhjd->bhij', qc, kc) * dmat
        o = jnp.einsum('bhij,bhje->bhie', att, vc) + jnp.einsum('bhid,bhde->bhie', qc * q_dec, S)
        S = c_dec * S + jnp.einsum('bhjd,bhje->bhde', kc * k_dec, vc)
        return S, o

    S, o = lax.scan(step, s0, (chunks(q), chunks(k), chunks(v)))
    o = jnp.moveaxis(o, 0, 2).reshape(b, h, L, dv)
    return o.astype(v.dtype), S


def short_conv(u, w, bias):
    L = u.shape[1]
    up = jnp.pad(u, ((0, 0), (1, 1), (0, 0)))
    return up[:, :L] * w[0] + up[:, 1:L + 1] * w[1] + up[:, 2:] * w[2] + bias


def hyena_filters(L, w1, b1, w2, b2, w3):
    t = jnp.linspace(0.0, 1.0, L, dtype=F32)[:, None]
    w = (2.0 * math.pi / L) * jnp.arange(L, dtype=F32)[:, None]
    bands = jnp.linspace(1e-4, HY_BANDS - 1.0, HY_BANDS, dtype=F32)[None, :]
    feats = jnp.concatenate([t, jnp.cos(bands * w), -jnp.sin(bands * w)], axis=-1)
    hdn = jnp.sin(feats @ w1.astype(F32) + b1.astype(F32))
    hdn = jnp.sin(hdn @ w2.astype(F32) + b2.astype(F32))
    filt = (hdn @ w3.astype(F32)).reshape(L, HY_ORDER, 2, HY_CH)
    deltas = jnp.abs(jnp.linspace(HY_MIN_DECAY, HY_MAX_DECAY, HY_CH, dtype=F32))
    filt = filt * (jnp.exp(-t * deltas[None, :]) + HY_SHIFT)[:, None, None, :]
    kern = jnp.concatenate([filt[:, :, 0], jnp.zeros((1, HY_ORDER, HY_CH), F32), filt[:0:-1, :, 1]], axis=0)
    kern = kern / jnp.sum(jnp.abs(kern), axis=0, keepdims=True)
    return jnp.fft.rfft(kern, axis=0)


def fft_conv(u, kf, bias):
    L = u.shape[1]
    uf = jnp.fft.rfft(u.astype(F32), n=2 * L, axis=1)
    y = jnp.fft.irfft(uf * kf[None], n=2 * L, axis=1)[:, :L]
    return (y + u.astype(F32) * bias.astype(F32)).astype(u.dtype)


def context_attention(q, k, v, sink):
    b, lc = q.shape[:2]
    s = jnp.einsum('bqhgd,bkhd->bhgqk', q, k).astype(F32)
    sk = jnp.broadcast_to(sink.astype(F32)[None, :, :, None, None], s.shape[:-1] + (1,))
    p = jax.nn.softmax(jnp.concatenate([s, sk], axis=-1), axis=-1)[..., :-1]
    o = jnp.einsum('bhgqk,bkhd->bqhgd', p.astype(v.dtype), v)
    return o.reshape(b, lc, -1)


def window_attention(q, k, v, kc, vc, sink):
    b, L = q.shape[:2]
    W = ATT_BLOCK
    nb = L // W
    qb = q.reshape(b, nb, W, ATT_KV_HEADS, ATT_GROUP, ATT_HD)
    pad = ((0, 0), (W, W), (0, 0), (0, 0))
    kp = jnp.pad(k, pad).reshape(b, nb + 2, W, ATT_KV_HEADS, ATT_HD)
    vp = jnp.pad(v, pad).reshape(b, nb + 2, W, ATT_KV_HEADS, ATT_HD)
    kw = jnp.concatenate([kp[:, :-2], kp[:, 1:-1], kp[:, 2:]], axis=2)
    vw = jnp.concatenate([vp[:, :-2], vp[:, 1:-1], vp[:, 2:]], axis=2)
    s_win = jnp.einsum('bnqhgd,bnkhd->bhgnqk', qb, kw).astype(F32)
    s_ctx = jnp.einsum('bnqhgd,bkhd->bhgnqk', qb, kc).astype(F32)
    blk = jnp.arange(nb)[:, None, None] * W
    qpos = blk + jnp.arange(W)[None, :, None]
    kpos = blk - W + jnp.arange(3 * W)[None, None, :]
    valid = (jnp.abs(qpos - kpos) <= ATT_WINDOW) & (kpos >= 0) & (kpos < L)
    s_win = jnp.where(valid, s_win, -jnp.inf)
    sk = jnp.broadcast_to(sink.astype(F32)[None, :, :, None, None, None], s_win.shape[:-1] + (1,))
    p = jax.nn.softmax(jnp.concatenate([s_win, s_ctx, sk], axis=-1), axis=-1)
    n_ctx = kc.shape[1]
    p_win = p[..., :3 * W].astype(v.dtype)
    p_ctx = p[..., 3 * W:3 * W + n_ctx].astype(v.dtype)
    o = jnp.einsum('bhgnqk,bnkhd->bnqhgd', p_win, vw) + jnp.einsum('bhgnqk,bkhd->bnqhgd', p_ctx, vc)
    return o.reshape(b, L, -1)


def even_mixer_core(zc, zl, gla_w_gate, gla_b_gate, gla_norm, hy_conv_w, hy_conv_b,
                    hy_w1, hy_b1, hy_w2, hy_b2, hy_w3, hy_bias, need_ctx):
    b = zl.shape[0]

    def prep(z):
        q, k, v, g, lr, u = split_cols(z, EVEN_SPLITS)
        q = to_heads(q, GLA_HEADS) * (GLA_DK ** -0.5)
        k = to_heads(k, GLA_HEADS)
        v = to_heads(v, GLA_HEADS)
        la = [to_heads(jax.nn.log_sigmoid((lr[..., d * GLA_LOWRANK:(d + 1) * GLA_LOWRANK] @ gla_w_gate[d]
                                          + gla_b_gate[d]).astype(F32)) / GLA_GATE_NORMALIZER, GLA_HEADS)
              for d in range(2)]
        return q, k, v, g, la[0], la[1], u

    def gla_merge(o, g):
        return from_heads(rms_norm(o, gla_norm)) * jax.nn.silu(g)

    def hyena(u):
        L = u.shape[1]
        u = short_conv(u, hy_conv_w, hy_conv_b)
        z, *gates = split_cols(u, (HY_CH,) * (HY_ORDER + 1))
        kf = hyena_filters(L, hy_w1, hy_b1, hy_w2, hy_b2, hy_w3)
        for n, gate in enumerate(gates):
            z = gate * fft_conv(z, kf[:, n], hy_bias[n])
        return z

    cq, ck, cv, cg, ca_f, ca_b, cu = prep(zc)
    lq, lk, lv, lg, la_f, la_b, lu = prep(zl)
    s_zero = jnp.zeros((b, GLA_HEADS, GLA_DK, GLA_DV), F32)
    mc = None
    if need_ctx:
        oc_f, sc_f = gla_chunked(cq, ck, cv, ca_f, s_zero)
        oc_b, sc_b = gla_chunked(flip(cq), flip(ck), flip(cv), flip(ca_b), s_zero)
        mc = jnp.concatenate([gla_merge(oc_f + flip(oc_b), cg), hyena(cu)], axis=-1)
    else:
        sc_f = decayed_state(ck, cv, ca_f)
        sc_b = decayed_state(flip(ck), flip(cv), flip(ca_b))
    ol_f, _ = gla_chunked(lq, lk, lv, la_f, sc_f)
    ol_b, _ = gla_chunked(flip(lq), flip(lk), flip(lv), flip(la_b), sc_b)
    ml = jnp.concatenate([gla_merge(ol_f + flip(ol_b), lg), hyena(lu)], axis=-1)
    return mc, ml


def odd_mixer_core(zc, zl, ret_norm, att_q_norm, att_k_norm, att_sink, need_ctx):
    b = zl.shape[0]
    ret_inv = ROPE_BASE ** (-jnp.linspace(0.0, 1.0, RET_DK // 2, dtype=F32))
    log_g_f = jnp.log(1.0 - 2.0 ** (-5.0 - jnp.arange(RET_HEADS, dtype=F32)))
    log_g_b = log_g_f[::-1]
    sink = att_sink.reshape(ATT_KV_HEADS, ATT_GROUP)

    def prep(z, latent):
        L = z.shape[1]
        rq, rk, rv, rg, aq, ak, av = split_cols(z, ODD_SPLITS)
        rq = rq.reshape(b, L, RET_HEADS, RET_DK)
        rk = rk.reshape(b, L, RET_HEADS, RET_DK)
        aq = rms_norm(aq.reshape(b, L, ATT_HEADS, ATT_HD), att_q_norm)
        ak = rms_norm(ak.reshape(b, L, ATT_KV_HEADS, ATT_HD), att_k_norm)
        if latent:
            pos = jnp.arange(L)
            rq, rk = rope(rq, pos, ret_inv), rope(rk, pos, ret_inv)
            aq, ak = axial_rope(aq), axial_rope(ak)
        rq = rq.transpose(0, 2, 1, 3)
        rk = (rk * (RET_DK ** -0.5)).transpose(0, 2, 1, 3)
        rv = to_heads(rv, RET_HEADS)
        aq = (aq * (ATT_HD ** -0.5)).reshape(b, L, ATT_KV_HEADS, ATT_GROUP, ATT_HD)
        av = av.reshape(b, L, ATT_KV_HEADS, ATT_HD)
        return rq, rk, rv, rg, aq, ak, av

    def ret_merge(o, g):
        return from_heads(rms_norm(o, ret_norm)) * jax.nn.silu(g)

    cq, ck, cv, cg, caq, cak, cav = prep(zc, False)
    lq, lk, lv, lg, laq, lak, lav = prep(zl, True)
    s_zero = jnp.zeros((b, RET_HEADS, RET_DK, RET_DV), F32)
    mc = None
    if need_ctx:
        oc_f, sc_f = retention_chunked(cq, ck, cv, log_g_f, s_zero)
        oc_b, sc_b = retention_chunked(flip(cq), flip(ck), flip(cv), log_g_b, s_zero)
        mc = jnp.concatenate([ret_merge(oc_f + flip(oc_b), cg),
                              context_attention(caq, cak, cav, sink)], axis=-1)
    else:
        lc = ck.shape[2]
        sc_f = decayed_state(ck, cv, jnp.broadcast_to(log_g_f[None, :, None, None], (1, RET_HEADS, lc, 1)))
        sc_b = decayed_state(flip(ck), flip(cv), jnp.broadcast_to(log_g_b[None, :, None, None], (1, RET_HEADS, lc, 1)))
    ol_f, _ = retention_chunked(lq, lk, lv, log_g_f, sc_f)
    ol_b, _ = retention_chunked(flip(lq), flip(lk), flip(lv), log_g_b, sc_b)
    ml = jnp.concatenate([ret_merge(ol_f + flip(ol_b), lg),
                          window_attention(laq, lak, lav, cak, cav, sink)], axis=-1)
    return mc, ml


def kernel(x, c, ctx, c_ctx, ada_w, ada_b, norm_mix, norm_ffn, ev_w_in, ev_w_out, gla_w_gate, gla_b_gate, gla_norm, hy_conv_w, hy_conv_b, hy_w1, hy_b1, hy_w2, hy_b2, hy_w3, hy_bias, od_w_in, od_w_out, ret_norm, att_q_norm, att_k_norm, att_sink, router_w, router_b, moe_w1, moe_w3, moe_w2):
    bsz, seq, d = x.shape
    n_ctx = ctx.shape[1]
    c_pad = jnp.zeros((8, d), F32).at[:bsz].set(c).at[bsz].set(c_ctx)
    mods = ada_modulation(c_pad, ada_w, ada_b)
    rw_pad = jnp.zeros((d, LOGIT_PAD), F32).at[:, :N_EXPERTS].set(router_w)

    xl, xc = x, ctx
    for layer in range(DEPTH):
        last = layer == DEPTH - 1
        ml = mods[layer, :bsz].reshape(bsz, 1, 6, d)
        mc = jnp.broadcast_to(mods[layer, bsz].reshape(1, 1, 6, d), (bsz, 1, 6, d))
        sh1, sc1, g1, sh2, sc2, g2 = [ml[:, :, i] for i in range(6)]
        csh1, csc1, cg1, csh2, csc2, cg2 = [mc[:, :, i] for i in range(6)]
        i = layer // 2
        if layer % 2 == 0:
            w_in, w_out = ev_w_in[i].astype(BF16), ev_w_out[i].astype(BF16)
            w_in = jnp.pad(w_in, ((0, 0), (0, EVEN_COLS_PAD - w_in.shape[1])))
            tn = EVEN_COLS_PAD // 7
        else:
            w_in, w_out = od_w_in[i].astype(BF16), od_w_out[i].astype(BF16)
            tn = w_in.shape[1] // 6
        zl = norm_mod_matmul(xl, norm_mix[layer], sc1, sh1, w_in, tm=1024, tn=tn, out_dtype=F32)
        zc = norm_mod_matmul(xc, norm_mix[layer], csc1, csh1, w_in, tm=n_ctx, tn=tn, out_dtype=F32)
        if layer % 2 == 0:
            mixc, mixl = even_mixer_core(zc, zl, gla_w_gate[i], gla_b_gate[i], gla_norm[i], hy_conv_w[i],
                                         hy_conv_b[i], hy_w1[i], hy_b1[i], hy_w2[i], hy_b2[i], hy_w3[i],
                                         hy_bias[i], not last)
        else:
            mixc, mixl = odd_mixer_core(zc, zl, ret_norm[i], att_q_norm[i], att_k_norm[i], att_sink[i],
                                        not last)
        xl, hl, lgl = outproj_residual_norm(mixl, w_out, xl, g1, norm_ffn[layer], sc2, sh2, rw_pad, tm=512)
        w1 = moe_w1[layer].astype(BF16)
        w3 = moe_w3[layer].astype(BF16)
        w2 = moe_w2[layer].astype(BF16)
        if not last:
            xc, hc, lgc = outproj_residual_norm(mixc, w_out, xc, cg1, norm_ffn[layer], csc2, csh2, rw_pad,
                                                tm=n_ctx)
            h = jnp.concatenate([hc.reshape(-1, d), hl.reshape(-1, d)], axis=0)
            lg = jnp.concatenate([lgc.reshape(-1, LOGIT_PAD), lgl.reshape(-1, LOGIT_PAD)], axis=0)
            y = moe_apply(h, lg, router_b, w1, w3, w2)
            yc = y[:bsz * n_ctx].reshape(bsz, n_ctx, d)
            yl = y[bsz * n_ctx:].reshape(bsz, seq, d)
            xc = xc + cg2 * yc
            xl = xl + g2 * yl
        else:
            y = moe_apply(hl.reshape(-1, d), lgl.reshape(-1, LOGIT_PAD), router_b, w1, w3, w2)
            xl = xl + g2 * y.reshape(bsz, seq, d)
    return xl
```

```python
import functools
import math

import jax
import jax.numpy as jnp
import numpy as np
from jax import lax
from jax.experimental import pallas as pl
from jax.experimental.pallas import tpu as pltpu

D_MODEL = 2048
BATCH = 2
SEQ = 8192
DEPTH = 2
GRID_W = 64
CTX_LEN = 256
MIX_WIDTH = D_MODEL
GROUP_WIDTH = MIX_WIDTH // 2
EPS = 1e-6
F32 = jnp.float32
BF16 = jnp.bfloat16

GLA_HEADS = 4
GLA_DK = GROUP_WIDTH // (2 * GLA_HEADS)
GLA_DV = GROUP_WIDTH // GLA_HEADS
GLA_LOWRANK = 16
GLA_GATE_NORMALIZER = 16.0
GLA_CHUNK = 64

HY_CH = GROUP_WIDTH
HY_ORDER = 2
HY_BANDS = 16
HY_EMB = 1 + 2 * HY_BANDS
HY_FFN = 64
HY_SHIFT = 0.05
HY_MIN_DECAY = math.log(1e-2) / 1.5
HY_MAX_DECAY = math.log(1e-2) / 0.3

RET_HEADS = 4
RET_DK = GROUP_WIDTH // (2 * RET_HEADS)
RET_DV = GROUP_WIDTH // RET_HEADS
RET_CHUNK = 64

ATT_HD = 128
ATT_HEADS = GROUP_WIDTH // ATT_HD
ATT_KV_HEADS = 2
ATT_GROUP = ATT_HEADS // ATT_KV_HEADS
ATT_WINDOW = 128
ATT_BLOCK = 128
ROPE_BASE = 10000.0

N_EXPERTS = 16
N_GROUPS = 4
EXPERTS_PER_GROUP = N_EXPERTS // N_GROUPS
TOP_K = 2
D_FF_EXPERT = D_MODEL // 2

EVEN_SPLITS = (GLA_HEADS * GLA_DK, GLA_HEADS * GLA_DK, GLA_HEADS * GLA_DV, GLA_HEADS * GLA_DV,
               2 * GLA_LOWRANK, (HY_ORDER + 1) * HY_CH)
ODD_SPLITS = (RET_HEADS * RET_DK, RET_HEADS * RET_DK, RET_HEADS * RET_DV, RET_HEADS * RET_DV,
              ATT_HEADS * ATT_HD, ATT_KV_HEADS * ATT_HD, ATT_KV_HEADS * ATT_HD)

LANE = 128
EVEN_COLS_PAD = -(-sum(EVEN_SPLITS) // (7 * LANE)) * (7 * LANE)
VMEM_LIMIT = 56 * 1024 * 1024
MOE_TM = 512
LOGIT_PAD = LANE


def _cparams(sem):
    return pltpu.CompilerParams(dimension_semantics=sem, vmem_limit_bytes=VMEM_LIMIT)


def _ada_kernel(c_ref, w_ref, b_ref, o_ref):
    c = c_ref[...]
    s = (c * jax.nn.sigmoid(c)).astype(BF16)
    o_ref[0] = jnp.dot(s, w_ref[0].astype(BF16), preferred_element_type=F32) + b_ref[0]


def ada_modulation(c_pad, ada_w, ada_b):
    depth, d, n = ada_w.shape
    tn = 1024
    return pl.pallas_call(
        _ada_kernel,
        grid=(depth, n // tn),
        in_specs=[
            pl.BlockSpec((8, d), lambda l, j: (0, 0)),
            pl.BlockSpec((1, d, tn), lambda l, j: (l, 0, j)),
            pl.BlockSpec((1, 1, tn), lambda l, j: (l, 0, j)),
        ],
        out_specs=pl.BlockSpec((1, 8, tn), lambda l, j: (l, 0, j)),
        out_shape=jax.ShapeDtypeStruct((depth, 8, n), F32),
        compiler_params=_cparams(("parallel", "arbitrary")),
        name="ada_modulation",
    )(c_pad, ada_w, ada_b.reshape(depth, 1, n))


def _norm_mod(x, gain, sc, sh):
    ms = jnp.mean(x * x, axis=-1, keepdims=True)
    return (x * lax.rsqrt(ms + EPS) * gain) * (1.0 + sc) + sh


def _inproj_kernel(x_ref, gain_ref, sc_ref, sh_ref, w_ref, o_ref, h_ref):
    @pl.when(pl.program_id(2) == 0)
    def _():
        h_ref[...] = _norm_mod(x_ref[0], gain_ref[...], sc_ref[0], sh_ref[0]).astype(BF16)

    o_ref[0] = jnp.dot(h_ref[...], w_ref[...], preferred_element_type=F32).astype(o_ref.dtype)


def norm_mod_matmul(x, gain, sc, sh, w, *, tm, tn, out_dtype):
    b, l, d = x.shape
    n = w.shape[1]
    return pl.pallas_call(
        _inproj_kernel,
        grid=(b, l // tm, n // tn),
        in_specs=[
            pl.BlockSpec((1, tm, d), lambda bi, i, j: (bi, i, 0)),
            pl.BlockSpec((1, d), lambda bi, i, j: (0, 0)),
            pl.BlockSpec((1, 1, d), lambda bi, i, j: (bi, 0, 0)),
            pl.BlockSpec((1, 1, d), lambda bi, i, j: (bi, 0, 0)),
            pl.BlockSpec((d, tn), lambda bi, i, j: (0, j)),
        ],
        out_specs=pl.BlockSpec((1, tm, tn), lambda bi, i, j: (bi, i, j)),
        out_shape=jax.ShapeDtypeStruct((b, l, n), out_dtype),
        scratch_shapes=[pltpu.VMEM((tm, d), BF16)],
        compiler_params=_cparams(("parallel", "parallel", "arbitrary")),
        name="norm_mod_matmul",
    )(x, gain.reshape(1, d), sc, sh, w)


def _outproj_kernel(m1_ref, m2_ref, w_ref, x_ref, g_ref, gain_ref, sc_ref, sh_ref, rw_ref,
                    xo_ref, h_ref, lg_ref):
    k1 = m1_ref.shape[-1]
    o = (jnp.dot(m1_ref[0], w_ref[:k1, :], preferred_element_type=F32)
         + jnp.dot(m2_ref[0], w_ref[k1:, :], preferred_element_type=F32))
    xn = x_ref[0] + g_ref[0] * o
    xo_ref[0] = xn
    h = _norm_mod(xn, gain_ref[...], sc_ref[0], sh_ref[0])
    h_ref[0] = h.astype(BF16)
    lg_ref[0] = jnp.dot(h, rw_ref[...], preferred_element_type=F32,
                        precision=lax.Precision.HIGHEST)


def outproj_residual_norm(m1, m2, w, x, g, gain, sc, sh, rw_pad, *, tm):
    b, l, d = x.shape
    k1, k2 = m1.shape[-1], m2.shape[-1]
    row = lambda bi, i: (bi, i, 0)
    mod = lambda bi, i: (bi, 0, 0)
    const = lambda bi, i: (0, 0)
    return pl.pallas_call(
        _outproj_kernel,
        grid=(b, l // tm),
        in_specs=[
            pl.BlockSpec((1, tm, k1), row),
            pl.BlockSpec((1, tm, k2), row),
            pl.BlockSpec((k1 + k2, d), const, pipeline_mode=pl.Buffered(1)),
            pl.BlockSpec((1, tm, d), row),
            pl.BlockSpec((1, 1, d), mod),
            pl.BlockSpec((1, d), const),
            pl.BlockSpec((1, 1, d), mod),
            pl.BlockSpec((1, 1, d), mod),
            pl.BlockSpec((d, LOGIT_PAD), const),
        ],
        out_specs=[
            pl.BlockSpec((1, tm, d), row),
            pl.BlockSpec((1, tm, d), row),
            pl.BlockSpec((1, tm, LOGIT_PAD), row),
        ],
        out_shape=[
            jax.ShapeDtypeStruct((b, l, d), F32),
            jax.ShapeDtypeStruct((b, l, d), BF16),
            jax.ShapeDtypeStruct((b, l, LOGIT_PAD), F32),
        ],
        compiler_params=_cparams(("parallel", "parallel")),
        name="outproj_residual_norm",
    )(m1, m2, w, x, g, gain.reshape(1, d), sc, sh, rw_pad)


def _moe_kernel(te_ref, tsrc_ref, tvalid_ref, x_ref, w1_ref, w3_ref, w2_ref, rw_ref, o_ref):
    i = pl.program_id(0)

    @pl.when(tvalid_ref[i] > 0)
    def _():
        x = x_ref[...]
        a = jnp.dot(x, w1_ref[0], preferred_element_type=F32)
        b = jnp.dot(x, w3_ref[0], preferred_element_type=F32)
        mid = (a * jax.nn.sigmoid(a) * b).astype(BF16)
        y = jnp.dot(mid, w2_ref[0], preferred_element_type=F32)
        o_ref[...] = (y * rw_ref[...]).astype(o_ref.dtype)

    @pl.when(tvalid_ref[i] == 0)
    def _():
        o_ref[...] = jnp.zeros_like(o_ref)


def moe_grouped(x_sorted, w1, w3, w2, row_w, tile_expert, tile_src, tile_valid):
    p, d = x_sorted.shape
    ff = w1.shape[-1]
    tm = MOE_TM
    grid_spec = pltpu.PrefetchScalarGridSpec(
        num_scalar_prefetch=3,
        grid=(p // tm,),
        in_specs=[
            pl.BlockSpec((tm, d), lambda i, te, ts, tv: (ts[i], 0)),
            pl.BlockSpec((1, d, ff), lambda i, te, ts, tv: (te[i], 0, 0)),
            pl.BlockSpec((1, d, ff), lambda i, te, ts, tv: (te[i], 0, 0)),
            pl.BlockSpec((1, ff, d), lambda i, te, ts, tv: (te[i], 0, 0)),
            pl.BlockSpec((tm, 1), lambda i, te, ts, tv: (ts[i], 0)),
        ],
        out_specs=pl.BlockSpec((tm, d), lambda i, te, ts, tv: (i, 0)),
    )
    return pl.pallas_call(
        _moe_kernel,
        grid_spec=grid_spec,
        out_shape=jax.ShapeDtypeStruct((p, d), BF16),
        compiler_params=_cparams(("arbitrary",)),
        name="moe_grouped",
    )(tile_expert, tile_src, tile_valid, x_sorted, w1, w3, w2, row_w)


CHUNK = 64
SCAN_T = 256
_NT = (((1,), (1,)), ((), ()))
_TN = (((0,), (0,)), ((), ()))
HIGHEST = lax.Precision.HIGHEST


def _chunk_keep(reverse):
    r = lax.broadcasted_iota(jnp.int32, (CHUNK, CHUNK), 0)
    c = lax.broadcasted_iota(jnp.int32, (CHUNK, CHUNK), 1)
    return (c >= r) if reverse else (c <= r)


def _chunk_step(q, k, v, e_q, e_k, e_s, e_last, keep, st):
    qe = (q * e_q).astype(BF16)
    ke = (k * e_k).astype(BF16)
    kd = (k * e_s).astype(BF16)
    vb = v.astype(BF16)
    att = lax.dot_general(qe, ke, _NT, preferred_element_type=F32)
    att = jnp.where(keep, att, 0.0).astype(BF16)
    o = (jnp.dot(att, vb, preferred_element_type=F32)
         + lax.dot_general(qe, st.astype(BF16), _NT, preferred_element_type=F32))
    st = e_last * st + lax.dot_general(vb, kd, _TN, preferred_element_type=F32)
    return o, st


def _merge_store(o, ob_ref, g_ref, gn_ref, o_ref, sl):
    tot = o + ob_ref[0, sl, :]
    y = tot * lax.rsqrt(jnp.mean(tot * tot, axis=-1, keepdims=True) + EPS) * gn_ref[...]
    g = g_ref[0, sl, :]
    o_ref[0, sl, :] = (y * (g * jax.nn.sigmoid(g))).astype(o_ref.dtype)


def _chunk_order(n, reverse):
    return range(n - 1, -1, -1) if reverse else range(n)


def _gla_kernel(qc, kc, vc, lrc, ql, kl, vl, lrl, wg, bg, *rest, reverse, merge):
    if merge:
        obc, obl, gc, gl, gn, oc_ref, ol_ref, st_ref = rest
    else:
        oc_ref, ol_ref, st_ref = rest
        obc = obl = gc = gl = gn = None
    keep = _chunk_keep(reverse)
    tri = keep.astype(F32)

    def run(q_ref, k_ref, v_ref, lr_ref, o_ref, ob_ref, g_ref):
        pre = jnp.dot(lr_ref[0], wg[...], preferred_element_type=F32, precision=HIGHEST) + bg[...]
        la = (jnp.minimum(pre, 0.0) - jnp.log(1.0 + jnp.exp(-jnp.abs(pre)))) * (1.0 / GLA_GATE_NORMALIZER)
        st = st_ref[...]
        for ci in _chunk_order(q_ref.shape[1] // CHUNK, reverse):
            sl = pl.ds(ci * CHUNK, CHUNK)
            cum = jnp.dot(tri, la[ci * CHUNK:(ci + 1) * CHUNK], preferred_element_type=F32,
                          precision=HIGHEST)
            last = cum[0:1] if reverse else cum[CHUNK - 1:CHUNK]
            o, st = _chunk_step(q_ref[0, sl, :] * (GLA_DK ** -0.5), k_ref[0, sl, :], v_ref[0, sl, :],
                                jnp.exp(cum), jnp.exp(-cum), jnp.exp(last - cum), jnp.exp(last), keep, st)
            if merge:
                _merge_store(o, ob_ref, g_ref, gn, o_ref, sl)
            else:
                o_ref[0, sl, :] = o
        st_ref[...] = st

    @pl.when(pl.program_id(2) == 0)
    def _():
        st_ref[...] = jnp.zeros_like(st_ref)
        run(qc, kc, vc, lrc, oc_ref, obc, gc)

    @pl.when(pl.program_id(2) > 0)
    def _():
        run(ql, kl, vl, lrl, ol_ref, obl, gl)


def _ret_kernel(qc, kc, vc, ql, kl, vl, cos_ref, sin_ref, cum_ref, *rest, reverse, merge):
    if merge:
        obc, obl, gc, gl, gn, oc_ref, ol_ref, st_ref = rest
    else:
        oc_ref, ol_ref, st_ref = rest
        obc = obl = gc = gl = gn = None
    keep = _chunk_keep(reverse)
    cum = cum_ref[0]
    last = cum[0:1] if reverse else cum[CHUNK - 1:CHUNK]
    e_q, e_k, e_s, e_last = jnp.exp(cum), jnp.exp(-cum), jnp.exp(last - cum), jnp.exp(last)

    def run(q_ref, k_ref, v_ref, o_ref, ob_ref, g_ref, latent):
        st = st_ref[...]
        for ci in _chunk_order(q_ref.shape[1] // CHUNK, reverse):
            sl = pl.ds(ci * CHUNK, CHUNK)
            q = q_ref[0, sl, :]
            k = k_ref[0, sl, :]
            if latent:
                cs = cos_ref[sl, :]
                sn = sin_ref[sl, :]
                q = q * cs + pltpu.roll(q, RET_DK // 2, 1) * sn
                k = k * cs + pltpu.roll(k, RET_DK // 2, 1) * sn
            o, st = _chunk_step(q, k * (RET_DK ** -0.5), v_ref[0, sl, :], e_q, e_k, e_s, e_last, keep, st)
            if merge:
                _merge_store(o, ob_ref, g_ref, gn, o_ref, sl)
            else:
                o_ref[0, sl, :] = o
        st_ref[...] = st

    @pl.when(pl.program_id(2) == 0)
    def _():
        st_ref[...] = jnp.zeros_like(st_ref)
        run(qc, kc, vc, oc_ref, obc, gc, False)

    @pl.when(pl.program_id(2) > 0)
    def _():
        run(ql, kl, vl, ol_ref, obl, gl, True)


def _scan_call(body, zc, zl, heads, dk, dv, col_q, col_k, col_v, shared_cols, extra_specs, merge_in,
               *, reverse, name):
    b, lc, _ = zc.shape
    l = zl.shape[1]
    t = SCAN_T
    nblk = l // t
    if reverse:
        blk = lambda i: jnp.where(i == 0, nblk - 1, nblk - i)
    else:
        blk = lambda i: jnp.maximum(i - 1, 0)

    def ctx_spec(w, col, per_head=True):
        return pl.BlockSpec((1, lc, w), lambda bi, h, i: (bi, 0, col // w + (h if per_head else 0)))

    def lat_spec(w, col, per_head=True):
        return pl.BlockSpec((1, t, w), lambda bi, h, i: (bi, blk(i), col // w + (h if per_head else 0)))

    ins = [zc, zc, zc] + [zc] * len(shared_cols) + [zl, zl, zl] + [zl] * len(shared_cols)
    specs = [ctx_spec(dk, col_q), ctx_spec(dk, col_k), ctx_spec(dv, col_v)]
    specs += [ctx_spec(w, col, False) for w, col in shared_cols]
    specs += [lat_spec(dk, col_q), lat_spec(dk, col_k), lat_spec(dv, col_v)]
    specs += [lat_spec(w, col, False) for w, col in shared_cols]
    for a, s in extra_specs:
        ins.append(a)
        specs.append(s)
    if merge_in is not None:
        obc, obl, col_g, gain = merge_in
        ins += [obc, obl, zc, zl, gain.reshape(1, dv)]
        specs += [ctx_spec(dv, 0), lat_spec(dv, 0), ctx_spec(dv, col_g), lat_spec(dv, col_g),
                  pl.BlockSpec((1, dv), lambda bi, h, i: (0, 0))]
        out_dtype = BF16
    else:
        out_dtype = F32
    return pl.pallas_call(
        functools.partial(body, reverse=reverse, merge=merge_in is not None),
        grid=(b, heads, 1 + nblk),
        in_specs=specs,
        out_specs=[ctx_spec(dv, 0), lat_spec(dv, 0)],
        out_shape=[jax.ShapeDtypeStruct((b, lc, heads * dv), out_dtype),
                   jax.ShapeDtypeStruct((b, l, heads * dv), out_dtype)],
        scratch_shapes=[pltpu.VMEM((dv, dk), F32)],
        compiler_params=_cparams(("parallel", "parallel", "arbitrary")),
        name=name,
    )(*ins)


EV_COL = dict(q=0, k=512, v=1024, g=2048, u=3072, lr=6144)
OD_COL = dict(q=0, k=512, v=1024, g=2048, aq=3072, ak=4096, av=4352)


def gla_bidir(zc, zl, gla_w_gate, gla_b_gate, gla_norm):
    hk = GLA_HEADS * GLA_DK
    outs = None
    for d, reverse in ((1, True), (0, False)):
        wg = jnp.zeros((LANE, hk), F32).at[d * GLA_LOWRANK:(d + 1) * GLA_LOWRANK].set(gla_w_gate[d])
        shared_cols = [(LANE, EV_COL["lr"])]
        extra_specs = [(wg, pl.BlockSpec((LANE, GLA_DK), lambda bi, h, i: (0, h))),
                       (gla_b_gate[d].reshape(1, hk), pl.BlockSpec((1, GLA_DK), lambda bi, h, i: (0, h)))]
        merge_in = None if outs is None else (outs[0], outs[1], EV_COL["g"], gla_norm)
        outs = _scan_call(_gla_kernel, zc, zl, GLA_HEADS, GLA_DK, GLA_DV, EV_COL["q"], EV_COL["k"],
                          EV_COL["v"], shared_cols, extra_specs, merge_in, reverse=reverse,
                          name="gla_bwd" if reverse else "gla_fwd_merge")
    return outs


def retention_bidir(zc, zl, ret_norm):
    l = zl.shape[1]
    inv = ROPE_BASE ** (-jnp.linspace(0.0, 1.0, RET_DK // 2, dtype=F32))
    ang = jnp.arange(l, dtype=F32)[:, None] * inv[None, :]
    cos2 = jnp.concatenate([jnp.cos(ang), jnp.cos(ang)], axis=-1)
    sin2 = jnp.concatenate([-jnp.sin(ang), jnp.sin(ang)], axis=-1)
    log_g_f = jnp.log(1.0 - 2.0 ** (-5.0 - jnp.arange(RET_HEADS, dtype=F32)))
    idx = jnp.arange(CHUNK, dtype=F32)
    nblk = l // SCAN_T
    outs = None
    for reverse in (True, False):
        if reverse:
            cum = (CHUNK - idx)[None, :, None] * log_g_f[::-1][:, None, None]
            blk = lambda i: jnp.where(i == 0, nblk - 1, nblk - i)
        else:
            cum = (idx + 1.0)[None, :, None] * log_g_f[:, None, None]
            blk = lambda i: jnp.maximum(i - 1, 0)
        cum = jnp.broadcast_to(cum, (RET_HEADS, CHUNK, RET_DK))
        tab_spec = pl.BlockSpec((SCAN_T, RET_DK), lambda bi, h, i, blk=blk: (blk(i), 0))
        extra_specs = [(cos2, tab_spec), (sin2, tab_spec),
                       (cum, pl.BlockSpec((1, CHUNK, RET_DK), lambda bi, h, i: (h, 0, 0)))]
        merge_in = None if outs is None else (outs[0], outs[1], OD_COL["g"], ret_norm)
        outs = _scan_call(_ret_kernel, zc, zl, RET_HEADS, RET_DK, RET_DV, OD_COL["q"], OD_COL["k"],
                          OD_COL["v"], [], extra_specs, merge_in, reverse=reverse,
                          name="ret_bwd" if reverse else "ret_fwd_merge")
    return outs


ATT_QKV = (ATT_HEADS + 2 * ATT_KV_HEADS) * ATT_HD


def _att_prep_kernel(z_ref, qn_ref, kn_ref, *rest, latent):
    if latent:
        cos_ref, sin_ref, o_ref = rest
        cs, sn = cos_ref[...], sin_ref[...]
        lane = lax.broadcasted_iota(jnp.int32, cs.shape, 1)
        low = (lane & (ATT_HD // 2 - 1)) < (ATT_HD // 4)
    else:
        (o_ref,) = rest
    for h in range(ATT_HEADS + ATT_KV_HEADS):
        x = z_ref[0, :, h * ATT_HD:(h + 1) * ATT_HD]
        gain = qn_ref[...] if h < ATT_HEADS else kn_ref[...]
        y = x * lax.rsqrt(jnp.mean(x * x, axis=-1, keepdims=True) + EPS) * gain
        if latent:
            partner = jnp.where(low, pltpu.roll(y, ATT_HD - ATT_HD // 4, 1), pltpu.roll(y, ATT_HD // 4, 1))
            y = y * cs + partner * sn
        if h < ATT_HEADS:
            y = y * (ATT_HD ** -0.5)
        o_ref[0, :, h * ATT_HD:(h + 1) * ATT_HD] = y.astype(o_ref.dtype)
    v0 = (ATT_HEADS + ATT_KV_HEADS) * ATT_HD
    o_ref[0, :, v0:] = z_ref[0, :, v0:].astype(o_ref.dtype)


def att_prep(z, att_q_norm, att_k_norm, tables, *, tm):
    b, l, _ = z.shape
    latent = tables is not None
    row = lambda bi, i: (bi, i, 0)
    const = lambda bi, i: (0, 0)
    ins = [z, att_q_norm.reshape(1, ATT_HD), att_k_norm.reshape(1, ATT_HD)]
    specs = [pl.BlockSpec((1, tm, ATT_QKV), lambda bi, i: (bi, i, OD_COL["aq"] // ATT_QKV)),
             pl.BlockSpec((1, ATT_HD), const), pl.BlockSpec((1, ATT_HD), const)]
    if latent:
        ins += list(tables)
        specs += [pl.BlockSpec((tm, ATT_HD), lambda bi, i: (i, 0))] * 2
    return pl.pallas_call(
        functools.partial(_att_prep_kernel, latent=latent),
        grid=(b, l // tm),
        in_specs=specs,
        out_specs=pl.BlockSpec((1, tm, ATT_QKV), row),
        out_shape=jax.ShapeDtypeStruct((b, l, ATT_QKV), BF16),
        compiler_params=_cparams(("parallel", "parallel")),
        name="att_prep",
    )(*ins)


def _axial_tables(l):
    quarter = ATT_HD // 4
    inv = ROPE_BASE ** (-jnp.arange(quarter, dtype=F32) / quarter)
    t = jnp.arange(l)
    a_row = (t // GRID_W).astype(F32)[:, None] * inv[None, :]
    a_col = (t % GRID_W).astype(F32)[:, None] * inv[None, :]
    cos = jnp.concatenate([jnp.cos(a_row)] * 2 + [jnp.cos(a_col)] * 2, axis=-1)
    sin = jnp.concatenate([-jnp.sin(a_row), jnp.sin(a_row), -jnp.sin(a_col), jnp.sin(a_col)], axis=-1)
    return cos, sin


def _att_kernel(sink_ref, q_ref, kp, kc, kn, vp, vc, vn, ck, cv, o_ref, *, seq):
    kvh = pl.program_id(1)
    n = pl.program_id(2)
    w = ATT_BLOCK
    q = jnp.concatenate([q_ref[0, :, g * ATT_HD:(g + 1) * ATT_HD] for g in range(ATT_GROUP)], axis=0)
    kw = jnp.concatenate([kp[0], kc[0], kn[0]], axis=0)
    vw = jnp.concatenate([vp[0], vc[0], vn[0]], axis=0)
    s_w = lax.dot_general(q, kw, _NT, preferred_element_type=F32)
    s_c = lax.dot_general(q, ck[0], _NT, preferred_element_type=F32)
    r = lax.broadcasted_iota(jnp.int32, s_w.shape, 0) & (w - 1)
    c = lax.broadcasted_iota(jnp.int32, s_w.shape, 1) - w
    kpos = n * w + c
    valid = (jnp.abs(r - c) <= ATT_WINDOW) & (kpos >= 0) & (kpos < seq)
    s_w = jnp.where(valid, s_w, -jnp.inf)
    sink = jnp.concatenate([jnp.full((w, 1), sink_ref[kvh * ATT_GROUP + g], F32) for g in range(ATT_GROUP)],
                           axis=0)
    m = jnp.maximum(jnp.maximum(jnp.max(s_w, axis=-1, keepdims=True), jnp.max(s_c, axis=-1, keepdims=True)),
                    sink)
    p_w = jnp.exp(s_w - m)
    p_c = jnp.exp(s_c - m)
    denom = (jnp.sum(p_w, axis=-1, keepdims=True) + jnp.sum(p_c, axis=-1, keepdims=True)
             + jnp.exp(sink - m))
    o = (jnp.dot(p_w.astype(BF16), vw, preferred_element_type=F32)
         + jnp.dot(p_c.astype(BF16), cv[0], preferred_element_type=F32)) / denom
    for g in range(ATT_GROUP):
        o_ref[0, :, g * ATT_HD:(g + 1) * ATT_HD] = o[g * w:(g + 1) * w].astype(o_ref.dtype)


def window_attention(qkv_l, qkv_c, att_sink):
    b, l, _ = qkv_l.shape
    lc = qkv_c.shape[1]
    w = ATT_BLOCK
    nb = l // w
    gw = ATT_GROUP * ATT_HD
    kcol = ATT_HEADS
    vcol = ATT_HEADS + ATT_KV_HEADS

    def kv_spec(col, shift):
        return pl.BlockSpec((1, w, ATT_HD),
                            lambda bi, h, n, s: (bi, jnp.clip(n + shift, 0, nb - 1), col + h))

    grid_spec = pltpu.PrefetchScalarGridSpec(
        num_scalar_prefetch=1,
        grid=(b, ATT_KV_HEADS, nb),
        in_specs=[pl.BlockSpec((1, w, gw), lambda bi, h, n, s: (bi, n, h))]
        + [kv_spec(kcol, d) for d in (-1, 0, 1)] + [kv_spec(vcol, d) for d in (-1, 0, 1)]
        + [pl.BlockSpec((1, lc, ATT_HD), lambda bi, h, n, s: (bi, 0, kcol + h)),
           pl.BlockSpec((1, lc, ATT_HD), lambda bi, h, n, s: (bi, 0, vcol + h))],
        out_specs=pl.BlockSpec((1, w, gw), lambda bi, h, n, s: (bi, n, h)),
    )
    return pl.pallas_call(
        functools.partial(_att_kernel, seq=l),
        grid_spec=grid_spec,
        out_shape=jax.ShapeDtypeStruct((b, l, ATT_HEADS * ATT_HD), BF16),
        compiler_params=_cparams(("parallel", "parallel", "arbitrary")),
        name="window_attention",
    )(att_sink.astype(F32), qkv_l, qkv_l, qkv_l, qkv_l, qkv_l, qkv_l, qkv_l, qkv_c, qkv_c)


def _route(logits, router_b):
    n = logits.shape[0]
    scores = jax.nn.softmax(logits, axis=-1)
    sel = scores + router_b.astype(F32)
    gscore = lax.top_k(sel.reshape(n, N_GROUPS, EXPERTS_PER_GROUP), TOP_K)[0].sum(-1)
    g_idx = jnp.argmax(gscore, axis=-1)
    in_group = (jnp.arange(N_EXPERTS) // EXPERTS_PER_GROUP)[None, :] == g_idx[:, None]
    _, e_idx = lax.top_k(jnp.where(in_group, sel, -jnp.inf), TOP_K)
    wts = jnp.take_along_axis(scores, e_idx, axis=-1)
    wts = wts / jnp.sum(wts, axis=-1, keepdims=True)
    return e_idx, wts


def moe_apply(h_bf16, logits, router_b, w1, w3, w2):
    n, d = h_bf16.shape
    tm = MOE_TM
    e_idx, wts = _route(logits[:, :N_EXPERTS], router_b)
    flat_e = e_idx.reshape(-1)
    flat_w = wts.reshape(-1)
    onehot = (flat_e[:, None] == jnp.arange(N_EXPERTS)[None, :]).astype(jnp.int32)
    rank = jnp.sum((jnp.cumsum(onehot, axis=0) - onehot) * onehot, axis=-1)
    counts = jnp.sum(onehot, axis=0)
    padded = ((counts + tm - 1) // tm) * tm
    ends = jnp.cumsum(padded)
    offs = ends - padded
    pos = offs[flat_e] + rank
    n_tiles = (2 * n + tm - 1) // tm + N_EXPERTS
    p = n_tiles * tm
    used = ends[-1] // tm
    tile_ids = jnp.arange(n_tiles, dtype=jnp.int32)
    tile_valid = (tile_ids < used).astype(jnp.int32)
    tile_src = jnp.minimum(tile_ids, used - 1).astype(jnp.int32)
    tile_expert = jnp.minimum(
        jnp.searchsorted(ends, tile_src * tm, side="right"), N_EXPERTS - 1).astype(jnp.int32)
    src = jnp.zeros((p,), jnp.int32).at[pos].set(jnp.arange(2 * n, dtype=jnp.int32) // 2,
                                                 unique_indices=True)
    row_w = jnp.zeros((p,), F32).at[pos].set(flat_w, unique_indices=True)
    x_sorted = jnp.take(h_bf16, src, axis=0)
    y_sorted = moe_grouped(x_sorted, w1, w3, w2, row_w.reshape(p, 1), tile_expert, tile_src,
                           tile_valid)
    pos2 = pos.reshape(n, 2)
    return (jnp.take(y_sorted, pos2[:, 0], axis=0).astype(F32)
            + jnp.take(y_sorted, pos2[:, 1], axis=0).astype(F32))


def split_cols(z, sizes):
    offs = np.cumsum([0] + list(sizes))
    return [z[..., int(offs[i]):int(offs[i + 1])] for i in range(len(sizes))]


def short_conv(u, w, bias):
    L = u.shape[1]
    up = jnp.pad(u, ((0, 0), (1, 1), (0, 0)))
    return up[:, :L] * w[0] + up[:, 1:L + 1] * w[1] + up[:, 2:] * w[2] + bias


def hyena_filters(L, w1, b1, w2, b2, w3):
    t = jnp.linspace(0.0, 1.0, L, dtype=F32)[:, None]
    w = (2.0 * math.pi / L) * jnp.arange(L, dtype=F32)[:, None]
    bands = jnp.linspace(1e-4, HY_BANDS - 1.0, HY_BANDS, dtype=F32)[None, :]
    feats = jnp.concatenate([t, jnp.cos(bands * w), -jnp.sin(bands * w)], axis=-1)
    hdn = jnp.sin(feats @ w1.astype(F32) + b1.astype(F32))
    hdn = jnp.sin(hdn @ w2.astype(F32) + b2.astype(F32))
    filt = (hdn @ w3.astype(F32)).reshape(L, HY_ORDER, 2, HY_CH)
    deltas = jnp.abs(jnp.linspace(HY_MIN_DECAY, HY_MAX_DECAY, HY_CH, dtype=F32))
    filt = filt * (jnp.exp(-t * deltas[None, :]) + HY_SHIFT)[:, None, None, :]
    kern = jnp.concatenate([filt[:, :, 0], jnp.zeros((1, HY_ORDER, HY_CH), F32), filt[:0:-1, :, 1]], axis=0)
    kern = kern / jnp.sum(jnp.abs(kern), axis=0, keepdims=True)
    return jnp.fft.rfft(kern, axis=0)


def fft_conv(u, kf, bias):
    L = u.shape[1]
    uf = jnp.fft.rfft(u.astype(F32), n=2 * L, axis=1)
    y = jnp.fft.irfft(uf * kf[None], n=2 * L, axis=1)[:, :L]
    return (y + u.astype(F32) * bias.astype(F32)).astype(u.dtype)


def even_mixer_core(zc, zl, gla_w_gate, gla_b_gate, gla_norm, hy_conv_w, hy_conv_b,
                    hy_w1, hy_b1, hy_w2, hy_b2, hy_w3, hy_bias):
    def hyena(u):
        L = u.shape[1]
        u = short_conv(u, hy_conv_w, hy_conv_b)
        z, *gates = split_cols(u, (HY_CH,) * (HY_ORDER + 1))
        kf = hyena_filters(L, hy_w1, hy_b1, hy_w2, hy_b2, hy_w3)
        for n, gate in enumerate(gates):
            z = gate * fft_conv(z, kf[:, n], hy_bias[n])
        return z

    gc, gl = gla_bidir(zc, zl, gla_w_gate, gla_b_gate, gla_norm)
    u0, u1 = EV_COL["u"], EV_COL["u"] + (HY_ORDER + 1) * HY_CH
    return ((gc, hyena(zc[..., u0:u1]).astype(BF16)), (gl, hyena(zl[..., u0:u1]).astype(BF16)))


def odd_mixer_core(zc, zl, ret_norm, att_q_norm, att_k_norm, att_sink):
    _, rl = retention_bidir(zc, zl, ret_norm)
    qkv_l = att_prep(zl, att_q_norm, att_k_norm, _axial_tables(zl.shape[1]), tm=512)
    qkv_c = att_prep(zc, att_q_norm, att_k_norm, None, tm=zc.shape[1])
    return rl, window_attention(qkv_l, qkv_c, att_sink)


def kernel(x, c, ctx, c_ctx, ada_w, ada_b, norm_mix, norm_ffn, ev_w_in, ev_w_out, gla_w_gate, gla_b_gate, gla_norm, hy_conv_w, hy_conv_b, hy_w1, hy_b1, hy_w2, hy_b2, hy_w3, hy_bias, od_w_in, od_w_out, ret_norm, att_q_norm, att_k_norm, att_sink, router_w, router_b, moe_w1, moe_w3, moe_w2):
    assert DEPTH == 2
    bsz, seq, d = x.shape
    n_ctx = ctx.shape[1]
    c_pad = jnp.zeros((8, d), F32).at[:bsz].set(c).at[bsz].set(c_ctx)
    mods = ada_modulation(c_pad, ada_w, ada_b)
    rw_pad = jnp.zeros((d, LOGIT_PAD), F32).at[:, :N_EXPERTS].set(router_w)

    xl, xc = x, ctx
    for layer in range(DEPTH):
        last = layer == DEPTH - 1
        ml = mods[layer, :bsz].reshape(bsz, 1, 6, d)
        mc = jnp.broadcast_to(mods[layer, bsz].reshape(1, 1, 6, d), (bsz, 1, 6, d))
        sh1, sc1, g1, sh2, sc2, g2 = [ml[:, :, i] for i in range(6)]
        csh1, csc1, cg1, csh2, csc2, cg2 = [mc[:, :, i] for i in range(6)]
        i = layer // 2
        if layer % 2 == 0:
            w_in, w_out = ev_w_in[i].astype(BF16), ev_w_out[i].astype(BF16)
            lr0 = sum(EVEN_SPLITS[:4])
            lr1 = lr0 + EVEN_SPLITS[4]
            w_in = jnp.concatenate([w_in[:, :lr0], w_in[:, lr1:], w_in[:, lr0:lr1],
                                    jnp.zeros((d, EVEN_COLS_PAD - w_in.shape[1]), BF16)], axis=1)
            tn = EVEN_COLS_PAD // 7
        else:
            w_in, w_out = od_w_in[i].astype(BF16), od_w_out[i].astype(BF16)
            tn = w_in.shape[1] // 6
        zl = norm_mod_matmul(xl, norm_mix[layer], sc1, sh1, w_in, tm=1024, tn=tn, out_dtype=F32)
        zc = norm_mod_matmul(xc, norm_mix[layer], csc1, csh1, w_in, tm=n_ctx, tn=tn, out_dtype=F32)
        if layer % 2 == 0:
            mixc, mixl = even_mixer_core(zc, zl, gla_w_gate[i], gla_b_gate[i], gla_norm[i], hy_conv_w[i],
                                         hy_conv_b[i], hy_w1[i], hy_b1[i], hy_w2[i], hy_b2[i], hy_w3[i],
                                         hy_bias[i])
        else:
            mixl = odd_mixer_core(zc, zl, ret_norm[i], att_q_norm[i], att_k_norm[i], att_sink[i])
        xl, hl, lgl = outproj_residual_norm(*mixl, w_out, xl, g1, norm_ffn[layer], sc2, sh2, rw_pad, tm=512)
        w1 = moe_w1[layer].astype(BF16)
        w3 = moe_w3[layer].astype(BF16)
        w2 = moe_w2[layer].astype(BF16)
        if not last:
            xc, hc, lgc = outproj_residual_norm(*mixc, w_out, xc, cg1, norm_ffn[layer], csc2, csh2, rw_pad,
                                                tm=n_ctx)
            h = jnp.concatenate([hc.reshape(-1, d), hl.reshape(-1, d)], axis=0)
            lg = jnp.concatenate([lgc.reshape(-1, LOGIT_PAD), lgl.reshape(-1, LOGIT_PAD)], axis=0)
            y = moe_apply(h, lg, router_b, w1, w3, w2)
            yc = y[:bsz * n_ctx].reshape(bsz, n_ctx, d)
            yl = y[bsz * n_ctx:].reshape(bsz, seq, d)
            xc = xc + cg2 * yc
            xl = xl + g2 * yl
        else:
            y = moe_apply(hl.reshape(-1, d), lgl.reshape(-1, LOGIT_PAD), router_b, w1, w3, w2)
            xl = xl + g2 * y.reshape(bsz, seq, d)
    return xl
```

```python
import functools
import math

import jax
import jax.numpy as jnp
import numpy as np
from jax import lax
from jax.experimental import pallas as pl
from jax.experimental.pallas import tpu as pltpu

D_MODEL = 2048
BATCH = 2
SEQ = 8192
DEPTH = 2
GRID_W = 64
CTX_LEN = 256
MIX_WIDTH = D_MODEL
GROUP_WIDTH = MIX_WIDTH // 2
EPS = 1e-6
F32 = jnp.float32
BF16 = jnp.bfloat16

GLA_HEADS = 4
GLA_DK = GROUP_WIDTH // (2 * GLA_HEADS)
GLA_DV = GROUP_WIDTH // GLA_HEADS
GLA_LOWRANK = 16
GLA_GATE_NORMALIZER = 16.0
GLA_CHUNK = 64

HY_CH = GROUP_WIDTH
HY_ORDER = 2
HY_BANDS = 16
HY_EMB = 1 + 2 * HY_BANDS
HY_FFN = 64
HY_SHIFT = 0.05
HY_MIN_DECAY = math.log(1e-2) / 1.5
HY_MAX_DECAY = math.log(1e-2) / 0.3

RET_HEADS = 4
RET_DK = GROUP_WIDTH // (2 * RET_HEADS)
RET_DV = GROUP_WIDTH // RET_HEADS
RET_CHUNK = 64

ATT_HD = 128
ATT_HEADS = GROUP_WIDTH // ATT_HD
ATT_KV_HEADS = 2
ATT_GROUP = ATT_HEADS // ATT_KV_HEADS
ATT_WINDOW = 128
ATT_BLOCK = 128
ROPE_BASE = 10000.0

N_EXPERTS = 16
N_GROUPS = 4
EXPERTS_PER_GROUP = N_EXPERTS // N_GROUPS
TOP_K = 2
D_FF_EXPERT = D_MODEL // 2

EVEN_SPLITS = (GLA_HEADS * GLA_DK, GLA_HEADS * GLA_DK, GLA_HEADS * GLA_DV, GLA_HEADS * GLA_DV,
               2 * GLA_LOWRANK, (HY_ORDER + 1) * HY_CH)
ODD_SPLITS = (RET_HEADS * RET_DK, RET_HEADS * RET_DK, RET_HEADS * RET_DV, RET_HEADS * RET_DV,
              ATT_HEADS * ATT_HD, ATT_KV_HEADS * ATT_HD, ATT_KV_HEADS * ATT_HD)

LANE = 128
EVEN_COLS_PAD = -(-sum(EVEN_SPLITS) // (7 * LANE)) * (7 * LANE)
VMEM_LIMIT = 56 * 1024 * 1024
MOE_TM = 512
LOGIT_PAD = LANE


def _cparams(sem):
    return pltpu.CompilerParams(dimension_semantics=sem, vmem_limit_bytes=VMEM_LIMIT)


def _ada_kernel(c_ref, w_ref, b_ref, o_ref):
    c = c_ref[...]
    s = (c * jax.nn.sigmoid(c)).astype(BF16)
    o_ref[0] = jnp.dot(s, w_ref[0].astype(BF16), preferred_element_type=F32) + b_ref[0]


def ada_modulation(c_pad, ada_w, ada_b):
    depth, d, n = ada_w.shape
    tn = 1024
    return pl.pallas_call(
        _ada_kernel,
        grid=(depth, n // tn),
        in_specs=[
            pl.BlockSpec((8, d), lambda l, j: (0, 0)),
            pl.BlockSpec((1, d, tn), lambda l, j: (l, 0, j)),
            pl.BlockSpec((1, 1, tn), lambda l, j: (l, 0, j)),
        ],
        out_specs=pl.BlockSpec((1, 8, tn), lambda l, j: (l, 0, j)),
        out_shape=jax.ShapeDtypeStruct((depth, 8, n), F32),
        compiler_params=_cparams(("parallel", "arbitrary")),
        name="ada_modulation",
    )(c_pad, ada_w, ada_b.reshape(depth, 1, n))


def _norm_mod(x, gain, sc, sh):
    ms = jnp.mean(x * x, axis=-1, keepdims=True)
    return (x * lax.rsqrt(ms + EPS) * gain) * (1.0 + sc) + sh


def _inproj_kernel(x_ref, gain_ref, sc_ref, sh_ref, w_ref, o_ref, h_ref):
    @pl.when(pl.program_id(2) == 0)
    def _():
        h_ref[...] = _norm_mod(x_ref[0], gain_ref[...], sc_ref[0], sh_ref[0]).astype(BF16)

    o_ref[0] = jnp.dot(h_ref[...], w_ref[...], preferred_element_type=F32).astype(o_ref.dtype)


def norm_mod_matmul(x, gain, sc, sh, w, *, tm, tn, out_dtype):
    b, l, d = x.shape
    n = w.shape[1]
    return pl.pallas_call(
        _inproj_kernel,
        grid=(b, l // tm, n // tn),
        in_specs=[
            pl.BlockSpec((1, tm, d), lambda bi, i, j: (bi, i, 0)),
            pl.BlockSpec((1, d), lambda bi, i, j: (0, 0)),
            pl.BlockSpec((1, 1, d), lambda bi, i, j: (bi, 0, 0)),
            pl.BlockSpec((1, 1, d), lambda bi, i, j: (bi, 0, 0)),
            pl.BlockSpec((d, tn), lambda bi, i, j: (0, j)),
        ],
        out_specs=pl.BlockSpec((1, tm, tn), lambda bi, i, j: (bi, i, j)),
        out_shape=jax.ShapeDtypeStruct((b, l, n), out_dtype),
        scratch_shapes=[pltpu.VMEM((tm, d), BF16)],
        compiler_params=_cparams(("parallel", "parallel", "arbitrary")),
        name="norm_mod_matmul",
    )(x, gain.reshape(1, d), sc, sh, w)


SUBLANE = 8
ROUTE_ROWS = SUBLANE


def _route_block(lg, rb_ref):
    tm = lg.shape[0]
    t = lg.T
    sub = lax.broadcasted_iota(jnp.int32, (SUBLANE, tm), 0)
    valid = sub < N_GROUPS
    members = range(EXPERTS_PER_GROUP)
    s = [jnp.where(valid, t[SUBLANE * j:SUBLANE * (j + 1), :], -jnp.inf) for j in members]
    mx = jnp.max(jnp.maximum(jnp.maximum(s[0], s[1]), jnp.maximum(s[2], s[3])), axis=0, keepdims=True)
    e = [jnp.exp(sj - mx) for sj in s]
    den = jnp.sum((e[0] + e[1]) + (e[2] + e[3]), axis=0, keepdims=True)
    sc = [ej / den for ej in e]
    sel = [sc[j] + jnp.concatenate([rb_ref[SUBLANE * j:SUBLANE * (j + 1), :]] * (tm // LANE), axis=1)
           for j in members]
    hi01, lo01 = jnp.maximum(sel[0], sel[1]), jnp.minimum(sel[0], sel[1])
    hi23, lo23 = jnp.maximum(sel[2], sel[3]), jnp.minimum(sel[2], sel[3])
    top1 = jnp.maximum(hi01, hi23)
    top2 = jnp.maximum(jnp.minimum(hi01, hi23), jnp.maximum(lo01, lo23))
    gscore = jnp.where(valid, top1 + top2, -jnp.inf)
    gmax = jnp.max(gscore, axis=0, keepdims=True)
    gidx = jnp.min(jnp.where(gscore == gmax, sub, SUBLANE), axis=0, keepdims=True)
    chosen = sub == gidx
    v = [jnp.sum(jnp.where(chosen, sel[j], 0.0), axis=0, keepdims=True) for j in members]
    p = [jnp.sum(jnp.where(chosen, sc[j], 0.0), axis=0, keepdims=True) for j in members]
    one = lambda cond: jnp.where(cond, 1.0, 0.0)
    picks = []
    for want in (0.0, 1.0):
        idx = jnp.zeros_like(v[0])
        wt = jnp.zeros_like(v[0])
        for j in members:
            rank = sum(one(v[k] > v[j]) for k in members if k != j) + sum(one(v[k] == v[j]) for k in range(j))
            hit = rank == want
            idx = idx + jnp.where(hit, float(j), 0.0)
            wt = wt + jnp.where(hit, p[j], 0.0)
        picks.append((idx, wt))
    (j1, p1), (j2, p2) = picks
    base = (EXPERTS_PER_GROUP * gidx).astype(F32)
    tot = p1 + p2
    rows = [base + j1, base + j2, p1 / tot, p2 / tot]
    return jnp.concatenate(rows + [jnp.zeros((ROUTE_ROWS - len(rows), tm), F32)], axis=0)


def _outproj_kernel(m1_ref, m2_ref, w_ref, x_ref, g_ref, gain_ref, sc_ref, sh_ref, rw_ref, rb_ref,
                    xo_ref, h_ref, rt_ref):
    k1 = m1_ref.shape[-1]
    o = (jnp.dot(m1_ref[0], w_ref[:k1, :], preferred_element_type=F32)
         + jnp.dot(m2_ref[0], w_ref[k1:, :], preferred_element_type=F32))
    xn = x_ref[0] + g_ref[0] * o
    xo_ref[0] = xn
    h = _norm_mod(xn, gain_ref[...], sc_ref[0], sh_ref[0])
    h_ref[0] = h.astype(BF16)
    lg = jnp.dot(h, rw_ref[...], preferred_element_type=F32, precision=lax.Precision.HIGHEST)
    rt_ref[0] = _route_block(lg, rb_ref)


def outproj_residual_norm(m1, m2, w, x, g, gain, sc, sh, rw_pad, rb_pad, *, tm):
    b, l, d = x.shape
    k1, k2 = m1.shape[-1], m2.shape[-1]
    row = lambda bi, i: (bi, i, 0)
    mod = lambda bi, i: (bi, 0, 0)
    const = lambda bi, i: (0, 0)
    return pl.pallas_call(
        _outproj_kernel,
        grid=(b, l // tm),
        in_specs=[
            pl.BlockSpec((1, tm, k1), row),
            pl.BlockSpec((1, tm, k2), row),
            pl.BlockSpec((k1 + k2, d), const, pipeline_mode=pl.Buffered(1)),
            pl.BlockSpec((1, tm, d), row),
            pl.BlockSpec((1, 1, d), mod),
            pl.BlockSpec((1, d), const),
            pl.BlockSpec((1, 1, d), mod),
            pl.BlockSpec((1, 1, d), mod),
            pl.BlockSpec((d, LOGIT_PAD), const),
            pl.BlockSpec(rb_pad.shape, const),
        ],
        out_specs=[
            pl.BlockSpec((1, tm, d), row),
            pl.BlockSpec((1, tm, d), row),
            pl.BlockSpec((1, ROUTE_ROWS, tm), lambda bi, i: (bi, 0, i)),
        ],
        out_shape=[
            jax.ShapeDtypeStruct((b, l, d), F32),
            jax.ShapeDtypeStruct((b, l, d), BF16),
            jax.ShapeDtypeStruct((b, ROUTE_ROWS, l), F32),
        ],
        compiler_params=_cparams(("parallel", "parallel")),
        name="outproj_residual_norm",
    )(m1, m2, w, x, g, gain.reshape(1, d), sc, sh, rw_pad, rb_pad)


def _moe_kernel(te_ref, tsrc_ref, tvalid_ref, x_ref, w1_ref, w3_ref, w2_ref, rw_ref, o_ref):
    i = pl.program_id(0)

    @pl.when(tvalid_ref[i] > 0)
    def _():
        x = x_ref[...]
        a = jnp.dot(x, w1_ref[0], preferred_element_type=F32)
        b = jnp.dot(x, w3_ref[0], preferred_element_type=F32)
        mid = (a * jax.nn.sigmoid(a) * b).astype(BF16)
        y = jnp.dot(mid, w2_ref[0], preferred_element_type=F32)
        o_ref[...] = (y * rw_ref[...]).astype(o_ref.dtype)

    @pl.when(tvalid_ref[i] == 0)
    def _():
        o_ref[...] = jnp.zeros_like(o_ref)


def moe_grouped(x_sorted, w1, w3, w2, row_w, tile_expert, tile_src, tile_valid):
    p, d = x_sorted.shape
    ff = w1.shape[-1]
    tm = MOE_TM
    grid_spec = pltpu.PrefetchScalarGridSpec(
        num_scalar_prefetch=3,
        grid=(p // tm,),
        in_specs=[
            pl.BlockSpec((tm, d), lambda i, te, ts, tv: (ts[i], 0)),
            pl.BlockSpec((1, d, ff), lambda i, te, ts, tv: (te[i], 0, 0)),
            pl.BlockSpec((1, d, ff), lambda i, te, ts, tv: (te[i], 0, 0)),
            pl.BlockSpec((1, ff, d), lambda i, te, ts, tv: (te[i], 0, 0)),
            pl.BlockSpec((tm, 1), lambda i, te, ts, tv: (ts[i], 0)),
        ],
        out_specs=pl.BlockSpec((tm, d), lambda i, te, ts, tv: (i, 0)),
    )
    return pl.pallas_call(
        _moe_kernel,
        grid_spec=grid_spec,
        out_shape=jax.ShapeDtypeStruct((p, d), BF16),
        compiler_params=_cparams(("arbitrary",)),
        name="moe_grouped",
    )(tile_expert, tile_src, tile_valid, x_sorted, w1, w3, w2, row_w)


CHUNK = 64
SCAN_T = 256
_NT = (((1,), (1,)), ((), ()))
_TN = (((0,), (0,)), ((), ()))
HIGHEST = lax.Precision.HIGHEST


def _chunk_keep(reverse):
    r = lax.broadcasted_iota(jnp.int32, (CHUNK, CHUNK), 0)
    c = lax.broadcasted_iota(jnp.int32, (CHUNK, CHUNK), 1)
    return (c >= r) if reverse else (c <= r)


def _chunk_step(q, k, v, e_q, e_k, e_s, e_last, keep, st):
    qe = (q * e_q).astype(BF16)
    ke = (k * e_k).astype(BF16)
    kd = (k * e_s).astype(BF16)
    vb = v.astype(BF16)
    att = lax.dot_general(qe, ke, _NT, preferred_element_type=F32)
    att = jnp.where(keep, att, 0.0).astype(BF16)
    o = (jnp.dot(att, vb, preferred_element_type=F32)
         + lax.dot_general(qe, st.astype(BF16), _NT, preferred_element_type=F32))
    st = e_last * st + lax.dot_general(vb, kd, _TN, preferred_element_type=F32)
    return o, st


def _merge_store(o, ob_ref, g_ref, gn_ref, o_ref, sl):
    tot = o + ob_ref[0, sl, :]
    y = tot * lax.rsqrt(jnp.mean(tot * tot, axis=-1, keepdims=True) + EPS) * gn_ref[...]
    g = g_ref[0, sl, :]
    o_ref[0, sl, :] = (y * (g * jax.nn.sigmoid(g))).astype(o_ref.dtype)


def _chunk_order(n, reverse):
    return range(n - 1, -1, -1) if reverse else range(n)


def _gla_kernel(qc, kc, vc, lrc, ql, kl, vl, lrl, wg, bg, *rest, reverse, merge):
    if merge:
        obc, obl, gc, gl, gn, oc_ref, ol_ref, st_ref = rest
    else:
        oc_ref, ol_ref, st_ref = rest
        obc = obl = gc = gl = gn = None
    keep = _chunk_keep(reverse)
    tri = keep.astype(F32)

    def run(q_ref, k_ref, v_ref, lr_ref, o_ref, ob_ref, g_ref):
        pre = jnp.dot(lr_ref[0], wg[...], preferred_element_type=F32, precision=HIGHEST) + bg[...]
        la = (jnp.minimum(pre, 0.0) - jnp.log(1.0 + jnp.exp(-jnp.abs(pre)))) * (1.0 / GLA_GATE_NORMALIZER)
        st = st_ref[...]
        for ci in _chunk_order(q_ref.shape[1] // CHUNK, reverse):
            sl = pl.ds(ci * CHUNK, CHUNK)
            cum = jnp.dot(tri, la[ci * CHUNK:(ci + 1) * CHUNK], preferred_element_type=F32,
                          precision=HIGHEST)
            last = cum[0:1] if reverse else cum[CHUNK - 1:CHUNK]
            o, st = _chunk_step(q_ref[0, sl, :] * (GLA_DK ** -0.5), k_ref[0, sl, :], v_ref[0, sl, :],
                                jnp.exp(cum), jnp.exp(-cum), jnp.exp(last - cum), jnp.exp(last), keep, st)
            if merge:
                _merge_store(o, ob_ref, g_ref, gn, o_ref, sl)
            else:
                o_ref[0, sl, :] = o
        st_ref[...] = st

    @pl.when(pl.program_id(2) == 0)
    def _():
        st_ref[...] = jnp.zeros_like(st_ref)
        run(qc, kc, vc, lrc, oc_ref, obc, gc)

    @pl.when(pl.program_id(2) > 0)
    def _():
        run(ql, kl, vl, lrl, ol_ref, obl, gl)


def _ret_kernel(qc, kc, vc, ql, kl, vl, cos_ref, sin_ref, cum_ref, *rest, reverse, merge):
    if merge:
        obc, obl, gc, gl, gn, oc_ref, ol_ref, st_ref = rest
    else:
        oc_ref, ol_ref, st_ref = rest
        obc = obl = gc = gl = gn = None
    keep = _chunk_keep(reverse)
    cum = cum_ref[0]
    last = cum[0:1] if reverse else cum[CHUNK - 1:CHUNK]
    e_q, e_k, e_s, e_last = jnp.exp(cum), jnp.exp(-cum), jnp.exp(last - cum), jnp.exp(last)

    def run(q_ref, k_ref, v_ref, o_ref, ob_ref, g_ref, latent):
        st = st_ref[...]
        for ci in _chunk_order(q_ref.shape[1] // CHUNK, reverse):
            sl = pl.ds(ci * CHUNK, CHUNK)
            q = q_ref[0, sl, :]
            k = k_ref[0, sl, :]
            if latent:
                cs = cos_ref[sl, :]
                sn = sin_ref[sl, :]
                q = q * cs + pltpu.roll(q, RET_DK // 2, 1) * sn
                k = k * cs + pltpu.roll(k, RET_DK // 2, 1) * sn
            o, st = _chunk_step(q, k * (RET_DK ** -0.5), v_ref[0, sl, :], e_q, e_k, e_s, e_last, keep, st)
            if merge:
                _merge_store(o, ob_ref, g_ref, gn, o_ref, sl)
            else:
                o_ref[0, sl, :] = o
        st_ref[...] = st

    @pl.when(pl.program_id(2) == 0)
    def _():
        st_ref[...] = jnp.zeros_like(st_ref)
        run(qc, kc, vc, oc_ref, obc, gc, False)

    @pl.when(pl.program_id(2) > 0)
    def _():
        run(ql, kl, vl, ol_ref, obl, gl, True)


def _scan_call(body, zc, zl, heads, dk, dv, col_q, col_k, col_v, shared_cols, extra_specs, merge_in,
               *, reverse, name):
    b, lc, _ = zc.shape
    l = zl.shape[1]
    t = SCAN_T
    nblk = l // t
    if reverse:
        blk = lambda i: jnp.where(i == 0, nblk - 1, nblk - i)
    else:
        blk = lambda i: jnp.maximum(i - 1, 0)

    def ctx_spec(w, col, per_head=True):
        return pl.BlockSpec((1, lc, w), lambda bi, h, i: (bi, 0, col // w + (h if per_head else 0)))

    def lat_spec(w, col, per_head=True):
        return pl.BlockSpec((1, t, w), lambda bi, h, i: (bi, blk(i), col // w + (h if per_head else 0)))

    ins = [zc, zc, zc] + [zc] * len(shared_cols) + [zl, zl, zl] + [zl] * len(shared_cols)
    specs = [ctx_spec(dk, col_q), ctx_spec(dk, col_k), ctx_spec(dv, col_v)]
    specs += [ctx_spec(w, col, False) for w, col in shared_cols]
    specs += [lat_spec(dk, col_q), lat_spec(dk, col_k), lat_spec(dv, col_v)]
    specs += [lat_spec(w, col, False) for w, col in shared_cols]
    for a, s in extra_specs:
        ins.append(a)
        specs.append(s)
    if merge_in is not None:
        obc, obl, col_g, gain = merge_in
        ins += [obc, obl, zc, zl, gain.reshape(1, dv)]
        specs += [ctx_spec(dv, 0), lat_spec(dv, 0), ctx_spec(dv, col_g), lat_spec(dv, col_g),
                  pl.BlockSpec((1, dv), lambda bi, h, i: (0, 0))]
        out_dtype = BF16
    else:
        out_dtype = F32
    return pl.pallas_call(
        functools.partial(body, reverse=reverse, merge=merge_in is not None),
        grid=(b, heads, 1 + nblk),
        in_specs=specs,
        out_specs=[ctx_spec(dv, 0), lat_spec(dv, 0)],
        out_shape=[jax.ShapeDtypeStruct((b, lc, heads * dv), out_dtype),
                   jax.ShapeDtypeStruct((b, l, heads * dv), out_dtype)],
        scratch_shapes=[pltpu.VMEM((dv, dk), F32)],
        compiler_params=_cparams(("parallel", "parallel", "arbitrary")),
        name=name,
    )(*ins)


EV_COL = dict(q=0, k=512, v=1024, g=2048, u=3072, lr=6144)
OD_COL = dict(q=0, k=512, v=1024, g=2048, aq=3072, ak=4096, av=4352)


def gla_bidir(zc, zl, gla_w_gate, gla_b_gate, gla_norm):
    hk = GLA_HEADS * GLA_DK
    outs = None
    for d, reverse in ((1, True), (0, False)):
        wg = jnp.zeros((LANE, hk), F32).at[d * GLA_LOWRANK:(d + 1) * GLA_LOWRANK].set(gla_w_gate[d])
        shared_cols = [(LANE, EV_COL["lr"])]
        extra_specs = [(wg, pl.BlockSpec((LANE, GLA_DK), lambda bi, h, i: (0, h))),
                       (gla_b_gate[d].reshape(1, hk), pl.BlockSpec((1, GLA_DK), lambda bi, h, i: (0, h)))]
        merge_in = None if outs is None else (outs[0], outs[1], EV_COL["g"], gla_norm)
        outs = _scan_call(_gla_kernel, zc, zl, GLA_HEADS, GLA_DK, GLA_DV, EV_COL["q"], EV_COL["k"],
                          EV_COL["v"], shared_cols, extra_specs, merge_in, reverse=reverse,
                          name="gla_bwd" if reverse else "gla_fwd_merge")
    return outs


def retention_bidir(zc, zl, ret_norm):
    l = zl.shape[1]
    inv = ROPE_BASE ** (-jnp.linspace(0.0, 1.0, RET_DK // 2, dtype=F32))
    ang = jnp.arange(l, dtype=F32)[:, None] * inv[None, :]
    cos2 = jnp.concatenate([jnp.cos(ang), jnp.cos(ang)], axis=-1)
    sin2 = jnp.concatenate([-jnp.sin(ang), jnp.sin(ang)], axis=-1)
    log_g_f = jnp.log(1.0 - 2.0 ** (-5.0 - jnp.arange(RET_HEADS, dtype=F32)))
    idx = jnp.arange(CHUNK, dtype=F32)
    nblk = l // SCAN_T
    outs = None
    for reverse in (True, False):
        if reverse:
            cum = (CHUNK - idx)[None, :, None] * log_g_f[::-1][:, None, None]
            blk = lambda i: jnp.where(i == 0, nblk - 1, nblk - i)
        else:
            cum = (idx + 1.0)[None, :, None] * log_g_f[:, None, None]
            blk = lambda i: jnp.maximum(i - 1, 0)
        cum = jnp.broadcast_to(cum, (RET_HEADS, CHUNK, RET_DK))
        tab_spec = pl.BlockSpec((SCAN_T, RET_DK), lambda bi, h, i, blk=blk: (blk(i), 0))
        extra_specs = [(cos2, tab_spec), (sin2, tab_spec),
                       (cum, pl.BlockSpec((1, CHUNK, RET_DK), lambda bi, h, i: (h, 0, 0)))]
        merge_in = None if outs is None else (outs[0], outs[1], OD_COL["g"], ret_norm)
        outs = _scan_call(_ret_kernel, zc, zl, RET_HEADS, RET_DK, RET_DV, OD_COL["q"], OD_COL["k"],
                          OD_COL["v"], [], extra_specs, merge_in, reverse=reverse,
                          name="ret_bwd" if reverse else "ret_fwd_merge")
    return outs


ATT_QKV = (ATT_HEADS + 2 * ATT_KV_HEADS) * ATT_HD


def _att_prep_kernel(z_ref, qn_ref, kn_ref, *rest, latent):
    if latent:
        cos_ref, sin_ref, o_ref = rest
        cs, sn = cos_ref[...], sin_ref[...]
        lane = lax.broadcasted_iota(jnp.int32, cs.shape, 1)
        low = (lane & (ATT_HD // 2 - 1)) < (ATT_HD // 4)
    else:
        (o_ref,) = rest
    for h in range(ATT_HEADS + ATT_KV_HEADS):
        x = z_ref[0, :, h * ATT_HD:(h + 1) * ATT_HD]
        gain = qn_ref[...] if h < ATT_HEADS else kn_ref[...]
        y = x * lax.rsqrt(jnp.mean(x * x, axis=-1, keepdims=True) + EPS) * gain
        if latent:
            partner = jnp.where(low, pltpu.roll(y, ATT_HD - ATT_HD // 4, 1), pltpu.roll(y, ATT_HD // 4, 1))
            y = y * cs + partner * sn
        if h < ATT_HEADS:
            y = y * (ATT_HD ** -0.5)
        o_ref[0, :, h * ATT_HD:(h + 1) * ATT_HD] = y.astype(o_ref.dtype)
    v0 = (ATT_HEADS + ATT_KV_HEADS) * ATT_HD
    o_ref[0, :, v0:] = z_ref[0, :, v0:].astype(o_ref.dtype)


def att_prep(z, att_q_norm, att_k_norm, tables, *, tm):
    b, l, _ = z.shape
    latent = tables is not None
    row = lambda bi, i: (bi, i, 0)
    const = lambda bi, i: (0, 0)
    ins = [z, att_q_norm.reshape(1, ATT_HD), att_k_norm.reshape(1, ATT_HD)]
    specs = [pl.BlockSpec((1, tm, ATT_QKV), lambda bi, i: (bi, i, OD_COL["aq"] // ATT_QKV)),
             pl.BlockSpec((1, ATT_HD), const), pl.BlockSpec((1, ATT_HD), const)]
    if latent:
        ins += list(tables)
        specs += [pl.BlockSpec((tm, ATT_HD), lambda bi, i: (i, 0))] * 2
    return pl.pallas_call(
        functools.partial(_att_prep_kernel, latent=latent),
        grid=(b, l // tm),
        in_specs=specs,
        out_specs=pl.BlockSpec((1, tm, ATT_QKV), row),
        out_shape=jax.ShapeDtypeStruct((b, l, ATT_QKV), BF16),
        compiler_params=_cparams(("parallel", "parallel")),
        name="att_prep",
    )(*ins)


def _axial_tables(l):
    quarter = ATT_HD // 4
    inv = ROPE_BASE ** (-jnp.arange(quarter, dtype=F32) / quarter)
    t = jnp.arange(l)
    a_row = (t // GRID_W).astype(F32)[:, None] * inv[None, :]
    a_col = (t % GRID_W).astype(F32)[:, None] * inv[None, :]
    cos = jnp.concatenate([jnp.cos(a_row)] * 2 + [jnp.cos(a_col)] * 2, axis=-1)
    sin = jnp.concatenate([-jnp.sin(a_row), jnp.sin(a_row), -jnp.sin(a_col), jnp.sin(a_col)], axis=-1)
    return cos, sin


def _att_kernel(sink_ref, q_ref, kp, kc, kn, vp, vc, vn, ck, cv, o_ref, *, seq):
    kvh = pl.program_id(1)
    n = pl.program_id(2)
    w = ATT_BLOCK
    q = jnp.concatenate([q_ref[0, :, g * ATT_HD:(g + 1) * ATT_HD] for g in range(ATT_GROUP)], axis=0)
    kw = jnp.concatenate([kp[0], kc[0], kn[0]], axis=0)
    vw = jnp.concatenate([vp[0], vc[0], vn[0]], axis=0)
    s_w = lax.dot_general(q, kw, _NT, preferred_element_type=F32)
    s_c = lax.dot_general(q, ck[0], _NT, preferred_element_type=F32)
    r = lax.broadcasted_iota(jnp.int32, s_w.shape, 0) & (w - 1)
    c = lax.broadcasted_iota(jnp.int32, s_w.shape, 1) - w
    kpos = n * w + c
    valid = (jnp.abs(r - c) <= ATT_WINDOW) & (kpos >= 0) & (kpos < seq)
    s_w = jnp.where(valid, s_w, -jnp.inf)
    sink = jnp.concatenate([jnp.full((w, 1), sink_ref[kvh * ATT_GROUP + g], F32) for g in range(ATT_GROUP)],
                           axis=0)
    m = jnp.maximum(jnp.maximum(jnp.max(s_w, axis=-1, keepdims=True), jnp.max(s_c, axis=-1, keepdims=True)),
                    sink)
    p_w = jnp.exp(s_w - m)
    p_c = jnp.exp(s_c - m)
    denom = (jnp.sum(p_w, axis=-1, keepdims=True) + jnp.sum(p_c, axis=-1, keepdims=True)
             + jnp.exp(sink - m))
    o = (jnp.dot(p_w.astype(BF16), vw, preferred_element_type=F32)
         + jnp.dot(p_c.astype(BF16), cv[0], preferred_element_type=F32)) / denom
    for g in range(ATT_GROUP):
        o_ref[0, :, g * ATT_HD:(g + 1) * ATT_HD] = o[g * w:(g + 1) * w].astype(o_ref.dtype)


def window_attention(qkv_l, qkv_c, att_sink):
    b, l, _ = qkv_l.shape
    lc = qkv_c.shape[1]
    w = ATT_BLOCK
    nb = l // w
    gw = ATT_GROUP * ATT_HD
    kcol = ATT_HEADS
    vcol = ATT_HEADS + ATT_KV_HEADS

    def kv_spec(col, shift):
        return pl.BlockSpec((1, w, ATT_HD),
                            lambda bi, h, n, s: (bi, jnp.clip(n + shift, 0, nb - 1), col + h))

    grid_spec = pltpu.PrefetchScalarGridSpec(
        num_scalar_prefetch=1,
        grid=(b, ATT_KV_HEADS, nb),
        in_specs=[pl.BlockSpec((1, w, gw), lambda bi, h, n, s: (bi, n, h))]
        + [kv_spec(kcol, d) for d in (-1, 0, 1)] + [kv_spec(vcol, d) for d in (-1, 0, 1)]
        + [pl.BlockSpec((1, lc, ATT_HD), lambda bi, h, n, s: (bi, 0, kcol + h)),
           pl.BlockSpec((1, lc, ATT_HD), lambda bi, h, n, s: (bi, 0, vcol + h))],
        out_specs=pl.BlockSpec((1, w, gw), lambda bi, h, n, s: (bi, n, h)),
    )
    return pl.pallas_call(
        functools.partial(_att_kernel, seq=l),
        grid_spec=grid_spec,
        out_shape=jax.ShapeDtypeStruct((b, l, ATT_HEADS * ATT_HD), BF16),
        compiler_params=_cparams(("parallel", "parallel", "arbitrary")),
        name="window_attention",
    )(att_sink.astype(F32), qkv_l, qkv_l, qkv_l, qkv_l, qkv_l, qkv_l, qkv_l, qkv_c, qkv_c)


def moe_apply(h_bf16, route, w1, w3, w2):
    n, d = h_bf16.shape
    tm = MOE_TM
    flat_e = route[0:TOP_K].astype(jnp.int32).T.reshape(-1)
    flat_w = route[TOP_K:2 * TOP_K].T.reshape(-1)
    onehot = (flat_e[:, None] == jnp.arange(N_EXPERTS)[None, :]).astype(jnp.int32)
    rank = jnp.sum((jnp.cumsum(onehot, axis=0) - onehot) * onehot, axis=-1)
    counts = jnp.sum(onehot, axis=0)
    padded = ((counts + tm - 1) // tm) * tm
    ends = jnp.cumsum(padded)
    offs = ends - padded
    pos = offs[flat_e] + rank
    n_tiles = (2 * n + tm - 1) // tm + N_EXPERTS
    p = n_tiles * tm
    used = ends[-1] // tm
    tile_ids = jnp.arange(n_tiles, dtype=jnp.int32)
    tile_valid = (tile_ids < used).astype(jnp.int32)
    tile_src = jnp.minimum(tile_ids, used - 1).astype(jnp.int32)
    tile_expert = jnp.minimum(
        jnp.searchsorted(ends, tile_src * tm, side="right"), N_EXPERTS - 1).astype(jnp.int32)
    src = jnp.zeros((p,), jnp.int32).at[pos].set(jnp.arange(2 * n, dtype=jnp.int32) // 2,
                                                 unique_indices=True)
    row_w = jnp.zeros((p,), F32).at[pos].set(flat_w, unique_indices=True)
    x_sorted = jnp.take(h_bf16, src, axis=0)
    y_sorted = moe_grouped(x_sorted, w1, w3, w2, row_w.reshape(p, 1), tile_expert, tile_src,
                           tile_valid)
    pos2 = pos.reshape(n, 2)
    return (jnp.take(y_sorted, pos2[:, 0], axis=0).astype(F32)
            + jnp.take(y_sorted, pos2[:, 1], axis=0).astype(F32))


def split_cols(z, sizes):
    offs = np.cumsum([0] + list(sizes))
    return [z[..., int(offs[i]):int(offs[i + 1])] for i in range(len(sizes))]


def short_conv(u, w, bias):
    L = u.shape[1]
    up = jnp.pad(u, ((0, 0), (1, 1), (0, 0)))
    return up[:, :L] * w[0] + up[:, 1:L + 1] * w[1] + up[:, 2:] * w[2] + bias


def hyena_filters(L, w1, b1, w2, b2, w3):
    t = jnp.linspace(0.0, 1.0, L, dtype=F32)[:, None]
    w = (2.0 * math.pi / L) * jnp.arange(L, dtype=F32)[:, None]
    bands = jnp.linspace(1e-4, HY_BANDS - 1.0, HY_BANDS, dtype=F32)[None, :]
    feats = jnp.concatenate([t, jnp.cos(bands * w), -jnp.sin(bands * w)], axis=-1)
    hdn = jnp.sin(feats @ w1.astype(F32) + b1.astype(F32))
    hdn = jnp.sin(hdn @ w2.astype(F32) + b2.astype(F32))
    filt = (hdn @ w3.astype(F32)).reshape(L, HY_ORDER, 2, HY_CH)
    deltas = jnp.abs(jnp.linspace(HY_MIN_DECAY, HY_MAX_DECAY, HY_CH, dtype=F32))
    filt = filt * (jnp.exp(-t * deltas[None, :]) + HY_SHIFT)[:, None, None, :]
    kern = jnp.concatenate([filt[:, :, 0], jnp.zeros((1, HY_ORDER, HY_CH), F32), filt[:0:-1, :, 1]], axis=0)
    kern = kern / jnp.sum(jnp.abs(kern), axis=0, keepdims=True)
    return jnp.fft.rfft(kern, axis=0)


def fft_conv(u, kf, bias):
    L = u.shape[1]
    uf = jnp.fft.rfft(u.astype(F32), n=2 * L, axis=1)
    y = jnp.fft.irfft(uf * kf[None], n=2 * L, axis=1)[:, :L]
    return (y + u.astype(F32) * bias.astype(F32)).astype(u.dtype)


def even_mixer_core(zc, zl, gla_w_gate, gla_b_gate, gla_norm, hy_conv_w, hy_conv_b,
                    hy_w1, hy_b1, hy_w2, hy_b2, hy_w3, hy_bias):
    def hyena(u):
        L = u.shape[1]
        u = short_conv(u, hy_conv_w, hy_conv_b)
        z, *gates = split_cols(u, (HY_CH,) * (HY_ORDER + 1))
        kf = hyena_filters(L, hy_w1, hy_b1, hy_w2, hy_b2, hy_w3)
        for n, gate in enumerate(gates):
            z = gate * fft_conv(z, kf[:, n], hy_bias[n])
        return z

    gc, gl = gla_bidir(zc, zl, gla_w_gate, gla_b_gate, gla_norm)
    u0, u1 = EV_COL["u"], EV_COL["u"] + (HY_ORDER + 1) * HY_CH
    return ((gc, hyena(zc[..., u0:u1]).astype(BF16)), (gl, hyena(zl[..., u0:u1]).astype(BF16)))


def odd_mixer_core(zc, zl, ret_norm, att_q_norm, att_k_norm, att_sink):
    _, rl = retention_bidir(zc, zl, ret_norm)
    qkv_l = att_prep(zl, att_q_norm, att_k_norm, _axial_tables(zl.shape[1]), tm=512)
    qkv_c = att_prep(zc, att_q_norm, att_k_norm, None, tm=zc.shape[1])
    return rl, window_attention(qkv_l, qkv_c, att_sink)


def kernel(x, c, ctx, c_ctx, ada_w, ada_b, norm_mix, norm_ffn, ev_w_in, ev_w_out, gla_w_gate, gla_b_gate, gla_norm, hy_conv_w, hy_conv_b, hy_w1, hy_b1, hy_w2, hy_b2, hy_w3, hy_bias, od_w_in, od_w_out, ret_norm, att_q_norm, att_k_norm, att_sink, router_w, router_b, moe_w1, moe_w3, moe_w2):
    assert DEPTH == 2
    bsz, seq, d = x.shape
    n_ctx = ctx.shape[1]
    c_pad = jnp.zeros((8, d), F32).at[:bsz].set(c).at[bsz].set(c_ctx)
    mods = ada_modulation(c_pad, ada_w, ada_b)
    slot = np.array([SUBLANE * (e % EXPERTS_PER_GROUP) + e // EXPERTS_PER_GROUP for e in range(N_EXPERTS)])
    rw_pad = jnp.zeros((d, LOGIT_PAD), F32).at[:, slot].set(router_w)
    rb_pad = jnp.zeros((EXPERTS_PER_GROUP * SUBLANE, LANE), F32).at[slot].set(
        jnp.broadcast_to(router_b.astype(F32)[:, None], (N_EXPERTS, LANE)))

    xl, xc = x, ctx
    for layer in range(DEPTH):
        last = layer == DEPTH - 1
        ml = mods[layer, :bsz].reshape(bsz, 1, 6, d)
        mc = jnp.broadcast_to(mods[layer, bsz].reshape(1, 1, 6, d), (bsz, 1, 6, d))
        sh1, sc1, g1, sh2, sc2, g2 = [ml[:, :, i] for i in range(6)]
        csh1, csc1, cg1, csh2, csc2, cg2 = [mc[:, :, i] for i in range(6)]
        i = layer // 2
        if layer % 2 == 0:
            w_in, w_out = ev_w_in[i].astype(BF16), ev_w_out[i].astype(BF16)
            lr0 = sum(EVEN_SPLITS[:4])
            lr1 = lr0 + EVEN_SPLITS[4]
            w_in = jnp.concatenate([w_in[:, :lr0], w_in[:, lr1:], w_in[:, lr0:lr1],
                                    jnp.zeros((d, EVEN_COLS_PAD - w_in.shape[1]), BF16)], axis=1)
            tn = EVEN_COLS_PAD // 7
        else:
            w_in, w_out = od_w_in[i].astype(BF16), od_w_out[i].astype(BF16)
            tn = w_in.shape[1] // 6
        zl = norm_mod_matmul(xl, norm_mix[layer], sc1, sh1, w_in, tm=1024, tn=tn, out_dtype=F32)
        zc = norm_mod_matmul(xc, norm_mix[layer], csc1, csh1, w_in, tm=n_ctx, tn=tn, out_dtype=F32)
        if layer % 2 == 0:
            mixc, mixl = even_mixer_core(zc, zl, gla_w_gate[i], gla_b_gate[i], gla_norm[i], hy_conv_w[i],
                                         hy_conv_b[i], hy_w1[i], hy_b1[i], hy_w2[i], hy_b2[i], hy_w3[i],
                                         hy_bias[i])
        else:
            mixl = odd_mixer_core(zc, zl, ret_norm[i], att_q_norm[i], att_k_norm[i], att_sink[i])
        xl, hl, rtl = outproj_residual_norm(*mixl, w_out, xl, g1, norm_ffn[layer], sc2, sh2, rw_pad, rb_pad,
                                            tm=512)
        rtl = rtl.transpose(1, 0, 2).reshape(ROUTE_ROWS, -1)
        w1 = moe_w1[layer].astype(BF16)
        w3 = moe_w3[layer].astype(BF16)
        w2 = moe_w2[layer].astype(BF16)
        if not last:
            xc, hc, rtc = outproj_residual_norm(*mixc, w_out, xc, cg1, norm_ffn[layer], csc2, csh2, rw_pad,
                                                rb_pad, tm=n_ctx)
            rtc = rtc.transpose(1, 0, 2).reshape(ROUTE_ROWS, -1)
            h = jnp.concatenate([hc.reshape(-1, d), hl.reshape(-1, d)], axis=0)
            y = moe_apply(h, jnp.concatenate([rtc, rtl], axis=1), w1, w3, w2)
            yc = y[:bsz * n_ctx].reshape(bsz, n_ctx, d)
            yl = y[bsz * n_ctx:].reshape(bsz, seq, d)
            xc = xc + cg2 * yc
            xl = xl + g2 * yl
        else:
            y = moe_apply(hl.reshape(-1, d), rtl, w1, w3, w2)
            xl = xl + g2 * y.reshape(bsz, seq, d)
    return xl
```

```python
import functools
import math

import jax
import jax.numpy as jnp
import numpy as np
from jax import lax
from jax.experimental import pallas as pl
from jax.experimental.pallas import tpu as pltpu

D_MODEL = 2048
BATCH = 2
SEQ = 8192
DEPTH = 2
GRID_W = 64
CTX_LEN = 256
MIX_WIDTH = D_MODEL
GROUP_WIDTH = MIX_WIDTH // 2
EPS = 1e-6
F32 = jnp.float32
BF16 = jnp.bfloat16

GLA_HEADS = 4
GLA_DK = GROUP_WIDTH // (2 * GLA_HEADS)
GLA_DV = GROUP_WIDTH // GLA_HEADS
GLA_LOWRANK = 16
GLA_GATE_NORMALIZER = 16.0
GLA_CHUNK = 64

HY_CH = GROUP_WIDTH
HY_ORDER = 2
HY_BANDS = 16
HY_EMB = 1 + 2 * HY_BANDS
HY_FFN = 64
HY_SHIFT = 0.05
HY_MIN_DECAY = math.log(1e-2) / 1.5
HY_MAX_DECAY = math.log(1e-2) / 0.3

RET_HEADS = 4
RET_DK = GROUP_WIDTH // (2 * RET_HEADS)
RET_DV = GROUP_WIDTH // RET_HEADS
RET_CHUNK = 64

ATT_HD = 128
ATT_HEADS = GROUP_WIDTH // ATT_HD
ATT_KV_HEADS = 2
ATT_GROUP = ATT_HEADS // ATT_KV_HEADS
ATT_WINDOW = 128
ATT_BLOCK = 128
ROPE_BASE = 10000.0

N_EXPERTS = 16
N_GROUPS = 4
EXPERTS_PER_GROUP = N_EXPERTS // N_GROUPS
TOP_K = 2
D_FF_EXPERT = D_MODEL // 2

EVEN_SPLITS = (GLA_HEADS * GLA_DK, GLA_HEADS * GLA_DK, GLA_HEADS * GLA_DV, GLA_HEADS * GLA_DV,
               2 * GLA_LOWRANK, (HY_ORDER + 1) * HY_CH)
ODD_SPLITS = (RET_HEADS * RET_DK, RET_HEADS * RET_DK, RET_HEADS * RET_DV, RET_HEADS * RET_DV,
              ATT_HEADS * ATT_HD, ATT_KV_HEADS * ATT_HD, ATT_KV_HEADS * ATT_HD)

LANE = 128
EVEN_COLS_PAD = -(-sum(EVEN_SPLITS) // (7 * LANE)) * (7 * LANE)
VMEM_LIMIT = 56 * 1024 * 1024
MOE_TM = 512
LOGIT_PAD = LANE


def _cparams(sem):
    return pltpu.CompilerParams(dimension_semantics=sem, vmem_limit_bytes=VMEM_LIMIT)


def _ada_kernel(c_ref, w_ref, b_ref, o_ref):
    c = c_ref[...]
    s = (c * jax.nn.sigmoid(c)).astype(BF16)
    o_ref[0] = jnp.dot(s, w_ref[0].astype(BF16), preferred_element_type=F32) + b_ref[0]


def ada_modulation(c_pad, ada_w, ada_b):
    depth, d, n = ada_w.shape
    tn = 1024
    return pl.pallas_call(
        _ada_kernel,
        grid=(depth, n // tn),
        in_specs=[
            pl.BlockSpec((8, d), lambda l, j: (0, 0)),
            pl.BlockSpec((1, d, tn), lambda l, j: (l, 0, j)),
            pl.BlockSpec((1, 1, tn), lambda l, j: (l, 0, j)),
        ],
        out_specs=pl.BlockSpec((1, 8, tn), lambda l, j: (l, 0, j)),
        out_shape=jax.ShapeDtypeStruct((depth, 8, n), F32),
        compiler_params=_cparams(("parallel", "arbitrary")),
        name="ada_modulation",
    )(c_pad, ada_w, ada_b.reshape(depth, 1, n))


def _norm_mod(x, gain, sc, sh):
    ms = jnp.mean(x * x, axis=-1, keepdims=True)
    return (x * lax.rsqrt(ms + EPS) * gain) * (1.0 + sc) + sh


def _inproj_kernel(x_ref, gain_ref, sc_ref, sh_ref, w_ref, o_ref, h_ref):
    @pl.when(pl.program_id(2) == 0)
    def _():
        h_ref[...] = _norm_mod(x_ref[0], gain_ref[...], sc_ref[0], sh_ref[0]).astype(BF16)

    o_ref[0] = jnp.dot(h_ref[...], w_ref[...], preferred_element_type=F32).astype(o_ref.dtype)


def norm_mod_matmul(x, gain, sc, sh, w, *, tm, tn, out_dtype):
    b, l, d = x.shape
    n = w.shape[1]
    return pl.pallas_call(
        _inproj_kernel,
        grid=(b, l // tm, n // tn),
        in_specs=[
            pl.BlockSpec((1, tm, d), lambda bi, i, j: (bi, i, 0)),
            pl.BlockSpec((1, d), lambda bi, i, j: (0, 0)),
            pl.BlockSpec((1, 1, d), lambda bi, i, j: (bi, 0, 0)),
            pl.BlockSpec((1, 1, d), lambda bi, i, j: (bi, 0, 0)),
            pl.BlockSpec((d, tn), lambda bi, i, j: (0, j)),
        ],
        out_specs=pl.BlockSpec((1, tm, tn), lambda bi, i, j: (bi, i, j)),
        out_shape=jax.ShapeDtypeStruct((b, l, n), out_dtype),
        scratch_shapes=[pltpu.VMEM((tm, d), BF16)],
        compiler_params=_cparams(("parallel", "parallel", "arbitrary")),
        name="norm_mod_matmul",
    )(x, gain.reshape(1, d), sc, sh, w)


SUBLANE = 8
ROUTE_ROWS = SUBLANE


def _route_block(lg, rb_ref):
    tm = lg.shape[0]
    t = lg.T
    sub = lax.broadcasted_iota(jnp.int32, (SUBLANE, tm), 0)
    valid = sub < N_GROUPS
    members = range(EXPERTS_PER_GROUP)
    s = [jnp.where(valid, t[SUBLANE * j:SUBLANE * (j + 1), :], -jnp.inf) for j in members]
    mx = jnp.max(jnp.maximum(jnp.maximum(s[0], s[1]), jnp.maximum(s[2], s[3])), axis=0, keepdims=True)
    e = [jnp.exp(sj - mx) for sj in s]
    den = jnp.sum((e[0] + e[1]) + (e[2] + e[3]), axis=0, keepdims=True)
    sc = [ej / den for ej in e]
    sel = [sc[j] + jnp.concatenate([rb_ref[SUBLANE * j:SUBLANE * (j + 1), :]] * (tm // LANE), axis=1)
           for j in members]
    hi01, lo01 = jnp.maximum(sel[0], sel[1]), jnp.minimum(sel[0], sel[1])
    hi23, lo23 = jnp.maximum(sel[2], sel[3]), jnp.minimum(sel[2], sel[3])
    top1 = jnp.maximum(hi01, hi23)
    top2 = jnp.maximum(jnp.minimum(hi01, hi23), jnp.maximum(lo01, lo23))
    gscore = jnp.where(valid, top1 + top2, -jnp.inf)
    gmax = jnp.max(gscore, axis=0, keepdims=True)
    gidx = jnp.min(jnp.where(gscore == gmax, sub, SUBLANE), axis=0, keepdims=True)
    chosen = sub == gidx
    v = [jnp.sum(jnp.where(chosen, sel[j], 0.0), axis=0, keepdims=True) for j in members]
    p = [jnp.sum(jnp.where(chosen, sc[j], 0.0), axis=0, keepdims=True) for j in members]
    one = lambda cond: jnp.where(cond, 1.0, 0.0)
    picks = []
    for want in (0.0, 1.0):
        idx = jnp.zeros_like(v[0])
        wt = jnp.zeros_like(v[0])
        for j in members:
            rank = sum(one(v[k] > v[j]) for k in members if k != j) + sum(one(v[k] == v[j]) for k in range(j))
            hit = rank == want
            idx = idx + jnp.where(hit, float(j), 0.0)
            wt = wt + jnp.where(hit, p[j], 0.0)
        picks.append((idx, wt))
    (j1, p1), (j2, p2) = picks
    base = (EXPERTS_PER_GROUP * gidx).astype(F32)
    tot = p1 + p2
    rows = [base + j1, base + j2, p1 / tot, p2 / tot]
    return jnp.concatenate(rows + [jnp.zeros((ROUTE_ROWS - len(rows), tm), F32)], axis=0)


def _outproj_kernel(m1_ref, m2_ref, w_ref, x_ref, g_ref, gain_ref, sc_ref, sh_ref, rw_ref, rb_ref,
                    xo_ref, h_ref, rt_ref):
    k1 = m1_ref.shape[-1]
    o = (jnp.dot(m1_ref[0].astype(BF16), w_ref[:k1, :], preferred_element_type=F32)
         + jnp.dot(m2_ref[0].astype(BF16), w_ref[k1:, :], preferred_element_type=F32))
    xn = x_ref[0] + g_ref[0] * o
    xo_ref[0] = xn
    h = _norm_mod(xn, gain_ref[...], sc_ref[0], sh_ref[0])
    h_ref[0] = h.astype(BF16)
    lg = jnp.dot(h, rw_ref[...], preferred_element_type=F32, precision=lax.Precision.HIGHEST)
    rt_ref[0] = _route_block(lg, rb_ref)


def outproj_residual_norm(m1, m2, w, x, g, gain, sc, sh, rw_pad, rb_pad, *, tm):
    b, l, d = x.shape
    k1, k2 = m1.shape[-1], m2.shape[-1]
    row = lambda bi, i: (bi, i, 0)
    mod = lambda bi, i: (bi, 0, 0)
    const = lambda bi, i: (0, 0)
    return pl.pallas_call(
        _outproj_kernel,
        grid=(b, l // tm),
        in_specs=[
            pl.BlockSpec((1, tm, k1), row),
            pl.BlockSpec((1, tm, k2), row),
            pl.BlockSpec((k1 + k2, d), const, pipeline_mode=pl.Buffered(1)),
            pl.BlockSpec((1, tm, d), row),
            pl.BlockSpec((1, 1, d), mod),
            pl.BlockSpec((1, d), const),
            pl.BlockSpec((1, 1, d), mod),
            pl.BlockSpec((1, 1, d), mod),
            pl.BlockSpec((d, LOGIT_PAD), const),
            pl.BlockSpec(rb_pad.shape, const),
        ],
        out_specs=[
            pl.BlockSpec((1, tm, d), row),
            pl.BlockSpec((1, tm, d), row),
            pl.BlockSpec((1, ROUTE_ROWS, tm), lambda bi, i: (bi, 0, i)),
        ],
        out_shape=[
            jax.ShapeDtypeStruct((b, l, d), F32),
            jax.ShapeDtypeStruct((b, l, d), BF16),
            jax.ShapeDtypeStruct((b, ROUTE_ROWS, l), F32),
        ],
        compiler_params=_cparams(("parallel", "parallel")),
        name="outproj_residual_norm",
    )(m1, m2, w, x, g, gain.reshape(1, d), sc, sh, rw_pad, rb_pad)


def _moe_kernel(te_ref, tsrc_ref, tvalid_ref, x_ref, w1_ref, w3_ref, w2_ref, rw_ref, o_ref):
    i = pl.program_id(0)

    @pl.when(tvalid_ref[i] > 0)
    def _():
        x = x_ref[...]
        a = jnp.dot(x, w1_ref[0], preferred_element_type=F32)
        b = jnp.dot(x, w3_ref[0], preferred_element_type=F32)
        mid = (a * jax.nn.sigmoid(a) * b).astype(BF16)
        y = jnp.dot(mid, w2_ref[0], preferred_element_type=F32)
        o_ref[...] = (y * rw_ref[...]).astype(o_ref.dtype)

    @pl.when(tvalid_ref[i] == 0)
    def _():
        o_ref[...] = jnp.zeros_like(o_ref)


def moe_grouped(x_sorted, w1, w3, w2, row_w, tile_expert, tile_src, tile_valid):
    p, d = x_sorted.shape
    ff = w1.shape[-1]
    tm = MOE_TM
    grid_spec = pltpu.PrefetchScalarGridSpec(
        num_scalar_prefetch=3,
        grid=(p // tm,),
        in_specs=[
            pl.BlockSpec((tm, d), lambda i, te, ts, tv: (ts[i], 0)),
            pl.BlockSpec((1, d, ff), lambda i, te, ts, tv: (te[i], 0, 0)),
            pl.BlockSpec((1, d, ff), lambda i, te, ts, tv: (te[i], 0, 0)),
            pl.BlockSpec((1, ff, d), lambda i, te, ts, tv: (te[i], 0, 0)),
            pl.BlockSpec((tm, 1), lambda i, te, ts, tv: (ts[i], 0)),
        ],
        out_specs=pl.BlockSpec((tm, d), lambda i, te, ts, tv: (i, 0)),
    )
    return pl.pallas_call(
        _moe_kernel,
        grid_spec=grid_spec,
        out_shape=jax.ShapeDtypeStruct((p, d), BF16),
        compiler_params=_cparams(("arbitrary",)),
        name="moe_grouped",
    )(tile_expert, tile_src, tile_valid, x_sorted, w1, w3, w2, row_w)


CHUNK = 64
SCAN_T = 256
_NT = (((1,), (1,)), ((), ()))
_TN = (((0,), (0,)), ((), ()))
HIGHEST = lax.Precision.HIGHEST


def _chunk_keep(reverse):
    r = lax.broadcasted_iota(jnp.int32, (CHUNK, CHUNK), 0)
    c = lax.broadcasted_iota(jnp.int32, (CHUNK, CHUNK), 1)
    return (c >= r) if reverse else (c <= r)


def _chunk_step(q, k, v, e_q, e_k, e_s, e_last, keep, st):
    qe = (q * e_q).astype(BF16)
    ke = (k * e_k).astype(BF16)
    kd = (k * e_s).astype(BF16)
    vb = v.astype(BF16)
    att = lax.dot_general(qe, ke, _NT, preferred_element_type=F32)
    att = jnp.where(keep, att, 0.0).astype(BF16)
    o = (jnp.dot(att, vb, preferred_element_type=F32)
         + lax.dot_general(qe, st.astype(BF16), _NT, preferred_element_type=F32))
    st = e_last * st + lax.dot_general(vb, kd, _TN, preferred_element_type=F32)
    return o, st


def _merge_store(o, ob_ref, g_ref, gn_ref, o_ref, sl):
    tot = o + ob_ref[0, sl, :]
    y = tot * lax.rsqrt(jnp.mean(tot * tot, axis=-1, keepdims=True) + EPS) * gn_ref[...]
    g = g_ref[0, sl, :]
    o_ref[0, sl, :] = (y * (g * jax.nn.sigmoid(g))).astype(o_ref.dtype)


def _chunk_order(n, reverse):
    return range(n - 1, -1, -1) if reverse else range(n)


def _gla_kernel(qc, kc, vc, lrc, ql, kl, vl, lrl, wg, bg, *rest, reverse, merge):
    if merge:
        obc, obl, gc, gl, gn, oc_ref, ol_ref, st_ref = rest
    else:
        oc_ref, ol_ref, st_ref = rest
        obc = obl = gc = gl = gn = None
    keep = _chunk_keep(reverse)
    tri = keep.astype(F32)

    def run(q_ref, k_ref, v_ref, lr_ref, o_ref, ob_ref, g_ref):
        pre = jnp.dot(lr_ref[0], wg[...], preferred_element_type=F32, precision=HIGHEST) + bg[...]
        la = (jnp.minimum(pre, 0.0) - jnp.log(1.0 + jnp.exp(-jnp.abs(pre)))) * (1.0 / GLA_GATE_NORMALIZER)
        st = st_ref[...]
        for ci in _chunk_order(q_ref.shape[1] // CHUNK, reverse):
            sl = pl.ds(ci * CHUNK, CHUNK)
            cum = jnp.dot(tri, la[ci * CHUNK:(ci + 1) * CHUNK], preferred_element_type=F32,
                          precision=HIGHEST)
            last = cum[0:1] if reverse else cum[CHUNK - 1:CHUNK]
            o, st = _chunk_step(q_ref[0, sl, :] * (GLA_DK ** -0.5), k_ref[0, sl, :], v_ref[0, sl, :],
                                jnp.exp(cum), jnp.exp(-cum), jnp.exp(last - cum), jnp.exp(last), keep, st)
            if merge:
                _merge_store(o, ob_ref, g_ref, gn, o_ref, sl)
            else:
                o_ref[0, sl, :] = o
        st_ref[...] = st

    @pl.when(pl.program_id(2) == 0)
    def _():
        st_ref[...] = jnp.zeros_like(st_ref)
        run(qc, kc, vc, lrc, oc_ref, obc, gc)

    @pl.when(pl.program_id(2) > 0)
    def _():
        run(ql, kl, vl, lrl, ol_ref, obl, gl)


def _ret_kernel(qc, kc, vc, ql, kl, vl, cos_ref, sin_ref, cum_ref, *rest, reverse, merge):
    if merge:
        obc, obl, gc, gl, gn, oc_ref, ol_ref, st_ref = rest
    else:
        oc_ref, ol_ref, st_ref = rest
        obc = obl = gc = gl = gn = None
    keep = _chunk_keep(reverse)
    cum = cum_ref[0]
    last = cum[0:1] if reverse else cum[CHUNK - 1:CHUNK]
    e_q, e_k, e_s, e_last = jnp.exp(cum), jnp.exp(-cum), jnp.exp(last - cum), jnp.exp(last)

    def run(q_ref, k_ref, v_ref, o_ref, ob_ref, g_ref, latent):
        st = st_ref[...]
        for ci in _chunk_order(q_ref.shape[1] // CHUNK, reverse):
            sl = pl.ds(ci * CHUNK, CHUNK)
            q = q_ref[0, sl, :]
            k = k_ref[0, sl, :]
            if latent:
                cs = cos_ref[sl, :]
                sn = sin_ref[sl, :]
                q = q * cs + pltpu.roll(q, RET_DK // 2, 1) * sn
                k = k * cs + pltpu.roll(k, RET_DK // 2, 1) * sn
            o, st = _chunk_step(q, k * (RET_DK ** -0.5), v_ref[0, sl, :], e_q, e_k, e_s, e_last, keep, st)
            if merge:
                _merge_store(o, ob_ref, g_ref, gn, o_ref, sl)
            else:
                o_ref[0, sl, :] = o
        st_ref[...] = st

    @pl.when(pl.program_id(2) == 0)
    def _():
        st_ref[...] = jnp.zeros_like(st_ref)
        run(qc, kc, vc, oc_ref, obc, gc, False)

    @pl.when(pl.program_id(2) > 0)
    def _():
        run(ql, kl, vl, ol_ref, obl, gl, True)


def _scan_call(body, zc, zl, heads, dk, dv, col_q, col_k, col_v, shared_cols, extra_specs, merge_in,
               *, reverse, name):
    b, lc, _ = zc.shape
    l = zl.shape[1]
    t = SCAN_T
    nblk = l // t
    if reverse:
        blk = lambda i: jnp.where(i == 0, nblk - 1, nblk - i)
    else:
        blk = lambda i: jnp.maximum(i - 1, 0)

    def ctx_spec(w, col, per_head=True):
        return pl.BlockSpec((1, lc, w), lambda bi, h, i: (bi, 0, col // w + (h if per_head else 0)))

    def lat_spec(w, col, per_head=True):
        return pl.BlockSpec((1, t, w), lambda bi, h, i: (bi, blk(i), col // w + (h if per_head else 0)))

    ins = [zc, zc, zc] + [zc] * len(shared_cols) + [zl, zl, zl] + [zl] * len(shared_cols)
    specs = [ctx_spec(dk, col_q), ctx_spec(dk, col_k), ctx_spec(dv, col_v)]
    specs += [ctx_spec(w, col, False) for w, col in shared_cols]
    specs += [lat_spec(dk, col_q), lat_spec(dk, col_k), lat_spec(dv, col_v)]
    specs += [lat_spec(w, col, False) for w, col in shared_cols]
    for a, s in extra_specs:
        ins.append(a)
        specs.append(s)
    if merge_in is not None:
        obc, obl, col_g, gain = merge_in
        ins += [obc, obl, zc, zl, gain.reshape(1, dv)]
        specs += [ctx_spec(dv, 0), lat_spec(dv, 0), ctx_spec(dv, col_g), lat_spec(dv, col_g),
                  pl.BlockSpec((1, dv), lambda bi, h, i: (0, 0))]
        out_dtype = BF16
    else:
        out_dtype = F32
    return pl.pallas_call(
        functools.partial(body, reverse=reverse, merge=merge_in is not None),
        grid=(b, heads, 1 + nblk),
        in_specs=specs,
        out_specs=[ctx_spec(dv, 0), lat_spec(dv, 0)],
        out_shape=[jax.ShapeDtypeStruct((b, lc, heads * dv), out_dtype),
                   jax.ShapeDtypeStruct((b, l, heads * dv), out_dtype)],
        scratch_shapes=[pltpu.VMEM((dv, dk), F32)],
        compiler_params=_cparams(("parallel", "parallel", "arbitrary")),
        name=name,
    )(*ins)


EV_COL = dict(q=0, k=512, v=1024, g=2048, u=3072, lr=6144)
OD_COL = dict(q=0, k=512, v=1024, g=2048, aq=3072, ak=4096, av=4352)


def gla_bidir(zc, zl, gla_w_gate, gla_b_gate, gla_norm):
    hk = GLA_HEADS * GLA_DK
    outs = None
    for d, reverse in ((1, True), (0, False)):
        wg = jnp.zeros((LANE, hk), F32).at[d * GLA_LOWRANK:(d + 1) * GLA_LOWRANK].set(gla_w_gate[d])
        shared_cols = [(LANE, EV_COL["lr"])]
        extra_specs = [(wg, pl.BlockSpec((LANE, GLA_DK), lambda bi, h, i: (0, h))),
                       (gla_b_gate[d].reshape(1, hk), pl.BlockSpec((1, GLA_DK), lambda bi, h, i: (0, h)))]
        merge_in = None if outs is None else (outs[0], outs[1], EV_COL["g"], gla_norm)
        outs = _scan_call(_gla_kernel, zc, zl, GLA_HEADS, GLA_DK, GLA_DV, EV_COL["q"], EV_COL["k"],
                          EV_COL["v"], shared_cols, extra_specs, merge_in, reverse=reverse,
                          name="gla_bwd" if reverse else "gla_fwd_merge")
    return outs


def retention_bidir(zc, zl, ret_norm):
    l = zl.shape[1]
    inv = ROPE_BASE ** (-jnp.linspace(0.0, 1.0, RET_DK // 2, dtype=F32))
    ang = jnp.arange(l, dtype=F32)[:, None] * inv[None, :]
    cos2 = jnp.concatenate([jnp.cos(ang), jnp.cos(ang)], axis=-1)
    sin2 = jnp.concatenate([-jnp.sin(ang), jnp.sin(ang)], axis=-1)
    log_g_f = jnp.log(1.0 - 2.0 ** (-5.0 - jnp.arange(RET_HEADS, dtype=F32)))
    idx = jnp.arange(CHUNK, dtype=F32)
    nblk = l // SCAN_T
    outs = None
    for reverse in (True, False):
        if reverse:
            cum = (CHUNK - idx)[None, :, None] * log_g_f[::-1][:, None, None]
            blk = lambda i: jnp.where(i == 0, nblk - 1, nblk - i)
        else:
            cum = (idx + 1.0)[None, :, None] * log_g_f[:, None, None]
            blk = lambda i: jnp.maximum(i - 1, 0)
        cum = jnp.broadcast_to(cum, (RET_HEADS, CHUNK, RET_DK))
        tab_spec = pl.BlockSpec((SCAN_T, RET_DK), lambda bi, h, i, blk=blk: (blk(i), 0))
        extra_specs = [(cos2, tab_spec), (sin2, tab_spec),
                       (cum, pl.BlockSpec((1, CHUNK, RET_DK), lambda bi, h, i: (h, 0, 0)))]
        merge_in = None if outs is None else (outs[0], outs[1], OD_COL["g"], ret_norm)
        outs = _scan_call(_ret_kernel, zc, zl, RET_HEADS, RET_DK, RET_DV, OD_COL["q"], OD_COL["k"],
                          OD_COL["v"], [], extra_specs, merge_in, reverse=reverse,
                          name="ret_bwd" if reverse else "ret_fwd_merge")
    return outs


ATT_QKV = (ATT_HEADS + 2 * ATT_KV_HEADS) * ATT_HD


def _att_prep_kernel(z_ref, qn_ref, kn_ref, *rest, latent):
    if latent:
        cos_ref, sin_ref, o_ref = rest
        cs, sn = cos_ref[...], sin_ref[...]
        lane = lax.broadcasted_iota(jnp.int32, cs.shape, 1)
        low = (lane & (ATT_HD // 2 - 1)) < (ATT_HD // 4)
    else:
        (o_ref,) = rest
    for h in range(ATT_HEADS + ATT_KV_HEADS):
        x = z_ref[0, :, h * ATT_HD:(h + 1) * ATT_HD]
        gain = qn_ref[...] if h < ATT_HEADS else kn_ref[...]
        y = x * lax.rsqrt(jnp.mean(x * x, axis=-1, keepdims=True) + EPS) * gain
        if latent:
            partner = jnp.where(low, pltpu.roll(y, ATT_HD - ATT_HD // 4, 1), pltpu.roll(y, ATT_HD // 4, 1))
            y = y * cs + partner * sn
        if h < ATT_HEADS:
            y = y * (ATT_HD ** -0.5)
        o_ref[0, :, h * ATT_HD:(h + 1) * ATT_HD] = y.astype(o_ref.dtype)
    v0 = (ATT_HEADS + ATT_KV_HEADS) * ATT_HD
    o_ref[0, :, v0:] = z_ref[0, :, v0:].astype(o_ref.dtype)


def att_prep(z, att_q_norm, att_k_norm, tables, *, tm):
    b, l, _ = z.shape
    latent = tables is not None
    row = lambda bi, i: (bi, i, 0)
    const = lambda bi, i: (0, 0)
    ins = [z, att_q_norm.reshape(1, ATT_HD), att_k_norm.reshape(1, ATT_HD)]
    specs = [pl.BlockSpec((1, tm, ATT_QKV), lambda bi, i: (bi, i, OD_COL["aq"] // ATT_QKV)),
             pl.BlockSpec((1, ATT_HD), const), pl.BlockSpec((1, ATT_HD), const)]
    if latent:
        ins += list(tables)
        specs += [pl.BlockSpec((tm, ATT_HD), lambda bi, i: (i, 0))] * 2
    return pl.pallas_call(
        functools.partial(_att_prep_kernel, latent=latent),
        grid=(b, l // tm),
        in_specs=specs,
        out_specs=pl.BlockSpec((1, tm, ATT_QKV), row),
        out_shape=jax.ShapeDtypeStruct((b, l, ATT_QKV), BF16),
        compiler_params=_cparams(("parallel", "parallel")),
        name="att_prep",
    )(*ins)


def _axial_tables(l):
    quarter = ATT_HD // 4
    inv = ROPE_BASE ** (-jnp.arange(quarter, dtype=F32) / quarter)
    t = jnp.arange(l)
    a_row = (t // GRID_W).astype(F32)[:, None] * inv[None, :]
    a_col = (t % GRID_W).astype(F32)[:, None] * inv[None, :]
    cos = jnp.concatenate([jnp.cos(a_row)] * 2 + [jnp.cos(a_col)] * 2, axis=-1)
    sin = jnp.concatenate([-jnp.sin(a_row), jnp.sin(a_row), -jnp.sin(a_col), jnp.sin(a_col)], axis=-1)
    return cos, sin


def _att_kernel(sink_ref, q_ref, kp, kc, kn, vp, vc, vn, ck, cv, o_ref, *, seq):
    kvh = pl.program_id(1)
    n = pl.program_id(2)
    w = ATT_BLOCK
    q = jnp.concatenate([q_ref[0, :, g * ATT_HD:(g + 1) * ATT_HD] for g in range(ATT_GROUP)], axis=0)
    kw = jnp.concatenate([kp[0], kc[0], kn[0]], axis=0)
    vw = jnp.concatenate([vp[0], vc[0], vn[0]], axis=0)
    s_w = lax.dot_general(q, kw, _NT, preferred_element_type=F32)
    s_c = lax.dot_general(q, ck[0], _NT, preferred_element_type=F32)
    r = lax.broadcasted_iota(jnp.int32, s_w.shape, 0) & (w - 1)
    c = lax.broadcasted_iota(jnp.int32, s_w.shape, 1) - w
    kpos = n * w + c
    valid = (jnp.abs(r - c) <= ATT_WINDOW) & (kpos >= 0) & (kpos < seq)
    s_w = jnp.where(valid, s_w, -jnp.inf)
    sink = jnp.concatenate([jnp.full((w, 1), sink_ref[kvh * ATT_GROUP + g], F32) for g in range(ATT_GROUP)],
                           axis=0)
    m = jnp.maximum(jnp.maximum(jnp.max(s_w, axis=-1, keepdims=True), jnp.max(s_c, axis=-1, keepdims=True)),
                    sink)
    p_w = jnp.exp(s_w - m)
    p_c = jnp.exp(s_c - m)
    denom = (jnp.sum(p_w, axis=-1, keepdims=True) + jnp.sum(p_c, axis=-1, keepdims=True)
             + jnp.exp(sink - m))
    o = (jnp.dot(p_w.astype(BF16), vw, preferred_element_type=F32)
         + jnp.dot(p_c.astype(BF16), cv[0], preferred_element_type=F32)) / denom
    for g in range(ATT_GROUP):
        o_ref[0, :, g * ATT_HD:(g + 1) * ATT_HD] = o[g * w:(g + 1) * w].astype(o_ref.dtype)


def window_attention(qkv_l, qkv_c, att_sink):
    b, l, _ = qkv_l.shape
    lc = qkv_c.shape[1]
    w = ATT_BLOCK
    nb = l // w
    gw = ATT_GROUP * ATT_HD
    kcol = ATT_HEADS
    vcol = ATT_HEADS + ATT_KV_HEADS

    def kv_spec(col, shift):
        return pl.BlockSpec((1, w, ATT_HD),
                            lambda bi, h, n, s: (bi, jnp.clip(n + shift, 0, nb - 1), col + h))

    grid_spec = pltpu.PrefetchScalarGridSpec(
        num_scalar_prefetch=1,
        grid=(b, ATT_KV_HEADS, nb),
        in_specs=[pl.BlockSpec((1, w, gw), lambda bi, h, n, s: (bi, n, h))]
        + [kv_spec(kcol, d) for d in (-1, 0, 1)] + [kv_spec(vcol, d) for d in (-1, 0, 1)]
        + [pl.BlockSpec((1, lc, ATT_HD), lambda bi, h, n, s: (bi, 0, kcol + h)),
           pl.BlockSpec((1, lc, ATT_HD), lambda bi, h, n, s: (bi, 0, vcol + h))],
        out_specs=pl.BlockSpec((1, w, gw), lambda bi, h, n, s: (bi, n, h)),
    )
    return pl.pallas_call(
        functools.partial(_att_kernel, seq=l),
        grid_spec=grid_spec,
        out_shape=jax.ShapeDtypeStruct((b, l, ATT_HEADS * ATT_HD), BF16),
        compiler_params=_cparams(("parallel", "parallel", "arbitrary")),
        name="window_attention",
    )(att_sink.astype(F32), qkv_l, qkv_l, qkv_l, qkv_l, qkv_l, qkv_l, qkv_l, qkv_c, qkv_c)


def moe_apply(h_bf16, route, w1, w3, w2):
    n, d = h_bf16.shape
    tm = MOE_TM
    flat_e = route[0:TOP_K].astype(jnp.int32).T.reshape(-1)
    flat_w = route[TOP_K:2 * TOP_K].T.reshape(-1)
    onehot = (flat_e[:, None] == jnp.arange(N_EXPERTS)[None, :]).astype(jnp.int32)
    rank = jnp.sum((jnp.cumsum(onehot, axis=0) - onehot) * onehot, axis=-1)
    counts = jnp.sum(onehot, axis=0)
    padded = ((counts + tm - 1) // tm) * tm
    ends = jnp.cumsum(padded)
    offs = ends - padded
    pos = offs[flat_e] + rank
    n_tiles = (2 * n + tm - 1) // tm + N_EXPERTS
    p = n_tiles * tm
    used = ends[-1] // tm
    tile_ids = jnp.arange(n_tiles, dtype=jnp.int32)
    tile_valid = (tile_ids < used).astype(jnp.int32)
    tile_src = jnp.minimum(tile_ids, used - 1).astype(jnp.int32)
    tile_expert = jnp.minimum(
        jnp.searchsorted(ends, tile_src * tm, side="right"), N_EXPERTS - 1).astype(jnp.int32)
    src = jnp.zeros((p,), jnp.int32).at[pos].set(jnp.arange(2 * n, dtype=jnp.int32) // 2,
                                                 unique_indices=True)
    row_w = jnp.zeros((p,), F32).at[pos].set(flat_w, unique_indices=True)
    x_sorted = jnp.take(h_bf16, src, axis=0)
    y_sorted = moe_grouped(x_sorted, w1, w3, w2, row_w.reshape(p, 1), tile_expert, tile_src,
                           tile_valid)
    pos2 = pos.reshape(n, 2)
    return (jnp.take(y_sorted, pos2[:, 0], axis=0).astype(F32)
            + jnp.take(y_sorted, pos2[:, 1], axis=0).astype(F32))


FFT_Q = 128
HY_COLS = (HY_ORDER + 1) * HY_CH


def _hy_conv_kernel(u_ref, prev_ref, next_ref, w_ref, b_ref, z_ref, *g_refs):
    i = pl.program_id(1)
    u = u_ref[0]
    t = u.shape[0]
    row = lax.broadcasted_iota(jnp.int32, (t, 1), 0)
    has_prev = (i > 0).astype(F32)
    has_next = (i < pl.num_programs(1) - 1).astype(F32)
    up = jnp.where(row == 0, prev_ref[0, SUBLANE - 1:SUBLANE, :] * has_prev, pltpu.roll(u, 1, 0))
    dn = jnp.where(row == t - 1, next_ref[0, 0:1, :] * has_next, pltpu.roll(u, t - 1, 0))
    y = up * w_ref[0:1, :] + u * w_ref[1:2, :] + dn * w_ref[2:3, :] + b_ref[...]
    for n, o_ref in enumerate((z_ref,) + g_refs):
        o_ref[0] = y[:, n * HY_CH:(n + 1) * HY_CH]


def hy_short_conv(z, w, bias, *, tm):
    b, l, _ = z.shape
    ucol = EV_COL["u"] // HY_COLS
    nh = l // SUBLANE
    hb = tm // SUBLANE
    row = lambda bi, i: (bi, i, 0)
    return pl.pallas_call(
        _hy_conv_kernel,
        grid=(b, l // tm),
        in_specs=[
            pl.BlockSpec((1, tm, HY_COLS), lambda bi, i: (bi, i, ucol)),
            pl.BlockSpec((1, SUBLANE, HY_COLS), lambda bi, i: (bi, jnp.maximum(i * hb - 1, 0), ucol)),
            pl.BlockSpec((1, SUBLANE, HY_COLS), lambda bi, i: (bi, jnp.minimum((i + 1) * hb, nh - 1), ucol)),
            pl.BlockSpec((3, HY_COLS), lambda bi, i: (0, 0)),
            pl.BlockSpec((1, HY_COLS), lambda bi, i: (0, 0)),
        ],
        out_specs=[pl.BlockSpec((1, tm, HY_CH), row)] * (HY_ORDER + 1),
        out_shape=[jax.ShapeDtypeStruct((b, l, HY_CH), F32)] * (HY_ORDER + 1),
        compiler_params=_cparams(("parallel", "parallel")),
        name="hy_short_conv",
    )(z, z, z, w, bias.reshape(1, HY_COLS))


def _hy_filter_kernel(feat_ref, w1_ref, b1_ref, w2_ref, b2_ref, w3_ref, delta_ref, kern_ref, den_ref):
    feat = feat_ref[...]
    h = jnp.sin(jnp.dot(feat, w1_ref[...], preferred_element_type=F32, precision=HIGHEST) + b1_ref[...])
    h = jnp.sin(jnp.dot(h, w2_ref[...], preferred_element_type=F32, precision=HIGHEST) + b2_ref[...])
    f = jnp.dot(h, w3_ref[...], preferred_element_type=F32, precision=HIGHEST)
    t = feat[:, 0:1]
    live = feat[:, LANE - 1:LANE]
    kern = f * (jnp.exp(-t * delta_ref[...]) + HY_SHIFT) * live
    kern_ref[...] = kern

    @pl.when((pl.program_id(0) == 0) & (pl.program_id(1) == 0))
    def _():
        den_ref[...] = jnp.zeros_like(den_ref)

    den_ref[...] += jnp.sum(jnp.abs(kern), axis=0, keepdims=True)


def hy_filters(l, w1, b1, w2, b2, w3, *, tr):
    t = jnp.linspace(0.0, 1.0, l, dtype=F32)[:, None]
    w = (2.0 * math.pi / l) * jnp.arange(l, dtype=F32)[:, None]
    bands = jnp.linspace(1e-4, HY_BANDS - 1.0, HY_BANDS, dtype=F32)[None, :]
    feats = jnp.concatenate([t, jnp.cos(bands * w), -jnp.sin(bands * w)], axis=-1)
    feats = jnp.pad(feats, ((0, 0), (0, LANE - 1 - HY_EMB)))
    feats = jnp.concatenate([feats, jnp.ones((l, 1), F32)], axis=-1)
    table = jnp.concatenate([feats, jnp.zeros((1, LANE), F32), feats[:0:-1]], axis=0)
    w1p = jnp.pad(w1.astype(F32), ((0, LANE - HY_EMB), (0, 0)))
    w3s = w3.astype(F32).reshape(HY_FFN, HY_ORDER, 2, HY_CH).transpose(0, 2, 1, 3).reshape(HY_FFN, -1)
    deltas = jnp.abs(jnp.linspace(HY_MIN_DECAY, HY_MAX_DECAY, HY_CH, dtype=F32))
    deltas = jnp.tile(deltas, HY_ORDER).reshape(1, -1)
    oc = HY_ORDER * HY_CH
    nb = l // tr
    const = lambda s, i: (0, 0)
    return pl.pallas_call(
        _hy_filter_kernel,
        grid=(2, nb),
        in_specs=[
            pl.BlockSpec((tr, LANE), lambda s, i: (s * nb + i, 0)),
            pl.BlockSpec((LANE, HY_FFN), const), pl.BlockSpec((1, HY_FFN), const),
            pl.BlockSpec((HY_FFN, HY_FFN), const), pl.BlockSpec((1, HY_FFN), const),
            pl.BlockSpec((HY_FFN, oc), lambda s, i: (0, s)),
            pl.BlockSpec((1, oc), const),
        ],
        out_specs=[pl.BlockSpec((tr, oc), lambda s, i: (s * nb + i, 0)), pl.BlockSpec((1, oc), const)],
        out_shape=[jax.ShapeDtypeStruct((2 * l, oc), F32), jax.ShapeDtypeStruct((1, oc), F32)],
        compiler_params=_cparams(("arbitrary", "arbitrary")),
        name="hy_filters",
    )(table, w1p, b1.reshape(1, -1).astype(F32), w2.astype(F32), b2.reshape(1, -1).astype(F32), w3s, deltas)


def _cis(num, den):
    ang = (-2.0 * math.pi / den) * (num % den).astype(F32)
    return jnp.cos(ang), jnp.sin(ang)


def _block_complex(re, im):
    return jnp.concatenate([jnp.concatenate([re, -im], axis=-1), jnp.concatenate([im, re], axis=-1)], axis=-2)


def _fft_tables(p):
    q = FFT_Q
    n = p * q
    i_p = jnp.arange(p, dtype=jnp.int32)
    i_q = jnp.arange(q, dtype=jnp.int32)
    fr, fi = _cis(i_p[:, None] * i_p[None, :], p)
    f_filt = jnp.concatenate([fr, fi], axis=0)
    f_in = _block_complex(fr[:, :p // 2], fi[:, :p // 2])
    f_out = _block_complex(fr[:p // 2, :], -fi[:p // 2, :]) * (1.0 / n)
    num = i_q[None, None, :] * i_p[:, None, None] + p * i_q[None, :, None] * i_q[None, None, :]
    gr, gi = _cis(num, n)
    g = _block_complex(gr, gi)
    h = _block_complex(jnp.swapaxes(gr, 1, 2), -jnp.swapaxes(gi, 1, 2))
    return tuple(a.astype(BF16) for a in (f_filt, f_in, g, h, f_out))


FFT_TC = 512


def _fft_in_kernel(f_ref, x_ref, o_ref, xs_ref):
    for s in range(x_ref.shape[1]):
        xs_ref[...] = x_ref[:, s, :]
        o_ref[:, s, :] = jnp.dot(f_ref[...], xs_ref[...].astype(BF16), preferred_element_type=F32)


def fft_in(f, x):
    k, q, c = x.shape
    tc = FFT_TC
    return pl.pallas_call(
        _fft_in_kernel,
        grid=(q // SUBLANE, c // tc),
        in_specs=[pl.BlockSpec(f.shape, lambda j, ci: (0, 0)),
                  pl.BlockSpec((k, SUBLANE, tc), lambda j, ci: (0, j, ci))],
        out_specs=pl.BlockSpec((f.shape[0], SUBLANE, tc), lambda j, ci: (0, j, ci)),
        out_shape=jax.ShapeDtypeStruct((f.shape[0], q, c), F32),
        scratch_shapes=[pltpu.VMEM((k, tc), F32)],
        compiler_params=_cparams(("parallel", "parallel")),
        name="fft_in",
    )(f, x)


def _fft_filt_mid_kernel(g_ref, a_ref, inv_ref, o_ref):
    q2, c = o_ref.shape[1], o_ref.shape[2]
    a = a_ref[...].reshape(q2, c).astype(BF16)
    o_ref[0] = jnp.dot(g_ref[0], a, preferred_element_type=F32) * inv_ref[...]


def fft_filt_mid(g, a, den):
    p = g.shape[0]
    c = den.shape[1]
    a = a.reshape(2, p, FFT_Q, c)
    return pl.pallas_call(
        _fft_filt_mid_kernel,
        grid=(p,),
        in_specs=[pl.BlockSpec((1, 2 * FFT_Q, 2 * FFT_Q), lambda k: (k, 0, 0)),
                  pl.BlockSpec((2, 1, FFT_Q, c), lambda k: (0, k, 0, 0)),
                  pl.BlockSpec((1, c), lambda k: (0, 0))],
        out_specs=pl.BlockSpec((1, 2 * FFT_Q, c), lambda k: (k, 0, 0)),
        out_shape=jax.ShapeDtypeStruct((p, 2 * FFT_Q, c), F32),
        compiler_params=_cparams(("parallel",)),
        name="fft_filt_mid",
    )(g, a, 1.0 / den)


def _cmul_rows(s, k):
    half = s.shape[0] // 2
    sr, si, kr, ki = s[:half], s[half:], k[:half], k[half:]
    return jnp.concatenate([sr * kr - si * ki, sr * ki + si * kr], axis=0)


def _fft_mid_kernel(g_ref, h_ref, a_ref, k_ref, o_ref):
    q2, c = k_ref.shape[1], k_ref.shape[2]
    a = a_ref[...].reshape(q2, c).astype(BF16)
    y = jnp.dot(g_ref[0], a, preferred_element_type=F32)
    z = _cmul_rows(y, k_ref[0]).astype(BF16)
    o_ref[...] = jnp.dot(h_ref[0], z, preferred_element_type=F32).reshape(o_ref.shape)


def fft_mid(g, h, a, kf, order):
    p = g.shape[0]
    c = HY_CH
    mat = pl.BlockSpec((1, 2 * FFT_Q, 2 * FFT_Q), lambda k: (k, 0, 0))
    sig = pl.BlockSpec((2, 1, FFT_Q, c), lambda k: (0, k, 0, 0))
    return pl.pallas_call(
        _fft_mid_kernel,
        grid=(p,),
        in_specs=[mat, mat, sig, pl.BlockSpec((1, 2 * FFT_Q, c), lambda k: (k, 0, order))],
        out_specs=sig,
        out_shape=jax.ShapeDtypeStruct((2, p, FFT_Q, c), F32),
        compiler_params=_cparams(("parallel",)),
        name="fft_mid",
    )(g, h, a.reshape(2, p, FFT_Q, c), kf).reshape(2 * p, FFT_Q, c)


def _fft_out_kernel(f_ref, c_ref, z_ref, gate_ref, bias_ref, o_ref, cs_ref, zs_ref, gs_ref):
    for s in range(c_ref.shape[1]):
        cs_ref[...] = c_ref[:, s, :]
        zs_ref[...] = z_ref[:, s, :]
        gs_ref[...] = gate_ref[:, s, :]
        y = jnp.dot(f_ref[...], cs_ref[...].astype(BF16), preferred_element_type=F32)
        o_ref[:, s, :] = gs_ref[...] * (y + zs_ref[...] * bias_ref[...])


def fft_out(f, cm, z, gate, bias):
    p2, q, c = cm.shape
    tc = FFT_TC
    rows = f.shape[0]
    half = pl.BlockSpec((rows, SUBLANE, tc), lambda j, ci: (0, j, ci))
    return pl.pallas_call(
        _fft_out_kernel,
        grid=(q // SUBLANE, c // tc),
        in_specs=[pl.BlockSpec(f.shape, lambda j, ci: (0, 0)),
                  pl.BlockSpec((p2, SUBLANE, tc), lambda j, ci: (0, j, ci)),
                  half, half,
                  pl.BlockSpec((1, tc), lambda j, ci: (0, ci))],
        out_specs=half,
        out_shape=jax.ShapeDtypeStruct(z.shape, F32),
        scratch_shapes=[pltpu.VMEM((p2, tc), F32), pltpu.VMEM((rows, tc), F32), pltpu.VMEM((rows, tc), F32)],
        compiler_params=_cparams(("parallel", "parallel")),
        name="fft_out",
    )(f, cm, z, gate, bias)


def _hy_ctx_kernel(ff_ref, fin_ref, fout_ref, kern_ref, inv_ref, z_ref, g1_ref, g2_ref, bias_ref, o_ref):
    b, l, c = z_ref.shape
    z = z_ref[...].reshape(b * l, c)
    gates = (g1_ref, g2_ref)
    for n in range(HY_ORDER):
        cols = slice(n * c, (n + 1) * c)
        kf = (jnp.dot(ff_ref[...], kern_ref[:, cols].astype(BF16), preferred_element_type=F32)
              * inv_ref[:, cols])
        s = jnp.dot(fin_ref[...], z.astype(BF16), preferred_element_type=F32)
        y = jnp.dot(fout_ref[...], _cmul_rows(s, kf).astype(BF16), preferred_element_type=F32)
        z = gates[n][...].reshape(b * l, c) * (y + z * bias_ref[n:n + 1, :])
    o_ref[...] = z.reshape(b, l, c).astype(o_ref.dtype)


def hy_ctx_conv(kern, den, z0, g1, g2, bias):
    b, l, c = z0.shape
    assert b == 2
    n = 2 * l
    i_n = jnp.arange(n, dtype=jnp.int32)
    fr, fi = _cis(i_n[:, None] * i_n[None, :], n)
    ff = jnp.concatenate([fr, fi], axis=0).astype(BF16)
    fin = _block_complex(fr[:, :l], fi[:, :l]).astype(BF16)
    fout = (_block_complex(fr[:l, :], -fi[:l, :]) * (1.0 / n)).astype(BF16)
    return pl.pallas_call(
        _hy_ctx_kernel,
        out_shape=jax.ShapeDtypeStruct((b, l, c), BF16),
        compiler_params=pltpu.CompilerParams(vmem_limit_bytes=VMEM_LIMIT),
        name="hy_ctx_conv",
    )(ff, fin, fout, kern, 1.0 / den, z0, g1, g2, bias)


def hyena_latent(z, conv_w, conv_b, w1, b1, w2, b2, w3, bias):
    b, l, _ = z.shape
    assert b == 2 and (2 * l) % (2 * FFT_Q) == 0
    p = 2 * l // FFT_Q
    z0, *gates = hy_short_conv(z, conv_w, conv_b, tm=256)
    kern, den = hy_filters(l, w1, b1, w2, b2, w3, tr=512)
    f_filt, f_in, g, h, f_out = _fft_tables(p)
    kf = fft_filt_mid(g, fft_in(f_filt, kern.reshape(p, FFT_Q, HY_ORDER * HY_CH)), den)
    split = lambda t: t.reshape(p, FFT_Q, HY_CH)
    zz = split(z0)
    for n in range(HY_ORDER):
        cm = fft_mid(g, h, fft_in(f_in, zz), kf, n)
        zz = fft_out(f_out, cm, zz, split(gates[n]), bias[n].astype(F32).reshape(1, HY_CH))
    return zz.reshape(b, l, HY_CH)


def hyena_ctx(z, conv_w, conv_b, w1, b1, w2, b2, w3, bias):
    b, l, _ = z.shape
    z0, g1, g2 = hy_short_conv(z, conv_w, conv_b, tm=l)
    kern, den = hy_filters(l, w1, b1, w2, b2, w3, tr=l)
    return hy_ctx_conv(kern, den, z0, g1, g2, bias.astype(F32))


def even_mixer_core(zc, zl, gla_w_gate, gla_b_gate, gla_norm, hy_conv_w, hy_conv_b,
                    hy_w1, hy_b1, hy_w2, hy_b2, hy_w3, hy_bias):
    gc, gl = gla_bidir(zc, zl, gla_w_gate, gla_b_gate, gla_norm)
    hy = (hy_conv_w, hy_conv_b, hy_w1, hy_b1, hy_w2, hy_b2, hy_w3, hy_bias)
    return (gc, hyena_ctx(zc, *hy)), (gl, hyena_latent(zl, *hy))


def odd_mixer_core(zc, zl, ret_norm, att_q_norm, att_k_norm, att_sink):
    _, rl = retention_bidir(zc, zl, ret_norm)
    qkv_l = att_prep(zl, att_q_norm, att_k_norm, _axial_tables(zl.shape[1]), tm=512)
    qkv_c = att_prep(zc, att_q_norm, att_k_norm, None, tm=zc.shape[1])
    return rl, window_attention(qkv_l, qkv_c, att_sink)


def kernel(x, c, ctx, c_ctx, ada_w, ada_b, norm_mix, norm_ffn, ev_w_in, ev_w_out, gla_w_gate, gla_b_gate, gla_norm, hy_conv_w, hy_conv_b, hy_w1, hy_b1, hy_w2, hy_b2, hy_w3, hy_bias, od_w_in, od_w_out, ret_norm, att_q_norm, att_k_norm, att_sink, router_w, router_b, moe_w1, moe_w3, moe_w2):
    assert DEPTH == 2
    bsz, seq, d = x.shape
    n_ctx = ctx.shape[1]
    c_pad = jnp.zeros((8, d), F32).at[:bsz].set(c).at[bsz].set(c_ctx)
    mods = ada_modulation(c_pad, ada_w, ada_b)
    slot = np.array([SUBLANE * (e % EXPERTS_PER_GROUP) + e // EXPERTS_PER_GROUP for e in range(N_EXPERTS)])
    rw_pad = jnp.zeros((d, LOGIT_PAD), F32).at[:, slot].set(router_w)
    rb_pad = jnp.zeros((EXPERTS_PER_GROUP * SUBLANE, LANE), F32).at[slot].set(
        jnp.broadcast_to(router_b.astype(F32)[:, None], (N_EXPERTS, LANE)))

    xl, xc = x, ctx
    for layer in range(DEPTH):
        last = layer == DEPTH - 1
        ml = mods[layer, :bsz].reshape(bsz, 1, 6, d)
        mc = jnp.broadcast_to(mods[layer, bsz].reshape(1, 1, 6, d), (bsz, 1, 6, d))
        sh1, sc1, g1, sh2, sc2, g2 = [ml[:, :, i] for i in range(6)]
        csh1, csc1, cg1, csh2, csc2, cg2 = [mc[:, :, i] for i in range(6)]
        i = layer // 2
        if layer % 2 == 0:
            w_in, w_out = ev_w_in[i].astype(BF16), ev_w_out[i].astype(BF16)
            lr0 = sum(EVEN_SPLITS[:4])
            lr1 = lr0 + EVEN_SPLITS[4]
            w_in = jnp.concatenate([w_in[:, :lr0], w_in[:, lr1:], w_in[:, lr0:lr1],
                                    jnp.zeros((d, EVEN_COLS_PAD - w_in.shape[1]), BF16)], axis=1)
            tn = EVEN_COLS_PAD // 7
        else:
            w_in, w_out = od_w_in[i].astype(BF16), od_w_out[i].astype(BF16)
            tn = w_in.shape[1] // 6
        zl = norm_mod_matmul(xl, norm_mix[layer], sc1, sh1, w_in, tm=1024, tn=tn, out_dtype=F32)
        zc = norm_mod_matmul(xc, norm_mix[layer], csc1, csh1, w_in, tm=n_ctx, tn=tn, out_dtype=F32)
        if layer % 2 == 0:
            mixc, mixl = even_mixer_core(zc, zl, gla_w_gate[i], gla_b_gate[i], gla_norm[i], hy_conv_w[i],
                                         hy_conv_b[i], hy_w1[i], hy_b1[i], hy_w2[i], hy_b2[i], hy_w3[i],
                                         hy_bias[i])
        else:
            mixl = odd_mixer_core(zc, zl, ret_norm[i], att_q_norm[i], att_k_norm[i], att_sink[i])
        xl, hl, rtl = outproj_residual_norm(*mixl, w_out, xl, g1, norm_ffn[layer], sc2, sh2, rw_pad, rb_pad,
                                            tm=512)
        rtl = rtl.transpose(1, 0, 2).reshape(ROUTE_ROWS, -1)
        w1 = moe_w1[layer].astype(BF16)
        w3 = moe_w3[layer].astype(BF16)
        w2 = moe_w2[layer].astype(BF16)
        if not last:
            xc, hc, rtc = outproj_residual_norm(*mixc, w_out, xc, cg1, norm_ffn[layer], csc2, csh2, rw_pad,
                                                rb_pad, tm=n_ctx)
            rtc = rtc.transpose(1, 0, 2).reshape(ROUTE_ROWS, -1)
            h = jnp.concatenate([hc.reshape(-1, d), hl.reshape(-1, d)], axis=0)
            y = moe_apply(h, jnp.concatenate([rtc, rtl], axis=1), w1, w3, w2)
            yc = y[:bsz * n_ctx].reshape(bsz, n_ctx, d)
            yl = y[bsz * n_ctx:].reshape(bsz, seq, d)
            xc = xc + cg2 * yc
            xl = xl + g2 * yl
        else:
            y = moe_apply(hl.reshape(-1, d), rtl, w1, w3, w2)
            xl = xl + g2 * y.reshape(bsz, seq, d)
    return xl
```

```python
import functools
import math

import jax
import jax.numpy as jnp
import numpy as np
from jax import lax
from jax.experimental import pallas as pl
from jax.experimental.pallas import tpu as pltpu

D_MODEL = 2048
BATCH = 2
SEQ = 8192
DEPTH = 2
GRID_W = 64
CTX_LEN = 256
MIX_WIDTH = D_MODEL
GROUP_WIDTH = MIX_WIDTH // 2
EPS = 1e-6
F32 = jnp.float32
BF16 = jnp.bfloat16

GLA_HEADS = 4
GLA_DK = GROUP_WIDTH // (2 * GLA_HEADS)
GLA_DV = GROUP_WIDTH // GLA_HEADS
GLA_LOWRANK = 16
GLA_GATE_NORMALIZER = 16.0
GLA_CHUNK = 64

HY_CH = GROUP_WIDTH
HY_ORDER = 2
HY_BANDS = 16
HY_EMB = 1 + 2 * HY_BANDS
HY_FFN = 64
HY_SHIFT = 0.05
HY_MIN_DECAY = math.log(1e-2) / 1.5
HY_MAX_DECAY = math.log(1e-2) / 0.3

RET_HEADS = 4
RET_DK = GROUP_WIDTH // (2 * RET_HEADS)
RET_DV = GROUP_WIDTH // RET_HEADS
RET_CHUNK = 64

ATT_HD = 128
ATT_HEADS = GROUP_WIDTH // ATT_HD
ATT_KV_HEADS = 2
ATT_GROUP = ATT_HEADS // ATT_KV_HEADS
ATT_WINDOW = 128
ATT_BLOCK = 128
ROPE_BASE = 10000.0

N_EXPERTS = 16
N_GROUPS = 4
EXPERTS_PER_GROUP = N_EXPERTS // N_GROUPS
TOP_K = 2
D_FF_EXPERT = D_MODEL // 2

EVEN_SPLITS = (GLA_HEADS * GLA_DK, GLA_HEADS * GLA_DK, GLA_HEADS * GLA_DV, GLA_HEADS * GLA_DV,
               2 * GLA_LOWRANK, (HY_ORDER + 1) * HY_CH)
ODD_SPLITS = (RET_HEADS * RET_DK, RET_HEADS * RET_DK, RET_HEADS * RET_DV, RET_HEADS * RET_DV,
              ATT_HEADS * ATT_HD, ATT_KV_HEADS * ATT_HD, ATT_KV_HEADS * ATT_HD)

LANE = 128
EVEN_COLS_PAD = -(-sum(EVEN_SPLITS) // (7 * LANE)) * (7 * LANE)
VMEM_LIMIT = 56 * 1024 * 1024
MOE_TM = 512
MOE_FF_SPLIT = 2
LOGIT_PAD = LANE


def _cparams(sem):
    return pltpu.CompilerParams(dimension_semantics=sem, vmem_limit_bytes=VMEM_LIMIT)


def _ada_kernel(c_ref, w_ref, b_ref, o_ref):
    c = c_ref[...]
    s = (c * jax.nn.sigmoid(c)).astype(BF16)
    o_ref[0] = jnp.dot(s, w_ref[0].astype(BF16), preferred_element_type=F32) + b_ref[0]


def ada_modulation(c_pad, ada_w, ada_b):
    depth, d, n = ada_w.shape
    tn = 1024
    return pl.pallas_call(
        _ada_kernel,
        grid=(depth, n // tn),
        in_specs=[
            pl.BlockSpec((8, d), lambda l, j: (0, 0)),
            pl.BlockSpec((1, d, tn), lambda l, j: (l, 0, j)),
            pl.BlockSpec((1, 1, tn), lambda l, j: (l, 0, j)),
        ],
        out_specs=pl.BlockSpec((1, 8, tn), lambda l, j: (l, 0, j)),
        out_shape=jax.ShapeDtypeStruct((depth, 8, n), F32),
        compiler_params=_cparams(("parallel", "arbitrary")),
        name="ada_modulation",
    )(c_pad, ada_w, ada_b.reshape(depth, 1, n))


def _norm_mod(x, gain, sc, sh):
    ms = jnp.mean(x * x, axis=-1, keepdims=True)
    return (x * lax.rsqrt(ms + EPS) * gain) * (1.0 + sc) + sh


def _inproj_kernel(x_ref, gain_ref, sc_ref, sh_ref, w_ref, o_ref, h_ref):
    @pl.when(pl.program_id(2) == 0)
    def _():
        h_ref[...] = _norm_mod(x_ref[0], gain_ref[...], sc_ref[0], sh_ref[0]).astype(BF16)

    o_ref[0] = jnp.dot(h_ref[...], w_ref[...].astype(BF16), preferred_element_type=F32).astype(o_ref.dtype)


def norm_mod_matmul(x, gain, sc, sh, w, *, tm, tn, out_dtype):
    b, l, d = x.shape
    n = w.shape[1]
    return pl.pallas_call(
        _inproj_kernel,
        grid=(b, l // tm, n // tn),
        in_specs=[
            pl.BlockSpec((1, tm, d), lambda bi, i, j: (bi, i, 0)),
            pl.BlockSpec((1, d), lambda bi, i, j: (0, 0)),
            pl.BlockSpec((1, 1, d), lambda bi, i, j: (bi, 0, 0)),
            pl.BlockSpec((1, 1, d), lambda bi, i, j: (bi, 0, 0)),
            pl.BlockSpec((d, tn), lambda bi, i, j: (0, j)),
        ],
        out_specs=pl.BlockSpec((1, tm, tn), lambda bi, i, j: (bi, i, j)),
        out_shape=jax.ShapeDtypeStruct((b, l, n), out_dtype),
        scratch_shapes=[pltpu.VMEM((tm, d), BF16)],
        compiler_params=_cparams(("parallel", "parallel", "arbitrary")),
        name="norm_mod_matmul",
    )(x, gain.reshape(1, d), sc, sh, w)


SUBLANE = 8
ROUTE_ROWS = SUBLANE


def _route_block(lg, rb_ref):
    tm = lg.shape[0]
    t = lg.T
    sub = lax.broadcasted_iota(jnp.int32, (SUBLANE, tm), 0)
    valid = sub < N_GROUPS
    members = range(EXPERTS_PER_GROUP)
    s = [jnp.where(valid, t[SUBLANE * j:SUBLANE * (j + 1), :], -jnp.inf) for j in members]
    mx = jnp.max(jnp.maximum(jnp.maximum(s[0], s[1]), jnp.maximum(s[2], s[3])), axis=0, keepdims=True)
    e = [jnp.exp(sj - mx) for sj in s]
    den = jnp.sum((e[0] + e[1]) + (e[2] + e[3]), axis=0, keepdims=True)
    sc = [ej / den for ej in e]
    sel = [sc[j] + jnp.concatenate([rb_ref[SUBLANE * j:SUBLANE * (j + 1), :]] * (tm // LANE), axis=1)
           for j in members]
    hi01, lo01 = jnp.maximum(sel[0], sel[1]), jnp.minimum(sel[0], sel[1])
    hi23, lo23 = jnp.maximum(sel[2], sel[3]), jnp.minimum(sel[2], sel[3])
    top1 = jnp.maximum(hi01, hi23)
    top2 = jnp.maximum(jnp.minimum(hi01, hi23), jnp.maximum(lo01, lo23))
    gscore = jnp.where(valid, top1 + top2, -jnp.inf)
    gmax = jnp.max(gscore, axis=0, keepdims=True)
    gidx = jnp.min(jnp.where(gscore == gmax, sub, SUBLANE), axis=0, keepdims=True)
    chosen = sub == gidx
    v = [jnp.sum(jnp.where(chosen, sel[j], 0.0), axis=0, keepdims=True) for j in members]
    p = [jnp.sum(jnp.where(chosen, sc[j], 0.0), axis=0, keepdims=True) for j in members]
    one = lambda cond: jnp.where(cond, 1.0, 0.0)
    picks = []
    for want in (0.0, 1.0):
        idx = jnp.zeros_like(v[0])
        wt = jnp.zeros_like(v[0])
        for j in members:
            rank = sum(one(v[k] > v[j]) for k in members if k != j) + sum(one(v[k] == v[j]) for k in range(j))
            hit = rank == want
            idx = idx + jnp.where(hit, float(j), 0.0)
            wt = wt + jnp.where(hit, p[j], 0.0)
        picks.append((idx, wt))
    (j1, p1), (j2, p2) = picks
    base = (EXPERTS_PER_GROUP * gidx).astype(F32)
    tot = p1 + p2
    rows = [base + j1, base + j2, p1 / tot, p2 / tot]
    return jnp.concatenate(rows + [jnp.zeros((ROUTE_ROWS - len(rows), tm), F32)], axis=0)


def _outproj_kernel(m1_ref, m2_ref, w_ref, x_ref, g_ref, gain_ref, sc_ref, sh_ref, rw_ref, rb_ref,
                    xo_ref, h_ref, rt_ref):
    k1 = m1_ref.shape[-1]
    o = (jnp.dot(m1_ref[0].astype(BF16), w_ref[:k1, :].astype(BF16), preferred_element_type=F32)
         + jnp.dot(m2_ref[0].astype(BF16), w_ref[k1:, :].astype(BF16), preferred_element_type=F32))
    xn = x_ref[0] + g_ref[0] * o
    xo_ref[0] = xn
    h = _norm_mod(xn, gain_ref[...], sc_ref[0], sh_ref[0])
    h_hi = h.astype(BF16)
    h_ref[0] = h_hi
    h_lo = (h - h_hi.astype(F32)).astype(BF16)
    both = jnp.dot(h_hi, rw_ref[...], preferred_element_type=F32)
    lg = (both[:, :LOGIT_PAD] + both[:, LOGIT_PAD:]
          + jnp.dot(h_lo, rw_ref[:, :LOGIT_PAD], preferred_element_type=F32))
    rt_ref[0] = _route_block(lg, rb_ref)


def outproj_residual_norm(m1, m2, w, x, g, gain, sc, sh, rw_pad, rb_pad, *, tm):
    b, l, d = x.shape
    k1, k2 = m1.shape[-1], m2.shape[-1]
    row = lambda bi, i: (bi, i, 0)
    mod = lambda bi, i: (bi, 0, 0)
    const = lambda bi, i: (0, 0)
    return pl.pallas_call(
        _outproj_kernel,
        grid=(b, l // tm),
        in_specs=[
            pl.BlockSpec((1, tm, k1), row),
            pl.BlockSpec((1, tm, k2), row),
            pl.BlockSpec((k1 + k2, d), const, pipeline_mode=pl.Buffered(1)),
            pl.BlockSpec((1, tm, d), row),
            pl.BlockSpec((1, 1, d), mod),
            pl.BlockSpec((1, d), const),
            pl.BlockSpec((1, 1, d), mod),
            pl.BlockSpec((1, 1, d), mod),
            pl.BlockSpec((d, 2 * LOGIT_PAD), const),
            pl.BlockSpec(rb_pad.shape, const),
        ],
        out_specs=[
            pl.BlockSpec((1, tm, d), row),
            pl.BlockSpec((1, tm, d), row),
            pl.BlockSpec((1, ROUTE_ROWS, tm), lambda bi, i: (bi, 0, i)),
        ],
        out_shape=[
            jax.ShapeDtypeStruct((b, l, d), F32),
            jax.ShapeDtypeStruct((b, l, d), BF16),
            jax.ShapeDtypeStruct((b, ROUTE_ROWS, l), F32),
        ],
        compiler_params=_cparams(("parallel", "parallel")),
        name="outproj_residual_norm",
    )(m1, m2, w, x, g, gain.reshape(1, d), sc, sh, rw_pad, rb_pad)


def _moe_kernel(te_ref, tsrc_ref, tvalid_ref, x_ref, w1_ref, w3_ref, w2_ref, rw_ref, o_ref):
    i = pl.program_id(0)

    @pl.when(tvalid_ref[i] > 0)
    def _():
        x = x_ref[...]
        ff = w1_ref.shape[-1]
        fc = ff // MOE_FF_SPLIT
        y = None
        for c in range(MOE_FF_SPLIT):
            cols = slice(c * fc, (c + 1) * fc)
            a = jnp.dot(x, w1_ref[0, 0, :, cols].astype(BF16), preferred_element_type=F32)
            b = jnp.dot(x, w3_ref[0, 0, :, cols].astype(BF16), preferred_element_type=F32)
            mid = (a * jax.nn.sigmoid(a) * b).astype(BF16)
            part = jnp.dot(mid, w2_ref[0, 0, cols, :].astype(BF16), preferred_element_type=F32)
            y = part if y is None else y + part
        o_ref[...] = (y * rw_ref[...]).astype(o_ref.dtype)

    @pl.when(tvalid_ref[i] == 0)
    def _():
        o_ref[...] = jnp.zeros_like(o_ref)


def moe_grouped(x_sorted, w1, w3, w2, layer, row_w, tile_expert, tile_src, tile_valid):
    p, d = x_sorted.shape
    ff = w1.shape[-1]
    tm = MOE_TM
    wmap = lambda i, te, ts, tv: (layer, te[i], 0, 0)
    grid_spec = pltpu.PrefetchScalarGridSpec(
        num_scalar_prefetch=3,
        grid=(p // tm,),
        in_specs=[
            pl.BlockSpec((tm, d), lambda i, te, ts, tv: (ts[i], 0)),
            pl.BlockSpec((1, 1, d, ff), wmap, pipeline_mode=pl.Buffered(1)),
            pl.BlockSpec((1, 1, d, ff), wmap, pipeline_mode=pl.Buffered(1)),
            pl.BlockSpec((1, 1, ff, d), wmap, pipeline_mode=pl.Buffered(1)),
            pl.BlockSpec((tm, 1), lambda i, te, ts, tv: (ts[i], 0)),
        ],
        out_specs=pl.BlockSpec((tm, d), lambda i, te, ts, tv: (i, 0)),
    )
    return pl.pallas_call(
        _moe_kernel,
        grid_spec=grid_spec,
        out_shape=jax.ShapeDtypeStruct((p, d), BF16),
        compiler_params=_cparams(("arbitrary",)),
        name="moe_grouped",
    )(tile_expert, tile_src, tile_valid, x_sorted, w1, w3, w2, row_w)


CHUNK = 64
SCAN_T = 256
_NT = (((1,), (1,)), ((), ()))
_TN = (((0,), (0,)), ((), ()))
HIGHEST = lax.Precision.HIGHEST


def _chunk_keep(reverse):
    r = lax.broadcasted_iota(jnp.int32, (CHUNK, CHUNK), 0)
    c = lax.broadcasted_iota(jnp.int32, (CHUNK, CHUNK), 1)
    return (c >= r) if reverse else (c <= r)


def _chunk_step(q, k, v, e_q, e_k, e_s, e_last, keep, st):
    qe = (q * e_q).astype(BF16)
    ke = (k * e_k).astype(BF16)
    kd = (k * e_s).astype(BF16)
    vb = v.astype(BF16)
    att = lax.dot_general(qe, ke, _NT, preferred_element_type=F32)
    att = jnp.where(keep, att, 0.0).astype(BF16)
    o = (jnp.dot(att, vb, preferred_element_type=F32)
         + lax.dot_general(qe, st.astype(BF16), _NT, preferred_element_type=F32))
    st = e_last * st + lax.dot_general(vb, kd, _TN, preferred_element_type=F32)
    return o, st


def _merge_store(o, ob_ref, g_ref, gn_ref, o_ref, sl):
    tot = o + ob_ref[0, sl, :]
    y = tot * lax.rsqrt(jnp.mean(tot * tot, axis=-1, keepdims=True) + EPS) * gn_ref[...]
    g = g_ref[0, sl, :]
    o_ref[0, sl, :] = (y * (g * jax.nn.sigmoid(g))).astype(o_ref.dtype)


def _chunk_order(n, reverse):
    return range(n - 1, -1, -1) if reverse else range(n)


def _gla_kernel(qc, kc, vc, lrc, ql, kl, vl, lrl, wg, bg, *rest, reverse, merge):
    if merge:
        obc, obl, gc, gl, gn, oc_ref, ol_ref, st_ref = rest
    else:
        oc_ref, ol_ref, st_ref = rest
        obc = obl = gc = gl = gn = None
    keep = _chunk_keep(reverse)
    tri = keep.astype(F32)

    def run(q_ref, k_ref, v_ref, lr_ref, o_ref, ob_ref, g_ref):
        pre = jnp.dot(lr_ref[0], wg[...], preferred_element_type=F32, precision=HIGHEST) + bg[...]
        la = (jnp.minimum(pre, 0.0) - jnp.log(1.0 + jnp.exp(-jnp.abs(pre)))) * (1.0 / GLA_GATE_NORMALIZER)
        st = st_ref[...]
        for ci in _chunk_order(q_ref.shape[1] // CHUNK, reverse):
            sl = pl.ds(ci * CHUNK, CHUNK)
            cum = jnp.dot(tri, la[ci * CHUNK:(ci + 1) * CHUNK], preferred_element_type=F32,
                          precision=HIGHEST)
            last = cum[0:1] if reverse else cum[CHUNK - 1:CHUNK]
            o, st = _chunk_step(q_ref[0, sl, :] * (GLA_DK ** -0.5), k_ref[0, sl, :], v_ref[0, sl, :],
                                jnp.exp(cum), jnp.exp(-cum), jnp.exp(last - cum), jnp.exp(last), keep, st)
            if merge:
                _merge_store(o, ob_ref, g_ref, gn, o_ref, sl)
            else:
                o_ref[0, sl, :] = o
        st_ref[...] = st

    @pl.when(pl.program_id(2) == 0)
    def _():
        st_ref[...] = jnp.zeros_like(st_ref)
        run(qc, kc, vc, lrc, oc_ref, obc, gc)

    @pl.when(pl.program_id(2) > 0)
    def _():
        run(ql, kl, vl, lrl, ol_ref, obl, gl)


def _ret_kernel(qc, kc, vc, ql, kl, vl, cos_ref, sin_ref, cum_ref, *rest, reverse, merge):
    if merge:
        obc, obl, gc, gl, gn, oc_ref, ol_ref, st_ref = rest
    else:
        oc_ref, ol_ref, st_ref = rest
        obc = obl = gc = gl = gn = None
    keep = _chunk_keep(reverse)
    cum = cum_ref[0]
    last = cum[0:1] if reverse else cum[CHUNK - 1:CHUNK]
    e_q, e_k, e_s, e_last = jnp.exp(cum), jnp.exp(-cum), jnp.exp(last - cum), jnp.exp(last)

    def run(q_ref, k_ref, v_ref, o_ref, ob_ref, g_ref, latent):
        st = st_ref[...]
        for ci in _chunk_order(q_ref.shape[1] // CHUNK, reverse):
            sl = pl.ds(ci * CHUNK, CHUNK)
            q = q_ref[0, sl, :]
            k = k_ref[0, sl, :]
            if latent:
                cs = cos_ref[sl, :]
                sn = sin_ref[sl, :]
                q = q * cs + pltpu.roll(q, RET_DK // 2, 1) * sn
                k = k * cs + pltpu.roll(k, RET_DK // 2, 1) * sn
            o, st = _chunk_step(q, k * (RET_DK ** -0.5), v_ref[0, sl, :], e_q, e_k, e_s, e_last, keep, st)
            if merge:
                _merge_store(o, ob_ref, g_ref, gn, o_ref, sl)
            else:
                o_ref[0, sl, :] = o
        st_ref[...] = st

    @pl.when(pl.program_id(2) == 0)
    def _():
        st_ref[...] = jnp.zeros_like(st_ref)
        run(qc, kc, vc, oc_ref, obc, gc, False)

    @pl.when(pl.program_id(2) > 0)
    def _():
        run(ql, kl, vl, ol_ref, obl, gl, True)


def _scan_call(body, zc, zl, heads, dk, dv, col_q, col_k, col_v, shared_cols, extra_specs, merge_in,
               *, reverse, name):
    b, lc, _ = zc.shape
    l = zl.shape[1]
    t = SCAN_T
    nblk = l // t
    if reverse:
        blk = lambda i: jnp.where(i == 0, nblk - 1, nblk - i)
    else:
        blk = lambda i: jnp.maximum(i - 1, 0)

    def ctx_spec(w, col, per_head=True):
        return pl.BlockSpec((1, lc, w), lambda bi, h, i: (bi, 0, col // w + (h if per_head else 0)))

    def lat_spec(w, col, per_head=True):
        return pl.BlockSpec((1, t, w), lambda bi, h, i: (bi, blk(i), col // w + (h if per_head else 0)))

    ins = [zc, zc, zc] + [zc] * len(shared_cols) + [zl, zl, zl] + [zl] * len(shared_cols)
    specs = [ctx_spec(dk, col_q), ctx_spec(dk, col_k), ctx_spec(dv, col_v)]
    specs += [ctx_spec(w, col, False) for w, col in shared_cols]
    specs += [lat_spec(dk, col_q), lat_spec(dk, col_k), lat_spec(dv, col_v)]
    specs += [lat_spec(w, col, False) for w, col in shared_cols]
    for a, s in extra_specs:
        ins.append(a)
        specs.append(s)
    if merge_in is not None:
        obc, obl, col_g, gain = merge_in
        ins += [obc, obl, zc, zl, gain.reshape(1, dv)]
        specs += [ctx_spec(dv, 0), lat_spec(dv, 0), ctx_spec(dv, col_g), lat_spec(dv, col_g),
                  pl.BlockSpec((1, dv), lambda bi, h, i: (0, 0))]
        out_dtype = BF16
    else:
        out_dtype = F32
    return pl.pallas_call(
        functools.partial(body, reverse=reverse, merge=merge_in is not None),
        grid=(b, heads, 1 + nblk),
        in_specs=specs,
        out_specs=[ctx_spec(dv, 0), lat_spec(dv, 0)],
        out_shape=[jax.ShapeDtypeStruct((b, lc, heads * dv), out_dtype),
                   jax.ShapeDtypeStruct((b, l, heads * dv), out_dtype)],
        scratch_shapes=[pltpu.VMEM((dv, dk), F32)],
        compiler_params=_cparams(("parallel", "parallel", "arbitrary")),
        name=name,
    )(*ins)


EV_COL = dict(q=0, k=512, v=1024, g=2048, u=3072, lr=6144)
OD_COL = dict(q=0, k=512, v=1024, g=2048, aq=3072, ak=4096, av=4352)


def gla_bidir(zc, zl, gla_w_gate, gla_b_gate, gla_norm):
    hk = GLA_HEADS * GLA_DK
    outs = None
    for d, reverse in ((1, True), (0, False)):
        wg = jnp.zeros((LANE, hk), F32).at[d * GLA_LOWRANK:(d + 1) * GLA_LOWRANK].set(gla_w_gate[d])
        shared_cols = [(LANE, EV_COL["lr"])]
        extra_specs = [(wg, pl.BlockSpec((LANE, GLA_DK), lambda bi, h, i: (0, h))),
                       (gla_b_gate[d].reshape(1, hk), pl.BlockSpec((1, GLA_DK), lambda bi, h, i: (0, h)))]
        merge_in = None if outs is None else (outs[0], outs[1], EV_COL["g"], gla_norm)
        outs = _scan_call(_gla_kernel, zc, zl, GLA_HEADS, GLA_DK, GLA_DV, EV_COL["q"], EV_COL["k"],
                          EV_COL["v"], shared_cols, extra_specs, merge_in, reverse=reverse,
                          name="gla_bwd" if reverse else "gla_fwd_merge")
    return outs


def retention_bidir(zc, zl, ret_norm):
    l = zl.shape[1]
    inv = ROPE_BASE ** (-jnp.linspace(0.0, 1.0, RET_DK // 2, dtype=F32))
    ang = jnp.arange(l, dtype=F32)[:, None] * inv[None, :]
    cos2 = jnp.concatenate([jnp.cos(ang), jnp.cos(ang)], axis=-1)
    sin2 = jnp.concatenate([-jnp.sin(ang), jnp.sin(ang)], axis=-1)
    log_g_f = jnp.log(1.0 - 2.0 ** (-5.0 - jnp.arange(RET_HEADS, dtype=F32)))
    idx = jnp.arange(CHUNK, dtype=F32)
    nblk = l // SCAN_T
    outs = None
    for reverse in (True, False):
        if reverse:
            cum = (CHUNK - idx)[None, :, None] * log_g_f[::-1][:, None, None]
            blk = lambda i: jnp.where(i == 0, nblk - 1, nblk - i)
        else:
            cum = (idx + 1.0)[None, :, None] * log_g_f[:, None, None]
            blk = lambda i: jnp.maximum(i - 1, 0)
        cum = jnp.broadcast_to(cum, (RET_HEADS, CHUNK, RET_DK))
        tab_spec = pl.BlockSpec((SCAN_T, RET_DK), lambda bi, h, i, blk=blk: (blk(i), 0))
        extra_specs = [(cos2, tab_spec), (sin2, tab_spec),
                       (cum, pl.BlockSpec((1, CHUNK, RET_DK), lambda bi, h, i: (h, 0, 0)))]
        merge_in = None if outs is None else (outs[0], outs[1], OD_COL["g"], ret_norm)
        outs = _scan_call(_ret_kernel, zc, zl, RET_HEADS, RET_DK, RET_DV, OD_COL["q"], OD_COL["k"],
                          OD_COL["v"], [], extra_specs, merge_in, reverse=reverse,
                          name="ret_bwd" if reverse else "ret_fwd_merge")
    return outs


ATT_QKV = (ATT_HEADS + 2 * ATT_KV_HEADS) * ATT_HD


def _att_prep_kernel(z_ref, qn_ref, kn_ref, *rest, latent):
    if latent:
        cos_ref, sin_ref, o_ref = rest
        cs, sn = cos_ref[...], sin_ref[...]
        lane = lax.broadcasted_iota(jnp.int32, cs.shape, 1)
        low = (lane & (ATT_HD // 2 - 1)) < (ATT_HD // 4)
    else:
        (o_ref,) = rest
    for h in range(ATT_HEADS + ATT_KV_HEADS):
        x = z_ref[0, :, h * ATT_HD:(h + 1) * ATT_HD]
        gain = qn_ref[...] if h < ATT_HEADS else kn_ref[...]
        y = x * lax.rsqrt(jnp.mean(x * x, axis=-1, keepdims=True) + EPS) * gain
        if latent:
            partner = jnp.where(low, pltpu.roll(y, ATT_HD - ATT_HD // 4, 1), pltpu.roll(y, ATT_HD // 4, 1))
            y = y * cs + partner * sn
        if h < ATT_HEADS:
            y = y * (ATT_HD ** -0.5)
        o_ref[0, :, h * ATT_HD:(h + 1) * ATT_HD] = y.astype(o_ref.dtype)
    v0 = (ATT_HEADS + ATT_KV_HEADS) * ATT_HD
    o_ref[0, :, v0:] = z_ref[0, :, v0:].astype(o_ref.dtype)


def att_prep(z, att_q_norm, att_k_norm, tables, *, tm):
    b, l, _ = z.shape
    latent = tables is not None
    row = lambda bi, i: (bi, i, 0)
    const = lambda bi, i: (0, 0)
    ins = [z, att_q_norm.reshape(1, ATT_HD), att_k_norm.reshape(1, ATT_HD)]
    specs = [pl.BlockSpec((1, tm, ATT_QKV), lambda bi, i: (bi, i, OD_COL["aq"] // ATT_QKV)),
             pl.BlockSpec((1, ATT_HD), const), pl.BlockSpec((1, ATT_HD), const)]
    if latent:
        ins += list(tables)
        specs += [pl.BlockSpec((tm, ATT_HD), lambda bi, i: (i, 0))] * 2
    return pl.pallas_call(
        functools.partial(_att_prep_kernel, latent=latent),
        grid=(b, l // tm),
        in_specs=specs,
        out_specs=pl.BlockSpec((1, tm, ATT_QKV), row),
        out_shape=jax.ShapeDtypeStruct((b, l, ATT_QKV), BF16),
        compiler_params=_cparams(("parallel", "parallel")),
        name="att_prep",
    )(*ins)


def _axial_tables(l):
    quarter = ATT_HD // 4
    inv = ROPE_BASE ** (-jnp.arange(quarter, dtype=F32) / quarter)
    t = jnp.arange(l)
    a_row = (t // GRID_W).astype(F32)[:, None] * inv[None, :]
    a_col = (t % GRID_W).astype(F32)[:, None] * inv[None, :]
    cos = jnp.concatenate([jnp.cos(a_row)] * 2 + [jnp.cos(a_col)] * 2, axis=-1)
    sin = jnp.concatenate([-jnp.sin(a_row), jnp.sin(a_row), -jnp.sin(a_col), jnp.sin(a_col)], axis=-1)
    return cos, sin


def _att_kernel(sink_ref, q_ref, kp, kc, kn, vp, vc, vn, ck, cv, o_ref, *, seq):
    kvh = pl.program_id(1)
    n = pl.program_id(2)
    w = ATT_BLOCK
    q = jnp.concatenate([q_ref[0, :, g * ATT_HD:(g + 1) * ATT_HD] for g in range(ATT_GROUP)], axis=0)
    kw = jnp.concatenate([kp[0], kc[0], kn[0]], axis=0)
    vw = jnp.concatenate([vp[0], vc[0], vn[0]], axis=0)
    s_w = lax.dot_general(q, kw, _NT, preferred_element_type=F32)
    s_c = lax.dot_general(q, ck[0], _NT, preferred_element_type=F32)
    r = lax.broadcasted_iota(jnp.int32, s_w.shape, 0) & (w - 1)
    c = lax.broadcasted_iota(jnp.int32, s_w.shape, 1) - w
    kpos = n * w + c
    valid = (jnp.abs(r - c) <= ATT_WINDOW) & (kpos >= 0) & (kpos < seq)
    s_w = jnp.where(valid, s_w, -jnp.inf)
    sink = jnp.concatenate([jnp.full((w, 1), sink_ref[kvh * ATT_GROUP + g], F32) for g in range(ATT_GROUP)],
                           axis=0)
    m = jnp.maximum(jnp.maximum(jnp.max(s_w, axis=-1, keepdims=True), jnp.max(s_c, axis=-1, keepdims=True)),
                    sink)
    p_w = jnp.exp(s_w - m)
    p_c = jnp.exp(s_c - m)
    denom = (jnp.sum(p_w, axis=-1, keepdims=True) + jnp.sum(p_c, axis=-1, keepdims=True)
             + jnp.exp(sink - m))
    o = (jnp.dot(p_w.astype(BF16), vw, preferred_element_type=F32)
         + jnp.dot(p_c.astype(BF16), cv[0], preferred_element_type=F32)) / denom
    for g in range(ATT_GROUP):
        o_ref[0, :, g * ATT_HD:(g + 1) * ATT_HD] = o[g * w:(g + 1) * w].astype(o_ref.dtype)


def window_attention(qkv_l, qkv_c, att_sink):
    b, l, _ = qkv_l.shape
    lc = qkv_c.shape[1]
    w = ATT_BLOCK
    nb = l // w
    gw = ATT_GROUP * ATT_HD
    kcol = ATT_HEADS
    vcol = ATT_HEADS + ATT_KV_HEADS

    def kv_spec(col, shift):
        return pl.BlockSpec((1, w, ATT_HD),
                            lambda bi, h, n, s: (bi, jnp.clip(n + shift, 0, nb - 1), col + h))

    grid_spec = pltpu.PrefetchScalarGridSpec(
        num_scalar_prefetch=1,
        grid=(b, ATT_KV_HEADS, nb),
        in_specs=[pl.BlockSpec((1, w, gw), lambda bi, h, n, s: (bi, n, h))]
        + [kv_spec(kcol, d) for d in (-1, 0, 1)] + [kv_spec(vcol, d) for d in (-1, 0, 1)]
        + [pl.BlockSpec((1, lc, ATT_HD), lambda bi, h, n, s: (bi, 0, kcol + h)),
           pl.BlockSpec((1, lc, ATT_HD), lambda bi, h, n, s: (bi, 0, vcol + h))],
        out_specs=pl.BlockSpec((1, w, gw), lambda bi, h, n, s: (bi, n, h)),
    )
    return pl.pallas_call(
        functools.partial(_att_kernel, seq=l),
        grid_spec=grid_spec,
        out_shape=jax.ShapeDtypeStruct((b, l, ATT_HEADS * ATT_HD), BF16),
        compiler_params=_cparams(("parallel", "parallel", "arbitrary")),
        name="window_attention",
    )(att_sink.astype(F32), qkv_l, qkv_l, qkv_l, qkv_l, qkv_l, qkv_l, qkv_l, qkv_c, qkv_c)


def moe_apply(h_bf16, route, w1, w3, w2, layer):
    n, d = h_bf16.shape
    tm = MOE_TM
    flat_e = route[0:TOP_K].astype(jnp.int32).T.reshape(-1)
    flat_w = route[TOP_K:2 * TOP_K].T.reshape(-1)
    onehot = (flat_e[:, None] == jnp.arange(N_EXPERTS)[None, :]).astype(jnp.int32)
    rank = jnp.sum((jnp.cumsum(onehot, axis=0) - onehot) * onehot, axis=-1)
    counts = jnp.sum(onehot, axis=0)
    padded = ((counts + tm - 1) // tm) * tm
    ends = jnp.cumsum(padded)
    offs = ends - padded
    pos = offs[flat_e] + rank
    n_tiles = (2 * n + tm - 1) // tm + N_EXPERTS
    p = n_tiles * tm
    used = ends[-1] // tm
    tile_ids = jnp.arange(n_tiles, dtype=jnp.int32)
    tile_valid = (tile_ids < used).astype(jnp.int32)
    tile_src = jnp.minimum(tile_ids, used - 1).astype(jnp.int32)
    tile_expert = jnp.minimum(
        jnp.searchsorted(ends, tile_src * tm, side="right"), N_EXPERTS - 1).astype(jnp.int32)
    src = jnp.zeros((p,), jnp.int32).at[pos].set(jnp.arange(2 * n, dtype=jnp.int32) // 2,
                                                 unique_indices=True)
    row_w = jnp.zeros((p,), F32).at[pos].set(flat_w, unique_indices=True)
    x_sorted = jnp.take(h_bf16, src, axis=0)
    y_sorted = moe_grouped(x_sorted, w1, w3, w2, layer, row_w.reshape(p, 1), tile_expert, tile_src,
                           tile_valid)
    pos2 = pos.reshape(n, 2)
    return (jnp.take(y_sorted, pos2[:, 0], axis=0).astype(F32)
            + jnp.take(y_sorted, pos2[:, 1], axis=0).astype(F32))


FFT_Q = 128
HY_COLS = (HY_ORDER + 1) * HY_CH


def _hy_conv_kernel(u_ref, prev_ref, next_ref, w_ref, b_ref, z_ref, *g_refs):
    i = pl.program_id(1)
    u = u_ref[0]
    t = u.shape[0]
    row = lax.broadcasted_iota(jnp.int32, (t, 1), 0)
    has_prev = (i > 0).astype(F32)
    has_next = (i < pl.num_programs(1) - 1).astype(F32)
    up = jnp.where(row == 0, prev_ref[0, SUBLANE - 1:SUBLANE, :] * has_prev, pltpu.roll(u, 1, 0))
    dn = jnp.where(row == t - 1, next_ref[0, 0:1, :] * has_next, pltpu.roll(u, t - 1, 0))
    y = up * w_ref[0:1, :] + u * w_ref[1:2, :] + dn * w_ref[2:3, :] + b_ref[...]
    for n, o_ref in enumerate((z_ref,) + g_refs):
        o_ref[0] = y[:, n * HY_CH:(n + 1) * HY_CH]


def hy_short_conv(z, w, bias, *, tm):
    b, l, _ = z.shape
    ucol = EV_COL["u"] // HY_COLS
    nh = l // SUBLANE
    hb = tm // SUBLANE
    row = lambda bi, i: (bi, i, 0)
    return pl.pallas_call(
        _hy_conv_kernel,
        grid=(b, l // tm),
        in_specs=[
            pl.BlockSpec((1, tm, HY_COLS), lambda bi, i: (bi, i, ucol)),
            pl.BlockSpec((1, SUBLANE, HY_COLS), lambda bi, i: (bi, jnp.maximum(i * hb - 1, 0), ucol)),
            pl.BlockSpec((1, SUBLANE, HY_COLS), lambda bi, i: (bi, jnp.minimum((i + 1) * hb, nh - 1), ucol)),
            pl.BlockSpec((3, HY_COLS), lambda bi, i: (0, 0)),
            pl.BlockSpec((1, HY_COLS), lambda bi, i: (0, 0)),
        ],
        out_specs=[pl.BlockSpec((1, tm, HY_CH), row)] * (HY_ORDER + 1),
        out_shape=[jax.ShapeDtypeStruct((b, l, HY_CH), F32)] * (HY_ORDER + 1),
        compiler_params=_cparams(("parallel", "parallel")),
        name="hy_short_conv",
    )(z, z, z, w, bias.reshape(1, HY_COLS))


def _hy_filter_kernel(feat_ref, w1_ref, b1_ref, w2_ref, b2_ref, w3_ref, delta_ref, kern_ref, den_ref):
    feat = feat_ref[...]
    h = jnp.sin(jnp.dot(feat, w1_ref[...], preferred_element_type=F32, precision=HIGHEST) + b1_ref[...])
    h = jnp.sin(jnp.dot(h, w2_ref[...], preferred_element_type=F32, precision=HIGHEST) + b2_ref[...])
    f = jnp.dot(h, w3_ref[...], preferred_element_type=F32, precision=HIGHEST)
    t = feat[:, 0:1]
    live = feat[:, LANE - 1:LANE]
    kern = f * (jnp.exp(-t * delta_ref[...]) + HY_SHIFT) * live
    kern_ref[...] = kern

    @pl.when((pl.program_id(0) == 0) & (pl.program_id(1) == 0))
    def _():
        den_ref[...] = jnp.zeros_like(den_ref)

    den_ref[...] += jnp.sum(jnp.abs(kern), axis=0, keepdims=True)


def hy_filters(l, w1, b1, w2, b2, w3, *, tr):
    t = jnp.linspace(0.0, 1.0, l, dtype=F32)[:, None]
    w = (2.0 * math.pi / l) * jnp.arange(l, dtype=F32)[:, None]
    bands = jnp.linspace(1e-4, HY_BANDS - 1.0, HY_BANDS, dtype=F32)[None, :]
    feats = jnp.concatenate([t, jnp.cos(bands * w), -jnp.sin(bands * w)], axis=-1)
    feats = jnp.pad(feats, ((0, 0), (0, LANE - 1 - HY_EMB)))
    feats = jnp.concatenate([feats, jnp.ones((l, 1), F32)], axis=-1)
    table = jnp.concatenate([feats, jnp.zeros((1, LANE), F32), feats[:0:-1]], axis=0)
    w1p = jnp.pad(w1.astype(F32), ((0, LANE - HY_EMB), (0, 0)))
    w3s = w3.astype(F32).reshape(HY_FFN, HY_ORDER, 2, HY_CH).transpose(0, 2, 1, 3).reshape(HY_FFN, -1)
    deltas = jnp.abs(jnp.linspace(HY_MIN_DECAY, HY_MAX_DECAY, HY_CH, dtype=F32))
    deltas = jnp.tile(deltas, HY_ORDER).reshape(1, -1)
    oc = HY_ORDER * HY_CH
    nb = l // tr
    const = lambda s, i: (0, 0)
    return pl.pallas_call(
        _hy_filter_kernel,
        grid=(2, nb),
        in_specs=[
            pl.BlockSpec((tr, LANE), lambda s, i: (s * nb + i, 0)),
            pl.BlockSpec((LANE, HY_FFN), const), pl.BlockSpec((1, HY_FFN), const),
            pl.BlockSpec((HY_FFN, HY_FFN), const), pl.BlockSpec((1, HY_FFN), const),
            pl.BlockSpec((HY_FFN, oc), lambda s, i: (0, s)),
            pl.BlockSpec((1, oc), const),
        ],
        out_specs=[pl.BlockSpec((tr, oc), lambda s, i: (s * nb + i, 0)), pl.BlockSpec((1, oc), const)],
        out_shape=[jax.ShapeDtypeStruct((2 * l, oc), F32), jax.ShapeDtypeStruct((1, oc), F32)],
        compiler_params=_cparams(("arbitrary", "arbitrary")),
        name="hy_filters",
    )(table, w1p, b1.reshape(1, -1).astype(F32), w2.astype(F32), b2.reshape(1, -1).astype(F32), w3s, deltas)


def _cis(num, den):
    ang = (-2.0 * math.pi / den) * (num % den).astype(F32)
    return jnp.cos(ang), jnp.sin(ang)


def _block_complex(re, im):
    return jnp.concatenate([jnp.concatenate([re, -im], axis=-1), jnp.concatenate([im, re], axis=-1)], axis=-2)


def _fft_tables(p):
    q = FFT_Q
    n = p * q
    i_p = jnp.arange(p, dtype=jnp.int32)
    i_q = jnp.arange(q, dtype=jnp.int32)
    fr, fi = _cis(i_p[:, None] * i_p[None, :], p)
    f_filt = jnp.concatenate([fr, fi], axis=0)
    f_in = _block_complex(fr[:, :p // 2], fi[:, :p // 2])
    f_out = _block_complex(fr[:p // 2, :], -fi[:p // 2, :]) * (1.0 / n)
    num = i_q[None, None, :] * i_p[:, None, None] + p * i_q[None, :, None] * i_q[None, None, :]
    gr, gi = _cis(num, n)
    g = _block_complex(gr, gi)
    h = _block_complex(jnp.swapaxes(gr, 1, 2), -jnp.swapaxes(gi, 1, 2))
    return tuple(a.astype(BF16) for a in (f_filt, f_in, g, h, f_out))


FFT_TC = 512


def _fft_in_kernel(f_ref, x_ref, o_ref, xs_ref):
    for s in range(x_ref.shape[1]):
        xs_ref[...] = x_ref[:, s, :]
        o_ref[:, s, :] = jnp.dot(f_ref[...], xs_ref[...].astype(BF16), preferred_element_type=F32)


def fft_in(f, x):
    k, q, c = x.shape
    tc = FFT_TC
    return pl.pallas_call(
        _fft_in_kernel,
        grid=(q // SUBLANE, c // tc),
        in_specs=[pl.BlockSpec(f.shape, lambda j, ci: (0, 0)),
                  pl.BlockSpec((k, SUBLANE, tc), lambda j, ci: (0, j, ci))],
        out_specs=pl.BlockSpec((f.shape[0], SUBLANE, tc), lambda j, ci: (0, j, ci)),
        out_shape=jax.ShapeDtypeStruct((f.shape[0], q, c), F32),
        scratch_shapes=[pltpu.VMEM((k, tc), F32)],
        compiler_params=_cparams(("parallel", "parallel")),
        name="fft_in",
    )(f, x)


def _fft_filt_mid_kernel(g_ref, a_ref, inv_ref, o_ref):
    q2, c = o_ref.shape[1], o_ref.shape[2]
    a = a_ref[...].reshape(q2, c).astype(BF16)
    o_ref[0] = jnp.dot(g_ref[0], a, preferred_element_type=F32) * inv_ref[...]


def fft_filt_mid(g, a, den):
    p = g.shape[0]
    c = den.shape[1]
    a = a.reshape(2, p, FFT_Q, c)
    return pl.pallas_call(
        _fft_filt_mid_kernel,
        grid=(p,),
        in_specs=[pl.BlockSpec((1, 2 * FFT_Q, 2 * FFT_Q), lambda k: (k, 0, 0)),
                  pl.BlockSpec((2, 1, FFT_Q, c), lambda k: (0, k, 0, 0)),
                  pl.BlockSpec((1, c), lambda k: (0, 0))],
        out_specs=pl.BlockSpec((1, 2 * FFT_Q, c), lambda k: (k, 0, 0)),
        out_shape=jax.ShapeDtypeStruct((p, 2 * FFT_Q, c), F32),
        compiler_params=_cparams(("parallel",)),
        name="fft_filt_mid",
    )(g, a, 1.0 / den)


def _cmul_rows(s, k):
    half = s.shape[0] // 2
    sr, si, kr, ki = s[:half], s[half:], k[:half], k[half:]
    return jnp.concatenate([sr * kr - si * ki, sr * ki + si * kr], axis=0)


def _fft_mid_kernel(g_ref, h_ref, a_ref, k_ref, o_ref):
    q2, c = k_ref.shape[1], k_ref.shape[2]
    a = a_ref[...].reshape(q2, c).astype(BF16)
    y = jnp.dot(g_ref[0], a, preferred_element_type=F32)
    z = _cmul_rows(y, k_ref[0]).astype(BF16)
    o_ref[...] = jnp.dot(h_ref[0], z, preferred_element_type=F32).reshape(o_ref.shape)


def fft_mid(g, h, a, kf, order):
    p = g.shape[0]
    c = HY_CH
    mat = pl.BlockSpec((1, 2 * FFT_Q, 2 * FFT_Q), lambda k: (k, 0, 0))
    sig = pl.BlockSpec((2, 1, FFT_Q, c), lambda k: (0, k, 0, 0))
    return pl.pallas_call(
        _fft_mid_kernel,
        grid=(p,),
        in_specs=[mat, mat, sig, pl.BlockSpec((1, 2 * FFT_Q, c), lambda k: (k, 0, order))],
        out_specs=sig,
        out_shape=jax.ShapeDtypeStruct((2, p, FFT_Q, c), F32),
        compiler_params=_cparams(("parallel",)),
        name="fft_mid",
    )(g, h, a.reshape(2, p, FFT_Q, c), kf).reshape(2 * p, FFT_Q, c)


def _fft_out_kernel(f_ref, c_ref, z_ref, gate_ref, bias_ref, o_ref, cs_ref, zs_ref, gs_ref):
    for s in range(c_ref.shape[1]):
        cs_ref[...] = c_ref[:, s, :]
        zs_ref[...] = z_ref[:, s, :]
        gs_ref[...] = gate_ref[:, s, :]
        y = jnp.dot(f_ref[...], cs_ref[...].astype(BF16), preferred_element_type=F32)
        o_ref[:, s, :] = gs_ref[...] * (y + zs_ref[...] * bias_ref[...])


def fft_out(f, cm, z, gate, bias):
    p2, q, c = cm.shape
    tc = FFT_TC
    rows = f.shape[0]
    half = pl.BlockSpec((rows, SUBLANE, tc), lambda j, ci: (0, j, ci))
    return pl.pallas_call(
        _fft_out_kernel,
        grid=(q // SUBLANE, c // tc),
        in_specs=[pl.BlockSpec(f.shape, lambda j, ci: (0, 0)),
                  pl.BlockSpec((p2, SUBLANE, tc), lambda j, ci: (0, j, ci)),
                  half, half,
                  pl.BlockSpec((1, tc), lambda j, ci: (0, ci))],
        out_specs=half,
        out_shape=jax.ShapeDtypeStruct(z.shape, F32),
        scratch_shapes=[pltpu.VMEM((p2, tc), F32), pltpu.VMEM((rows, tc), F32), pltpu.VMEM((rows, tc), F32)],
        compiler_params=_cparams(("parallel", "parallel")),
        name="fft_out",
    )(f, cm, z, gate, bias)


def _hy_ctx_kernel(ff_ref, fin_ref, fout_ref, kern_ref, inv_ref, z_ref, g1_ref, g2_ref, bias_ref, o_ref):
    b, l, c = z_ref.shape
    z = z_ref[...].reshape(b * l, c)
    gates = (g1_ref, g2_ref)
    for n in range(HY_ORDER):
        cols = slice(n * c, (n + 1) * c)
        kf = (jnp.dot(ff_ref[...], kern_ref[:, cols].astype(BF16), preferred_element_type=F32)
              * inv_ref[:, cols])
        s = jnp.dot(fin_ref[...], z.astype(BF16), preferred_element_type=F32)
        y = jnp.dot(fout_ref[...], _cmul_rows(s, kf).astype(BF16), preferred_element_type=F32)
        z = gates[n][...].reshape(b * l, c) * (y + z * bias_ref[n:n + 1, :])
    o_ref[...] = z.reshape(b, l, c).astype(o_ref.dtype)


def hy_ctx_conv(kern, den, z0, g1, g2, bias):
    b, l, c = z0.shape
    assert b == 2
    n = 2 * l
    i_n = jnp.arange(n, dtype=jnp.int32)
    fr, fi = _cis(i_n[:, None] * i_n[None, :], n)
    ff = jnp.concatenate([fr, fi], axis=0).astype(BF16)
    fin = _block_complex(fr[:, :l], fi[:, :l]).astype(BF16)
    fout = (_block_complex(fr[:l, :], -fi[:l, :]) * (1.0 / n)).astype(BF16)
    return pl.pallas_call(
        _hy_ctx_kernel,
        out_shape=jax.ShapeDtypeStruct((b, l, c), BF16),
        compiler_params=pltpu.CompilerParams(vmem_limit_bytes=VMEM_LIMIT),
        name="hy_ctx_conv",
    )(ff, fin, fout, kern, 1.0 / den, z0, g1, g2, bias)


def hyena_latent(z, conv_w, conv_b, w1, b1, w2, b2, w3, bias):
    b, l, _ = z.shape
    assert b == 2 and (2 * l) % (2 * FFT_Q) == 0
    p = 2 * l // FFT_Q
    z0, *gates = hy_short_conv(z, conv_w, conv_b, tm=256)
    kern, den = hy_filters(l, w1, b1, w2, b2, w3, tr=512)
    f_filt, f_in, g, h, f_out = _fft_tables(p)
    kf = fft_filt_mid(g, fft_in(f_filt, kern.reshape(p, FFT_Q, HY_ORDER * HY_CH)), den)
    split = lambda t: t.reshape(p, FFT_Q, HY_CH)
    zz = split(z0)
    for n in range(HY_ORDER):
        cm = fft_mid(g, h, fft_in(f_in, zz), kf, n)
        zz = fft_out(f_out, cm, zz, split(gates[n]), bias[n].astype(F32).reshape(1, HY_CH))
    return zz.reshape(b, l, HY_CH)


def hyena_ctx(z, conv_w, conv_b, w1, b1, w2, b2, w3, bias):
    b, l, _ = z.shape
    z0, g1, g2 = hy_short_conv(z, conv_w, conv_b, tm=l)
    kern, den = hy_filters(l, w1, b1, w2, b2, w3, tr=l)
    return hy_ctx_conv(kern, den, z0, g1, g2, bias.astype(F32))


def even_mixer_core(zc, zl, gla_w_gate, gla_b_gate, gla_norm, hy_conv_w, hy_conv_b,
                    hy_w1, hy_b1, hy_w2, hy_b2, hy_w3, hy_bias):
    gc, gl = gla_bidir(zc, zl, gla_w_gate, gla_b_gate, gla_norm)
    hy = (hy_conv_w, hy_conv_b, hy_w1, hy_b1, hy_w2, hy_b2, hy_w3, hy_bias)
    return (gc, hyena_ctx(zc, *hy)), (gl, hyena_latent(zl, *hy))


def odd_mixer_core(zc, zl, ret_norm, att_q_norm, att_k_norm, att_sink):
    _, rl = retention_bidir(zc, zl, ret_norm)
    qkv_l = att_prep(zl, att_q_norm, att_k_norm, _axial_tables(zl.shape[1]), tm=512)
    qkv_c = att_prep(zc, att_q_norm, att_k_norm, None, tm=zc.shape[1])
    return rl, window_attention(qkv_l, qkv_c, att_sink)


def kernel(x, c, ctx, c_ctx, ada_w, ada_b, norm_mix, norm_ffn, ev_w_in, ev_w_out, gla_w_gate, gla_b_gate, gla_norm, hy_conv_w, hy_conv_b, hy_w1, hy_b1, hy_w2, hy_b2, hy_w3, hy_bias, od_w_in, od_w_out, ret_norm, att_q_norm, att_k_norm, att_sink, router_w, router_b, moe_w1, moe_w3, moe_w2):
    assert DEPTH == 2
    bsz, seq, d = x.shape
    n_ctx = ctx.shape[1]
    c_pad = jnp.zeros((8, d), F32).at[:bsz].set(c).at[bsz].set(c_ctx)
    mods = ada_modulation(c_pad, ada_w, ada_b)
    slot = np.array([SUBLANE * (e % EXPERTS_PER_GROUP) + e // EXPERTS_PER_GROUP for e in range(N_EXPERTS)])
    rw_pad = jnp.zeros((d, LOGIT_PAD), F32).at[:, slot].set(router_w)
    rw_hi = rw_pad.astype(BF16)
    rw_pad = jnp.concatenate([rw_hi, (rw_pad - rw_hi.astype(F32)).astype(BF16)], axis=1)
    rb_pad = jnp.zeros((EXPERTS_PER_GROUP * SUBLANE, LANE), F32).at[slot].set(
        jnp.broadcast_to(router_b.astype(F32)[:, None], (N_EXPERTS, LANE)))

    xl, xc = x, ctx
    for layer in range(DEPTH):
        last = layer == DEPTH - 1
        ml = mods[layer, :bsz].reshape(bsz, 1, 6, d)
        mc = jnp.broadcast_to(mods[layer, bsz].reshape(1, 1, 6, d), (bsz, 1, 6, d))
        sh1, sc1, g1, sh2, sc2, g2 = [ml[:, :, i] for i in range(6)]
        csh1, csc1, cg1, csh2, csc2, cg2 = [mc[:, :, i] for i in range(6)]
        i = layer // 2
        if layer % 2 == 0:
            w_in, w_out = ev_w_in[i], ev_w_out[i]
            lr0 = sum(EVEN_SPLITS[:4])
            lr1 = lr0 + EVEN_SPLITS[4]
            w_in = jnp.concatenate([w_in[:, :lr0], w_in[:, lr1:], w_in[:, lr0:lr1],
                                    jnp.zeros((d, EVEN_COLS_PAD - w_in.shape[1]), F32)], axis=1)
            tn = EVEN_COLS_PAD // 7
        else:
            w_in, w_out = od_w_in[i], od_w_out[i]
            tn = w_in.shape[1] // 6
        zl = norm_mod_matmul(xl, norm_mix[layer], sc1, sh1, w_in, tm=1024, tn=tn, out_dtype=F32)
        zc = norm_mod_matmul(xc, norm_mix[layer], csc1, csh1, w_in, tm=n_ctx, tn=tn, out_dtype=F32)
        if layer % 2 == 0:
            mixc, mixl = even_mixer_core(zc, zl, gla_w_gate[i], gla_b_gate[i], gla_norm[i], hy_conv_w[i],
                                         hy_conv_b[i], hy_w1[i], hy_b1[i], hy_w2[i], hy_b2[i], hy_w3[i],
                                         hy_bias[i])
        else:
            mixl = odd_mixer_core(zc, zl, ret_norm[i], att_q_norm[i], att_k_norm[i], att_sink[i])
        xl, hl, rtl = outproj_residual_norm(*mixl, w_out, xl, g1, norm_ffn[layer], sc2, sh2, rw_pad, rb_pad,
                                            tm=512)
        rtl = rtl.transpose(1, 0, 2).reshape(ROUTE_ROWS, -1)
        if not last:
            xc, hc, rtc = outproj_residual_norm(*mixc, w_out, xc, cg1, norm_ffn[layer], csc2, csh2, rw_pad,
                                                rb_pad, tm=n_ctx)
            rtc = rtc.transpose(1, 0, 2).reshape(ROUTE_ROWS, -1)
            h = jnp.concatenate([hc.reshape(-1, d), hl.reshape(-1, d)], axis=0)
            y = moe_apply(h, jnp.concatenate([rtc, rtl], axis=1), moe_w1, moe_w3, moe_w2, layer)
            yc = y[:bsz * n_ctx].reshape(bsz, n_ctx, d)
            yl = y[bsz * n_ctx:].reshape(bsz, seq, d)
            xc = xc + cg2 * yc
            xl = xl + g2 * yl
        else:
            y = moe_apply(hl.reshape(-1, d), rtl, moe_w1, moe_w3, moe_w2, layer)
            xl = xl + g2 * y.reshape(bsz, seq, d)
    return xl
```

```python
import functools
import math

import jax
import jax.numpy as jnp
import numpy as np
from jax import lax
from jax.experimental import pallas as pl
from jax.experimental.pallas import tpu as pltpu

D_MODEL = 2048
BATCH = 2
SEQ = 8192
DEPTH = 2
GRID_W = 64
CTX_LEN = 256
MIX_WIDTH = D_MODEL
GROUP_WIDTH = MIX_WIDTH // 2
EPS = 1e-6
F32 = jnp.float32
BF16 = jnp.bfloat16

GLA_HEADS = 4
GLA_DK = GROUP_WIDTH // (2 * GLA_HEADS)
GLA_DV = GROUP_WIDTH // GLA_HEADS
GLA_LOWRANK = 16
GLA_GATE_NORMALIZER = 16.0
GLA_CHUNK = 64

HY_CH = GROUP_WIDTH
HY_ORDER = 2
HY_BANDS = 16
HY_EMB = 1 + 2 * HY_BANDS
HY_FFN = 64
HY_SHIFT = 0.05
HY_MIN_DECAY = math.log(1e-2) / 1.5
HY_MAX_DECAY = math.log(1e-2) / 0.3

RET_HEADS = 4
RET_DK = GROUP_WIDTH // (2 * RET_HEADS)
RET_DV = GROUP_WIDTH // RET_HEADS
RET_CHUNK = 64

ATT_HD = 128
ATT_HEADS = GROUP_WIDTH // ATT_HD
ATT_KV_HEADS = 2
ATT_GROUP = ATT_HEADS // ATT_KV_HEADS
ATT_WINDOW = 128
ATT_BLOCK = 128
ROPE_BASE = 10000.0

N_EXPERTS = 16
N_GROUPS = 4
EXPERTS_PER_GROUP = N_EXPERTS // N_GROUPS
TOP_K = 2
D_FF_EXPERT = D_MODEL // 2

EVEN_SPLITS = (GLA_HEADS * GLA_DK, GLA_HEADS * GLA_DK, GLA_HEADS * GLA_DV, GLA_HEADS * GLA_DV,
               2 * GLA_LOWRANK, (HY_ORDER + 1) * HY_CH)
ODD_SPLITS = (RET_HEADS * RET_DK, RET_HEADS * RET_DK, RET_HEADS * RET_DV, RET_HEADS * RET_DV,
              ATT_HEADS * ATT_HD, ATT_KV_HEADS * ATT_HD, ATT_KV_HEADS * ATT_HD)

LANE = 128
EVEN_COLS_PAD = -(-sum(EVEN_SPLITS) // (7 * LANE)) * (7 * LANE)
VMEM_LIMIT = 56 * 1024 * 1024
MOE_TM = 512
MOE_FF_SPLIT = 2
LOGIT_PAD = LANE


def _cparams(sem):
    return pltpu.CompilerParams(dimension_semantics=sem, vmem_limit_bytes=VMEM_LIMIT)


def _ada_kernel(c_ref, w_ref, b_ref, o_ref):
    c = c_ref[...]
    s = (c * jax.nn.sigmoid(c)).astype(BF16)
    o_ref[0] = jnp.dot(s, w_ref[0].astype(BF16), preferred_element_type=F32) + b_ref[0]


def ada_modulation(c_pad, ada_w, ada_b):
    depth, d, n = ada_w.shape
    tn = 1024
    return pl.pallas_call(
        _ada_kernel,
        grid=(depth, n // tn),
        in_specs=[
            pl.BlockSpec((8, d), lambda l, j: (0, 0)),
            pl.BlockSpec((1, d, tn), lambda l, j: (l, 0, j)),
            pl.BlockSpec((1, 1, tn), lambda l, j: (l, 0, j)),
        ],
        out_specs=pl.BlockSpec((1, 8, tn), lambda l, j: (l, 0, j)),
        out_shape=jax.ShapeDtypeStruct((depth, 8, n), F32),
        compiler_params=_cparams(("parallel", "arbitrary")),
        name="ada_modulation",
    )(c_pad, ada_w, ada_b.reshape(depth, 1, n))


def _norm_mod(x, gain, sc, sh):
    ms = jnp.mean(x * x, axis=-1, keepdims=True)
    return (x * lax.rsqrt(ms + EPS) * gain) * (1.0 + sc) + sh


def _inproj_kernel(x_ref, gain_ref, sc_ref, sh_ref, w_ref, o_ref, h_ref):
    @pl.when(pl.program_id(2) == 0)
    def _():
        h_ref[...] = _norm_mod(x_ref[0], gain_ref[...], sc_ref[0], sh_ref[0]).astype(BF16)

    o_ref[0] = jnp.dot(h_ref[...], w_ref[...].astype(BF16), preferred_element_type=F32).astype(o_ref.dtype)


def norm_mod_matmul(x, gain, sc, sh, w, *, tm, tn, out_dtype):
    b, l, d = x.shape
    n = w.shape[1]
    return pl.pallas_call(
        _inproj_kernel,
        grid=(b, l // tm, n // tn),
        in_specs=[
            pl.BlockSpec((1, tm, d), lambda bi, i, j: (bi, i, 0)),
            pl.BlockSpec((1, d), lambda bi, i, j: (0, 0)),
            pl.BlockSpec((1, 1, d), lambda bi, i, j: (bi, 0, 0)),
            pl.BlockSpec((1, 1, d), lambda bi, i, j: (bi, 0, 0)),
            pl.BlockSpec((d, tn), lambda bi, i, j: (0, j)),
        ],
        out_specs=pl.BlockSpec((1, tm, tn), lambda bi, i, j: (bi, i, j)),
        out_shape=jax.ShapeDtypeStruct((b, l, n), out_dtype),
        scratch_shapes=[pltpu.VMEM((tm, d), BF16)],
        compiler_params=_cparams(("parallel", "parallel", "arbitrary")),
        name="norm_mod_matmul",
    )(x, gain.reshape(1, d), sc, sh, w)


SUBLANE = 8
ROUTE_ROWS = SUBLANE


def _route_block(lg, rb_ref):
    tm = lg.shape[0]
    t = lg.T
    sub = lax.broadcasted_iota(jnp.int32, (SUBLANE, tm), 0)
    valid = sub < N_GROUPS
    members = range(EXPERTS_PER_GROUP)
    s = [jnp.where(valid, t[SUBLANE * j:SUBLANE * (j + 1), :], -jnp.inf) for j in members]
    mx = jnp.max(jnp.maximum(jnp.maximum(s[0], s[1]), jnp.maximum(s[2], s[3])), axis=0, keepdims=True)
    e = [jnp.exp(sj - mx) for sj in s]
    den = jnp.sum((e[0] + e[1]) + (e[2] + e[3]), axis=0, keepdims=True)
    sc = [ej / den for ej in e]
    sel = [sc[j] + jnp.concatenate([rb_ref[SUBLANE * j:SUBLANE * (j + 1), :]] * (tm // LANE), axis=1)
           for j in members]
    hi01, lo01 = jnp.maximum(sel[0], sel[1]), jnp.minimum(sel[0], sel[1])
    hi23, lo23 = jnp.maximum(sel[2], sel[3]), jnp.minimum(sel[2], sel[3])
    top1 = jnp.maximum(hi01, hi23)
    top2 = jnp.maximum(jnp.minimum(hi01, hi23), jnp.maximum(lo01, lo23))
    gscore = jnp.where(valid, top1 + top2, -jnp.inf)
    gmax = jnp.max(gscore, axis=0, keepdims=True)
    gidx = jnp.min(jnp.where(gscore == gmax, sub, SUBLANE), axis=0, keepdims=True)
    chosen = sub == gidx
    v = [jnp.sum(jnp.where(chosen, sel[j], 0.0), axis=0, keepdims=True) for j in members]
    p = [jnp.sum(jnp.where(chosen, sc[j], 0.0), axis=0, keepdims=True) for j in members]
    one = lambda cond: jnp.where(cond, 1.0, 0.0)
    picks = []
    for want in (0.0, 1.0):
        idx = jnp.zeros_like(v[0])
        wt = jnp.zeros_like(v[0])
        for j in members:
            rank = sum(one(v[k] > v[j]) for k in members if k != j) + sum(one(v[k] == v[j]) for k in range(j))
            hit = rank == want
            idx = idx + jnp.where(hit, float(j), 0.0)
            wt = wt + jnp.where(hit, p[j], 0.0)
        picks.append((idx, wt))
    (j1, p1), (j2, p2) = picks
    base = (EXPERTS_PER_GROUP * gidx).astype(F32)
    tot = p1 + p2
    rows = [base + j1, base + j2, p1 / tot, p2 / tot]
    return jnp.concatenate(rows + [jnp.zeros((ROUTE_ROWS - len(rows), tm), F32)], axis=0)


def _outproj_kernel(m1_ref, m2_ref, w_ref, x_ref, g_ref, gain_ref, sc_ref, sh_ref, rw_ref, rb_ref,
                    xo_ref, h_ref, rt_ref):
    k1 = m1_ref.shape[-1]
    o = (jnp.dot(m1_ref[0].astype(BF16), w_ref[:k1, :].astype(BF16), preferred_element_type=F32)
         + jnp.dot(m2_ref[0].astype(BF16), w_ref[k1:, :].astype(BF16), preferred_element_type=F32))
    xn = x_ref[0] + g_ref[0] * o
    xo_ref[0] = xn
    h = _norm_mod(xn, gain_ref[...], sc_ref[0], sh_ref[0])
    h_ref[...] = h
    h_hi = h.astype(BF16)
    h_lo = (h - h_hi.astype(F32)).astype(BF16)
    both = jnp.dot(h_hi, rw_ref[...], preferred_element_type=F32)
    lg = (both[:, :LOGIT_PAD] + both[:, LOGIT_PAD:]
          + jnp.dot(h_lo, rw_ref[:, :LOGIT_PAD], preferred_element_type=F32))
    rt_ref[0] = _route_block(lg, rb_ref)


def outproj_residual_norm(m1, m2, w, x, g, gain, sc, sh, rw_pad, rb_pad, *, tm):
    b, l, d = x.shape
    k1, k2 = m1.shape[-1], m2.shape[-1]
    nb = l // tm
    row = lambda bi, i: (bi, i, 0)
    mod = lambda bi, i: (bi, 0, 0)
    const = lambda bi, i: (0, 0)
    return pl.pallas_call(
        _outproj_kernel,
        grid=(b, nb),
        in_specs=[
            pl.BlockSpec((1, tm, k1), row),
            pl.BlockSpec((1, tm, k2), row),
            pl.BlockSpec((k1 + k2, d), const, pipeline_mode=pl.Buffered(1)),
            pl.BlockSpec((1, tm, d), row),
            pl.BlockSpec((1, 1, d), mod),
            pl.BlockSpec((1, d), const),
            pl.BlockSpec((1, 1, d), mod),
            pl.BlockSpec((1, 1, d), mod),
            pl.BlockSpec((d, 2 * LOGIT_PAD), const),
            pl.BlockSpec(rb_pad.shape, const),
        ],
        out_specs=[
            pl.BlockSpec((1, tm, d), row),
            pl.BlockSpec((tm, d), lambda bi, i: (bi * nb + i, 0)),
            pl.BlockSpec((1, ROUTE_ROWS, tm), lambda bi, i: (bi, 0, i)),
        ],
        out_shape=[
            jax.ShapeDtypeStruct((b, l, d), F32),
            jax.ShapeDtypeStruct((b * l, d), F32),
            jax.ShapeDtypeStruct((b, ROUTE_ROWS, l), F32),
        ],
        compiler_params=_cparams(("parallel", "parallel")),
        name="outproj_residual_norm",
    )(m1, m2, w, x, g, gain.reshape(1, d), sc, sh, rw_pad, rb_pad)


def _moe_kernel(te_ref, tsrc_ref, tvalid_ref, x_ref, w1_ref, w3_ref, w2_ref, o_ref):
    i = pl.program_id(0)

    @pl.when(tvalid_ref[i] > 0)
    def _():
        x = x_ref[...].astype(BF16)
        ff = w1_ref.shape[-1]
        fc = ff // MOE_FF_SPLIT
        y = None
        for c in range(MOE_FF_SPLIT):
            cols = slice(c * fc, (c + 1) * fc)
            a = jnp.dot(x, w1_ref[0, 0, :, cols].astype(BF16), preferred_element_type=F32)
            b = jnp.dot(x, w3_ref[0, 0, :, cols].astype(BF16), preferred_element_type=F32)
            mid = (a * jax.nn.sigmoid(a) * b).astype(BF16)
            part = jnp.dot(mid, w2_ref[0, 0, cols, :].astype(BF16), preferred_element_type=F32)
            y = part if y is None else y + part
        o_ref[...] = y

    @pl.when(tvalid_ref[i] == 0)
    def _():
        o_ref[...] = jnp.zeros_like(o_ref)


def moe_grouped(x_sorted, w1, w3, w2, layer, tile_expert, tile_src, tile_valid):
    p, d = x_sorted.shape
    ff = w1.shape[-1]
    tm = MOE_TM
    wmap = lambda i, te, ts, tv: (layer, te[i], 0, 0)
    grid_spec = pltpu.PrefetchScalarGridSpec(
        num_scalar_prefetch=3,
        grid=(p // tm,),
        in_specs=[
            pl.BlockSpec((tm, d), lambda i, te, ts, tv: (ts[i], 0)),
            pl.BlockSpec((1, 1, d, ff), wmap, pipeline_mode=pl.Buffered(1)),
            pl.BlockSpec((1, 1, d, ff), wmap, pipeline_mode=pl.Buffered(1)),
            pl.BlockSpec((1, 1, ff, d), wmap, pipeline_mode=pl.Buffered(1)),
        ],
        out_specs=pl.BlockSpec((tm, d), lambda i, te, ts, tv: (i, 0)),
    )
    return pl.pallas_call(
        _moe_kernel,
        grid_spec=grid_spec,
        out_shape=jax.ShapeDtypeStruct((p, d), F32),
        compiler_params=_cparams(("arbitrary",)),
        name="moe_grouped",
    )(tile_expert, tile_src, tile_valid, x_sorted, w1, w3, w2)


MOVE_ROWS = 1024


def _row_move_kernel(src_ref, dst_ref, tab_ref, *rest):
    out_ref, sem = rest[-2:]
    n = src_ref.shape[-1]

    def issue(r, carry):
        pltpu.make_async_copy(tab_ref.at[pl.ds(src_ref[0, 0, r], 1)], out_ref.at[pl.ds(dst_ref[0, 0, r], 1)],
                              sem).start()
        return carry

    lax.fori_loop(0, n, issue, 0, unroll=4)
    pltpu.make_async_copy(out_ref.at[pl.ds(0, n)], out_ref.at[pl.ds(0, n)], sem).wait()


def row_move(table, src, dst, out_rows, init=None):
    m = src.shape[0]
    d = table.shape[1]
    steps = m // MOVE_ROWS
    assert steps * MOVE_ROWS == m and out_rows >= MOVE_ROWS
    idx_spec = pl.BlockSpec((1, 1, MOVE_ROWS), lambda i: (i, 0, 0), memory_space=pltpu.SMEM)
    any_spec = pl.BlockSpec(memory_space=pl.ANY)
    ins = [src.reshape(steps, 1, MOVE_ROWS), dst.reshape(steps, 1, MOVE_ROWS), table]
    if init is not None:
        ins.append(init)
    return pl.pallas_call(
        _row_move_kernel,
        grid=(steps,),
        in_specs=[idx_spec, idx_spec] + [any_spec] * (len(ins) - 2),
        out_specs=any_spec,
        out_shape=jax.ShapeDtypeStruct((out_rows, d), table.dtype),
        scratch_shapes=[pltpu.SemaphoreType.DMA(())],
        input_output_aliases={3: 0} if init is not None else {},
        compiler_params=pltpu.CompilerParams(dimension_semantics=("arbitrary",)),
        name="row_move",
    )(*ins)


def _combine_kernel(x_ref, g_ref, y0_ref, y1_ref, w0_ref, w1_ref, o_ref):
    y = w0_ref[...] * y0_ref[...] + w1_ref[...] * y1_ref[...]
    o_ref[0] = x_ref[0] + g_ref[0] * y


def moe_combine(x, g, y_tok, w_tok, row0, *, tm):
    b, l, d = x.shape
    n = y_tok.shape[0] // TOP_K
    nb = l // tm
    assert row0 % tm == 0 and n % tm == 0
    pick = lambda k: (lambda bi, i: ((k * n + row0) // tm + bi * nb + i, 0))
    return pl.pallas_call(
        _combine_kernel,
        grid=(b, nb),
        in_specs=[pl.BlockSpec((1, tm, d), lambda bi, i: (bi, i, 0)),
                  pl.BlockSpec((1, 1, d), lambda bi, i: (bi, 0, 0)),
                  pl.BlockSpec((tm, d), pick(0)), pl.BlockSpec((tm, d), pick(1)),
                  pl.BlockSpec((tm, 1), pick(0)), pl.BlockSpec((tm, 1), pick(1))],
        out_specs=pl.BlockSpec((1, tm, d), lambda bi, i: (bi, i, 0)),
        out_shape=jax.ShapeDtypeStruct(x.shape, x.dtype),
        compiler_params=_cparams(("parallel", "parallel")),
        name="moe_combine",
    )(x, g, y_tok, y_tok, w_tok, w_tok)


CHUNK = 64
SCAN_T = 256
_NT = (((1,), (1,)), ((), ()))
_TN = (((0,), (0,)), ((), ()))
HIGHEST = lax.Precision.HIGHEST


def _chunk_keep(reverse):
    r = lax.broadcasted_iota(jnp.int32, (CHUNK, CHUNK), 0)
    c = lax.broadcasted_iota(jnp.int32, (CHUNK, CHUNK), 1)
    return (c >= r) if reverse else (c <= r)


def _chunk_step(q, k, v, e_q, e_k, e_s, e_last, keep, st):
    qe = (q * e_q).astype(BF16)
    ke = (k * e_k).astype(BF16)
    kd = (k * e_s).astype(BF16)
    vb = v.astype(BF16)
    att = lax.dot_general(qe, ke, _NT, preferred_element_type=F32)
    att = jnp.where(keep, att, 0.0).astype(BF16)
    o = (jnp.dot(att, vb, preferred_element_type=F32)
         + lax.dot_general(qe, st.astype(BF16), _NT, preferred_element_type=F32))
    st = e_last * st + lax.dot_general(vb, kd, _TN, preferred_element_type=F32)
    return o, st


def _merge_store(o, ob_ref, g_ref, gn_ref, o_ref, sl):
    tot = o + ob_ref[0, sl, :]
    y = tot * lax.rsqrt(jnp.mean(tot * tot, axis=-1, keepdims=True) + EPS) * gn_ref[...]
    g = g_ref[0, sl, :]
    o_ref[0, sl, :] = (y * (g * jax.nn.sigmoid(g))).astype(o_ref.dtype)


def _chunk_order(n, reverse):
    return range(n - 1, -1, -1) if reverse else range(n)


def _gla_kernel(qc, kc, vc, lrc, ql, kl, vl, lrl, wg, bg, *rest, reverse, merge):
    if merge:
        obc, obl, gc, gl, gn, oc_ref, ol_ref, st_ref = rest
    else:
        oc_ref, ol_ref, st_ref = rest
        obc = obl = gc = gl = gn = None
    keep = _chunk_keep(reverse)
    tri = keep.astype(F32)

    def run(q_ref, k_ref, v_ref, lr_ref, o_ref, ob_ref, g_ref):
        pre = jnp.dot(lr_ref[0], wg[...], preferred_element_type=F32, precision=HIGHEST) + bg[...]
        la = (jnp.minimum(pre, 0.0) - jnp.log(1.0 + jnp.exp(-jnp.abs(pre)))) * (1.0 / GLA_GATE_NORMALIZER)
        st = st_ref[...]
        for ci in _chunk_order(q_ref.shape[1] // CHUNK, reverse):
            sl = pl.ds(ci * CHUNK, CHUNK)
            cum = jnp.dot(tri, la[ci * CHUNK:(ci + 1) * CHUNK], preferred_element_type=F32,
                          precision=HIGHEST)
            last = cum[0:1] if reverse else cum[CHUNK - 1:CHUNK]
            o, st = _chunk_step(q_ref[0, sl, :] * (GLA_DK ** -0.5), k_ref[0, sl, :], v_ref[0, sl, :],
                                jnp.exp(cum), jnp.exp(-cum), jnp.exp(last - cum), jnp.exp(last), keep, st)
            if merge:
                _merge_store(o, ob_ref, g_ref, gn, o_ref, sl)
            else:
                o_ref[0, sl, :] = o
        st_ref[...] = st

    @pl.when(pl.program_id(2) == 0)
    def _():
        st_ref[...] = jnp.zeros_like(st_ref)
        run(qc, kc, vc, lrc, oc_ref, obc, gc)

    @pl.when(pl.program_id(2) > 0)
    def _():
        run(ql, kl, vl, lrl, ol_ref, obl, gl)


def _ret_kernel(qc, kc, vc, ql, kl, vl, cos_ref, sin_ref, cum_ref, *rest, reverse, merge):
    if merge:
        obc, obl, gc, gl, gn, oc_ref, ol_ref, st_ref = rest
    else:
        oc_ref, ol_ref, st_ref = rest
        obc = obl = gc = gl = gn = None
    keep = _chunk_keep(reverse)
    cum = cum_ref[0]
    last = cum[0:1] if reverse else cum[CHUNK - 1:CHUNK]
    e_q, e_k, e_s, e_last = jnp.exp(cum), jnp.exp(-cum), jnp.exp(last - cum), jnp.exp(last)

    def run(q_ref, k_ref, v_ref, o_ref, ob_ref, g_ref, latent):
        st = st_ref[...]
        for ci in _chunk_order(q_ref.shape[1] // CHUNK, reverse):
            sl = pl.ds(ci * CHUNK, CHUNK)
            q = q_ref[0, sl, :]
            k = k_ref[0, sl, :]
            if latent:
                cs = cos_ref[sl, :]
                sn = sin_ref[sl, :]
                q = q * cs + pltpu.roll(q, RET_DK // 2, 1) * sn
                k = k * cs + pltpu.roll(k, RET_DK // 2, 1) * sn
            o, st = _chunk_step(q, k * (RET_DK ** -0.5), v_ref[0, sl, :], e_q, e_k, e_s, e_last, keep, st)
            if merge:
                _merge_store(o, ob_ref, g_ref, gn, o_ref, sl)
            else:
                o_ref[0, sl, :] = o
        st_ref[...] = st

    @pl.when(pl.program_id(2) == 0)
    def _():
        st_ref[...] = jnp.zeros_like(st_ref)
        run(qc, kc, vc, oc_ref, obc, gc, False)

    @pl.when(pl.program_id(2) > 0)
    def _():
        run(ql, kl, vl, ol_ref, obl, gl, True)


def _scan_call(body, zc, zl, heads, dk, dv, col_q, col_k, col_v, shared_cols, extra_specs, merge_in,
               *, reverse, name):
    b, lc, _ = zc.shape
    l = zl.shape[1]
    t = SCAN_T
    nblk = l // t
    if reverse:
        blk = lambda i: jnp.where(i == 0, nblk - 1, nblk - i)
    else:
        blk = lambda i: jnp.maximum(i - 1, 0)

    def ctx_spec(w, col, per_head=True):
        return pl.BlockSpec((1, lc, w), lambda bi, h, i: (bi, 0, col // w + (h if per_head else 0)))

    def lat_spec(w, col, per_head=True):
        return pl.BlockSpec((1, t, w), lambda bi, h, i: (bi, blk(i), col // w + (h if per_head else 0)))

    ins = [zc, zc, zc] + [zc] * len(shared_cols) + [zl, zl, zl] + [zl] * len(shared_cols)
    specs = [ctx_spec(dk, col_q), ctx_spec(dk, col_k), ctx_spec(dv, col_v)]
    specs += [ctx_spec(w, col, False) for w, col in shared_cols]
    specs += [lat_spec(dk, col_q), lat_spec(dk, col_k), lat_spec(dv, col_v)]
    specs += [lat_spec(w, col, False) for w, col in shared_cols]
    for a, s in extra_specs:
        ins.append(a)
        specs.append(s)
    if merge_in is not None:
        obc, obl, col_g, gain = merge_in
        ins += [obc, obl, zc, zl, gain.reshape(1, dv)]
        specs += [ctx_spec(dv, 0), lat_spec(dv, 0), ctx_spec(dv, col_g), lat_spec(dv, col_g),
                  pl.BlockSpec((1, dv), lambda bi, h, i: (0, 0))]
        out_dtype = BF16
    else:
        out_dtype = F32
    return pl.pallas_call(
        functools.partial(body, reverse=reverse, merge=merge_in is not None),
        grid=(b, heads, 1 + nblk),
        in_specs=specs,
        out_specs=[ctx_spec(dv, 0), lat_spec(dv, 0)],
        out_shape=[jax.ShapeDtypeStruct((b, lc, heads * dv), out_dtype),
                   jax.ShapeDtypeStruct((b, l, heads * dv), out_dtype)],
        scratch_shapes=[pltpu.VMEM((dv, dk), F32)],
        compiler_params=_cparams(("parallel", "parallel", "arbitrary")),
        name=name,
    )(*ins)


EV_COL = dict(q=0, k=512, v=1024, g=2048, u=3072, lr=6144)
OD_COL = dict(q=0, k=512, v=1024, g=2048, aq=3072, ak=4096, av=4352)


def gla_bidir(zc, zl, gla_w_gate, gla_b_gate, gla_norm):
    hk = GLA_HEADS * GLA_DK
    outs = None
    for d, reverse in ((1, True), (0, False)):
        wg = jnp.zeros((LANE, hk), F32).at[d * GLA_LOWRANK:(d + 1) * GLA_LOWRANK].set(gla_w_gate[d])
        shared_cols = [(LANE, EV_COL["lr"])]
        extra_specs = [(wg, pl.BlockSpec((LANE, GLA_DK), lambda bi, h, i: (0, h))),
                       (gla_b_gate[d].reshape(1, hk), pl.BlockSpec((1, GLA_DK), lambda bi, h, i: (0, h)))]
        merge_in = None if outs is None else (outs[0], outs[1], EV_COL["g"], gla_norm)
        outs = _scan_call(_gla_kernel, zc, zl, GLA_HEADS, GLA_DK, GLA_DV, EV_COL["q"], EV_COL["k"],
                          EV_COL["v"], shared_cols, extra_specs, merge_in, reverse=reverse,
                          name="gla_bwd" if reverse else "gla_fwd_merge")
    return outs


def retention_bidir(zc, zl, ret_norm):
    l = zl.shape[1]
    inv = ROPE_BASE ** (-jnp.linspace(0.0, 1.0, RET_DK // 2, dtype=F32))
    ang = jnp.arange(l, dtype=F32)[:, None] * inv[None, :]
    cos2 = jnp.concatenate([jnp.cos(ang), jnp.cos(ang)], axis=-1)
    sin2 = jnp.concatenate([-jnp.sin(ang), jnp.sin(ang)], axis=-1)
    log_g_f = jnp.log(1.0 - 2.0 ** (-5.0 - jnp.arange(RET_HEADS, dtype=F32)))
    idx = jnp.arange(CHUNK, dtype=F32)
    nblk = l // SCAN_T
    outs = None
    for reverse in (True, False):
        if reverse:
            cum = (CHUNK - idx)[None, :, None] * log_g_f[::-1][:, None, None]
            blk = lambda i: jnp.where(i == 0, nblk - 1, nblk - i)
        else:
            cum = (idx + 1.0)[None, :, None] * log_g_f[:, None, None]
            blk = lambda i: jnp.maximum(i - 1, 0)
        cum = jnp.broadcast_to(cum, (RET_HEADS, CHUNK, RET_DK))
        tab_spec = pl.BlockSpec((SCAN_T, RET_DK), lambda bi, h, i, blk=blk: (blk(i), 0))
        extra_specs = [(cos2, tab_spec), (sin2, tab_spec),
                       (cum, pl.BlockSpec((1, CHUNK, RET_DK), lambda bi, h, i: (h, 0, 0)))]
        merge_in = None if outs is None else (outs[0], outs[1], OD_COL["g"], ret_norm)
        outs = _scan_call(_ret_kernel, zc, zl, RET_HEADS, RET_DK, RET_DV, OD_COL["q"], OD_COL["k"],
                          OD_COL["v"], [], extra_specs, merge_in, reverse=reverse,
                          name="ret_bwd" if reverse else "ret_fwd_merge")
    return outs


ATT_QKV = (ATT_HEADS + 2 * ATT_KV_HEADS) * ATT_HD


def _att_prep_kernel(z_ref, qn_ref, kn_ref, *rest, latent):
    if latent:
        cos_ref, sin_ref, o_ref = rest
        cs, sn = cos_ref[...], sin_ref[...]
        lane = lax.broadcasted_iota(jnp.int32, cs.shape, 1)
        low = (lane & (ATT_HD // 2 - 1)) < (ATT_HD // 4)
    else:
        (o_ref,) = rest
    for h in range(ATT_HEADS + ATT_KV_HEADS):
        x = z_ref[0, :, h * ATT_HD:(h + 1) * ATT_HD]
        gain = qn_ref[...] if h < ATT_HEADS else kn_ref[...]
        y = x * lax.rsqrt(jnp.mean(x * x, axis=-1, keepdims=True) + EPS) * gain
        if latent:
            partner = jnp.where(low, pltpu.roll(y, ATT_HD - ATT_HD // 4, 1), pltpu.roll(y, ATT_HD // 4, 1))
            y = y * cs + partner * sn
        if h < ATT_HEADS:
            y = y * (ATT_HD ** -0.5)
        o_ref[0, :, h * ATT_HD:(h + 1) * ATT_HD] = y.astype(o_ref.dtype)
    v0 = (ATT_HEADS + ATT_KV_HEADS) * ATT_HD
    o_ref[0, :, v0:] = z_ref[0, :, v0:].astype(o_ref.dtype)


def att_prep(z, att_q_norm, att_k_norm, tables, *, tm):
    b, l, _ = z.shape
    latent = tables is not None
    row = lambda bi, i: (bi, i, 0)
    const = lambda bi, i: (0, 0)
    ins = [z, att_q_norm.reshape(1, ATT_HD), att_k_norm.reshape(1, ATT_HD)]
    specs = [pl.BlockSpec((1, tm, ATT_QKV), lambda bi, i: (bi, i, OD_COL["aq"] // ATT_QKV)),
             pl.BlockSpec((1, ATT_HD), const), pl.BlockSpec((1, ATT_HD), const)]
    if latent:
        ins += list(tables)
        specs += [pl.BlockSpec((tm, ATT_HD), lambda bi, i: (i, 0))] * 2
    return pl.pallas_call(
        functools.partial(_att_prep_kernel, latent=latent),
        grid=(b, l // tm),
        in_specs=specs,
        out_specs=pl.BlockSpec((1, tm, ATT_QKV), row),
        out_shape=jax.ShapeDtypeStruct((b, l, ATT_QKV), BF16),
        compiler_params=_cparams(("parallel", "parallel")),
        name="att_prep",
    )(*ins)


def _axial_tables(l):
    quarter = ATT_HD // 4
    inv = ROPE_BASE ** (-jnp.arange(quarter, dtype=F32) / quarter)
    t = jnp.arange(l)
    a_row = (t // GRID_W).astype(F32)[:, None] * inv[None, :]
    a_col = (t % GRID_W).astype(F32)[:, None] * inv[None, :]
    cos = jnp.concatenate([jnp.cos(a_row)] * 2 + [jnp.cos(a_col)] * 2, axis=-1)
    sin = jnp.concatenate([-jnp.sin(a_row), jnp.sin(a_row), -jnp.sin(a_col), jnp.sin(a_col)], axis=-1)
    return cos, sin


def _att_kernel(sink_ref, q_ref, kp, kc, kn, vp, vc, vn, ck, cv, o_ref, *, seq):
    kvh = pl.program_id(1)
    n = pl.program_id(2)
    w = ATT_BLOCK
    q = jnp.concatenate([q_ref[0, :, g * ATT_HD:(g + 1) * ATT_HD] for g in range(ATT_GROUP)], axis=0)
    kw = jnp.concatenate([kp[0], kc[0], kn[0]], axis=0)
    vw = jnp.concatenate([vp[0], vc[0], vn[0]], axis=0)
    s_w = lax.dot_general(q, kw, _NT, preferred_element_type=F32)
    s_c = lax.dot_general(q, ck[0], _NT, preferred_element_type=F32)
    r = lax.broadcasted_iota(jnp.int32, s_w.shape, 0) & (w - 1)
    c = lax.broadcasted_iota(jnp.int32, s_w.shape, 1) - w
    kpos = n * w + c
    valid = (jnp.abs(r - c) <= ATT_WINDOW) & (kpos >= 0) & (kpos < seq)
    s_w = jnp.where(valid, s_w, -jnp.inf)
    sink = jnp.concatenate([jnp.full((w, 1), sink_ref[kvh * ATT_GROUP + g], F32) for g in range(ATT_GROUP)],
                           axis=0)
    m = jnp.maximum(jnp.maximum(jnp.max(s_w, axis=-1, keepdims=True), jnp.max(s_c, axis=-1, keepdims=True)),
                    sink)
    p_w = jnp.exp(s_w - m)
    p_c = jnp.exp(s_c - m)
    denom = (jnp.sum(p_w, axis=-1, keepdims=True) + jnp.sum(p_c, axis=-1, keepdims=True)
             + jnp.exp(sink - m))
    o = (jnp.dot(p_w.astype(BF16), vw, preferred_element_type=F32)
         + jnp.dot(p_c.astype(BF16), cv[0], preferred_element_type=F32)) / denom
    for g in range(ATT_GROUP):
        o_ref[0, :, g * ATT_HD:(g + 1) * ATT_HD] = o[g * w:(g + 1) * w].astype(o_ref.dtype)


def window_attention(qkv_l, qkv_c, att_sink):
    b, l, _ = qkv_l.shape
    lc = qkv_c.shape[1]
    w = ATT_BLOCK
    nb = l // w
    gw = ATT_GROUP * ATT_HD
    kcol = ATT_HEADS
    vcol = ATT_HEADS + ATT_KV_HEADS

    def kv_spec(col, shift):
        return pl.BlockSpec((1, w, ATT_HD),
                            lambda bi, h, n, s: (bi, jnp.clip(n + shift, 0, nb - 1), col + h))

    grid_spec = pltpu.PrefetchScalarGridSpec(
        num_scalar_prefetch=1,
        grid=(b, ATT_KV_HEADS, nb),
        in_specs=[pl.BlockSpec((1, w, gw), lambda bi, h, n, s: (bi, n, h))]
        + [kv_spec(kcol, d) for d in (-1, 0, 1)] + [kv_spec(vcol, d) for d in (-1, 0, 1)]
        + [pl.BlockSpec((1, lc, ATT_HD), lambda bi, h, n, s: (bi, 0, kcol + h)),
           pl.BlockSpec((1, lc, ATT_HD), lambda bi, h, n, s: (bi, 0, vcol + h))],
        out_specs=pl.BlockSpec((1, w, gw), lambda bi, h, n, s: (bi, n, h)),
    )
    return pl.pallas_call(
        functools.partial(_att_kernel, seq=l),
        grid_spec=grid_spec,
        out_shape=jax.ShapeDtypeStruct((b, l, ATT_HEADS * ATT_HD), BF16),
        compiler_params=_cparams(("parallel", "parallel", "arbitrary")),
        name="window_attention",
    )(att_sink.astype(F32), qkv_l, qkv_l, qkv_l, qkv_l, qkv_l, qkv_l, qkv_l, qkv_c, qkv_c)


def moe_experts(hs, route, w1, w3, w2, layer):
    n = sum(h.shape[0] for h in hs)
    d = hs[0].shape[1]
    tm = MOE_TM
    flat_e = route[0:TOP_K].astype(jnp.int32).reshape(-1)
    w_tok = route[TOP_K:2 * TOP_K].reshape(-1, 1)
    onehot = (flat_e[:, None] == jnp.arange(N_EXPERTS)[None, :]).astype(jnp.int32)
    rank = jnp.sum((jnp.cumsum(onehot, axis=0) - onehot) * onehot, axis=-1)
    counts = jnp.sum(onehot, axis=0)
    padded = ((counts + tm - 1) // tm) * tm
    ends = jnp.cumsum(padded)
    offs = ends - padded
    pos = (offs[flat_e] + rank).astype(jnp.int32)
    n_tiles = (2 * n + tm - 1) // tm + N_EXPERTS
    p = n_tiles * tm
    used = ends[-1] // tm
    tile_ids = jnp.arange(n_tiles, dtype=jnp.int32)
    tile_valid = (tile_ids < used).astype(jnp.int32)
    tile_src = jnp.minimum(tile_ids, used - 1).astype(jnp.int32)
    tile_expert = jnp.minimum(
        jnp.searchsorted(ends, tile_src * tm, side="right"), N_EXPERTS - 1).astype(jnp.int32)
    x_sorted = jnp.zeros((p, d), F32)
    pos_kn = pos.reshape(TOP_K, n)
    r0 = 0
    for h in hs:
        rows = h.shape[0]
        local = jnp.tile(jnp.arange(rows, dtype=jnp.int32), TOP_K)
        x_sorted = row_move(h, local, pos_kn[:, r0:r0 + rows].reshape(-1), p, init=x_sorted)
        r0 += rows
    y_sorted = moe_grouped(x_sorted, w1, w3, w2, layer, tile_expert, tile_src, tile_valid)
    y_tok = row_move(y_sorted, pos, jnp.arange(TOP_K * n, dtype=jnp.int32), TOP_K * n)
    return y_tok, w_tok


FFT_Q = 128
HY_COLS = (HY_ORDER + 1) * HY_CH


def _hy_conv_kernel(u_ref, prev_ref, next_ref, w_ref, b_ref, z_ref, *g_refs):
    i = pl.program_id(1)
    u = u_ref[0]
    t = u.shape[0]
    row = lax.broadcasted_iota(jnp.int32, (t, 1), 0)
    has_prev = (i > 0).astype(F32)
    has_next = (i < pl.num_programs(1) - 1).astype(F32)
    up = jnp.where(row == 0, prev_ref[0, SUBLANE - 1:SUBLANE, :] * has_prev, pltpu.roll(u, 1, 0))
    dn = jnp.where(row == t - 1, next_ref[0, 0:1, :] * has_next, pltpu.roll(u, t - 1, 0))
    y = up * w_ref[0:1, :] + u * w_ref[1:2, :] + dn * w_ref[2:3, :] + b_ref[...]
    for n, o_ref in enumerate((z_ref,) + g_refs):
        o_ref[0] = y[:, n * HY_CH:(n + 1) * HY_CH]


def hy_short_conv(z, w, bias, *, tm):
    b, l, _ = z.shape
    ucol = EV_COL["u"] // HY_COLS
    nh = l // SUBLANE
    hb = tm // SUBLANE
    row = lambda bi, i: (bi, i, 0)
    return pl.pallas_call(
        _hy_conv_kernel,
        grid=(b, l // tm),
        in_specs=[
            pl.BlockSpec((1, tm, HY_COLS), lambda bi, i: (bi, i, ucol)),
            pl.BlockSpec((1, SUBLANE, HY_COLS), lambda bi, i: (bi, jnp.maximum(i * hb - 1, 0), ucol)),
            pl.BlockSpec((1, SUBLANE, HY_COLS), lambda bi, i: (bi, jnp.minimum((i + 1) * hb, nh - 1), ucol)),
            pl.BlockSpec((3, HY_COLS), lambda bi, i: (0, 0)),
            pl.BlockSpec((1, HY_COLS), lambda bi, i: (0, 0)),
        ],
        out_specs=[pl.BlockSpec((1, tm, HY_CH), row)] * (HY_ORDER + 1),
        out_shape=[jax.ShapeDtypeStruct((b, l, HY_CH), F32)] * (HY_ORDER + 1),
        compiler_params=_cparams(("parallel", "parallel")),
        name="hy_short_conv",
    )(z, z, z, w, bias.reshape(1, HY_COLS))


def _hy_filter_kernel(feat_ref, w1_ref, b1_ref, w2_ref, b2_ref, w3_ref, delta_ref, kern_ref, den_ref):
    feat = feat_ref[...]
    h = jnp.sin(jnp.dot(feat, w1_ref[...], preferred_element_type=F32, precision=HIGHEST) + b1_ref[...])
    h = jnp.sin(jnp.dot(h, w2_ref[...], preferred_element_type=F32, precision=HIGHEST) + b2_ref[...])
    f = jnp.dot(h, w3_ref[...], preferred_element_type=F32, precision=HIGHEST)
    t = feat[:, 0:1]
    live = feat[:, LANE - 1:LANE]
    kern = f * (jnp.exp(-t * delta_ref[...]) + HY_SHIFT) * live
    kern_ref[...] = kern

    @pl.when((pl.program_id(0) == 0) & (pl.program_id(1) == 0))
    def _():
        den_ref[...] = jnp.zeros_like(den_ref)

    den_ref[...] += jnp.sum(jnp.abs(kern), axis=0, keepdims=True)


def hy_filters(l, w1, b1, w2, b2, w3, *, tr):
    t = jnp.linspace(0.0, 1.0, l, dtype=F32)[:, None]
    w = (2.0 * math.pi / l) * jnp.arange(l, dtype=F32)[:, None]
    bands = jnp.linspace(1e-4, HY_BANDS - 1.0, HY_BANDS, dtype=F32)[None, :]
    feats = jnp.concatenate([t, jnp.cos(bands * w), -jnp.sin(bands * w)], axis=-1)
    feats = jnp.pad(feats, ((0, 0), (0, LANE - 1 - HY_EMB)))
    feats = jnp.concatenate([feats, jnp.ones((l, 1), F32)], axis=-1)
    table = jnp.concatenate([feats, jnp.zeros((1, LANE), F32), feats[:0:-1]], axis=0)
    w1p = jnp.pad(w1.astype(F32), ((0, LANE - HY_EMB), (0, 0)))
    w3s = w3.astype(F32).reshape(HY_FFN, HY_ORDER, 2, HY_CH).transpose(0, 2, 1, 3).reshape(HY_FFN, -1)
    deltas = jnp.abs(jnp.linspace(HY_MIN_DECAY, HY_MAX_DECAY, HY_CH, dtype=F32))
    deltas = jnp.tile(deltas, HY_ORDER).reshape(1, -1)
    oc = HY_ORDER * HY_CH
    nb = l // tr
    const = lambda s, i: (0, 0)
    return pl.pallas_call(
        _hy_filter_kernel,
        grid=(2, nb),
        in_specs=[
            pl.BlockSpec((tr, LANE), lambda s, i: (s * nb + i, 0)),
            pl.BlockSpec((LANE, HY_FFN), const), pl.BlockSpec((1, HY_FFN), const),
            pl.BlockSpec((HY_FFN, HY_FFN), const), pl.BlockSpec((1, HY_FFN), const),
            pl.BlockSpec((HY_FFN, oc), lambda s, i: (0, s)),
            pl.BlockSpec((1, oc), const),
        ],
        out_specs=[pl.BlockSpec((tr, oc), lambda s, i: (s * nb + i, 0)), pl.BlockSpec((1, oc), const)],
        out_shape=[jax.ShapeDtypeStruct((2 * l, oc), F32), jax.ShapeDtypeStruct((1, oc), F32)],
        compiler_params=_cparams(("arbitrary", "arbitrary")),
        name="hy_filters",
    )(table, w1p, b1.reshape(1, -1).astype(F32), w2.astype(F32), b2.reshape(1, -1).astype(F32), w3s, deltas)


def _cis(num, den):
    ang = (-2.0 * math.pi / den) * (num % den).astype(F32)
    return jnp.cos(ang), jnp.sin(ang)


def _block_complex(re, im):
    return jnp.concatenate([jnp.concatenate([re, -im], axis=-1), jnp.concatenate([im, re], axis=-1)], axis=-2)


def _fft_tables(p):
    q = FFT_Q
    n = p * q
    i_p = jnp.arange(p, dtype=jnp.int32)
    i_q = jnp.arange(q, dtype=jnp.int32)
    fr, fi = _cis(i_p[:, None] * i_p[None, :], p)
    f_filt = jnp.concatenate([fr, fi], axis=0)
    f_in = _block_complex(fr[:, :p // 2], fi[:, :p // 2])
    f_out = _block_complex(fr[:p // 2, :], -fi[:p // 2, :]) * (1.0 / n)
    num = i_q[None, None, :] * i_p[:, None, None] + p * i_q[None, :, None] * i_q[None, None, :]
    gr, gi = _cis(num, n)
    g = _block_complex(gr, gi)
    h = _block_complex(jnp.swapaxes(gr, 1, 2), -jnp.swapaxes(gi, 1, 2))
    return tuple(a.astype(BF16) for a in (f_filt, f_in, g, h, f_out))


FFT_TC = 512


def _fft_in_kernel(f_ref, x_ref, o_ref, xs_ref):
    for s in range(x_ref.shape[1]):
        xs_ref[...] = x_ref[:, s, :]
        o_ref[:, s, :] = jnp.dot(f_ref[...], xs_ref[...].astype(BF16), preferred_element_type=F32)


def fft_in(f, x):
    k, q, c = x.shape
    tc = FFT_TC
    return pl.pallas_call(
        _fft_in_kernel,
        grid=(q // SUBLANE, c // tc),
        in_specs=[pl.BlockSpec(f.shape, lambda j, ci: (0, 0)),
                  pl.BlockSpec((k, SUBLANE, tc), lambda j, ci: (0, j, ci))],
        out_specs=pl.BlockSpec((f.shape[0], SUBLANE, tc), lambda j, ci: (0, j, ci)),
        out_shape=jax.ShapeDtypeStruct((f.shape[0], q, c), F32),
        scratch_shapes=[pltpu.VMEM((k, tc), F32)],
        compiler_params=_cparams(("parallel", "parallel")),
        name="fft_in",
    )(f, x)


def _fft_filt_mid_kernel(g_ref, a_ref, inv_ref, o_ref):
    q2, c = o_ref.shape[1], o_ref.shape[2]
    a = a_ref[...].reshape(q2, c).astype(BF16)
    o_ref[0] = jnp.dot(g_ref[0], a, preferred_element_type=F32) * inv_ref[...]


def fft_filt_mid(g, a, den):
    p = g.shape[0]
    c = den.shape[1]
    a = a.reshape(2, p, FFT_Q, c)
    return pl.pallas_call(
        _fft_filt_mid_kernel,
        grid=(p,),
        in_specs=[pl.BlockSpec((1, 2 * FFT_Q, 2 * FFT_Q), lambda k: (k, 0, 0)),
                  pl.BlockSpec((2, 1, FFT_Q, c), lambda k: (0, k, 0, 0)),
                  pl.BlockSpec((1, c), lambda k: (0, 0))],
        out_specs=pl.BlockSpec((1, 2 * FFT_Q, c), lambda k: (k, 0, 0)),
        out_shape=jax.ShapeDtypeStruct((p, 2 * FFT_Q, c), F32),
        compiler_params=_cparams(("parallel",)),
        name="fft_filt_mid",
    )(g, a, 1.0 / den)


def _cmul_rows(s, k):
    half = s.shape[0] // 2
    sr, si, kr, ki = s[:half], s[half:], k[:half], k[half:]
    return jnp.concatenate([sr * kr - si * ki, sr * ki + si * kr], axis=0)


def _fft_mid_kernel(g_ref, h_ref, a_ref, k_ref, o_ref):
    q2, c = k_ref.shape[1], k_ref.shape[2]
    a = a_ref[...].reshape(q2, c).astype(BF16)
    y = jnp.dot(g_ref[0], a, preferred_element_type=F32)
    z = _cmul_rows(y, k_ref[0]).astype(BF16)
    o_ref[...] = jnp.dot(h_ref[0], z, preferred_element_type=F32).reshape(o_ref.shape)


def fft_mid(g, h, a, kf, order):
    p = g.shape[0]
    c = HY_CH
    mat = pl.BlockSpec((1, 2 * FFT_Q, 2 * FFT_Q), lambda k: (k, 0, 0))
    sig = pl.BlockSpec((2, 1, FFT_Q, c), lambda k: (0, k, 0, 0))
    return pl.pallas_call(
        _fft_mid_kernel,
        grid=(p,),
        in_specs=[mat, mat, sig, pl.BlockSpec((1, 2 * FFT_Q, c), lambda k: (k, 0, order))],
        out_specs=sig,
        out_shape=jax.ShapeDtypeStruct((2, p, FFT_Q, c), F32),
        compiler_params=_cparams(("parallel",)),
        name="fft_mid",
    )(g, h, a.reshape(2, p, FFT_Q, c), kf).reshape(2 * p, FFT_Q, c)


def _fft_out_kernel(f_ref, c_ref, z_ref, gate_ref, bias_ref, o_ref, cs_ref, zs_ref, gs_ref):
    for s in range(c_ref.shape[1]):
        cs_ref[...] = c_ref[:, s, :]
        zs_ref[...] = z_ref[:, s, :]
        gs_ref[...] = gate_ref[:, s, :]
        y = jnp.dot(f_ref[...], cs_ref[...].astype(BF16), preferred_element_type=F32)
        o_ref[:, s, :] = gs_ref[...] * (y + zs_ref[...] * bias_ref[...])


def fft_out(f, cm, z, gate, bias):
    p2, q, c = cm.shape
    tc = FFT_TC
    rows = f.shape[0]
    half = pl.BlockSpec((rows, SUBLANE, tc), lambda j, ci: (0, j, ci))
    return pl.pallas_call(
        _fft_out_kernel,
        grid=(q // SUBLANE, c // tc),
        in_specs=[pl.BlockSpec(f.shape, lambda j, ci: (0, 0)),
                  pl.BlockSpec((p2, SUBLANE, tc), lambda j, ci: (0, j, ci)),
                  half, half,
                  pl.BlockSpec((1, tc), lambda j, ci: (0, ci))],
        out_specs=half,
        out_shape=jax.ShapeDtypeStruct(z.shape, F32),
        scratch_shapes=[pltpu.VMEM((p2, tc), F32), pltpu.VMEM((rows, tc), F32), pltpu.VMEM((rows, tc), F32)],
        compiler_params=_cparams(("parallel", "parallel")),
        name="fft_out",
    )(f, cm, z, gate, bias)


def _hy_ctx_kernel(ff_ref, fin_ref, fout_ref, kern_ref, inv_ref, z_ref, g1_ref, g2_ref, bias_ref, o_ref):
    b, l, c = z_ref.shape
    z = z_ref[...].reshape(b * l, c)
    gates = (g1_ref, g2_ref)
    for n in range(HY_ORDER):
        cols = slice(n * c, (n + 1) * c)
        kf = (jnp.dot(ff_ref[...], kern_ref[:, cols].astype(BF16), preferred_element_type=F32)
              * inv_ref[:, cols])
        s = jnp.dot(fin_ref[...], z.astype(BF16), preferred_element_type=F32)
        y = jnp.dot(fout_ref[...], _cmul_rows(s, kf).astype(BF16), preferred_element_type=F32)
        z = gates[n][...].reshape(b * l, c) * (y + z * bias_ref[n:n + 1, :])
    o_ref[...] = z.reshape(b, l, c).astype(o_ref.dtype)


def hy_ctx_conv(kern, den, z0, g1, g2, bias):
    b, l, c = z0.shape
    assert b == 2
    n = 2 * l
    i_n = jnp.arange(n, dtype=jnp.int32)
    fr, fi = _cis(i_n[:, None] * i_n[None, :], n)
    ff = jnp.concatenate([fr, fi], axis=0).astype(BF16)
    fin = _block_complex(fr[:, :l], fi[:, :l]).astype(BF16)
    fout = (_block_complex(fr[:l, :], -fi[:l, :]) * (1.0 / n)).astype(BF16)
    return pl.pallas_call(
        _hy_ctx_kernel,
        out_shape=jax.ShapeDtypeStruct((b, l, c), BF16),
        compiler_params=pltpu.CompilerParams(vmem_limit_bytes=VMEM_LIMIT),
        name="hy_ctx_conv",
    )(ff, fin, fout, kern, 1.0 / den, z0, g1, g2, bias)


def hyena_latent(z, conv_w, conv_b, w1, b1, w2, b2, w3, bias):
    b, l, _ = z.shape
    assert b == 2 and (2 * l) % (2 * FFT_Q) == 0
    p = 2 * l // FFT_Q
    z0, *gates = hy_short_conv(z, conv_w, conv_b, tm=256)
    kern, den = hy_filters(l, w1, b1, w2, b2, w3, tr=512)
    f_filt, f_in, g, h, f_out = _fft_tables(p)
    kf = fft_filt_mid(g, fft_in(f_filt, kern.reshape(p, FFT_Q, HY_ORDER * HY_CH)), den)
    split = lambda t: t.reshape(p, FFT_Q, HY_CH)
    zz = split(z0)
    for n in range(HY_ORDER):
        cm = fft_mid(g, h, fft_in(f_in, zz), kf, n)
        zz = fft_out(f_out, cm, zz, split(gates[n]), bias[n].astype(F32).reshape(1, HY_CH))
    return zz.reshape(b, l, HY_CH)


def hyena_ctx(z, conv_w, conv_b, w1, b1, w2, b2, w3, bias):
    b, l, _ = z.shape
    z0, g1, g2 = hy_short_conv(z, conv_w, conv_b, tm=l)
    kern, den = hy_filters(l, w1, b1, w2, b2, w3, tr=l)
    return hy_ctx_conv(kern, den, z0, g1, g2, bias.astype(F32))


def even_mixer_core(zc, zl, gla_w_gate, gla_b_gate, gla_norm, hy_conv_w, hy_conv_b,
                    hy_w1, hy_b1, hy_w2, hy_b2, hy_w3, hy_bias):
    gc, gl = gla_bidir(zc, zl, gla_w_gate, gla_b_gate, gla_norm)
    hy = (hy_conv_w, hy_conv_b, hy_w1, hy_b1, hy_w2, hy_b2, hy_w3, hy_bias)
    return (gc, hyena_ctx(zc, *hy)), (gl, hyena_latent(zl, *hy))


def odd_mixer_core(zc, zl, ret_norm, att_q_norm, att_k_norm, att_sink):
    _, rl = retention_bidir(zc, zl, ret_norm)
    qkv_l = att_prep(zl, att_q_norm, att_k_norm, _axial_tables(zl.shape[1]), tm=512)
    qkv_c = att_prep(zc, att_q_norm, att_k_norm, None, tm=zc.shape[1])
    return rl, window_attention(qkv_l, qkv_c, att_sink)


def kernel(x, c, ctx, c_ctx, ada_w, ada_b, norm_mix, norm_ffn, ev_w_in, ev_w_out, gla_w_gate, gla_b_gate, gla_norm, hy_conv_w, hy_conv_b, hy_w1, hy_b1, hy_w2, hy_b2, hy_w3, hy_bias, od_w_in, od_w_out, ret_norm, att_q_norm, att_k_norm, att_sink, router_w, router_b, moe_w1, moe_w3, moe_w2):
    assert DEPTH == 2
    bsz, seq, d = x.shape
    n_ctx = ctx.shape[1]
    c_pad = jnp.zeros((8, d), F32).at[:bsz].set(c).at[bsz].set(c_ctx)
    mods = ada_modulation(c_pad, ada_w, ada_b)
    slot = np.array([SUBLANE * (e % EXPERTS_PER_GROUP) + e // EXPERTS_PER_GROUP for e in range(N_EXPERTS)])
    rw_pad = jnp.zeros((d, LOGIT_PAD), F32).at[:, slot].set(router_w)
    rw_hi = rw_pad.astype(BF16)
    rw_pad = jnp.concatenate([rw_hi, (rw_pad - rw_hi.astype(F32)).astype(BF16)], axis=1)
    rb_pad = jnp.zeros((EXPERTS_PER_GROUP * SUBLANE, LANE), F32).at[slot].set(
        jnp.broadcast_to(router_b.astype(F32)[:, None], (N_EXPERTS, LANE)))

    xl, xc = x, ctx
    for layer in range(DEPTH):
        last = layer == DEPTH - 1
        ml = mods[layer, :bsz].reshape(bsz, 1, 6, d)
        mc = jnp.broadcast_to(mods[layer, bsz].reshape(1, 1, 6, d), (bsz, 1, 6, d))
        sh1, sc1, g1, sh2, sc2, g2 = [ml[:, :, i] for i in range(6)]
        csh1, csc1, cg1, csh2, csc2, cg2 = [mc[:, :, i] for i in range(6)]
        i = layer // 2
        if layer % 2 == 0:
            w_in, w_out = ev_w_in[i], ev_w_out[i]
            lr0 = sum(EVEN_SPLITS[:4])
            lr1 = lr0 + EVEN_SPLITS[4]
            w_in = jnp.concatenate([w_in[:, :lr0], w_in[:, lr1:], w_in[:, lr0:lr1],
                                    jnp.zeros((d, EVEN_COLS_PAD - w_in.shape[1]), F32)], axis=1)
            tn = EVEN_COLS_PAD // 7
        else:
            w_in, w_out = od_w_in[i], od_w_out[i]
            tn = w_in.shape[1] // 6
        zl = norm_mod_matmul(xl, norm_mix[layer], sc1, sh1, w_in, tm=1024, tn=tn, out_dtype=F32)
        zc = norm_mod_matmul(xc, norm_mix[layer], csc1, csh1, w_in, tm=n_ctx, tn=tn, out_dtype=F32)
        if layer % 2 == 0:
            mixc, mixl = even_mixer_core(zc, zl, gla_w_gate[i], gla_b_gate[i], gla_norm[i], hy_conv_w[i],
                                         hy_conv_b[i], hy_w1[i], hy_b1[i], hy_w2[i], hy_b2[i], hy_w3[i],
                                         hy_bias[i])
        else:
            mixl = odd_mixer_core(zc, zl, ret_norm[i], att_q_norm[i], att_k_norm[i], att_sink[i])
        lat0 = 0 if last else bsz * n_ctx
        flat = lambda rt: rt.transpose(1, 0, 2).reshape(ROUTE_ROWS, -1)
        hs, route = [], []
        if not last:
            xc, hc, rtc = outproj_residual_norm(*mixc, w_out, xc, cg1, norm_ffn[layer], csc2, csh2, rw_pad,
                                                rb_pad, tm=n_ctx)
            hs.append(hc)
            route.append(flat(rtc))
        xl, hl, rtl = outproj_residual_norm(*mixl, w_out, xl, g1, norm_ffn[layer], sc2, sh2, rw_pad, rb_pad,
                                            tm=512)
        hs.append(hl)
        route.append(flat(rtl))
        y_tok, w_tok = moe_experts(hs, jnp.concatenate(route, axis=1), moe_w1, moe_w3, moe_w2, layer)
        if not last:
            xc = moe_combine(xc, cg2, y_tok, w_tok, 0, tm=n_ctx)
        xl = moe_combine(xl, g2, y_tok, w_tok, lat0, tm=512)
    return xl
```

```python
import functools
import math

import jax
import jax.numpy as jnp
import numpy as np
from jax import lax
from jax.experimental import pallas as pl
from jax.experimental.pallas import tpu as pltpu

D_MODEL = 2048
BATCH = 2
SEQ = 8192
DEPTH = 2
GRID_W = 64
CTX_LEN = 256
MIX_WIDTH = D_MODEL
GROUP_WIDTH = MIX_WIDTH // 2
EPS = 1e-6
F32 = jnp.float32
BF16 = jnp.bfloat16

GLA_HEADS = 4
GLA_DK = GROUP_WIDTH // (2 * GLA_HEADS)
GLA_DV = GROUP_WIDTH // GLA_HEADS
GLA_LOWRANK = 16
GLA_GATE_NORMALIZER = 16.0
GLA_CHUNK = 64

HY_CH = GROUP_WIDTH
HY_ORDER = 2
HY_BANDS = 16
HY_EMB = 1 + 2 * HY_BANDS
HY_FFN = 64
HY_SHIFT = 0.05
HY_MIN_DECAY = math.log(1e-2) / 1.5
HY_MAX_DECAY = math.log(1e-2) / 0.3

RET_HEADS = 4
RET_DK = GROUP_WIDTH // (2 * RET_HEADS)
RET_DV = GROUP_WIDTH // RET_HEADS
RET_CHUNK = 64

ATT_HD = 128
ATT_HEADS = GROUP_WIDTH // ATT_HD
ATT_KV_HEADS = 2
ATT_GROUP = ATT_HEADS // ATT_KV_HEADS
ATT_WINDOW = 128
ATT_BLOCK = 128
ROPE_BASE = 10000.0

N_EXPERTS = 16
N_GROUPS = 4
EXPERTS_PER_GROUP = N_EXPERTS // N_GROUPS
TOP_K = 2
D_FF_EXPERT = D_MODEL // 2

EVEN_SPLITS = (GLA_HEADS * GLA_DK, GLA_HEADS * GLA_DK, GLA_HEADS * GLA_DV, GLA_HEADS * GLA_DV,
               2 * GLA_LOWRANK, (HY_ORDER + 1) * HY_CH)
ODD_SPLITS = (RET_HEADS * RET_DK, RET_HEADS * RET_DK, RET_HEADS * RET_DV, RET_HEADS * RET_DV,
              ATT_HEADS * ATT_HD, ATT_KV_HEADS * ATT_HD, ATT_KV_HEADS * ATT_HD)

LANE = 128
EVEN_COLS_PAD = -(-sum(EVEN_SPLITS) // (7 * LANE)) * (7 * LANE)
VMEM_LIMIT = 56 * 1024 * 1024
MOE_TM = 512
MOE_FF_SPLIT = 2
MOVE_ROWS = 1024
LOGIT_PAD = LANE


def _cparams(sem):
    return pltpu.CompilerParams(dimension_semantics=sem, vmem_limit_bytes=VMEM_LIMIT)


def _ada_kernel(c_ref, w_ref, b_ref, o_ref):
    c = c_ref[...]
    s = (c * jax.nn.sigmoid(c)).astype(BF16)
    o_ref[0] = jnp.dot(s, w_ref[0].astype(BF16), preferred_element_type=F32) + b_ref[0]


def ada_modulation(c_pad, ada_w, ada_b):
    depth, d, n = ada_w.shape
    tn = 1024
    return pl.pallas_call(
        _ada_kernel,
        grid=(depth, n // tn),
        in_specs=[
            pl.BlockSpec((8, d), lambda l, j: (0, 0)),
            pl.BlockSpec((1, d, tn), lambda l, j: (l, 0, j)),
            pl.BlockSpec((1, 1, tn), lambda l, j: (l, 0, j)),
        ],
        out_specs=pl.BlockSpec((1, 8, tn), lambda l, j: (l, 0, j)),
        out_shape=jax.ShapeDtypeStruct((depth, 8, n), F32),
        compiler_params=_cparams(("parallel", "arbitrary")),
        name="ada_modulation",
    )(c_pad, ada_w, ada_b.reshape(depth, 1, n))


def _norm_mod(x, gain, sc, sh):
    ms = jnp.mean(x * x, axis=-1, keepdims=True)
    return (x * lax.rsqrt(ms + EPS) * gain) * (1.0 + sc) + sh


def _inproj_kernel(x_ref, gain_ref, sc_ref, sh_ref, w_ref, o_ref, h_ref):
    @pl.when(pl.program_id(2) == 0)
    def _():
        h_ref[...] = _norm_mod(x_ref[0], gain_ref[...], sc_ref[0], sh_ref[0]).astype(BF16)

    o_ref[0] = jnp.dot(h_ref[...], w_ref[...].astype(BF16), preferred_element_type=F32).astype(o_ref.dtype)


def norm_mod_matmul(x, gain, sc, sh, w, *, tm, tn, out_dtype):
    b, l, d = x.shape
    n = w.shape[1]
    return pl.pallas_call(
        _inproj_kernel,
        grid=(b, l // tm, n // tn),
        in_specs=[
            pl.BlockSpec((1, tm, d), lambda bi, i, j: (bi, i, 0)),
            pl.BlockSpec((1, d), lambda bi, i, j: (0, 0)),
            pl.BlockSpec((1, 1, d), lambda bi, i, j: (bi, 0, 0)),
            pl.BlockSpec((1, 1, d), lambda bi, i, j: (bi, 0, 0)),
            pl.BlockSpec((d, tn), lambda bi, i, j: (0, j)),
        ],
        out_specs=pl.BlockSpec((1, tm, tn), lambda bi, i, j: (bi, i, j)),
        out_shape=jax.ShapeDtypeStruct((b, l, n), out_dtype),
        scratch_shapes=[pltpu.VMEM((tm, d), BF16)],
        compiler_params=_cparams(("parallel", "parallel", "arbitrary")),
        name="norm_mod_matmul",
    )(x, gain.reshape(1, d), sc, sh, w)


SUBLANE = 8
ROUTE_ROWS = SUBLANE


def _route_block(lg, rb_ref):
    tm = lg.shape[0]
    t = lg.T
    sub = lax.broadcasted_iota(jnp.int32, (SUBLANE, tm), 0)
    valid = sub < N_GROUPS
    members = range(EXPERTS_PER_GROUP)
    s = [jnp.where(valid, t[SUBLANE * j:SUBLANE * (j + 1), :], -jnp.inf) for j in members]
    mx = jnp.max(jnp.maximum(jnp.maximum(s[0], s[1]), jnp.maximum(s[2], s[3])), axis=0, keepdims=True)
    e = [jnp.exp(sj - mx) for sj in s]
    den = jnp.sum((e[0] + e[1]) + (e[2] + e[3]), axis=0, keepdims=True)
    sc = [ej / den for ej in e]
    sel = [sc[j] + jnp.concatenate([rb_ref[SUBLANE * j:SUBLANE * (j + 1), :]] * (tm // LANE), axis=1)
           for j in members]
    hi01, lo01 = jnp.maximum(sel[0], sel[1]), jnp.minimum(sel[0], sel[1])
    hi23, lo23 = jnp.maximum(sel[2], sel[3]), jnp.minimum(sel[2], sel[3])
    top1 = jnp.maximum(hi01, hi23)
    top2 = jnp.maximum(jnp.minimum(hi01, hi23), jnp.maximum(lo01, lo23))
    gscore = jnp.where(valid, top1 + top2, -jnp.inf)
    gmax = jnp.max(gscore, axis=0, keepdims=True)
    gidx = jnp.min(jnp.where(gscore == gmax, sub, SUBLANE), axis=0, keepdims=True)
    chosen = sub == gidx
    v = [jnp.sum(jnp.where(chosen, sel[j], 0.0), axis=0, keepdims=True) for j in members]
    p = [jnp.sum(jnp.where(chosen, sc[j], 0.0), axis=0, keepdims=True) for j in members]
    one = lambda cond: jnp.where(cond, 1.0, 0.0)
    picks = []
    for want in (0.0, 1.0):
        idx = jnp.zeros_like(v[0])
        wt = jnp.zeros_like(v[0])
        for j in members:
            rank = sum(one(v[k] > v[j]) for k in members if k != j) + sum(one(v[k] == v[j]) for k in range(j))
            hit = rank == want
            idx = idx + jnp.where(hit, float(j), 0.0)
            wt = wt + jnp.where(hit, p[j], 0.0)
        picks.append((idx, wt))
    (j1, p1), (j2, p2) = picks
    base = (EXPERTS_PER_GROUP * gidx).astype(F32)
    tot = p1 + p2
    rows = [base + j1, base + j2, p1 / tot, p2 / tot]
    return jnp.concatenate(rows + [jnp.zeros((ROUTE_ROWS - len(rows), tm), F32)], axis=0)


def _outproj_kernel(m1_ref, m2_ref, w_ref, x_ref, g_ref, gain_ref, sc_ref, sh_ref, rw_ref, rb_ref,
                    xo_ref, h_ref, rt_ref):
    k1 = m1_ref.shape[-1]
    o = (jnp.dot(m1_ref[0].astype(BF16), w_ref[:k1, :].astype(BF16), preferred_element_type=F32)
         + jnp.dot(m2_ref[0].astype(BF16), w_ref[k1:, :].astype(BF16), preferred_element_type=F32))
    xn = x_ref[0] + g_ref[0] * o
    xo_ref[0] = xn
    h = _norm_mod(xn, gain_ref[...], sc_ref[0], sh_ref[0])
    h_ref[...] = h
    h_hi = h.astype(BF16)
    h_lo = (h - h_hi.astype(F32)).astype(BF16)
    both = jnp.dot(h_hi, rw_ref[...], preferred_element_type=F32)
    lg = (both[:, :LOGIT_PAD] + both[:, LOGIT_PAD:]
          + jnp.dot(h_lo, rw_ref[:, :LOGIT_PAD], preferred_element_type=F32))
    rt_ref[0] = _route_block(lg, rb_ref)


def outproj_residual_norm(m1, m2, w, x, g, gain, sc, sh, rw_pad, rb_pad, *, tm):
    b, l, d = x.shape
    k1, k2 = m1.shape[-1], m2.shape[-1]
    nb = l // tm
    row = lambda bi, i: (bi, i, 0)
    mod = lambda bi, i: (bi, 0, 0)
    const = lambda bi, i: (0, 0)
    return pl.pallas_call(
        _outproj_kernel,
        grid=(b, nb),
        in_specs=[
            pl.BlockSpec((1, tm, k1), row),
            pl.BlockSpec((1, tm, k2), row),
            pl.BlockSpec((k1 + k2, d), const, pipeline_mode=pl.Buffered(1)),
            pl.BlockSpec((1, tm, d), row),
            pl.BlockSpec((1, 1, d), mod),
            pl.BlockSpec((1, d), const),
            pl.BlockSpec((1, 1, d), mod),
            pl.BlockSpec((1, 1, d), mod),
            pl.BlockSpec((d, 2 * LOGIT_PAD), const),
            pl.BlockSpec(rb_pad.shape, const),
        ],
        out_specs=[
            pl.BlockSpec((1, tm, d), row),
            pl.BlockSpec((tm, d), lambda bi, i: (bi * nb + i, 0)),
            pl.BlockSpec((1, ROUTE_ROWS, tm), lambda bi, i: (bi, 0, i)),
        ],
        out_shape=[
            jax.ShapeDtypeStruct((b, l, d), F32),
            jax.ShapeDtypeStruct((b * l, d), F32),
            jax.ShapeDtypeStruct((b, ROUTE_ROWS, l), F32),
        ],
        compiler_params=_cparams(("parallel", "parallel")),
        name="outproj_residual_norm",
    )(m1, m2, w, x, g, gain.reshape(1, d), sc, sh, rw_pad, rb_pad)


def _moe_kernel(te_ref, tsrc_ref, tvalid_ref, x_ref, w1_ref, w3_ref, w2_ref, o_ref):
    i = pl.program_id(0)

    @pl.when(tvalid_ref[i] > 0)
    def _():
        x = x_ref[...].astype(BF16)
        ff = w1_ref.shape[-1]
        fc = ff // MOE_FF_SPLIT
        y = None
        for c in range(MOE_FF_SPLIT):
            cols = slice(c * fc, (c + 1) * fc)
            a = jnp.dot(x, w1_ref[0, 0, :, cols].astype(BF16), preferred_element_type=F32)
            b = jnp.dot(x, w3_ref[0, 0, :, cols].astype(BF16), preferred_element_type=F32)
            mid = (a * jax.nn.sigmoid(a) * b).astype(BF16)
            part = jnp.dot(mid, w2_ref[0, 0, cols, :].astype(BF16), preferred_element_type=F32)
            y = part if y is None else y + part
        o_ref[...] = y

    @pl.when(tvalid_ref[i] == 0)
    def _():
        o_ref[...] = jnp.zeros_like(o_ref)


def moe_grouped(x_sorted, w1, w3, w2, layer, tile_expert, tile_src, tile_valid):
    p, d = x_sorted.shape
    ff = w1.shape[-1]
    tm = MOE_TM
    wmap = lambda i, te, ts, tv: (layer, te[i], 0, 0)
    grid_spec = pltpu.PrefetchScalarGridSpec(
        num_scalar_prefetch=3,
        grid=(p // tm,),
        in_specs=[
            pl.BlockSpec((tm, d), lambda i, te, ts, tv: (ts[i], 0)),
            pl.BlockSpec((1, 1, d, ff), wmap, pipeline_mode=pl.Buffered(1)),
            pl.BlockSpec((1, 1, d, ff), wmap, pipeline_mode=pl.Buffered(1)),
            pl.BlockSpec((1, 1, ff, d), wmap, pipeline_mode=pl.Buffered(1)),
        ],
        out_specs=pl.BlockSpec((tm, d), lambda i, te, ts, tv: (i, 0)),
    )
    return pl.pallas_call(
        _moe_kernel,
        grid_spec=grid_spec,
        out_shape=jax.ShapeDtypeStruct((p, d), F32),
        compiler_params=_cparams(("arbitrary",)),
        name="moe_grouped",
    )(tile_expert, tile_src, tile_valid, x_sorted, w1, w3, w2)


def _row_scatter_kernel(dst_ref, h_ref, init_ref, out_ref, sem):
    del init_ref
    n = h_ref.shape[0]

    def issue(r, carry):
        pltpu.make_async_copy(h_ref.at[pl.ds(r, 1)], out_ref.at[pl.ds(dst_ref[0, 0, r], 1)], sem).start()
        return carry

    lax.fori_loop(0, n, issue, 0, unroll=4)
    pltpu.make_async_copy(h_ref, h_ref, sem).wait()


def row_scatter(h, dst, init, *, step):
    rows, d = h.shape
    nb = rows // step
    assert nb * step == rows
    dst = dst.reshape(TOP_K, nb, 1, step).transpose(1, 0, 2, 3).reshape(nb * TOP_K, 1, step)
    return pl.pallas_call(
        _row_scatter_kernel,
        grid=(nb, TOP_K),
        in_specs=[pl.BlockSpec((1, 1, step), lambda j, k: (j * TOP_K + k, 0, 0), memory_space=pltpu.SMEM),
                  pl.BlockSpec((step, d), lambda j, k: (j, 0)),
                  pl.BlockSpec(memory_space=pl.ANY)],
        out_specs=pl.BlockSpec(memory_space=pl.ANY),
        out_shape=jax.ShapeDtypeStruct(init.shape, init.dtype),
        scratch_shapes=[pltpu.SemaphoreType.DMA(())],
        input_output_aliases={2: 0},
        compiler_params=_cparams(("arbitrary", "arbitrary")),
        name="row_scatter",
    )(dst, h, init)


def _gather_combine_kernel(p0_ref, p1_ref, x_ref, g_ref, w0_ref, w1_ref, y_ref, o_ref, buf, sem):
    tm = x_ref.shape[1]

    def issue(r, carry):
        pltpu.make_async_copy(y_ref.at[pl.ds(p0_ref[0, 0, r], 1)], buf.at[0, pl.ds(r, 1)], sem).start()
        pltpu.make_async_copy(y_ref.at[pl.ds(p1_ref[0, 0, r], 1)], buf.at[1, pl.ds(r, 1)], sem).start()
        return carry

    lax.fori_loop(0, tm, issue, 0, unroll=2)
    pltpu.make_async_copy(buf, buf, sem).wait()
    y = w0_ref[...] * buf[0] + w1_ref[...] * buf[1]
    o_ref[0] = x_ref[0] + g_ref[0] * y


def moe_gather_combine(x, g, y_sorted, pos, w_tok, row0, *, tm):
    b, l, d = x.shape
    n = pos.shape[1]
    nb = l // tm
    assert row0 % tm == 0 and n % tm == 0
    tok = lambda bi, i: row0 // tm + bi * nb + i
    pos = pos.reshape(TOP_K * (n // tm), 1, tm)
    w_col = w_tok.reshape(TOP_K * n, 1)
    idx = lambda k: pl.BlockSpec((1, 1, tm), lambda bi, i: (k * (n // tm) + tok(bi, i), 0, 0),
                                 memory_space=pltpu.SMEM)
    wgt = lambda k: pl.BlockSpec((tm, 1), lambda bi, i: (k * (n // tm) + tok(bi, i), 0))
    return pl.pallas_call(
        _gather_combine_kernel,
        grid=(b, nb),
        in_specs=[idx(0), idx(1),
                  pl.BlockSpec((1, tm, d), lambda bi, i: (bi, i, 0)),
                  pl.BlockSpec((1, 1, d), lambda bi, i: (bi, 0, 0)),
                  wgt(0), wgt(1),
                  pl.BlockSpec(memory_space=pl.ANY)],
        out_specs=pl.BlockSpec((1, tm, d), lambda bi, i: (bi, i, 0)),
        out_shape=jax.ShapeDtypeStruct(x.shape, x.dtype),
        scratch_shapes=[pltpu.VMEM((TOP_K, tm, d), F32), pltpu.SemaphoreType.DMA(())],
        compiler_params=_cparams(("arbitrary", "arbitrary")),
        name="moe_gather_combine",
    )(pos, pos, x, g, w_col, w_col, y_sorted)


CHUNK = 64
SCAN_T = 256
_NT = (((1,), (1,)), ((), ()))
_TN = (((0,), (0,)), ((), ()))
HIGHEST = lax.Precision.HIGHEST


def _chunk_keep(reverse):
    r = lax.broadcasted_iota(jnp.int32, (CHUNK, CHUNK), 0)
    c = lax.broadcasted_iota(jnp.int32, (CHUNK, CHUNK), 1)
    return (c >= r) if reverse else (c <= r)


def _chunk_step(q, k, v, e_q, e_k, e_s, e_last, keep, st):
    qe = (q * e_q).astype(BF16)
    ke = (k * e_k).astype(BF16)
    kd = (k * e_s).astype(BF16)
    vb = v.astype(BF16)
    att = lax.dot_general(qe, ke, _NT, preferred_element_type=F32)
    att = jnp.where(keep, att, 0.0).astype(BF16)
    o = (jnp.dot(att, vb, preferred_element_type=F32)
         + lax.dot_general(qe, st.astype(BF16), _NT, preferred_element_type=F32))
    st = e_last * st + lax.dot_general(vb, kd, _TN, preferred_element_type=F32)
    return o, st


def _merge_store(o, ob_ref, g_ref, gn_ref, o_ref, sl):
    tot = o + ob_ref[0, sl, :]
    y = tot * lax.rsqrt(jnp.mean(tot * tot, axis=-1, keepdims=True) + EPS) * gn_ref[...]
    g = g_ref[0, sl, :]
    o_ref[0, sl, :] = (y * (g * jax.nn.sigmoid(g))).astype(o_ref.dtype)


def _chunk_order(n, reverse):
    return range(n - 1, -1, -1) if reverse else range(n)


def _gla_kernel(qc, kc, vc, lrc, ql, kl, vl, lrl, wg, bg, *rest, reverse, merge):
    if merge:
        obc, obl, gc, gl, gn, oc_ref, ol_ref, st_ref = rest
    else:
        oc_ref, ol_ref, st_ref = rest
        obc = obl = gc = gl = gn = None
    keep = _chunk_keep(reverse)
    tri = keep.astype(F32)

    def run(q_ref, k_ref, v_ref, lr_ref, o_ref, ob_ref, g_ref):
        pre = jnp.dot(lr_ref[0], wg[...], preferred_element_type=F32, precision=HIGHEST) + bg[...]
        la = (jnp.minimum(pre, 0.0) - jnp.log(1.0 + jnp.exp(-jnp.abs(pre)))) * (1.0 / GLA_GATE_NORMALIZER)
        st = st_ref[...]
        for ci in _chunk_order(q_ref.shape[1] // CHUNK, reverse):
            sl = pl.ds(ci * CHUNK, CHUNK)
            cum = jnp.dot(tri, la[ci * CHUNK:(ci + 1) * CHUNK], preferred_element_type=F32,
                          precision=HIGHEST)
            last = cum[0:1] if reverse else cum[CHUNK - 1:CHUNK]
            o, st = _chunk_step(q_ref[0, sl, :] * (GLA_DK ** -0.5), k_ref[0, sl, :], v_ref[0, sl, :],
                                jnp.exp(cum), jnp.exp(-cum), jnp.exp(last - cum), jnp.exp(last), keep, st)
            if merge:
                _merge_store(o, ob_ref, g_ref, gn, o_ref, sl)
            else:
                o_ref[0, sl, :] = o
        st_ref[...] = st

    @pl.when(pl.program_id(2) == 0)
    def _():
        st_ref[...] = jnp.zeros_like(st_ref)
        run(qc, kc, vc, lrc, oc_ref, obc, gc)

    @pl.when(pl.program_id(2) > 0)
    def _():
        run(ql, kl, vl, lrl, ol_ref, obl, gl)


def _ret_kernel(qc, kc, vc, ql, kl, vl, cos_ref, sin_ref, cum_ref, *rest, reverse, merge):
    if merge:
        obc, obl, gc, gl, gn, oc_ref, ol_ref, st_ref = rest
    else:
        oc_ref, ol_ref, st_ref = rest
        obc = obl = gc = gl = gn = None
    keep = _chunk_keep(reverse)
    cum = cum_ref[0]
    last = cum[0:1] if reverse else cum[CHUNK - 1:CHUNK]
    e_q, e_k, e_s, e_last = jnp.exp(cum), jnp.exp(-cum), jnp.exp(last - cum), jnp.exp(last)

    def run(q_ref, k_ref, v_ref, o_ref, ob_ref, g_ref, latent):
        st = st_ref[...]
        for ci in _chunk_order(q_ref.shape[1] // CHUNK, reverse):
            sl = pl.ds(ci * CHUNK, CHUNK)
            q = q_ref[0, sl, :]
            k = k_ref[0, sl, :]
            if latent:
                cs = cos_ref[sl, :]
                sn = sin_ref[sl, :]
                q = q * cs + pltpu.roll(q, RET_DK // 2, 1) * sn
                k = k * cs + pltpu.roll(k, RET_DK // 2, 1) * sn
            o, st = _chunk_step(q, k * (RET_DK ** -0.5), v_ref[0, sl, :], e_q, e_k, e_s, e_last, keep, st)
            if merge:
                _merge_store(o, ob_ref, g_ref, gn, o_ref, sl)
            else:
                o_ref[0, sl, :] = o
        st_ref[...] = st

    @pl.when(pl.program_id(2) == 0)
    def _():
        st_ref[...] = jnp.zeros_like(st_ref)
        run(qc, kc, vc, oc_ref, obc, gc, False)

    @pl.when(pl.program_id(2) > 0)
    def _():
        run(ql, kl, vl, ol_ref, obl, gl, True)


def _scan_call(body, zc, zl, heads, dk, dv, col_q, col_k, col_v, shared_cols, extra_specs, merge_in,
               *, reverse, name):
    b, lc, _ = zc.shape
    l = zl.shape[1]
    t = SCAN_T
    nblk = l // t
    if reverse:
        blk = lambda i: jnp.where(i == 0, nblk - 1, nblk - i)
    else:
        blk = lambda i: jnp.maximum(i - 1, 0)

    def ctx_spec(w, col, per_head=True):
        return pl.BlockSpec((1, lc, w), lambda bi, h, i: (bi, 0, col // w + (h if per_head else 0)))

    def lat_spec(w, col, per_head=True):
        return pl.BlockSpec((1, t, w), lambda bi, h, i: (bi, blk(i), col // w + (h if per_head else 0)))

    ins = [zc, zc, zc] + [zc] * len(shared_cols) + [zl, zl, zl] + [zl] * len(shared_cols)
    specs = [ctx_spec(dk, col_q), ctx_spec(dk, col_k), ctx_spec(dv, col_v)]
    specs += [ctx_spec(w, col, False) for w, col in shared_cols]
    specs += [lat_spec(dk, col_q), lat_spec(dk, col_k), lat_spec(dv, col_v)]
    specs += [lat_spec(w, col, False) for w, col in shared_cols]
    for a, s in extra_specs:
        ins.append(a)
        specs.append(s)
    if merge_in is not None:
        obc, obl, col_g, gain = merge_in
        ins += [obc, obl, zc, zl, gain.reshape(1, dv)]
        specs += [ctx_spec(dv, 0), lat_spec(dv, 0), ctx_spec(dv, col_g), lat_spec(dv, col_g),
                  pl.BlockSpec((1, dv), lambda bi, h, i: (0, 0))]
        out_dtype = BF16
    else:
        out_dtype = F32
    return pl.pallas_call(
        functools.partial(body, reverse=reverse, merge=merge_in is not None),
        grid=(b, heads, 1 + nblk),
        in_specs=specs,
        out_specs=[ctx_spec(dv, 0), lat_spec(dv, 0)],
        out_shape=[jax.ShapeDtypeStruct((b, lc, heads * dv), out_dtype),
                   jax.ShapeDtypeStruct((b, l, heads * dv), out_dtype)],
        scratch_shapes=[pltpu.VMEM((dv, dk), F32)],
        compiler_params=_cparams(("parallel", "parallel", "arbitrary")),
        name=name,
    )(*ins)


EV_COL = dict(q=0, k=512, v=1024, g=2048, u=3072, lr=6144)
OD_COL = dict(q=0, k=512, v=1024, g=2048, aq=3072, ak=4096, av=4352)


def gla_bidir(zc, zl, gla_w_gate, gla_b_gate, gla_norm):
    hk = GLA_HEADS * GLA_DK
    outs = None
    for d, reverse in ((1, True), (0, False)):
        wg = jnp.zeros((LANE, hk), F32).at[d * GLA_LOWRANK:(d + 1) * GLA_LOWRANK].set(gla_w_gate[d])
        shared_cols = [(LANE, EV_COL["lr"])]
        extra_specs = [(wg, pl.BlockSpec((LANE, GLA_DK), lambda bi, h, i: (0, h))),
                       (gla_b_gate[d].reshape(1, hk), pl.BlockSpec((1, GLA_DK), lambda bi, h, i: (0, h)))]
        merge_in = None if outs is None else (outs[0], outs[1], EV_COL["g"], gla_norm)
        outs = _scan_call(_gla_kernel, zc, zl, GLA_HEADS, GLA_DK, GLA_DV, EV_COL["q"], EV_COL["k"],
                          EV_COL["v"], shared_cols, extra_specs, merge_in, reverse=reverse,
                          name="gla_bwd" if reverse else "gla_fwd_merge")
    return outs


def retention_bidir(zc, zl, ret_norm):
    l = zl.shape[1]
    inv = ROPE_BASE ** (-jnp.linspace(0.0, 1.0, RET_DK // 2, dtype=F32))
    ang = jnp.arange(l, dtype=F32)[:, None] * inv[None, :]
    cos2 = jnp.concatenate([jnp.cos(ang), jnp.cos(ang)], axis=-1)
    sin2 = jnp.concatenate([-jnp.sin(ang), jnp.sin(ang)], axis=-1)
    log_g_f = jnp.log(1.0 - 2.0 ** (-5.0 - jnp.arange(RET_HEADS, dtype=F32)))
    idx = jnp.arange(CHUNK, dtype=F32)
    nblk = l // SCAN_T
    outs = None
    for reverse in (True, False):
        if reverse:
            cum = (CHUNK - idx)[None, :, None] * log_g_f[::-1][:, None, None]
            blk = lambda i: jnp.where(i == 0, nblk - 1, nblk - i)
        else:
            cum = (idx + 1.0)[None, :, None] * log_g_f[:, None, None]
            blk = lambda i: jnp.maximum(i - 1, 0)
        cum = jnp.broadcast_to(cum, (RET_HEADS, CHUNK, RET_DK))
        tab_spec = pl.BlockSpec((SCAN_T, RET_DK), lambda bi, h, i, blk=blk: (blk(i), 0))
        extra_specs = [(cos2, tab_spec), (sin2, tab_spec),
                       (cum, pl.BlockSpec((1, CHUNK, RET_DK), lambda bi, h, i: (h, 0, 0)))]
        merge_in = None if outs is None else (outs[0], outs[1], OD_COL["g"], ret_norm)
        outs = _scan_call(_ret_kernel, zc, zl, RET_HEADS, RET_DK, RET_DV, OD_COL["q"], OD_COL["k"],
                          OD_COL["v"], [], extra_specs, merge_in, reverse=reverse,
                          name="ret_bwd" if reverse else "ret_fwd_merge")
    return outs


ATT_QKV = (ATT_HEADS + 2 * ATT_KV_HEADS) * ATT_HD


def _att_prep_kernel(z_ref, qn_ref, kn_ref, *rest, latent):
    if latent:
        cos_ref, sin_ref, o_ref = rest
        cs, sn = cos_ref[...], sin_ref[...]
        lane = lax.broadcasted_iota(jnp.int32, cs.shape, 1)
        low = (lane & (ATT_HD // 2 - 1)) < (ATT_HD // 4)
    else:
        (o_ref,) = rest
    for h in range(ATT_HEADS + ATT_KV_HEADS):
        x = z_ref[0, :, h * ATT_HD:(h + 1) * ATT_HD]
        gain = qn_ref[...] if h < ATT_HEADS else kn_ref[...]
        y = x * lax.rsqrt(jnp.mean(x * x, axis=-1, keepdims=True) + EPS) * gain
        if latent:
            partner = jnp.where(low, pltpu.roll(y, ATT_HD - ATT_HD // 4, 1), pltpu.roll(y, ATT_HD // 4, 1))
            y = y * cs + partner * sn
        if h < ATT_HEADS:
            y = y * (ATT_HD ** -0.5)
        o_ref[0, :, h * ATT_HD:(h + 1) * ATT_HD] = y.astype(o_ref.dtype)
    v0 = (ATT_HEADS + ATT_KV_HEADS) * ATT_HD
    o_ref[0, :, v0:] = z_ref[0, :, v0:].astype(o_ref.dtype)


def att_prep(z, att_q_norm, att_k_norm, tables, *, tm):
    b, l, _ = z.shape
    latent = tables is not None
    row = lambda bi, i: (bi, i, 0)
    const = lambda bi, i: (0, 0)
    ins = [z, att_q_norm.reshape(1, ATT_HD), att_k_norm.reshape(1, ATT_HD)]
    specs = [pl.BlockSpec((1, tm, ATT_QKV), lambda bi, i: (bi, i, OD_COL["aq"] // ATT_QKV)),
             pl.BlockSpec((1, ATT_HD), const), pl.BlockSpec((1, ATT_HD), const)]
    if latent:
        ins += list(tables)
        specs += [pl.BlockSpec((tm, ATT_HD), lambda bi, i: (i, 0))] * 2
    return pl.pallas_call(
        functools.partial(_att_prep_kernel, latent=latent),
        grid=(b, l // tm),
        in_specs=specs,
        out_specs=pl.BlockSpec((1, tm, ATT_QKV), row),
        out_shape=jax.ShapeDtypeStruct((b, l, ATT_QKV), BF16),
        compiler_params=_cparams(("parallel", "parallel")),
        name="att_prep",
    )(*ins)


def _axial_tables(l):
    quarter = ATT_HD // 4
    inv = ROPE_BASE ** (-jnp.arange(quarter, dtype=F32) / quarter)
    t = jnp.arange(l)
    a_row = (t // GRID_W).astype(F32)[:, None] * inv[None, :]
    a_col = (t % GRID_W).astype(F32)[:, None] * inv[None, :]
    cos = jnp.concatenate([jnp.cos(a_row)] * 2 + [jnp.cos(a_col)] * 2, axis=-1)
    sin = jnp.concatenate([-jnp.sin(a_row), jnp.sin(a_row), -jnp.sin(a_col), jnp.sin(a_col)], axis=-1)
    return cos, sin


def _att_kernel(sink_ref, q_ref, kp, kc, kn, vp, vc, vn, ck, cv, o_ref, *, seq):
    kvh = pl.program_id(1)
    n = pl.program_id(2)
    w = ATT_BLOCK
    q = jnp.concatenate([q_ref[0, :, g * ATT_HD:(g + 1) * ATT_HD] for g in range(ATT_GROUP)], axis=0)
    kw = jnp.concatenate([kp[0], kc[0], kn[0]], axis=0)
    vw = jnp.concatenate([vp[0], vc[0], vn[0]], axis=0)
    s_w = lax.dot_general(q, kw, _NT, preferred_element_type=F32)
    s_c = lax.dot_general(q, ck[0], _NT, preferred_element_type=F32)
    r = lax.broadcasted_iota(jnp.int32, s_w.shape, 0) & (w - 1)
    c = lax.broadcasted_iota(jnp.int32, s_w.shape, 1) - w
    kpos = n * w + c
    valid = (jnp.abs(r - c) <= ATT_WINDOW) & (kpos >= 0) & (kpos < seq)
    s_w = jnp.where(valid, s_w, -jnp.inf)
    sink = jnp.concatenate([jnp.full((w, 1), sink_ref[kvh * ATT_GROUP + g], F32) for g in range(ATT_GROUP)],
                           axis=0)
    m = jnp.maximum(jnp.maximum(jnp.max(s_w, axis=-1, keepdims=True), jnp.max(s_c, axis=-1, keepdims=True)),
                    sink)
    p_w = jnp.exp(s_w - m)
    p_c = jnp.exp(s_c - m)
    denom = (jnp.sum(p_w, axis=-1, keepdims=True) + jnp.sum(p_c, axis=-1, keepdims=True)
             + jnp.exp(sink - m))
    o = (jnp.dot(p_w.astype(BF16), vw, preferred_element_type=F32)
         + jnp.dot(p_c.astype(BF16), cv[0], preferred_element_type=F32)) / denom
    for g in range(ATT_GROUP):
        o_ref[0, :, g * ATT_HD:(g + 1) * ATT_HD] = o[g * w:(g + 1) * w].astype(o_ref.dtype)


def window_attention(qkv_l, qkv_c, att_sink):
    b, l, _ = qkv_l.shape
    lc = qkv_c.shape[1]
    w = ATT_BLOCK
    nb = l // w
    gw = ATT_GROUP * ATT_HD
    kcol = ATT_HEADS
    vcol = ATT_HEADS + ATT_KV_HEADS

    def kv_spec(col, shift):
        return pl.BlockSpec((1, w, ATT_HD),
                            lambda bi, h, n, s: (bi, jnp.clip(n + shift, 0, nb - 1), col + h))

    grid_spec = pltpu.PrefetchScalarGridSpec(
        num_scalar_prefetch=1,
        grid=(b, ATT_KV_HEADS, nb),
        in_specs=[pl.BlockSpec((1, w, gw), lambda bi, h, n, s: (bi, n, h))]
        + [kv_spec(kcol, d) for d in (-1, 0, 1)] + [kv_spec(vcol, d) for d in (-1, 0, 1)]
        + [pl.BlockSpec((1, lc, ATT_HD), lambda bi, h, n, s: (bi, 0, kcol + h)),
           pl.BlockSpec((1, lc, ATT_HD), lambda bi, h, n, s: (bi, 0, vcol + h))],
        out_specs=pl.BlockSpec((1, w, gw), lambda bi, h, n, s: (bi, n, h)),
    )
    return pl.pallas_call(
        functools.partial(_att_kernel, seq=l),
        grid_spec=grid_spec,
        out_shape=jax.ShapeDtypeStruct((b, l, ATT_HEADS * ATT_HD), BF16),
        compiler_params=_cparams(("parallel", "parallel", "arbitrary")),
        name="window_attention",
    )(att_sink.astype(F32), qkv_l, qkv_l, qkv_l, qkv_l, qkv_l, qkv_l, qkv_l, qkv_c, qkv_c)


def moe_experts(hs, route, w1, w3, w2, layer):
    n = sum(h.shape[0] for h in hs)
    d = hs[0].shape[1]
    tm = MOE_TM
    flat_e = route[0:TOP_K].astype(jnp.int32).reshape(-1)
    w_tok = route[TOP_K:2 * TOP_K]
    onehot = (flat_e[:, None] == jnp.arange(N_EXPERTS)[None, :]).astype(jnp.int32)
    rank = jnp.sum((jnp.cumsum(onehot, axis=0) - onehot) * onehot, axis=-1)
    counts = jnp.sum(onehot, axis=0)
    padded = ((counts + tm - 1) // tm) * tm
    ends = jnp.cumsum(padded)
    offs = ends - padded
    pos = (offs[flat_e] + rank).astype(jnp.int32)
    n_tiles = (2 * n + tm - 1) // tm + N_EXPERTS
    p = n_tiles * tm
    used = ends[-1] // tm
    tile_ids = jnp.arange(n_tiles, dtype=jnp.int32)
    tile_valid = (tile_ids < used).astype(jnp.int32)
    tile_src = jnp.minimum(tile_ids, used - 1).astype(jnp.int32)
    tile_expert = jnp.minimum(
        jnp.searchsorted(ends, tile_src * tm, side="right"), N_EXPERTS - 1).astype(jnp.int32)
    x_sorted = jnp.zeros((p, d), F32)
    pos = pos.reshape(TOP_K, n)
    r0 = 0
    for h in hs:
        rows = h.shape[0]
        x_sorted = row_scatter(h, pos[:, r0:r0 + rows], x_sorted, step=min(rows, MOVE_ROWS))
        r0 += rows
    y_sorted = moe_grouped(x_sorted, w1, w3, w2, layer, tile_expert, tile_src, tile_valid)
    return y_sorted, pos, w_tok


FFT_Q = 128
HY_COLS = (HY_ORDER + 1) * HY_CH


def _hy_conv_kernel(u_ref, prev_ref, next_ref, w_ref, b_ref, z_ref, *g_refs):
    i = pl.program_id(1)
    u = u_ref[0]
    t = u.shape[0]
    row = lax.broadcasted_iota(jnp.int32, (t, 1), 0)
    has_prev = (i > 0).astype(F32)
    has_next = (i < pl.num_programs(1) - 1).astype(F32)
    up = jnp.where(row == 0, prev_ref[0, SUBLANE - 1:SUBLANE, :] * has_prev, pltpu.roll(u, 1, 0))
    dn = jnp.where(row == t - 1, next_ref[0, 0:1, :] * has_next, pltpu.roll(u, t - 1, 0))
    y = up * w_ref[0:1, :] + u * w_ref[1:2, :] + dn * w_ref[2:3, :] + b_ref[...]
    for n, o_ref in enumerate((z_ref,) + g_refs):
        o_ref[0] = y[:, n * HY_CH:(n + 1) * HY_CH]


def hy_short_conv(z, w, bias, *, tm):
    b, l, _ = z.shape
    ucol = EV_COL["u"] // HY_COLS
    nh = l // SUBLANE
    hb = tm // SUBLANE
    row = lambda bi, i: (bi, i, 0)
    return pl.pallas_call(
        _hy_conv_kernel,
        grid=(b, l // tm),
        in_specs=[
            pl.BlockSpec((1, tm, HY_COLS), lambda bi, i: (bi, i, ucol)),
            pl.BlockSpec((1, SUBLANE, HY_COLS), lambda bi, i: (bi, jnp.maximum(i * hb - 1, 0), ucol)),
            pl.BlockSpec((1, SUBLANE, HY_COLS), lambda bi, i: (bi, jnp.minimum((i + 1) * hb, nh - 1), ucol)),
            pl.BlockSpec((3, HY_COLS), lambda bi, i: (0, 0)),
            pl.BlockSpec((1, HY_COLS), lambda bi, i: (0, 0)),
        ],
        out_specs=[pl.BlockSpec((1, tm, HY_CH), row)] * (HY_ORDER + 1),
        out_shape=[jax.ShapeDtypeStruct((b, l, HY_CH), F32)] * (HY_ORDER + 1),
        compiler_params=_cparams(("parallel", "parallel")),
        name="hy_short_conv",
    )(z, z, z, w, bias.reshape(1, HY_COLS))


def _hy_filter_kernel(feat_ref, w1_ref, b1_ref, w2_ref, b2_ref, w3_ref, delta_ref, kern_ref, den_ref):
    feat = feat_ref[...]
    h = jnp.sin(jnp.dot(feat, w1_ref[...], preferred_element_type=F32, precision=HIGHEST) + b1_ref[...])
    h = jnp.sin(jnp.dot(h, w2_ref[...], preferred_element_type=F32, precision=HIGHEST) + b2_ref[...])
    f = jnp.dot(h, w3_ref[...], preferred_element_type=F32, precision=HIGHEST)
    t = feat[:, 0:1]
    live = feat[:, LANE - 1:LANE]
    kern = f * (jnp.exp(-t * delta_ref[...]) + HY_SHIFT) * live
    kern_ref[...] = kern

    @pl.when((pl.program_id(0) == 0) & (pl.program_id(1) == 0))
    def _():
        den_ref[...] = jnp.zeros_like(den_ref)

    den_ref[...] += jnp.sum(jnp.abs(kern), axis=0, keepdims=True)


def hy_filters(l, w1, b1, w2, b2, w3, *, tr):
    t = jnp.linspace(0.0, 1.0, l, dtype=F32)[:, None]
    w = (2.0 * math.pi / l) * jnp.arange(l, dtype=F32)[:, None]
    bands = jnp.linspace(1e-4, HY_BANDS - 1.0, HY_BANDS, dtype=F32)[None, :]
    feats = jnp.concatenate([t, jnp.cos(bands * w), -jnp.sin(bands * w)], axis=-1)
    feats = jnp.pad(feats, ((0, 0), (0, LANE - 1 - HY_EMB)))
    feats = jnp.concatenate([feats, jnp.ones((l, 1), F32)], axis=-1)
    table = jnp.concatenate([feats, jnp.zeros((1, LANE), F32), feats[:0:-1]], axis=0)
    w1p = jnp.pad(w1.astype(F32), ((0, LANE - HY_EMB), (0, 0)))
    w3s = w3.astype(F32).reshape(HY_FFN, HY_ORDER, 2, HY_CH).transpose(0, 2, 1, 3).reshape(HY_FFN, -1)
    deltas = jnp.abs(jnp.linspace(HY_MIN_DECAY, HY_MAX_DECAY, HY_CH, dtype=F32))
    deltas = jnp.tile(deltas, HY_ORDER).reshape(1, -1)
    oc = HY_ORDER * HY_CH
    nb = l // tr
    const = lambda s, i: (0, 0)
    return pl.pallas_call(
        _hy_filter_kernel,
        grid=(2, nb),
        in_specs=[
            pl.BlockSpec((tr, LANE), lambda s, i: (s * nb + i, 0)),
            pl.BlockSpec((LANE, HY_FFN), const), pl.BlockSpec((1, HY_FFN), const),
            pl.BlockSpec((HY_FFN, HY_FFN), const), pl.BlockSpec((1, HY_FFN), const),
            pl.BlockSpec((HY_FFN, oc), lambda s, i: (0, s)),
            pl.BlockSpec((1, oc), const),
        ],
        out_specs=[pl.BlockSpec((tr, oc), lambda s, i: (s * nb + i, 0)), pl.BlockSpec((1, oc), const)],
        out_shape=[jax.ShapeDtypeStruct((2 * l, oc), F32), jax.ShapeDtypeStruct((1, oc), F32)],
        compiler_params=_cparams(("arbitrary", "arbitrary")),
        name="hy_filters",
    )(table, w1p, b1.reshape(1, -1).astype(F32), w2.astype(F32), b2.reshape(1, -1).astype(F32), w3s, deltas)


def _cis(num, den):
    ang = (-2.0 * math.pi / den) * (num % den).astype(F32)
    return jnp.cos(ang), jnp.sin(ang)


def _block_complex(re, im):
    return jnp.concatenate([jnp.concatenate([re, -im], axis=-1), jnp.concatenate([im, re], axis=-1)], axis=-2)


def _fft_tables(p):
    q = FFT_Q
    n = p * q
    i_p = jnp.arange(p, dtype=jnp.int32)
    i_q = jnp.arange(q, dtype=jnp.int32)
    fr, fi = _cis(i_p[:, None] * i_p[None, :], p)
    f_filt = jnp.concatenate([fr, fi], axis=0)
    f_in = _block_complex(fr[:, :p // 2], fi[:, :p // 2])
    f_out = _block_complex(fr[:p // 2, :], -fi[:p // 2, :]) * (1.0 / n)
    num = i_q[None, None, :] * i_p[:, None, None] + p * i_q[None, :, None] * i_q[None, None, :]
    gr, gi = _cis(num, n)
    g = _block_complex(gr, gi)
    h = _block_complex(jnp.swapaxes(gr, 1, 2), -jnp.swapaxes(gi, 1, 2))
    return tuple(a.astype(BF16) for a in (f_filt, f_in, g, h, f_out))


FFT_TC = 512


def _fft_in_kernel(f_ref, x_ref, o_ref, xs_ref):
    for s in range(x_ref.shape[1]):
        xs_ref[...] = x_ref[:, s, :]
        o_ref[:, s, :] = jnp.dot(f_ref[...], xs_ref[...].astype(BF16), preferred_element_type=F32)


def fft_in(f, x):
    k, q, c = x.shape
    tc = FFT_TC
    return pl.pallas_call(
        _fft_in_kernel,
        grid=(q // SUBLANE, c // tc),
        in_specs=[pl.BlockSpec(f.shape, lambda j, ci: (0, 0)),
                  pl.BlockSpec((k, SUBLANE, tc), lambda j, ci: (0, j, ci))],
        out_specs=pl.BlockSpec((f.shape[0], SUBLANE, tc), lambda j, ci: (0, j, ci)),
        out_shape=jax.ShapeDtypeStruct((f.shape[0], q, c), F32),
        scratch_shapes=[pltpu.VMEM((k, tc), F32)],
        compiler_params=_cparams(("parallel", "parallel")),
        name="fft_in",
    )(f, x)


def _fft_filt_mid_kernel(g_ref, a_ref, inv_ref, o_ref):
    q2, c = o_ref.shape[1], o_ref.shape[2]
    a = a_ref[...].reshape(q2, c).astype(BF16)
    o_ref[0] = jnp.dot(g_ref[0], a, preferred_element_type=F32) * inv_ref[...]


def fft_filt_mid(g, a, den):
    p = g.shape[0]
    c = den.shape[1]
    a = a.reshape(2, p, FFT_Q, c)
    return pl.pallas_call(
        _fft_filt_mid_kernel,
        grid=(p,),
        in_specs=[pl.BlockSpec((1, 2 * FFT_Q, 2 * FFT_Q), lambda k: (k, 0, 0)),
                  pl.BlockSpec((2, 1, FFT_Q, c), lambda k: (0, k, 0, 0)),
                  pl.BlockSpec((1, c), lambda k: (0, 0))],
        out_specs=pl.BlockSpec((1, 2 * FFT_Q, c), lambda k: (k, 0, 0)),
        out_shape=jax.ShapeDtypeStruct((p, 2 * FFT_Q, c), F32),
        compiler_params=_cparams(("parallel",)),
        name="fft_filt_mid",
    )(g, a, 1.0 / den)


def _cmul_rows(s, k):
    half = s.shape[0] // 2
    sr, si, kr, ki = s[:half], s[half:], k[:half], k[half:]
    return jnp.concatenate([sr * kr - si * ki, sr * ki + si * kr], axis=0)


def _fft_mid_kernel(g_ref, h_ref, a_ref, k_ref, o_ref):
    q2, c = k_ref.shape[1], k_ref.shape[2]
    a = a_ref[...].reshape(q2, c).astype(BF16)
    y = jnp.dot(g_ref[0], a, preferred_element_type=F32)
    z = _cmul_rows(y, k_ref[0]).astype(BF16)
    o_ref[...] = jnp.dot(h_ref[0], z, preferred_element_type=F32).reshape(o_ref.shape)


def fft_mid(g, h, a, kf, order):
    p = g.shape[0]
    c = HY_CH
    mat = pl.BlockSpec((1, 2 * FFT_Q, 2 * FFT_Q), lambda k: (k, 0, 0))
    sig = pl.BlockSpec((2, 1, FFT_Q, c), lambda k: (0, k, 0, 0))
    return pl.pallas_call(
        _fft_mid_kernel,
        grid=(p,),
        in_specs=[mat, mat, sig, pl.BlockSpec((1, 2 * FFT_Q, c), lambda k: (k, 0, order))],
        out_specs=sig,
        out_shape=jax.ShapeDtypeStruct((2, p, FFT_Q, c), F32),
        compiler_params=_cparams(("parallel",)),
        name="fft_mid",
    )(g, h, a.reshape(2, p, FFT_Q, c), kf).reshape(2 * p, FFT_Q, c)


def _fft_out_kernel(f_ref, c_ref, z_ref, gate_ref, bias_ref, o_ref, cs_ref, zs_ref, gs_ref):
    for s in range(c_ref.shape[1]):
        cs_ref[...] = c_ref[:, s, :]
        zs_ref[...] = z_ref[:, s, :]
        gs_ref[...] = gate_ref[:, s, :]
        y = jnp.dot(f_ref[...], cs_ref[...].astype(BF16), preferred_element_type=F32)
        o_ref[:, s, :] = gs_ref[...] * (y + zs_ref[...] * bias_ref[...])


def fft_out(f, cm, z, gate, bias):
    p2, q, c = cm.shape
    tc = FFT_TC
    rows = f.shape[0]
    half = pl.BlockSpec((rows, SUBLANE, tc), lambda j, ci: (0, j, ci))
    return pl.pallas_call(
        _fft_out_kernel,
        grid=(q // SUBLANE, c // tc),
        in_specs=[pl.BlockSpec(f.shape, lambda j, ci: (0, 0)),
                  pl.BlockSpec((p2, SUBLANE, tc), lambda j, ci: (0, j, ci)),
                  half, half,
                  pl.BlockSpec((1, tc), lambda j, ci: (0, ci))],
        out_specs=half,
        out_shape=jax.ShapeDtypeStruct(z.shape, F32),
        scratch_shapes=[pltpu.VMEM((p2, tc), F32), pltpu.VMEM((rows, tc), F32), pltpu.VMEM((rows, tc), F32)],
        compiler_params=_cparams(("parallel", "parallel")),
        name="fft_out",
    )(f, cm, z, gate, bias)


def _hy_ctx_kernel(ff_ref, fin_ref, fout_ref, kern_ref, inv_ref, z_ref, g1_ref, g2_ref, bias_ref, o_ref):
    b, l, c = z_ref.shape
    z = z_ref[...].reshape(b * l, c)
    gates = (g1_ref, g2_ref)
    for n in range(HY_ORDER):
        cols = slice(n * c, (n + 1) * c)
        kf = (jnp.dot(ff_ref[...], kern_ref[:, cols].astype(BF16), preferred_element_type=F32)
              * inv_ref[:, cols])
        s = jnp.dot(fin_ref[...], z.astype(BF16), preferred_element_type=F32)
        y = jnp.dot(fout_ref[...], _cmul_rows(s, kf).astype(BF16), preferred_element_type=F32)
        z = gates[n][...].reshape(b * l, c) * (y + z * bias_ref[n:n + 1, :])
    o_ref[...] = z.reshape(b, l, c).astype(o_ref.dtype)


def hy_ctx_conv(kern, den, z0, g1, g2, bias):
    b, l, c = z0.shape
    assert b == 2
    n = 2 * l
    i_n = jnp.arange(n, dtype=jnp.int32)
    fr, fi = _cis(i_n[:, None] * i_n[None, :], n)
    ff = jnp.concatenate([fr, fi], axis=0).astype(BF16)
    fin = _block_complex(fr[:, :l], fi[:, :l]).astype(BF16)
    fout = (_block_complex(fr[:l, :], -fi[:l, :]) * (1.0 / n)).astype(BF16)
    return pl.pallas_call(
        _hy_ctx_kernel,
        out_shape=jax.ShapeDtypeStruct((b, l, c), BF16),
        compiler_params=pltpu.CompilerParams(vmem_limit_bytes=VMEM_LIMIT),
        name="hy_ctx_conv",
    )(ff, fin, fout, kern, 1.0 / den, z0, g1, g2, bias)


def hyena_latent(z, conv_w, conv_b, w1, b1, w2, b2, w3, bias):
    b, l, _ = z.shape
    assert b == 2 and (2 * l) % (2 * FFT_Q) == 0
    p = 2 * l // FFT_Q
    z0, *gates = hy_short_conv(z, conv_w, conv_b, tm=256)
    kern, den = hy_filters(l, w1, b1, w2, b2, w3, tr=512)
    f_filt, f_in, g, h, f_out = _fft_tables(p)
    kf = fft_filt_mid(g, fft_in(f_filt, kern.reshape(p, FFT_Q, HY_ORDER * HY_CH)), den)
    split = lambda t: t.reshape(p, FFT_Q, HY_CH)
    zz = split(z0)
    for n in range(HY_ORDER):
        cm = fft_mid(g, h, fft_in(f_in, zz), kf, n)
        zz = fft_out(f_out, cm, zz, split(gates[n]), bias[n].astype(F32).reshape(1, HY_CH))
    return zz.reshape(b, l, HY_CH)


def hyena_ctx(z, conv_w, conv_b, w1, b1, w2, b2, w3, bias):
    b, l, _ = z.shape
    z0, g1, g2 = hy_short_conv(z, conv_w, conv_b, tm=l)
    kern, den = hy_filters(l, w1, b1, w2, b2, w3, tr=l)
    return hy_ctx_conv(kern, den, z0, g1, g2, bias.astype(F32))


def even_mixer_core(zc, zl, gla_w_gate, gla_b_gate, gla_norm, hy_conv_w, hy_conv_b,
                    hy_w1, hy_b1, hy_w2, hy_b2, hy_w3, hy_bias):
    gc, gl = gla_bidir(zc, zl, gla_w_gate, gla_b_gate, gla_norm)
    hy = (hy_conv_w, hy_conv_b, hy_w1, hy_b1, hy_w2, hy_b2, hy_w3, hy_bias)
    return (gc, hyena_ctx(zc, *hy)), (gl, hyena_latent(zl, *hy))


def odd_mixer_core(zc, zl, ret_norm, att_q_norm, att_k_norm, att_sink):
    _, rl = retention_bidir(zc, zl, ret_norm)
    qkv_l = att_prep(zl, att_q_norm, att_k_norm, _axial_tables(zl.shape[1]), tm=512)
    qkv_c = att_prep(zc, att_q_norm, att_k_norm, None, tm=zc.shape[1])
    return rl, window_attention(qkv_l, qkv_c, att_sink)


def kernel(x, c, ctx, c_ctx, ada_w, ada_b, norm_mix, norm_ffn, ev_w_in, ev_w_out, gla_w_gate, gla_b_gate, gla_norm, hy_conv_w, hy_conv_b, hy_w1, hy_b1, hy_w2, hy_b2, hy_w3, hy_bias, od_w_in, od_w_out, ret_norm, att_q_norm, att_k_norm, att_sink, router_w, router_b, moe_w1, moe_w3, moe_w2):
    assert DEPTH == 2
    bsz, seq, d = x.shape
    n_ctx = ctx.shape[1]
    c_pad = jnp.zeros((8, d), F32).at[:bsz].set(c).at[bsz].set(c_ctx)
    mods = ada_modulation(c_pad, ada_w, ada_b)
    slot = np.array([SUBLANE * (e % EXPERTS_PER_GROUP) + e // EXPERTS_PER_GROUP for e in range(N_EXPERTS)])
    rw_pad = jnp.zeros((d, LOGIT_PAD), F32).at[:, slot].set(router_w)
    rw_hi = rw_pad.astype(BF16)
    rw_pad = jnp.concatenate([rw_hi, (rw_pad - rw_hi.astype(F32)).astype(BF16)], axis=1)
    rb_pad = jnp.zeros((EXPERTS_PER_GROUP * SUBLANE, LANE), F32).at[slot].set(
        jnp.broadcast_to(router_b.astype(F32)[:, None], (N_EXPERTS, LANE)))

    xl, xc = x, ctx
    for layer in range(DEPTH):
        last = layer == DEPTH - 1
        ml = mods[layer, :bsz].reshape(bsz, 1, 6, d)
        mc = jnp.broadcast_to(mods[layer, bsz].reshape(1, 1, 6, d), (bsz, 1, 6, d))
        sh1, sc1, g1, sh2, sc2, g2 = [ml[:, :, i] for i in range(6)]
        csh1, csc1, cg1, csh2, csc2, cg2 = [mc[:, :, i] for i in range(6)]
        i = layer // 2
        if layer % 2 == 0:
            w_in, w_out = ev_w_in[i], ev_w_out[i]
            lr0 = sum(EVEN_SPLITS[:4])
            lr1 = lr0 + EVEN_SPLITS[4]
            w_in = jnp.concatenate([w_in[:, :lr0], w_in[:, lr1:], w_in[:, lr0:lr1],
                                    jnp.zeros((d, EVEN_COLS_PAD - w_in.shape[1]), F32)], axis=1)
            tn = EVEN_COLS_PAD // 7
        else:
            w_in, w_out = od_w_in[i], od_w_out[i]
            tn = w_in.shape[1] // 6
        zl = norm_mod_matmul(xl, norm_mix[layer], sc1, sh1, w_in, tm=1024, tn=tn, out_dtype=F32)
        zc = norm_mod_matmul(xc, norm_mix[layer], csc1, csh1, w_in, tm=n_ctx, tn=tn, out_dtype=F32)
        if layer % 2 == 0:
            mixc, mixl = even_mixer_core(zc, zl, gla_w_gate[i], gla_b_gate[i], gla_norm[i], hy_conv_w[i],
                                         hy_conv_b[i], hy_w1[i], hy_b1[i], hy_w2[i], hy_b2[i], hy_w3[i],
                                         hy_bias[i])
        else:
            mixl = odd_mixer_core(zc, zl, ret_norm[i], att_q_norm[i], att_k_norm[i], att_sink[i])
        lat0 = 0 if last else bsz * n_ctx
        flat = lambda rt: rt.transpose(1, 0, 2).reshape(ROUTE_ROWS, -1)
        hs, route = [], []
        if not last:
            xc, hc, rtc = outproj_residual_norm(*mixc, w_out, xc, cg1, norm_ffn[layer], csc2, csh2, rw_pad,
                                                rb_pad, tm=n_ctx)
            hs.append(hc)
            route.append(flat(rtc))
        xl, hl, rtl = outproj_residual_norm(*mixl, w_out, xl, g1, norm_ffn[layer], sc2, sh2, rw_pad, rb_pad,
                                            tm=512)
        hs.append(hl)
        route.append(flat(rtl))
        y_sorted, pos, w_tok = moe_experts(hs, jnp.concatenate(route, axis=1), moe_w1, moe_w3, moe_w2, layer)
        if not last:
            xc = moe_gather_combine(xc, cg2, y_sorted, pos, w_tok, 0, tm=n_ctx)
        xl = moe_gather_combine(xl, g2, y_sorted, pos, w_tok, lat0, tm=512)
    return xl
```

```python
import functools
import math

import jax
import jax.numpy as jnp
import numpy as np
from jax import lax
from jax.experimental import pallas as pl
from jax.experimental.pallas import tpu as pltpu

D_MODEL = 2048
BATCH = 2
SEQ = 8192
DEPTH = 2
GRID_W = 64
CTX_LEN = 256
MIX_WIDTH = D_MODEL
GROUP_WIDTH = MIX_WIDTH // 2
EPS = 1e-6
F32 = jnp.float32
BF16 = jnp.bfloat16

GLA_HEADS = 4
GLA_DK = GROUP_WIDTH // (2 * GLA_HEADS)
GLA_DV = GROUP_WIDTH // GLA_HEADS
GLA_LOWRANK = 16
GLA_GATE_NORMALIZER = 16.0
GLA_CHUNK = 64

HY_CH = GROUP_WIDTH
HY_ORDER = 2
HY_BANDS = 16
HY_EMB = 1 + 2 * HY_BANDS
HY_FFN = 64
HY_SHIFT = 0.05
HY_MIN_DECAY = math.log(1e-2) / 1.5
HY_MAX_DECAY = math.log(1e-2) / 0.3

RET_HEADS = 4
RET_DK = GROUP_WIDTH // (2 * RET_HEADS)
RET_DV = GROUP_WIDTH // RET_HEADS
RET_CHUNK = 64

ATT_HD = 128
ATT_HEADS = GROUP_WIDTH // ATT_HD
ATT_KV_HEADS = 2
ATT_GROUP = ATT_HEADS // ATT_KV_HEADS
ATT_WINDOW = 128
ATT_BLOCK = 128
ROPE_BASE = 10000.0

N_EXPERTS = 16
N_GROUPS = 4
EXPERTS_PER_GROUP = N_EXPERTS // N_GROUPS
TOP_K = 2
D_FF_EXPERT = D_MODEL // 2

EVEN_SPLITS = (GLA_HEADS * GLA_DK, GLA_HEADS * GLA_DK, GLA_HEADS * GLA_DV, GLA_HEADS * GLA_DV,
               2 * GLA_LOWRANK, (HY_ORDER + 1) * HY_CH)
ODD_SPLITS = (RET_HEADS * RET_DK, RET_HEADS * RET_DK, RET_HEADS * RET_DV, RET_HEADS * RET_DV,
              ATT_HEADS * ATT_HD, ATT_KV_HEADS * ATT_HD, ATT_KV_HEADS * ATT_HD)

LANE = 128
EVEN_COLS_PAD = -(-sum(EVEN_SPLITS) // (7 * LANE)) * (7 * LANE)
VMEM_LIMIT = 56 * 1024 * 1024
MOE_TM = 512
MOE_FF_SPLIT = 2
MOVE_ROWS = 1024
LOGIT_PAD = LANE


def _cparams(sem):
    return pltpu.CompilerParams(dimension_semantics=sem, vmem_limit_bytes=VMEM_LIMIT)


def _ada_kernel(c_ref, w_ref, b_ref, o_ref):
    c = c_ref[...]
    s = (c * jax.nn.sigmoid(c)).astype(BF16)
    o_ref[0] = jnp.dot(s, w_ref[0].astype(BF16), preferred_element_type=F32) + b_ref[0]


def ada_modulation(c_pad, ada_w, ada_b):
    depth, d, n = ada_w.shape
    tn = 1024
    return pl.pallas_call(
        _ada_kernel,
        grid=(depth, n // tn),
        in_specs=[
            pl.BlockSpec((8, d), lambda l, j: (0, 0)),
            pl.BlockSpec((1, d, tn), lambda l, j: (l, 0, j)),
            pl.BlockSpec((1, 1, tn), lambda l, j: (l, 0, j)),
        ],
        out_specs=pl.BlockSpec((1, 8, tn), lambda l, j: (l, 0, j)),
        out_shape=jax.ShapeDtypeStruct((depth, 8, n), F32),
        compiler_params=_cparams(("parallel", "arbitrary")),
        name="ada_modulation",
    )(c_pad, ada_w, ada_b.reshape(depth, 1, n))


def _norm_mod(x, gain, sc, sh):
    ms = jnp.mean(x * x, axis=-1, keepdims=True)
    return (x * lax.rsqrt(ms + EPS) * gain) * (1.0 + sc) + sh


def _inproj_kernel(x_ref, gain_ref, sc_ref, sh_ref, w_ref, o_ref, h_ref):
    @pl.when(pl.program_id(2) == 0)
    def _():
        h_ref[...] = _norm_mod(x_ref[0], gain_ref[...], sc_ref[0], sh_ref[0]).astype(BF16)

    o_ref[0] = jnp.dot(h_ref[...], w_ref[...].astype(BF16), preferred_element_type=F32).astype(o_ref.dtype)


def norm_mod_matmul(x, gain, sc, sh, w, *, tm, tn, out_dtype):
    b, l, d = x.shape
    n = w.shape[1]
    return pl.pallas_call(
        _inproj_kernel,
        grid=(b, l // tm, n // tn),
        in_specs=[
            pl.BlockSpec((1, tm, d), lambda bi, i, j: (bi, i, 0)),
            pl.BlockSpec((1, d), lambda bi, i, j: (0, 0)),
            pl.BlockSpec((1, 1, d), lambda bi, i, j: (bi, 0, 0)),
            pl.BlockSpec((1, 1, d), lambda bi, i, j: (bi, 0, 0)),
            pl.BlockSpec((d, tn), lambda bi, i, j: (0, j)),
        ],
        out_specs=pl.BlockSpec((1, tm, tn), lambda bi, i, j: (bi, i, j)),
        out_shape=jax.ShapeDtypeStruct((b, l, n), out_dtype),
        scratch_shapes=[pltpu.VMEM((tm, d), BF16)],
        compiler_params=_cparams(("parallel", "parallel", "arbitrary")),
        name="norm_mod_matmul",
    )(x, gain.reshape(1, d), sc, sh, w)


SUBLANE = 8
ROUTE_ROWS = SUBLANE


def _route_block(lg, rb_ref):
    tm = lg.shape[0]
    t = lg.T
    sub = lax.broadcasted_iota(jnp.int32, (SUBLANE, tm), 0)
    valid = sub < N_GROUPS
    members = range(EXPERTS_PER_GROUP)
    s = [jnp.where(valid, t[SUBLANE * j:SUBLANE * (j + 1), :], -jnp.inf) for j in members]
    mx = jnp.max(jnp.maximum(jnp.maximum(s[0], s[1]), jnp.maximum(s[2], s[3])), axis=0, keepdims=True)
    e = [jnp.exp(sj - mx) for sj in s]
    den = jnp.sum((e[0] + e[1]) + (e[2] + e[3]), axis=0, keepdims=True)
    sc = [ej / den for ej in e]
    sel = [sc[j] + jnp.concatenate([rb_ref[SUBLANE * j:SUBLANE * (j + 1), :]] * (tm // LANE), axis=1)
           for j in members]
    hi01, lo01 = jnp.maximum(sel[0], sel[1]), jnp.minimum(sel[0], sel[1])
    hi23, lo23 = jnp.maximum(sel[2], sel[3]), jnp.minimum(sel[2], sel[3])
    top1 = jnp.maximum(hi01, hi23)
    top2 = jnp.maximum(jnp.minimum(hi01, hi23), jnp.maximum(lo01, lo23))
    gscore = jnp.where(valid, top1 + top2, -jnp.inf)
    gmax = jnp.max(gscore, axis=0, keepdims=True)
    gidx = jnp.min(jnp.where(gscore == gmax, sub, SUBLANE), axis=0, keepdims=True)
    chosen = sub == gidx
    v = [jnp.sum(jnp.where(chosen, sel[j], 0.0), axis=0, keepdims=True) for j in members]
    p = [jnp.sum(jnp.where(chosen, sc[j], 0.0), axis=0, keepdims=True) for j in members]
    one = lambda cond: jnp.where(cond, 1.0, 0.0)
    picks = []
    for want in (0.0, 1.0):
        idx = jnp.zeros_like(v[0])
        wt = jnp.zeros_like(v[0])
        for j in members:
            rank = sum(one(v[k] > v[j]) for k in members if k != j) + sum(one(v[k] == v[j]) for k in range(j))
            hit = rank == want
            idx = idx + jnp.where(hit, float(j), 0.0)
            wt = wt + jnp.where(hit, p[j], 0.0)
        picks.append((idx, wt))
    (j1, p1), (j2, p2) = picks
    base = (EXPERTS_PER_GROUP * gidx).astype(F32)
    tot = p1 + p2
    rows = [base + j1, base + j2, p1 / tot, p2 / tot]
    return jnp.concatenate(rows + [jnp.zeros((ROUTE_ROWS - len(rows), tm), F32)], axis=0)


def _outproj_kernel(m1_ref, m2_ref, w_ref, x_ref, g_ref, gain_ref, sc_ref, sh_ref, rw_ref, rb_ref,
                    xo_ref, h_ref, rt_ref):
    k1 = m1_ref.shape[-1]
    o = (jnp.dot(m1_ref[0].astype(BF16), w_ref[:k1, :].astype(BF16), preferred_element_type=F32)
         + jnp.dot(m2_ref[0].astype(BF16), w_ref[k1:, :].astype(BF16), preferred_element_type=F32))
    xn = x_ref[0] + g_ref[0] * o
    xo_ref[0] = xn
    h = _norm_mod(xn, gain_ref[...], sc_ref[0], sh_ref[0])
    h_ref[...] = h
    h_hi = h.astype(BF16)
    h_lo = (h - h_hi.astype(F32)).astype(BF16)
    both = jnp.dot(h_hi, rw_ref[...], preferred_element_type=F32)
    lg = (both[:, :LOGIT_PAD] + both[:, LOGIT_PAD:]
          + jnp.dot(h_lo, rw_ref[:, :LOGIT_PAD], preferred_element_type=F32))
    rt_ref[0] = _route_block(lg, rb_ref)


def outproj_residual_norm(m1, m2, w, x, g, gain, sc, sh, rw_pad, rb_pad, *, tm):
    b, l, d = x.shape
    k1, k2 = m1.shape[-1], m2.shape[-1]
    nb = l // tm
    row = lambda bi, i: (bi, i, 0)
    mod = lambda bi, i: (bi, 0, 0)
    const = lambda bi, i: (0, 0)
    return pl.pallas_call(
        _outproj_kernel,
        grid=(b, nb),
        in_specs=[
            pl.BlockSpec((1, tm, k1), row),
            pl.BlockSpec((1, tm, k2), row),
            pl.BlockSpec((k1 + k2, d), const, pipeline_mode=pl.Buffered(1)),
            pl.BlockSpec((1, tm, d), row),
            pl.BlockSpec((1, 1, d), mod),
            pl.BlockSpec((1, d), const),
            pl.BlockSpec((1, 1, d), mod),
            pl.BlockSpec((1, 1, d), mod),
            pl.BlockSpec((d, 2 * LOGIT_PAD), const),
            pl.BlockSpec(rb_pad.shape, const),
        ],
        out_specs=[
            pl.BlockSpec((1, tm, d), row),
            pl.BlockSpec((tm, d), lambda bi, i: (bi * nb + i, 0)),
            pl.BlockSpec((1, ROUTE_ROWS, tm), lambda bi, i: (bi, 0, i)),
        ],
        out_shape=[
            jax.ShapeDtypeStruct((b, l, d), F32),
            jax.ShapeDtypeStruct((b * l, d), F32),
            jax.ShapeDtypeStruct((b, ROUTE_ROWS, l), F32),
        ],
        compiler_params=_cparams(("parallel", "parallel")),
        name="outproj_residual_norm",
    )(m1, m2, w, x, g, gain.reshape(1, d), sc, sh, rw_pad, rb_pad)


def _moe_kernel(te_ref, tsrc_ref, tvalid_ref, x_ref, w1_ref, w3_ref, w2_ref, o_ref):
    i = pl.program_id(0)

    @pl.when(tvalid_ref[i] > 0)
    def _():
        x = x_ref[...].astype(BF16)
        ff = w1_ref.shape[-1]
        fc = ff // MOE_FF_SPLIT
        y = None
        for c in range(MOE_FF_SPLIT):
            cols = slice(c * fc, (c + 1) * fc)
            a = jnp.dot(x, w1_ref[0, 0, :, cols].astype(BF16), preferred_element_type=F32)
            b = jnp.dot(x, w3_ref[0, 0, :, cols].astype(BF16), preferred_element_type=F32)
            mid = (a * jax.nn.sigmoid(a) * b).astype(BF16)
            part = jnp.dot(mid, w2_ref[0, 0, cols, :].astype(BF16), preferred_element_type=F32)
            y = part if y is None else y + part
        o_ref[...] = y

    @pl.when(tvalid_ref[i] == 0)
    def _():
        o_ref[...] = jnp.zeros_like(o_ref)


def moe_grouped(x_sorted, w1, w3, w2, layer, tile_expert, tile_src, tile_valid):
    p, d = x_sorted.shape
    ff = w1.shape[-1]
    tm = MOE_TM
    wmap = lambda i, te, ts, tv: (layer, te[i], 0, 0)
    grid_spec = pltpu.PrefetchScalarGridSpec(
        num_scalar_prefetch=3,
        grid=(p // tm,),
        in_specs=[
            pl.BlockSpec((tm, d), lambda i, te, ts, tv: (ts[i], 0)),
            pl.BlockSpec((1, 1, d, ff), wmap, pipeline_mode=pl.Buffered(1)),
            pl.BlockSpec((1, 1, d, ff), wmap, pipeline_mode=pl.Buffered(1)),
            pl.BlockSpec((1, 1, ff, d), wmap, pipeline_mode=pl.Buffered(1)),
        ],
        out_specs=pl.BlockSpec((tm, d), lambda i, te, ts, tv: (i, 0)),
    )
    return pl.pallas_call(
        _moe_kernel,
        grid_spec=grid_spec,
        out_shape=jax.ShapeDtypeStruct((p, d), F32),
        compiler_params=_cparams(("arbitrary",)),
        name="moe_grouped",
    )(tile_expert, tile_src, tile_valid, x_sorted, w1, w3, w2)


def _row_scatter_kernel(dst_ref, h_ref, init_ref, out_ref, sem):
    del init_ref
    n = h_ref.shape[0]

    def issue(r, carry):
        pltpu.make_async_copy(h_ref.at[pl.ds(r, 1)], out_ref.at[pl.ds(dst_ref[0, 0, r], 1)], sem).start()
        return carry

    lax.fori_loop(0, n, issue, 0, unroll=4)
    pltpu.make_async_copy(h_ref, h_ref, sem).wait()


def row_scatter(h, dst, init, *, step):
    rows, d = h.shape
    nb = rows // step
    assert nb * step == rows
    dst = dst.reshape(TOP_K, nb, 1, step).transpose(1, 0, 2, 3).reshape(nb * TOP_K, 1, step)
    return pl.pallas_call(
        _row_scatter_kernel,
        grid=(nb, TOP_K),
        in_specs=[pl.BlockSpec((1, 1, step), lambda j, k: (j * TOP_K + k, 0, 0), memory_space=pltpu.SMEM),
                  pl.BlockSpec((step, d), lambda j, k: (j, 0)),
                  pl.BlockSpec(memory_space=pl.ANY)],
        out_specs=pl.BlockSpec(memory_space=pl.ANY),
        out_shape=jax.ShapeDtypeStruct(init.shape, init.dtype),
        scratch_shapes=[pltpu.SemaphoreType.DMA(())],
        input_output_aliases={2: 0},
        compiler_params=_cparams(("arbitrary", "arbitrary")),
        name="row_scatter",
    )(dst, h, init)


def _gather_combine_kernel(p0_ref, p1_ref, x_ref, g_ref, w0_ref, w1_ref, y_ref, o_ref, buf, sem):
    tm = x_ref.shape[1]

    def issue(r, carry):
        pltpu.make_async_copy(y_ref.at[pl.ds(p0_ref[0, 0, r], 1)], buf.at[0, pl.ds(r, 1)], sem).start()
        pltpu.make_async_copy(y_ref.at[pl.ds(p1_ref[0, 0, r], 1)], buf.at[1, pl.ds(r, 1)], sem).start()
        return carry

    lax.fori_loop(0, tm, issue, 0, unroll=2)
    pltpu.make_async_copy(buf, buf, sem).wait()
    y = w0_ref[...] * buf[0] + w1_ref[...] * buf[1]
    o_ref[0] = x_ref[0] + g_ref[0] * y


def moe_gather_combine(x, g, y_sorted, pos, w_tok, row0, *, tm):
    b, l, d = x.shape
    n = pos.shape[1]
    nb = l // tm
    assert row0 % tm == 0 and n % tm == 0
    tok = lambda bi, i: row0 // tm + bi * nb + i
    pos = pos.reshape(TOP_K * (n // tm), 1, tm)
    w_col = w_tok.reshape(TOP_K * n, 1)
    idx = lambda k: pl.BlockSpec((1, 1, tm), lambda bi, i: (k * (n // tm) + tok(bi, i), 0, 0),
                                 memory_space=pltpu.SMEM)
    wgt = lambda k: pl.BlockSpec((tm, 1), lambda bi, i: (k * (n // tm) + tok(bi, i), 0))
    return pl.pallas_call(
        _gather_combine_kernel,
        grid=(b, nb),
        in_specs=[idx(0), idx(1),
                  pl.BlockSpec((1, tm, d), lambda bi, i: (bi, i, 0)),
                  pl.BlockSpec((1, 1, d), lambda bi, i: (bi, 0, 0)),
                  wgt(0), wgt(1),
                  pl.BlockSpec(memory_space=pl.ANY)],
        out_specs=pl.BlockSpec((1, tm, d), lambda bi, i: (bi, i, 0)),
        out_shape=jax.ShapeDtypeStruct(x.shape, x.dtype),
        scratch_shapes=[pltpu.VMEM((TOP_K, tm, d), F32), pltpu.SemaphoreType.DMA(())],
        compiler_params=_cparams(("arbitrary", "arbitrary")),
        name="moe_gather_combine",
    )(pos, pos, x, g, w_col, w_col, y_sorted)


CHUNK = 64
SCAN_T = 256
_NT = (((1,), (1,)), ((), ()))
_TN = (((0,), (0,)), ((), ()))
HIGHEST = lax.Precision.HIGHEST


def _block_keep(t, reverse):
    r = lax.broadcasted_iota(jnp.int32, (t, t), 0)
    c = lax.broadcasted_iota(jnp.int32, (t, t), 1)
    shift = CHUNK.bit_length() - 1
    same = (r >> shift) == (c >> shift)
    return same & ((c >= r) if reverse else (c <= r))


def _chunk_order(n, reverse):
    return range(n - 1, -1, -1) if reverse else range(n)


def _block_scan(q, k, v, cum, st, keep, reverse):
    n = q.shape[0] // CHUNK
    ends = [cum[c * CHUNK:c * CHUNK + 1] if reverse else cum[(c + 1) * CHUNK - 1:(c + 1) * CHUNK]
            for c in range(n)]
    last = jnp.concatenate([jnp.broadcast_to(e, (CHUNK, e.shape[1])) for e in ends], axis=0)
    qe = (q * jnp.exp(cum)).astype(BF16)
    ke = (k * jnp.exp(-cum)).astype(BF16)
    kd = (k * jnp.exp(last - cum)).astype(BF16)
    vb = v.astype(BF16)
    att = lax.dot_general(qe, ke, _NT, preferred_element_type=F32)
    att = jnp.where(keep, att, 0.0).astype(BF16)
    intra = jnp.dot(att, vb, preferred_element_type=F32)
    outs = [None] * n
    for c in _chunk_order(n, reverse):
        sl = slice(c * CHUNK, (c + 1) * CHUNK)
        outs[c] = intra[sl] + lax.dot_general(qe[sl], st.astype(BF16), _NT, preferred_element_type=F32)
        st = jnp.exp(ends[c]) * st + lax.dot_general(vb[sl], kd[sl], _TN, preferred_element_type=F32)
    return jnp.concatenate(outs, axis=0), st


def _merge(o, ob, g, gain):
    tot = o + ob
    y = tot * lax.rsqrt(jnp.mean(tot * tot, axis=-1, keepdims=True) + EPS) * gain
    return y * (g * jax.nn.sigmoid(g))


def _scan_refs(rest, merge):
    if merge:
        return rest
    oc_ref, ol_ref, st_ref = rest
    return (None,) * 5 + (oc_ref, ol_ref, st_ref)


def _scan_heads(head_fn, dk, dv, v_ref, o_ref, ob_ref, g_ref, gn_ref, st_ref, keep, reverse):
    outs = []
    for h in range(st_ref.shape[0]):
        kcol = slice(h * dk, (h + 1) * dk)
        vcol = slice(h * dv, (h + 1) * dv)
        q, k, cum = head_fn(h, kcol)
        o, st = _block_scan(q, k, v_ref[0, :, vcol], cum, st_ref[h], keep, reverse)
        st_ref[h] = st
        outs.append(o if ob_ref is None else _merge(o, ob_ref[0, :, vcol], g_ref[0, :, vcol], gn_ref[...]))
    o_ref[0] = jnp.concatenate(outs, axis=1).astype(o_ref.dtype)


def _gla_kernel(qc, kc, vc, lrc, ql, kl, vl, lrl, wg, bg, *rest, reverse, merge):
    obc, obl, gc, gl, gn, oc_ref, ol_ref, st_ref = _scan_refs(rest, merge)

    def run(q_ref, k_ref, v_ref, lr_ref, o_ref, ob_ref, g_ref):
        keep = _block_keep(q_ref.shape[1], reverse)
        pre = jnp.dot(lr_ref[0], wg[...], preferred_element_type=F32, precision=HIGHEST) + bg[...]
        la = (jnp.minimum(pre, 0.0) - jnp.log(1.0 + jnp.exp(-jnp.abs(pre)))) * (1.0 / GLA_GATE_NORMALIZER)
        cum = jnp.dot(keep.astype(F32), la, preferred_element_type=F32, precision=HIGHEST)

        def head(h, kcol):
            return q_ref[0, :, kcol] * (GLA_DK ** -0.5), k_ref[0, :, kcol], cum[:, kcol]

        _scan_heads(head, GLA_DK, GLA_DV, v_ref, o_ref, ob_ref, g_ref, gn, st_ref, keep, reverse)

    @pl.when(pl.program_id(2) == 0)
    def _():
        st_ref[...] = jnp.zeros_like(st_ref)
        run(qc, kc, vc, lrc, oc_ref, obc, gc)

    @pl.when(pl.program_id(2) > 0)
    def _():
        run(ql, kl, vl, lrl, ol_ref, obl, gl)


def _ret_kernel(qc, kc, vc, ql, kl, vl, cos_ref, sin_ref, cum_ref, *rest, reverse, merge):
    obc, obl, gc, gl, gn, oc_ref, ol_ref, st_ref = _scan_refs(rest, merge)

    def run(q_ref, k_ref, v_ref, o_ref, ob_ref, g_ref, latent):
        t = q_ref.shape[1]
        keep = _block_keep(t, reverse)

        def head(h, kcol):
            q = q_ref[0, :, kcol]
            k = k_ref[0, :, kcol]
            if latent:
                cs = cos_ref[...]
                sn = sin_ref[...]
                q = q * cs + pltpu.roll(q, RET_DK // 2, 1) * sn
                k = k * cs + pltpu.roll(k, RET_DK // 2, 1) * sn
            return q, k * (RET_DK ** -0.5), jnp.concatenate([cum_ref[h]] * (t // CHUNK), axis=0)

        _scan_heads(head, RET_DK, RET_DV, v_ref, o_ref, ob_ref, g_ref, gn, st_ref, keep, reverse)

    @pl.when(pl.program_id(2) == 0)
    def _():
        st_ref[...] = jnp.zeros_like(st_ref)
        run(qc, kc, vc, oc_ref, obc, gc, False)

    @pl.when(pl.program_id(2) > 0)
    def _():
        run(ql, kl, vl, ol_ref, obl, gl, True)


def _scan_call(body, zc, zl, heads, dk, dv, col_q, col_k, col_v, shared_cols, extra_specs, merge_in,
               *, reverse, name):
    b, lc, _ = zc.shape
    l = zl.shape[1]
    t = SCAN_T
    nblk = l // t
    if reverse:
        blk = lambda i: jnp.where(i == 0, nblk - 1, nblk - i)
    else:
        blk = lambda i: jnp.maximum(i - 1, 0)

    def ctx_spec(w, col, per_head=True):
        return pl.BlockSpec((1, lc, w), lambda bi, h, i: (bi, 0, col // w + (h if per_head else 0)))

    def lat_spec(w, col, per_head=True):
        return pl.BlockSpec((1, t, w), lambda bi, h, i: (bi, blk(i), col // w + (h if per_head else 0)))

    hk, hv = heads * dk, heads * dv
    ins = [zc, zc, zc] + [zc] * len(shared_cols) + [zl, zl, zl] + [zl] * len(shared_cols)
    specs = [ctx_spec(hk, col_q), ctx_spec(hk, col_k), ctx_spec(hv, col_v)]
    specs += [ctx_spec(w, col, False) for w, col in shared_cols]
    specs += [lat_spec(hk, col_q), lat_spec(hk, col_k), lat_spec(hv, col_v)]
    specs += [lat_spec(w, col, False) for w, col in shared_cols]
    for a, s in extra_specs:
        ins.append(a)
        specs.append(s)
    if merge_in is not None:
        obc, obl, col_g, gain = merge_in
        ins += [obc, obl, zc, zl, gain.reshape(1, dv)]
        specs += [ctx_spec(hv, 0), lat_spec(hv, 0), ctx_spec(hv, col_g), lat_spec(hv, col_g),
                  pl.BlockSpec((1, dv), lambda bi, h, i: (0, 0))]
        out_dtype = BF16
    else:
        out_dtype = F32
    return pl.pallas_call(
        functools.partial(body, reverse=reverse, merge=merge_in is not None),
        grid=(b, 1, 1 + nblk),
        in_specs=specs,
        out_specs=[ctx_spec(hv, 0), lat_spec(hv, 0)],
        out_shape=[jax.ShapeDtypeStruct((b, lc, heads * dv), out_dtype),
                   jax.ShapeDtypeStruct((b, l, heads * dv), out_dtype)],
        scratch_shapes=[pltpu.VMEM((heads, dv, dk), F32)],
        compiler_params=_cparams(("parallel", "parallel", "arbitrary")),
        name=name,
    )(*ins)


EV_COL = dict(q=0, k=512, v=1024, g=2048, u=3072, lr=6144)
OD_COL = dict(q=0, k=512, v=1024, g=2048, aq=3072, ak=4096, av=4352)


def gla_bidir(zc, zl, gla_w_gate, gla_b_gate, gla_norm):
    hk = GLA_HEADS * GLA_DK
    outs = None
    for d, reverse in ((1, True), (0, False)):
        wg = jnp.zeros((LANE, hk), F32).at[d * GLA_LOWRANK:(d + 1) * GLA_LOWRANK].set(gla_w_gate[d])
        shared_cols = [(LANE, EV_COL["lr"])]
        extra_specs = [(wg, pl.BlockSpec((LANE, hk), lambda bi, h, i: (0, 0))),
                       (gla_b_gate[d].reshape(1, hk), pl.BlockSpec((1, hk), lambda bi, h, i: (0, 0)))]
        merge_in = None if outs is None else (outs[0], outs[1], EV_COL["g"], gla_norm)
        outs = _scan_call(_gla_kernel, zc, zl, GLA_HEADS, GLA_DK, GLA_DV, EV_COL["q"], EV_COL["k"],
                          EV_COL["v"], shared_cols, extra_specs, merge_in, reverse=reverse,
                          name="gla_bwd" if reverse else "gla_fwd_merge")
    return outs


def retention_bidir(zc, zl, ret_norm):
    l = zl.shape[1]
    inv = ROPE_BASE ** (-jnp.linspace(0.0, 1.0, RET_DK // 2, dtype=F32))
    ang = jnp.arange(l, dtype=F32)[:, None] * inv[None, :]
    cos2 = jnp.concatenate([jnp.cos(ang), jnp.cos(ang)], axis=-1)
    sin2 = jnp.concatenate([-jnp.sin(ang), jnp.sin(ang)], axis=-1)
    log_g_f = jnp.log(1.0 - 2.0 ** (-5.0 - jnp.arange(RET_HEADS, dtype=F32)))
    idx = jnp.arange(CHUNK, dtype=F32)
    nblk = l // SCAN_T
    outs = None
    for reverse in (True, False):
        if reverse:
            cum = (CHUNK - idx)[None, :, None] * log_g_f[::-1][:, None, None]
            blk = lambda i: jnp.where(i == 0, nblk - 1, nblk - i)
        else:
            cum = (idx + 1.0)[None, :, None] * log_g_f[:, None, None]
            blk = lambda i: jnp.maximum(i - 1, 0)
        cum = jnp.broadcast_to(cum, (RET_HEADS, CHUNK, RET_DK))
        tab_spec = pl.BlockSpec((SCAN_T, RET_DK), lambda bi, h, i, blk=blk: (blk(i), 0))
        extra_specs = [(cos2, tab_spec), (sin2, tab_spec),
                       (cum, pl.BlockSpec((RET_HEADS, CHUNK, RET_DK), lambda bi, h, i: (0, 0, 0)))]
        merge_in = None if outs is None else (outs[0], outs[1], OD_COL["g"], ret_norm)
        outs = _scan_call(_ret_kernel, zc, zl, RET_HEADS, RET_DK, RET_DV, OD_COL["q"], OD_COL["k"],
                          OD_COL["v"], [], extra_specs, merge_in, reverse=reverse,
                          name="ret_bwd" if reverse else "ret_fwd_merge")
    return outs


ATT_QKV = (ATT_HEADS + 2 * ATT_KV_HEADS) * ATT_HD


def _att_prep_kernel(z_ref, qn_ref, kn_ref, *rest, latent):
    if latent:
        cos_ref, sin_ref, o_ref = rest
        cs, sn = cos_ref[...], sin_ref[...]
        lane = lax.broadcasted_iota(jnp.int32, cs.shape, 1)
        low = (lane & (ATT_HD // 2 - 1)) < (ATT_HD // 4)
    else:
        (o_ref,) = rest
    for h in range(ATT_HEADS + ATT_KV_HEADS):
        x = z_ref[0, :, h * ATT_HD:(h + 1) * ATT_HD]
        gain = qn_ref[...] if h < ATT_HEADS else kn_ref[...]
        y = x * lax.rsqrt(jnp.mean(x * x, axis=-1, keepdims=True) + EPS) * gain
        if latent:
            partner = jnp.where(low, pltpu.roll(y, ATT_HD - ATT_HD // 4, 1), pltpu.roll(y, ATT_HD // 4, 1))
            y = y * cs + partner * sn
        if h < ATT_HEADS:
            y = y * (ATT_HD ** -0.5)
        o_ref[0, :, h * ATT_HD:(h + 1) * ATT_HD] = y.astype(o_ref.dtype)
    v0 = (ATT_HEADS + ATT_KV_HEADS) * ATT_HD
    o_ref[0, :, v0:] = z_ref[0, :, v0:].astype(o_ref.dtype)


def att_prep(z, att_q_norm, att_k_norm, tables, *, tm):
    b, l, _ = z.shape
    latent = tables is not None
    row = lambda bi, i: (bi, i, 0)
    const = lambda bi, i: (0, 0)
    ins = [z, att_q_norm.reshape(1, ATT_HD), att_k_norm.reshape(1, ATT_HD)]
    specs = [pl.BlockSpec((1, tm, ATT_QKV), lambda bi, i: (bi, i, OD_COL["aq"] // ATT_QKV)),
             pl.BlockSpec((1, ATT_HD), const), pl.BlockSpec((1, ATT_HD), const)]
    if latent:
        ins += list(tables)
        specs += [pl.BlockSpec((tm, ATT_HD), lambda bi, i: (i, 0))] * 2
    return pl.pallas_call(
        functools.partial(_att_prep_kernel, latent=latent),
        grid=(b, l // tm),
        in_specs=specs,
        out_specs=pl.BlockSpec((1, tm, ATT_QKV), row),
        out_shape=jax.ShapeDtypeStruct((b, l, ATT_QKV), BF16),
        compiler_params=_cparams(("parallel", "parallel")),
        name="att_prep",
    )(*ins)


def _axial_tables(l):
    quarter = ATT_HD // 4
    inv = ROPE_BASE ** (-jnp.arange(quarter, dtype=F32) / quarter)
    t = jnp.arange(l)
    a_row = (t // GRID_W).astype(F32)[:, None] * inv[None, :]
    a_col = (t % GRID_W).astype(F32)[:, None] * inv[None, :]
    cos = jnp.concatenate([jnp.cos(a_row)] * 2 + [jnp.cos(a_col)] * 2, axis=-1)
    sin = jnp.concatenate([-jnp.sin(a_row), jnp.sin(a_row), -jnp.sin(a_col), jnp.sin(a_col)], axis=-1)
    return cos, sin


def _att_kernel(sink_ref, q_ref, kp, kc, kn, vp, vc, vn, ck, cv, o_ref, *, seq):
    kvh = pl.program_id(1)
    n = pl.program_id(2)
    w = ATT_BLOCK
    q = jnp.concatenate([q_ref[0, :, g * ATT_HD:(g + 1) * ATT_HD] for g in range(ATT_GROUP)], axis=0)
    kw = jnp.concatenate([kp[0], kc[0], kn[0]], axis=0)
    vw = jnp.concatenate([vp[0], vc[0], vn[0]], axis=0)
    s_w = lax.dot_general(q, kw, _NT, preferred_element_type=F32)
    s_c = lax.dot_general(q, ck[0], _NT, preferred_element_type=F32)
    r = lax.broadcasted_iota(jnp.int32, s_w.shape, 0) & (w - 1)
    c = lax.broadcasted_iota(jnp.int32, s_w.shape, 1) - w
    kpos = n * w + c
    valid = (jnp.abs(r - c) <= ATT_WINDOW) & (kpos >= 0) & (kpos < seq)
    s_w = jnp.where(valid, s_w, -jnp.inf)
    sink = jnp.concatenate([jnp.full((w, 1), sink_ref[kvh * ATT_GROUP + g], F32) for g in range(ATT_GROUP)],
                           axis=0)
    m = jnp.maximum(jnp.maximum(jnp.max(s_w, axis=-1, keepdims=True), jnp.max(s_c, axis=-1, keepdims=True)),
                    sink)
    p_w = jnp.exp(s_w - m)
    p_c = jnp.exp(s_c - m)
    denom = (jnp.sum(p_w, axis=-1, keepdims=True) + jnp.sum(p_c, axis=-1, keepdims=True)
             + jnp.exp(sink - m))
    o = (jnp.dot(p_w.astype(BF16), vw, preferred_element_type=F32)
         + jnp.dot(p_c.astype(BF16), cv[0], preferred_element_type=F32)) / denom
    for g in range(ATT_GROUP):
        o_ref[0, :, g * ATT_HD:(g + 1) * ATT_HD] = o[g * w:(g + 1) * w].astype(o_ref.dtype)


def window_attention(qkv_l, qkv_c, att_sink):
    b, l, _ = qkv_l.shape
    lc = qkv_c.shape[1]
    w = ATT_BLOCK
    nb = l // w
    gw = ATT_GROUP * ATT_HD
    kcol = ATT_HEADS
    vcol = ATT_HEADS + ATT_KV_HEADS

    def kv_spec(col, shift):
        return pl.BlockSpec((1, w, ATT_HD),
                            lambda bi, h, n, s: (bi, jnp.clip(n + shift, 0, nb - 1), col + h))

    grid_spec = pltpu.PrefetchScalarGridSpec(
        num_scalar_prefetch=1,
        grid=(b, ATT_KV_HEADS, nb),
        in_specs=[pl.BlockSpec((1, w, gw), lambda bi, h, n, s: (bi, n, h))]
        + [kv_spec(kcol, d) for d in (-1, 0, 1)] + [kv_spec(vcol, d) for d in (-1, 0, 1)]
        + [pl.BlockSpec((1, lc, ATT_HD), lambda bi, h, n, s: (bi, 0, kcol + h)),
           pl.BlockSpec((1, lc, ATT_HD), lambda bi, h, n, s: (bi, 0, vcol + h))],
        out_specs=pl.BlockSpec((1, w, gw), lambda bi, h, n, s: (bi, n, h)),
    )
    return pl.pallas_call(
        functools.partial(_att_kernel, seq=l),
        grid_spec=grid_spec,
        out_shape=jax.ShapeDtypeStruct((b, l, ATT_HEADS * ATT_HD), BF16),
        compiler_params=_cparams(("parallel", "parallel", "arbitrary")),
        name="window_attention",
    )(att_sink.astype(F32), qkv_l, qkv_l, qkv_l, qkv_l, qkv_l, qkv_l, qkv_l, qkv_c, qkv_c)


def moe_experts(hs, route, w1, w3, w2, layer):
    n = sum(h.shape[0] for h in hs)
    d = hs[0].shape[1]
    tm = MOE_TM
    flat_e = route[0:TOP_K].astype(jnp.int32).reshape(-1)
    w_tok = route[TOP_K:2 * TOP_K]
    onehot = (flat_e[:, None] == jnp.arange(N_EXPERTS)[None, :]).astype(jnp.int32)
    rank = jnp.sum((jnp.cumsum(onehot, axis=0) - onehot) * onehot, axis=-1)
    counts = jnp.sum(onehot, axis=0)
    padded = ((counts + tm - 1) // tm) * tm
    ends = jnp.cumsum(padded)
    offs = ends - padded
    pos = (offs[flat_e] + rank).astype(jnp.int32)
    n_tiles = (2 * n + tm - 1) // tm + N_EXPERTS
    p = n_tiles * tm
    used = ends[-1] // tm
    tile_ids = jnp.arange(n_tiles, dtype=jnp.int32)
    tile_valid = (tile_ids < used).astype(jnp.int32)
    tile_src = jnp.minimum(tile_ids, used - 1).astype(jnp.int32)
    tile_expert = jnp.minimum(
        jnp.searchsorted(ends, tile_src * tm, side="right"), N_EXPERTS - 1).astype(jnp.int32)
    x_sorted = jnp.zeros((p, d), F32)
    pos = pos.reshape(TOP_K, n)
    r0 = 0
    for h in hs:
        rows = h.shape[0]
        x_sorted = row_scatter(h, pos[:, r0:r0 + rows], x_sorted, step=min(rows, MOVE_ROWS))
        r0 += rows
    y_sorted = moe_grouped(x_sorted, w1, w3, w2, layer, tile_expert, tile_src, tile_valid)
    return y_sorted, pos, w_tok


FFT_Q = 128
HY_COLS = (HY_ORDER + 1) * HY_CH


def _hy_conv_kernel(u_ref, prev_ref, next_ref, w_ref, b_ref, z_ref, *g_refs):
    i = pl.program_id(1)
    u = u_ref[0]
    t = u.shape[0]
    row = lax.broadcasted_iota(jnp.int32, (t, 1), 0)
    has_prev = (i > 0).astype(F32)
    has_next = (i < pl.num_programs(1) - 1).astype(F32)
    up = jnp.where(row == 0, prev_ref[0, SUBLANE - 1:SUBLANE, :] * has_prev, pltpu.roll(u, 1, 0))
    dn = jnp.where(row == t - 1, next_ref[0, 0:1, :] * has_next, pltpu.roll(u, t - 1, 0))
    y = up * w_ref[0:1, :] + u * w_ref[1:2, :] + dn * w_ref[2:3, :] + b_ref[...]
    for n, o_ref in enumerate((z_ref,) + g_refs):
        o_ref[0] = y[:, n * HY_CH:(n + 1) * HY_CH]


def hy_short_conv(z, w, bias, *, tm):
    b, l, _ = z.shape
    ucol = EV_COL["u"] // HY_COLS
    nh = l // SUBLANE
    hb = tm // SUBLANE
    row = lambda bi, i: (bi, i, 0)
    return pl.pallas_call(
        _hy_conv_kernel,
        grid=(b, l // tm),
        in_specs=[
            pl.BlockSpec((1, tm, HY_COLS), lambda bi, i: (bi, i, ucol)),
            pl.BlockSpec((1, SUBLANE, HY_COLS), lambda bi, i: (bi, jnp.maximum(i * hb - 1, 0), ucol)),
            pl.BlockSpec((1, SUBLANE, HY_COLS), lambda bi, i: (bi, jnp.minimum((i + 1) * hb, nh - 1), ucol)),
            pl.BlockSpec((3, HY_COLS), lambda bi, i: (0, 0)),
            pl.BlockSpec((1, HY_COLS), lambda bi, i: (0, 0)),
        ],
        out_specs=[pl.BlockSpec((1, tm, HY_CH), row)] * (HY_ORDER + 1),
        out_shape=[jax.ShapeDtypeStruct((b, l, HY_CH), F32)] * (HY_ORDER + 1),
        compiler_params=_cparams(("parallel", "parallel")),
        name="hy_short_conv",
    )(z, z, z, w, bias.reshape(1, HY_COLS))


def _hy_filter_kernel(feat_ref, w1_ref, b1_ref, w2_ref, b2_ref, w3_ref, delta_ref, kern_ref, den_ref):
    feat = feat_ref[...]
    h = jnp.sin(jnp.dot(feat, w1_ref[...], preferred_element_type=F32, precision=HIGHEST) + b1_ref[...])
    h = jnp.sin(jnp.dot(h, w2_ref[...], preferred_element_type=F32, precision=HIGHEST) + b2_ref[...])
    f = jnp.dot(h.astype(BF16), w3_ref[...].astype(BF16), preferred_element_type=F32)
    t = feat[:, 0:1]
    live = feat[:, LANE - 1:LANE]
    kern = f * (jnp.exp(-t * delta_ref[...]) + HY_SHIFT) * live
    kern_ref[...] = kern

    @pl.when((pl.program_id(0) == 0) & (pl.program_id(1) == 0))
    def _():
        den_ref[...] = jnp.zeros_like(den_ref)

    den_ref[...] += jnp.sum(jnp.abs(kern), axis=0, keepdims=True)


def hy_filters(l, w1, b1, w2, b2, w3, *, tr):
    t = jnp.linspace(0.0, 1.0, l, dtype=F32)[:, None]
    w = (2.0 * math.pi / l) * jnp.arange(l, dtype=F32)[:, None]
    bands = jnp.linspace(1e-4, HY_BANDS - 1.0, HY_BANDS, dtype=F32)[None, :]
    feats = jnp.concatenate([t, jnp.cos(bands * w), -jnp.sin(bands * w)], axis=-1)
    feats = jnp.pad(feats, ((0, 0), (0, LANE - 1 - HY_EMB)))
    feats = jnp.concatenate([feats, jnp.ones((l, 1), F32)], axis=-1)
    table = jnp.concatenate([feats, jnp.zeros((1, LANE), F32), feats[:0:-1]], axis=0)
    w1p = jnp.pad(w1.astype(F32), ((0, LANE - HY_EMB), (0, 0)))
    w3s = w3.astype(F32).reshape(HY_FFN, HY_ORDER, 2, HY_CH).transpose(0, 2, 1, 3).reshape(HY_FFN, -1)
    deltas = jnp.abs(jnp.linspace(HY_MIN_DECAY, HY_MAX_DECAY, HY_CH, dtype=F32))
    deltas = jnp.tile(deltas, HY_ORDER).reshape(1, -1)
    oc = HY_ORDER * HY_CH
    nb = l // tr
    const = lambda s, i: (0, 0)
    return pl.pallas_call(
        _hy_filter_kernel,
        grid=(2, nb),
        in_specs=[
            pl.BlockSpec((tr, LANE), lambda s, i: (s * nb + i, 0)),
            pl.BlockSpec((LANE, HY_FFN), const), pl.BlockSpec((1, HY_FFN), const),
            pl.BlockSpec((HY_FFN, HY_FFN), const), pl.BlockSpec((1, HY_FFN), const),
            pl.BlockSpec((HY_FFN, oc), lambda s, i: (0, s)),
            pl.BlockSpec((1, oc), const),
        ],
        out_specs=[pl.BlockSpec((tr, oc), lambda s, i: (s * nb + i, 0)), pl.BlockSpec((1, oc), const)],
        out_shape=[jax.ShapeDtypeStruct((2 * l, oc), F32), jax.ShapeDtypeStruct((1, oc), F32)],
        compiler_params=_cparams(("arbitrary", "arbitrary")),
        name="hy_filters",
    )(table, w1p, b1.reshape(1, -1).astype(F32), w2.astype(F32), b2.reshape(1, -1).astype(F32), w3s, deltas)


def _cis(num, den):
    ang = (-2.0 * math.pi / den) * (num % den).astype(F32)
    return jnp.cos(ang), jnp.sin(ang)


def _block_complex(re, im):
    return jnp.concatenate([jnp.concatenate([re, -im], axis=-1), jnp.concatenate([im, re], axis=-1)], axis=-2)


def _fft_tables(p):
    q = FFT_Q
    n = p * q
    i_p = jnp.arange(p, dtype=jnp.int32)
    i_q = jnp.arange(q, dtype=jnp.int32)
    fr, fi = _cis(i_p[:, None] * i_p[None, :], p)
    f_filt = jnp.concatenate([fr, fi], axis=0)
    f_in = _block_complex(fr[:, :p // 2], fi[:, :p // 2])
    f_out = _block_complex(fr[:p // 2, :], -fi[:p // 2, :]) * (1.0 / n)
    num = i_q[None, None, :] * i_p[:, None, None] + p * i_q[None, :, None] * i_q[None, None, :]
    gr, gi = _cis(num, n)
    g = _block_complex(gr, gi)
    h = _block_complex(jnp.swapaxes(gr, 1, 2), -jnp.swapaxes(gi, 1, 2))
    return tuple(a.astype(BF16) for a in (f_filt, f_in, g, h, f_out))


FFT_TC = 512


def _fft_in_kernel(f_ref, x_ref, o_ref, xs_ref):
    for s in range(x_ref.shape[1]):
        xs_ref[...] = x_ref[:, s, :]
        o_ref[:, s, :] = jnp.dot(f_ref[...], xs_ref[...].astype(BF16), preferred_element_type=F32)


def fft_in(f, x):
    k, q, c = x.shape
    tc = FFT_TC
    return pl.pallas_call(
        _fft_in_kernel,
        grid=(q // SUBLANE, c // tc),
        in_specs=[pl.BlockSpec(f.shape, lambda j, ci: (0, 0)),
                  pl.BlockSpec((k, SUBLANE, tc), lambda j, ci: (0, j, ci))],
        out_specs=pl.BlockSpec((f.shape[0], SUBLANE, tc), lambda j, ci: (0, j, ci)),
        out_shape=jax.ShapeDtypeStruct((f.shape[0], q, c), F32),
        scratch_shapes=[pltpu.VMEM((k, tc), F32)],
        compiler_params=_cparams(("parallel", "parallel")),
        name="fft_in",
    )(f, x)


def _fft_filt_mid_kernel(g_ref, a_ref, inv_ref, o_ref):
    q2, c = o_ref.shape[1], o_ref.shape[2]
    a = a_ref[...].reshape(q2, c).astype(BF16)
    o_ref[0] = jnp.dot(g_ref[0], a, preferred_element_type=F32) * inv_ref[...]


def fft_filt_mid(g, a, den):
    p = g.shape[0]
    c = den.shape[1]
    a = a.reshape(2, p, FFT_Q, c)
    return pl.pallas_call(
        _fft_filt_mid_kernel,
        grid=(p,),
        in_specs=[pl.BlockSpec((1, 2 * FFT_Q, 2 * FFT_Q), lambda k: (k, 0, 0)),
                  pl.BlockSpec((2, 1, FFT_Q, c), lambda k: (0, k, 0, 0)),
                  pl.BlockSpec((1, c), lambda k: (0, 0))],
        out_specs=pl.BlockSpec((1, 2 * FFT_Q, c), lambda k: (k, 0, 0)),
        out_shape=jax.ShapeDtypeStruct((p, 2 * FFT_Q, c), F32),
        compiler_params=_cparams(("parallel",)),
        name="fft_filt_mid",
    )(g, a, 1.0 / den)


def _cmul_rows(s, k):
    half = s.shape[0] // 2
    sr, si, kr, ki = s[:half], s[half:], k[:half], k[half:]
    return jnp.concatenate([sr * kr - si * ki, sr * ki + si * kr], axis=0)


def _fft_mid_kernel(g_ref, h_ref, a_ref, k_ref, o_ref):
    q2, c = k_ref.shape[1], k_ref.shape[2]
    a = a_ref[...].reshape(q2, c).astype(BF16)
    y = jnp.dot(g_ref[0], a, preferred_element_type=F32)
    z = _cmul_rows(y, k_ref[0]).astype(BF16)
    o_ref[...] = jnp.dot(h_ref[0], z, preferred_element_type=F32).reshape(o_ref.shape)


def fft_mid(g, h, a, kf, order):
    p = g.shape[0]
    c = HY_CH
    mat = pl.BlockSpec((1, 2 * FFT_Q, 2 * FFT_Q), lambda k: (k, 0, 0))
    sig = pl.BlockSpec((2, 1, FFT_Q, c), lambda k: (0, k, 0, 0))
    return pl.pallas_call(
        _fft_mid_kernel,
        grid=(p,),
        in_specs=[mat, mat, sig, pl.BlockSpec((1, 2 * FFT_Q, c), lambda k: (k, 0, order))],
        out_specs=sig,
        out_shape=jax.ShapeDtypeStruct((2, p, FFT_Q, c), F32),
        compiler_params=_cparams(("parallel",)),
        name="fft_mid",
    )(g, h, a.reshape(2, p, FFT_Q, c), kf).reshape(2 * p, FFT_Q, c)


def _fft_out_kernel(f_ref, c_ref, z_ref, gate_ref, bias_ref, o_ref, cs_ref, zs_ref, gs_ref):
    for s in range(c_ref.shape[1]):
        cs_ref[...] = c_ref[:, s, :]
        zs_ref[...] = z_ref[:, s, :]
        gs_ref[...] = gate_ref[:, s, :]
        y = jnp.dot(f_ref[...], cs_ref[...].astype(BF16), preferred_element_type=F32)
        o_ref[:, s, :] = gs_ref[...] * (y + zs_ref[...] * bias_ref[...])


def fft_out(f, cm, z, gate, bias):
    p2, q, c = cm.shape
    tc = FFT_TC
    rows = f.shape[0]
    half = pl.BlockSpec((rows, SUBLANE, tc), lambda j, ci: (0, j, ci))
    return pl.pallas_call(
        _fft_out_kernel,
        grid=(q // SUBLANE, c // tc),
        in_specs=[pl.BlockSpec(f.shape, lambda j, ci: (0, 0)),
                  pl.BlockSpec((p2, SUBLANE, tc), lambda j, ci: (0, j, ci)),
                  half, half,
                  pl.BlockSpec((1, tc), lambda j, ci: (0, ci))],
        out_specs=half,
        out_shape=jax.ShapeDtypeStruct(z.shape, F32),
        scratch_shapes=[pltpu.VMEM((p2, tc), F32), pltpu.VMEM((rows, tc), F32), pltpu.VMEM((rows, tc), F32)],
        compiler_params=_cparams(("parallel", "parallel")),
        name="fft_out",
    )(f, cm, z, gate, bias)


def _hy_ctx_kernel(ff_ref, fin_ref, fout_ref, kern_ref, inv_ref, z_ref, g1_ref, g2_ref, bias_ref, o_ref):
    b, l, c = z_ref.shape
    z = z_ref[...].reshape(b * l, c)
    gates = (g1_ref, g2_ref)
    for n in range(HY_ORDER):
        cols = slice(n * c, (n + 1) * c)
        kf = (jnp.dot(ff_ref[...], kern_ref[:, cols].astype(BF16), preferred_element_type=F32)
              * inv_ref[:, cols])
        s = jnp.dot(fin_ref[...], z.astype(BF16), preferred_element_type=F32)
        y = jnp.dot(fout_ref[...], _cmul_rows(s, kf).astype(BF16), preferred_element_type=F32)
        z = gates[n][...].reshape(b * l, c) * (y + z * bias_ref[n:n + 1, :])
    o_ref[...] = z.reshape(b, l, c).astype(o_ref.dtype)


def hy_ctx_conv(kern, den, z0, g1, g2, bias):
    b, l, c = z0.shape
    assert b == 2
    n = 2 * l
    i_n = jnp.arange(n, dtype=jnp.int32)
    fr, fi = _cis(i_n[:, None] * i_n[None, :], n)
    ff = jnp.concatenate([fr, fi], axis=0).astype(BF16)
    fin = _block_complex(fr[:, :l], fi[:, :l]).astype(BF16)
    fout = (_block_complex(fr[:l, :], -fi[:l, :]) * (1.0 / n)).astype(BF16)
    return pl.pallas_call(
        _hy_ctx_kernel,
        out_shape=jax.ShapeDtypeStruct((b, l, c), BF16),
        compiler_params=pltpu.CompilerParams(vmem_limit_bytes=VMEM_LIMIT),
        name="hy_ctx_conv",
    )(ff, fin, fout, kern, 1.0 / den, z0, g1, g2, bias)


def hyena_latent(z, conv_w, conv_b, w1, b1, w2, b2, w3, bias):
    b, l, _ = z.shape
    assert b == 2 and (2 * l) % (2 * FFT_Q) == 0
    p = 2 * l // FFT_Q
    z0, *gates = hy_short_conv(z, conv_w, conv_b, tm=256)
    kern, den = hy_filters(l, w1, b1, w2, b2, w3, tr=512)
    f_filt, f_in, g, h, f_out = _fft_tables(p)
    kf = fft_filt_mid(g, fft_in(f_filt, kern.reshape(p, FFT_Q, HY_ORDER * HY_CH)), den)
    split = lambda t: t.reshape(p, FFT_Q, HY_CH)
    zz = split(z0)
    for n in range(HY_ORDER):
        cm = fft_mid(g, h, fft_in(f_in, zz), kf, n)
        zz = fft_out(f_out, cm, zz, split(gates[n]), bias[n].astype(F32).reshape(1, HY_CH))
    return zz.reshape(b, l, HY_CH)


def hyena_ctx(z, conv_w, conv_b, w1, b1, w2, b2, w3, bias):
    b, l, _ = z.shape
    z0, g1, g2 = hy_short_conv(z, conv_w, conv_b, tm=l)
    kern, den = hy_filters(l, w1, b1, w2, b2, w3, tr=l)
    return hy_ctx_conv(kern, den, z0, g1, g2, bias.astype(F32))


def even_mixer_core(zc, zl, gla_w_gate, gla_b_gate, gla_norm, hy_conv_w, hy_conv_b,
                    hy_w1, hy_b1, hy_w2, hy_b2, hy_w3, hy_bias):
    gc, gl = gla_bidir(zc, zl, gla_w_gate, gla_b_gate, gla_norm)
    hy = (hy_conv_w, hy_conv_b, hy_w1, hy_b1, hy_w2, hy_b2, hy_w3, hy_bias)
    return (gc, hyena_ctx(zc, *hy)), (gl, hyena_latent(zl, *hy))


def odd_mixer_core(zc, zl, ret_norm, att_q_norm, att_k_norm, att_sink):
    _, rl = retention_bidir(zc, zl, ret_norm)
    qkv_l = att_prep(zl, att_q_norm, att_k_norm, _axial_tables(zl.shape[1]), tm=512)
    qkv_c = att_prep(zc, att_q_norm, att_k_norm, None, tm=zc.shape[1])
    return rl, window_attention(qkv_l, qkv_c, att_sink)


def kernel(x, c, ctx, c_ctx, ada_w, ada_b, norm_mix, norm_ffn, ev_w_in, ev_w_out, gla_w_gate, gla_b_gate, gla_norm, hy_conv_w, hy_conv_b, hy_w1, hy_b1, hy_w2, hy_b2, hy_w3, hy_bias, od_w_in, od_w_out, ret_norm, att_q_norm, att_k_norm, att_sink, router_w, router_b, moe_w1, moe_w3, moe_w2):
    assert DEPTH == 2
    bsz, seq, d = x.shape
    n_ctx = ctx.shape[1]
    c_pad = jnp.zeros((8, d), F32).at[:bsz].set(c).at[bsz].set(c_ctx)
    mods = ada_modulation(c_pad, ada_w, ada_b)
    slot = np.array([SUBLANE * (e % EXPERTS_PER_GROUP) + e // EXPERTS_PER_GROUP for e in range(N_EXPERTS)])
    rw_pad = jnp.zeros((d, LOGIT_PAD), F32).at[:, slot].set(router_w)
    rw_hi = rw_pad.astype(BF16)
    rw_pad = jnp.concatenate([rw_hi, (rw_pad - rw_hi.astype(F32)).astype(BF16)], axis=1)
    rb_pad = jnp.zeros((EXPERTS_PER_GROUP * SUBLANE, LANE), F32).at[slot].set(
        jnp.broadcast_to(router_b.astype(F32)[:, None], (N_EXPERTS, LANE)))

    xl, xc = x, ctx
    for layer in range(DEPTH):
        last = layer == DEPTH - 1
        ml = mods[layer, :bsz].reshape(bsz, 1, 6, d)
        mc = jnp.broadcast_to(mods[layer, bsz].reshape(1, 1, 6, d), (bsz, 1, 6, d))
        sh1, sc1, g1, sh2, sc2, g2 = [ml[:, :, i] for i in range(6)]
        csh1, csc1, cg1, csh2, csc2, cg2 = [mc[:, :, i] for i in range(6)]
        i = layer // 2
        if layer % 2 == 0:
            w_in, w_out = ev_w_in[i], ev_w_out[i]
            lr0 = sum(EVEN_SPLITS[:4])
            lr1 = lr0 + EVEN_SPLITS[4]
            w_in = jnp.concatenate([w_in[:, :lr0], w_in[:, lr1:], w_in[:, lr0:lr1],
                                    jnp.zeros((d, EVEN_COLS_PAD - w_in.shape[1]), F32)], axis=1)
            tn = EVEN_COLS_PAD // 7
        else:
            w_in, w_out = od_w_in[i], od_w_out[i]
            tn = w_in.shape[1] // 6
        zl = norm_mod_matmul(xl, norm_mix[layer], sc1, sh1, w_in, tm=1024, tn=tn, out_dtype=F32)
        zc = norm_mod_matmul(xc, norm_mix[layer], csc1, csh1, w_in, tm=n_ctx, tn=tn, out_dtype=F32)
        if layer % 2 == 0:
            mixc, mixl = even_mixer_core(zc, zl, gla_w_gate[i], gla_b_gate[i], gla_norm[i], hy_conv_w[i],
                                         hy_conv_b[i], hy_w1[i], hy_b1[i], hy_w2[i], hy_b2[i], hy_w3[i],
                                         hy_bias[i])
        else:
            mixl = odd_mixer_core(zc, zl, ret_norm[i], att_q_norm[i], att_k_norm[i], att_sink[i])
        lat0 = 0 if last else bsz * n_ctx
        flat = lambda rt: rt.transpose(1, 0, 2).reshape(ROUTE_ROWS, -1)
        hs, route = [], []
        if not last:
            xc, hc, rtc = outproj_residual_norm(*mixc, w_out, xc, cg1, norm_ffn[layer], csc2, csh2, rw_pad,
                                                rb_pad, tm=n_ctx)
            hs.append(hc)
            route.append(flat(rtc))
        xl, hl, rtl = outproj_residual_norm(*mixl, w_out, xl, g1, norm_ffn[layer], sc2, sh2, rw_pad, rb_pad,
                                            tm=512)
        hs.append(hl)
        route.append(flat(rtl))
        y_sorted, pos, w_tok = moe_experts(hs, jnp.concatenate(route, axis=1), moe_w1, moe_w3, moe_w2, layer)
        if not last:
            xc = moe_gather_combine(xc, cg2, y_sorted, pos, w_tok, 0, tm=n_ctx)
        xl = moe_gather_combine(xl, g2, y_sorted, pos, w_tok, lat0, tm=512)
    return xl
```

```python
import functools
import math

import jax
import jax.numpy as jnp
import numpy as np
from jax import lax
from jax.experimental import pallas as pl
from jax.experimental.pallas import tpu as pltpu

D_MODEL = 2048
BATCH = 2
SEQ = 8192
DEPTH = 2
GRID_W = 64
CTX_LEN = 256
MIX_WIDTH = D_MODEL
GROUP_WIDTH = MIX_WIDTH // 2
EPS = 1e-6
F32 = jnp.float32
BF16 = jnp.bfloat16

GLA_HEADS = 4
GLA_DK = GROUP_WIDTH // (2 * GLA_HEADS)
GLA_DV = GROUP_WIDTH // GLA_HEADS
GLA_LOWRANK = 16
GLA_GATE_NORMALIZER = 16.0
GLA_CHUNK = 64

HY_CH = GROUP_WIDTH
HY_ORDER = 2
HY_BANDS = 16
HY_EMB = 1 + 2 * HY_BANDS
HY_FFN = 64
HY_SHIFT = 0.05
HY_MIN_DECAY = math.log(1e-2) / 1.5
HY_MAX_DECAY = math.log(1e-2) / 0.3

RET_HEADS = 4
RET_DK = GROUP_WIDTH // (2 * RET_HEADS)
RET_DV = GROUP_WIDTH // RET_HEADS
RET_CHUNK = 64

ATT_HD = 128
ATT_HEADS = GROUP_WIDTH // ATT_HD
ATT_KV_HEADS = 2
ATT_GROUP = ATT_HEADS // ATT_KV_HEADS
ATT_WINDOW = 128
ATT_BLOCK = 128
ROPE_BASE = 10000.0

N_EXPERTS = 16
N_GROUPS = 4
EXPERTS_PER_GROUP = N_EXPERTS // N_GROUPS
TOP_K = 2
D_FF_EXPERT = D_MODEL // 2

EVEN_SPLITS = (GLA_HEADS * GLA_DK, GLA_HEADS * GLA_DK, GLA_HEADS * GLA_DV, GLA_HEADS * GLA_DV,
               2 * GLA_LOWRANK, (HY_ORDER + 1) * HY_CH)
ODD_SPLITS = (RET_HEADS * RET_DK, RET_HEADS * RET_DK, RET_HEADS * RET_DV, RET_HEADS * RET_DV,
              ATT_HEADS * ATT_HD, ATT_KV_HEADS * ATT_HD, ATT_KV_HEADS * ATT_HD)

LANE = 128
EVEN_COLS_PAD = -(-sum(EVEN_SPLITS) // (7 * LANE)) * (7 * LANE)
VMEM_LIMIT = 56 * 1024 * 1024
MOE_TM = 512
MOE_FF_SPLIT = 2
MOVE_ROWS = 512
LOGIT_PAD = LANE


def _cparams(sem):
    return pltpu.CompilerParams(dimension_semantics=sem, vmem_limit_bytes=VMEM_LIMIT)


def _ada_kernel(c_ref, w_ref, b_ref, o_ref):
    c = c_ref[...]
    s = (c * jax.nn.sigmoid(c)).astype(BF16)
    o_ref[0] = jnp.dot(s, w_ref[0].astype(BF16), preferred_element_type=F32) + b_ref[0]


def ada_modulation(c_pad, ada_w, ada_b):
    depth, d, n = ada_w.shape
    tn = 1024
    return pl.pallas_call(
        _ada_kernel,
        grid=(depth, n // tn),
        in_specs=[
            pl.BlockSpec((8, d), lambda l, j: (0, 0)),
            pl.BlockSpec((1, d, tn), lambda l, j: (l, 0, j)),
            pl.BlockSpec((1, 1, tn), lambda l, j: (l, 0, j)),
        ],
        out_specs=pl.BlockSpec((1, 8, tn), lambda l, j: (l, 0, j)),
        out_shape=jax.ShapeDtypeStruct((depth, 8, n), F32),
        compiler_params=_cparams(("parallel", "arbitrary")),
        name="ada_modulation",
    )(c_pad, ada_w, ada_b.reshape(depth, 1, n))


def _norm_mod(x, gain, sc, sh):
    ms = jnp.mean(x * x, axis=-1, keepdims=True)
    return (x * lax.rsqrt(ms + EPS) * gain) * (1.0 + sc) + sh


def _cast_kernel(x_ref, o_ref):
    o_ref[...] = x_ref[...].astype(o_ref.dtype)


def cast_bf16(w, *, tr):
    r, c = w.shape
    return pl.pallas_call(
        _cast_kernel,
        grid=(r // tr,),
        in_specs=[pl.BlockSpec((tr, c), lambda i: (i, 0))],
        out_specs=pl.BlockSpec((tr, c), lambda i: (i, 0)),
        out_shape=jax.ShapeDtypeStruct((r, c), BF16),
        compiler_params=_cparams(("parallel",)),
        name="cast_bf16",
    )(w)


def _inproj_kernel(x_ref, gain_ref, sc_ref, sh_ref, w_ref, o_ref, h_ref):
    @pl.when(pl.program_id(2) == 0)
    def _():
        h_ref[...] = _norm_mod(x_ref[0], gain_ref[...], sc_ref[0], sh_ref[0]).astype(BF16)

    o_ref[0] = jnp.dot(h_ref[...], w_ref[...].astype(BF16), preferred_element_type=F32).astype(o_ref.dtype)


def norm_mod_matmul(x, gain, sc, sh, w, *, tm, tn, out_dtype):
    b, l, d = x.shape
    n = w.shape[1]
    return pl.pallas_call(
        _inproj_kernel,
        grid=(b, l // tm, n // tn),
        in_specs=[
            pl.BlockSpec((1, tm, d), lambda bi, i, j: (bi, i, 0)),
            pl.BlockSpec((1, d), lambda bi, i, j: (0, 0)),
            pl.BlockSpec((1, 1, d), lambda bi, i, j: (bi, 0, 0)),
            pl.BlockSpec((1, 1, d), lambda bi, i, j: (bi, 0, 0)),
            pl.BlockSpec((d, tn), lambda bi, i, j: (0, j)),
        ],
        out_specs=pl.BlockSpec((1, tm, tn), lambda bi, i, j: (bi, i, j)),
        out_shape=jax.ShapeDtypeStruct((b, l, n), out_dtype),
        scratch_shapes=[pltpu.VMEM((tm, d), BF16)],
        compiler_params=_cparams(("parallel", "parallel", "arbitrary")),
        name="norm_mod_matmul",
    )(x, gain.reshape(1, d), sc, sh, w)


SUBLANE = 8
ROUTE_ROWS = SUBLANE


def _route_block(lg, rb_ref):
    tm = lg.shape[0]
    t = lg.T
    sub = lax.broadcasted_iota(jnp.int32, (SUBLANE, tm), 0)
    valid = sub < N_GROUPS
    members = range(EXPERTS_PER_GROUP)
    s = [jnp.where(valid, t[SUBLANE * j:SUBLANE * (j + 1), :], -jnp.inf) for j in members]
    mx = jnp.max(jnp.maximum(jnp.maximum(s[0], s[1]), jnp.maximum(s[2], s[3])), axis=0, keepdims=True)
    e = [jnp.exp(sj - mx) for sj in s]
    den = jnp.sum((e[0] + e[1]) + (e[2] + e[3]), axis=0, keepdims=True)
    sc = [ej / den for ej in e]
    sel = [sc[j] + jnp.concatenate([rb_ref[SUBLANE * j:SUBLANE * (j + 1), :]] * (tm // LANE), axis=1)
           for j in members]
    hi01, lo01 = jnp.maximum(sel[0], sel[1]), jnp.minimum(sel[0], sel[1])
    hi23, lo23 = jnp.maximum(sel[2], sel[3]), jnp.minimum(sel[2], sel[3])
    top1 = jnp.maximum(hi01, hi23)
    top2 = jnp.maximum(jnp.minimum(hi01, hi23), jnp.maximum(lo01, lo23))
    gscore = jnp.where(valid, top1 + top2, -jnp.inf)
    gmax = jnp.max(gscore, axis=0, keepdims=True)
    gidx = jnp.min(jnp.where(gscore == gmax, sub, SUBLANE), axis=0, keepdims=True)
    chosen = sub == gidx
    v = [jnp.sum(jnp.where(chosen, sel[j], 0.0), axis=0, keepdims=True) for j in members]
    p = [jnp.sum(jnp.where(chosen, sc[j], 0.0), axis=0, keepdims=True) for j in members]
    one = lambda cond: jnp.where(cond, 1.0, 0.0)
    picks = []
    for want in (0.0, 1.0):
        idx = jnp.zeros_like(v[0])
        wt = jnp.zeros_like(v[0])
        for j in members:
            rank = sum(one(v[k] > v[j]) for k in members if k != j) + sum(one(v[k] == v[j]) for k in range(j))
            hit = rank == want
            idx = idx + jnp.where(hit, float(j), 0.0)
            wt = wt + jnp.where(hit, p[j], 0.0)
        picks.append((idx, wt))
    (j1, p1), (j2, p2) = picks
    base = (EXPERTS_PER_GROUP * gidx).astype(F32)
    tot = p1 + p2
    rows = [base + j1, base + j2, p1 / tot, p2 / tot]
    return jnp.concatenate(rows + [jnp.zeros((ROUTE_ROWS - len(rows), tm), F32)], axis=0)


def _outproj_kernel(m1_ref, m2_ref, w_ref, x_ref, g_ref, gain_ref, sc_ref, sh_ref, rw_ref, rb_ref,
                    xo_ref, h_ref, rt_ref):
    k1 = m1_ref.shape[-1]
    o = (jnp.dot(m1_ref[0].astype(BF16), w_ref[:k1, :].astype(BF16), preferred_element_type=F32)
         + jnp.dot(m2_ref[0].astype(BF16), w_ref[k1:, :].astype(BF16), preferred_element_type=F32))
    xn = x_ref[0] + g_ref[0] * o
    xo_ref[0] = xn
    h = _norm_mod(xn, gain_ref[...], sc_ref[0], sh_ref[0])
    h_ref[...] = h
    h_hi = h.astype(BF16)
    h_lo = (h - h_hi.astype(F32)).astype(BF16)
    both = jnp.dot(h_hi, rw_ref[...], preferred_element_type=F32)
    lg = (both[:, :LOGIT_PAD] + both[:, LOGIT_PAD:]
          + jnp.dot(h_lo, rw_ref[:, :LOGIT_PAD], preferred_element_type=F32))
    rt_ref[0] = _route_block(lg, rb_ref)


def outproj_residual_norm(m1, m2, w, x, g, gain, sc, sh, rw_pad, rb_pad, *, tm):
    b, l, d = x.shape
    k1, k2 = m1.shape[-1], m2.shape[-1]
    nb = l // tm
    row = lambda bi, i: (bi, i, 0)
    mod = lambda bi, i: (bi, 0, 0)
    const = lambda bi, i: (0, 0)
    return pl.pallas_call(
        _outproj_kernel,
        grid=(b, nb),
        in_specs=[
            pl.BlockSpec((1, tm, k1), row),
            pl.BlockSpec((1, tm, k2), row),
            pl.BlockSpec((k1 + k2, d), const, pipeline_mode=pl.Buffered(1)),
            pl.BlockSpec((1, tm, d), row),
            pl.BlockSpec((1, 1, d), mod),
            pl.BlockSpec((1, d), const),
            pl.BlockSpec((1, 1, d), mod),
            pl.BlockSpec((1, 1, d), mod),
            pl.BlockSpec((d, 2 * LOGIT_PAD), const),
            pl.BlockSpec(rb_pad.shape, const),
        ],
        out_specs=[
            pl.BlockSpec((1, tm, d), row),
            pl.BlockSpec((tm, d), lambda bi, i: (bi * nb + i, 0)),
            pl.BlockSpec((1, ROUTE_ROWS, tm), lambda bi, i: (bi, 0, i)),
        ],
        out_shape=[
            jax.ShapeDtypeStruct((b, l, d), F32),
            jax.ShapeDtypeStruct((b * l, d), F32),
            jax.ShapeDtypeStruct((b, ROUTE_ROWS, l), F32),
        ],
        compiler_params=_cparams(("parallel", "parallel")),
        name="outproj_residual_norm",
    )(m1, m2, w, x, g, gain.reshape(1, d), sc, sh, rw_pad, rb_pad)


def _moe_kernel(te_ref, tsrc_ref, tvalid_ref, x_ref, w1_ref, w3_ref, w2_ref, o_ref):
    i = pl.program_id(0)

    @pl.when(tvalid_ref[i] > 0)
    def _():
        x = x_ref[...].astype(BF16)
        ff = w1_ref.shape[-1]
        fc = ff // MOE_FF_SPLIT
        y = None
        for c in range(MOE_FF_SPLIT):
            cols = slice(c * fc, (c + 1) * fc)
            a = jnp.dot(x, w1_ref[0, 0, :, cols].astype(BF16), preferred_element_type=F32)
            b = jnp.dot(x, w3_ref[0, 0, :, cols].astype(BF16), preferred_element_type=F32)
            mid = (a * jax.nn.sigmoid(a) * b).astype(BF16)
            part = jnp.dot(mid, w2_ref[0, 0, cols, :].astype(BF16), preferred_element_type=F32)
            y = part if y is None else y + part
        o_ref[...] = y

    @pl.when(tvalid_ref[i] == 0)
    def _():
        o_ref[...] = jnp.zeros_like(o_ref)


def moe_grouped(x_sorted, w1, w3, w2, layer, tile_expert, tile_src, tile_valid):
    p, d = x_sorted.shape
    ff = w1.shape[-1]
    tm = MOE_TM
    wmap = lambda i, te, ts, tv: (layer, te[i], 0, 0)
    grid_spec = pltpu.PrefetchScalarGridSpec(
        num_scalar_prefetch=3,
        grid=(p // tm,),
        in_specs=[
            pl.BlockSpec((tm, d), lambda i, te, ts, tv: (ts[i], 0)),
            pl.BlockSpec((1, 1, d, ff), wmap, pipeline_mode=pl.Buffered(1)),
            pl.BlockSpec((1, 1, d, ff), wmap, pipeline_mode=pl.Buffered(1)),
            pl.BlockSpec((1, 1, ff, d), wmap, pipeline_mode=pl.Buffered(1)),
        ],
        out_specs=pl.BlockSpec((tm, d), lambda i, te, ts, tv: (i, 0)),
    )
    return pl.pallas_call(
        _moe_kernel,
        grid_spec=grid_spec,
        out_shape=jax.ShapeDtypeStruct((p, d), F32),
        compiler_params=_cparams(("arbitrary",)),
        name="moe_grouped",
    )(tile_expert, tile_src, tile_valid, x_sorted, w1, w3, w2)


ZERO_ROWS = MOE_TM + SUBLANE


def _row_scatter_kernel(dst_ref, fill_ref, *rest, starts):
    h_refs, (out_ref, sem, zbuf) = rest[:-3], rest[-3:]
    j = pl.program_id(0)

    @pl.when((j == 0) & (pl.program_id(1) == 0))
    def _():
        zbuf[...] = jnp.zeros_like(zbuf)
        for e in range(fill_ref.shape[0]):
            fill = pltpu.make_async_copy(
                zbuf, out_ref.at[pl.ds(pl.multiple_of(fill_ref[e], SUBLANE), ZERO_ROWS)], sem)
            fill.start()
            fill.wait()

    for s, h_ref in enumerate(h_refs):
        @pl.when((j >= starts[s]) & (j < starts[s + 1]))
        def _(h_ref=h_ref):
            def issue(r, carry):
                pltpu.make_async_copy(h_ref.at[pl.ds(r, 1)], out_ref.at[pl.ds(dst_ref[0, 0, r], 1)], sem).start()
                return carry

            lax.fori_loop(0, h_ref.shape[0], issue, 0, unroll=4)
            pltpu.make_async_copy(h_ref, h_ref, sem).wait()


def row_scatter(hs, dst, out_rows, fill_start, *, step):
    d = hs[0].shape[1]
    nbs = [h.shape[0] // step for h in hs]
    assert all(nb * step == h.shape[0] for nb, h in zip(nbs, hs))
    starts = [0]
    for nb in nbs:
        starts.append(starts[-1] + nb)
    nb = starts[-1]
    dst = dst.reshape(TOP_K, nb, 1, step).transpose(1, 0, 2, 3).reshape(nb * TOP_K, 1, step)
    seg_spec = lambda s: pl.BlockSpec((step, d), lambda j, k: (jnp.clip(j - starts[s], 0, nbs[s] - 1), 0))
    return pl.pallas_call(
        functools.partial(_row_scatter_kernel, starts=tuple(starts)),
        grid=(nb, TOP_K),
        in_specs=[pl.BlockSpec((1, 1, step), lambda j, k: (j * TOP_K + k, 0, 0), memory_space=pltpu.SMEM),
                  pl.BlockSpec(memory_space=pltpu.SMEM)] + [seg_spec(s) for s in range(len(hs))],
        out_specs=pl.BlockSpec(memory_space=pl.ANY),
        out_shape=jax.ShapeDtypeStruct((out_rows, d), hs[0].dtype),
        scratch_shapes=[pltpu.SemaphoreType.DMA(()), pltpu.VMEM((ZERO_ROWS, d), hs[0].dtype)],
        compiler_params=_cparams(("arbitrary", "arbitrary")),
        name="row_scatter",
    )(dst, fill_start, *hs)


def _gather_combine_kernel(p0_ref, p1_ref, x_ref, g_ref, w0_ref, w1_ref, y_ref, o_ref, buf, sem):
    tm = x_ref.shape[1]

    def issue(r, carry):
        pltpu.make_async_copy(y_ref.at[pl.ds(p0_ref[0, 0, r], 1)], buf.at[0, pl.ds(r, 1)], sem).start()
        pltpu.make_async_copy(y_ref.at[pl.ds(p1_ref[0, 0, r], 1)], buf.at[1, pl.ds(r, 1)], sem).start()
        return carry

    lax.fori_loop(0, tm, issue, 0, unroll=2)
    pltpu.make_async_copy(buf, buf, sem).wait()
    y = w0_ref[...] * buf[0] + w1_ref[...] * buf[1]
    o_ref[0] = x_ref[0] + g_ref[0] * y


def moe_gather_combine(x, g, y_sorted, pos, w_tok, row0, *, tm):
    b, l, d = x.shape
    n = pos.shape[1]
    nb = l // tm
    assert row0 % tm == 0 and n % tm == 0
    tok = lambda bi, i: row0 // tm + bi * nb + i
    pos = pos.reshape(TOP_K * (n // tm), 1, tm)
    w_col = w_tok.reshape(TOP_K * n, 1)
    idx = lambda k: pl.BlockSpec((1, 1, tm), lambda bi, i: (k * (n // tm) + tok(bi, i), 0, 0),
                                 memory_space=pltpu.SMEM)
    wgt = lambda k: pl.BlockSpec((tm, 1), lambda bi, i: (k * (n // tm) + tok(bi, i), 0))
    return pl.pallas_call(
        _gather_combine_kernel,
        grid=(b, nb),
        in_specs=[idx(0), idx(1),
                  pl.BlockSpec((1, tm, d), lambda bi, i: (bi, i, 0)),
                  pl.BlockSpec((1, 1, d), lambda bi, i: (bi, 0, 0)),
                  wgt(0), wgt(1),
                  pl.BlockSpec(memory_space=pl.ANY)],
        out_specs=pl.BlockSpec((1, tm, d), lambda bi, i: (bi, i, 0)),
        out_shape=jax.ShapeDtypeStruct(x.shape, x.dtype),
        scratch_shapes=[pltpu.VMEM((TOP_K, tm, d), F32), pltpu.SemaphoreType.DMA(())],
        compiler_params=_cparams(("arbitrary", "arbitrary")),
        name="moe_gather_combine",
    )(pos, pos, x, g, w_col, w_col, y_sorted)


CHUNK = 64
SCAN_T = 256
_NT = (((1,), (1,)), ((), ()))
_TN = (((0,), (0,)), ((), ()))
HIGHEST = lax.Precision.HIGHEST


def _block_keep(t, reverse):
    r = lax.broadcasted_iota(jnp.int32, (t, t), 0)
    c = lax.broadcasted_iota(jnp.int32, (t, t), 1)
    shift = CHUNK.bit_length() - 1
    same = (r >> shift) == (c >> shift)
    return same & ((c >= r) if reverse else (c <= r))


def _chunk_order(n, reverse):
    return range(n - 1, -1, -1) if reverse else range(n)


def _block_scan(q, k, v, cum, st, keep, reverse):
    n = q.shape[0] // CHUNK
    ends = [cum[c * CHUNK:c * CHUNK + 1] if reverse else cum[(c + 1) * CHUNK - 1:(c + 1) * CHUNK]
            for c in range(n)]
    last = jnp.concatenate([jnp.broadcast_to(e, (CHUNK, e.shape[1])) for e in ends], axis=0)
    qe = (q * jnp.exp(cum)).astype(BF16)
    ke = (k * jnp.exp(-cum)).astype(BF16)
    kd = (k * jnp.exp(last - cum)).astype(BF16)
    vb = v.astype(BF16)
    att = lax.dot_general(qe, ke, _NT, preferred_element_type=F32)
    att = jnp.where(keep, att, 0.0).astype(BF16)
    intra = jnp.dot(att, vb, preferred_element_type=F32)
    outs = [None] * n
    for c in _chunk_order(n, reverse):
        sl = slice(c * CHUNK, (c + 1) * CHUNK)
        outs[c] = intra[sl] + lax.dot_general(qe[sl], st.astype(BF16), _NT, preferred_element_type=F32)
        st = jnp.exp(ends[c]) * st + lax.dot_general(vb[sl], kd[sl], _TN, preferred_element_type=F32)
    return jnp.concatenate(outs, axis=0), st


def _merge(o, ob, g, gain):
    tot = o + ob
    y = tot * lax.rsqrt(jnp.mean(tot * tot, axis=-1, keepdims=True) + EPS) * gain
    return y * (g * jax.nn.sigmoid(g))


def _scan_refs(rest, merge):
    if merge:
        return rest
    oc_ref, ol_ref, st_ref = rest
    return (None,) * 5 + (oc_ref, ol_ref, st_ref)


def _scan_heads(head_fn, dk, dv, v_ref, o_ref, ob_ref, g_ref, gn_ref, st_ref, keep, reverse):
    outs = []
    for h in range(st_ref.shape[0]):
        kcol = slice(h * dk, (h + 1) * dk)
        vcol = slice(h * dv, (h + 1) * dv)
        q, k, cum = head_fn(h, kcol)
        o, st = _block_scan(q, k, v_ref[0, :, vcol], cum, st_ref[h], keep, reverse)
        st_ref[h] = st
        outs.append(o if ob_ref is None else _merge(o, ob_ref[0, :, vcol], g_ref[0, :, vcol], gn_ref[...]))
    o_ref[0] = jnp.concatenate(outs, axis=1).astype(o_ref.dtype)


def _gla_kernel(qc, kc, vc, lrc, ql, kl, vl, lrl, wg, bg, *rest, reverse, merge):
    obc, obl, gc, gl, gn, oc_ref, ol_ref, st_ref = _scan_refs(rest, merge)

    def run(q_ref, k_ref, v_ref, lr_ref, o_ref, ob_ref, g_ref):
        keep = _block_keep(q_ref.shape[1], reverse)
        pre = jnp.dot(lr_ref[0], wg[...], preferred_element_type=F32, precision=HIGHEST) + bg[...]
        la = (jnp.minimum(pre, 0.0) - jnp.log(1.0 + jnp.exp(-jnp.abs(pre)))) * (1.0 / GLA_GATE_NORMALIZER)
        cum = jnp.dot(keep.astype(F32), la, preferred_element_type=F32, precision=HIGHEST)

        def head(h, kcol):
            return q_ref[0, :, kcol] * (GLA_DK ** -0.5), k_ref[0, :, kcol], cum[:, kcol]

        _scan_heads(head, GLA_DK, GLA_DV, v_ref, o_ref, ob_ref, g_ref, gn, st_ref, keep, reverse)

    @pl.when(pl.program_id(2) == 0)
    def _():
        st_ref[...] = jnp.zeros_like(st_ref)
        run(qc, kc, vc, lrc, oc_ref, obc, gc)

    @pl.when(pl.program_id(2) > 0)
    def _():
        run(ql, kl, vl, lrl, ol_ref, obl, gl)


def _ret_kernel(qc, kc, vc, ql, kl, vl, cos_ref, sin_ref, cum_ref, *rest, reverse, merge):
    obc, obl, gc, gl, gn, oc_ref, ol_ref, st_ref = _scan_refs(rest, merge)

    def run(q_ref, k_ref, v_ref, o_ref, ob_ref, g_ref, latent):
        t = q_ref.shape[1]
        keep = _block_keep(t, reverse)

        def head(h, kcol):
            q = q_ref[0, :, kcol]
            k = k_ref[0, :, kcol]
            if latent:
                cs = cos_ref[...]
                sn = sin_ref[...]
                q = q * cs + pltpu.roll(q, RET_DK // 2, 1) * sn
                k = k * cs + pltpu.roll(k, RET_DK // 2, 1) * sn
            return q, k * (RET_DK ** -0.5), jnp.concatenate([cum_ref[h]] * (t // CHUNK), axis=0)

        _scan_heads(head, RET_DK, RET_DV, v_ref, o_ref, ob_ref, g_ref, gn, st_ref, keep, reverse)

    @pl.when(pl.program_id(2) == 0)
    def _():
        st_ref[...] = jnp.zeros_like(st_ref)
        run(qc, kc, vc, oc_ref, obc, gc, False)

    @pl.when(pl.program_id(2) > 0)
    def _():
        run(ql, kl, vl, ol_ref, obl, gl, True)


def _scan_call(body, zc, zl, heads, dk, dv, col_q, col_k, col_v, shared_cols, extra_specs, merge_in,
               *, reverse, name):
    b, lc, _ = zc.shape
    l = zl.shape[1]
    t = SCAN_T
    nblk = l // t
    if reverse:
        blk = lambda i: jnp.where(i == 0, nblk - 1, nblk - i)
    else:
        blk = lambda i: jnp.maximum(i - 1, 0)

    def ctx_spec(w, col, per_head=True):
        return pl.BlockSpec((1, lc, w), lambda bi, h, i: (bi, 0, col // w + (h if per_head else 0)))

    def lat_spec(w, col, per_head=True):
        return pl.BlockSpec((1, t, w), lambda bi, h, i: (bi, blk(i), col // w + (h if per_head else 0)))

    hk, hv = heads * dk, heads * dv
    ins = [zc, zc, zc] + [zc] * len(shared_cols) + [zl, zl, zl] + [zl] * len(shared_cols)
    specs = [ctx_spec(hk, col_q), ctx_spec(hk, col_k), ctx_spec(hv, col_v)]
    specs += [ctx_spec(w, col, False) for w, col in shared_cols]
    specs += [lat_spec(hk, col_q), lat_spec(hk, col_k), lat_spec(hv, col_v)]
    specs += [lat_spec(w, col, False) for w, col in shared_cols]
    for a, s in extra_specs:
        ins.append(a)
        specs.append(s)
    if merge_in is not None:
        obc, obl, col_g, gain = merge_in
        ins += [obc, obl, zc, zl, gain.reshape(1, dv)]
        specs += [ctx_spec(hv, 0), lat_spec(hv, 0), ctx_spec(hv, col_g), lat_spec(hv, col_g),
                  pl.BlockSpec((1, dv), lambda bi, h, i: (0, 0))]
        out_dtype = BF16
    else:
        out_dtype = F32
    return pl.pallas_call(
        functools.partial(body, reverse=reverse, merge=merge_in is not None),
        grid=(b, 1, 1 + nblk),
        in_specs=specs,
        out_specs=[ctx_spec(hv, 0), lat_spec(hv, 0)],
        out_shape=[jax.ShapeDtypeStruct((b, lc, heads * dv), out_dtype),
                   jax.ShapeDtypeStruct((b, l, heads * dv), out_dtype)],
        scratch_shapes=[pltpu.VMEM((heads, dv, dk), F32)],
        compiler_params=_cparams(("parallel", "parallel", "arbitrary")),
        name=name,
    )(*ins)


EV_COL = dict(q=0, k=512, v=1024, g=2048, u=3072, lr=6144)
OD_COL = dict(q=0, k=512, v=1024, g=2048, aq=3072, ak=4096, av=4352)


def gla_bidir(zc, zl, gla_w_gate, gla_b_gate, gla_norm):
    hk = GLA_HEADS * GLA_DK
    outs = None
    for d, reverse in ((1, True), (0, False)):
        wg = jnp.zeros((LANE, hk), F32).at[d * GLA_LOWRANK:(d + 1) * GLA_LOWRANK].set(gla_w_gate[d])
        shared_cols = [(LANE, EV_COL["lr"])]
        extra_specs = [(wg, pl.BlockSpec((LANE, hk), lambda bi, h, i: (0, 0))),
                       (gla_b_gate[d].reshape(1, hk), pl.BlockSpec((1, hk), lambda bi, h, i: (0, 0)))]
        merge_in = None if outs is None else (outs[0], outs[1], EV_COL["g"], gla_norm)
        outs = _scan_call(_gla_kernel, zc, zl, GLA_HEADS, GLA_DK, GLA_DV, EV_COL["q"], EV_COL["k"],
                          EV_COL["v"], shared_cols, extra_specs, merge_in, reverse=reverse,
                          name="gla_bwd" if reverse else "gla_fwd_merge")
    return outs


def retention_bidir(zc, zl, ret_norm):
    l = zl.shape[1]
    inv = ROPE_BASE ** (-jnp.linspace(0.0, 1.0, RET_DK // 2, dtype=F32))
    ang = jnp.arange(l, dtype=F32)[:, None] * inv[None, :]
    cos2 = jnp.concatenate([jnp.cos(ang), jnp.cos(ang)], axis=-1)
    sin2 = jnp.concatenate([-jnp.sin(ang), jnp.sin(ang)], axis=-1)
    log_g_f = jnp.log(1.0 - 2.0 ** (-5.0 - jnp.arange(RET_HEADS, dtype=F32)))
    idx = jnp.arange(CHUNK, dtype=F32)
    nblk = l // SCAN_T
    outs = None
    for reverse in (True, False):
        if reverse:
            cum = (CHUNK - idx)[None, :, None] * log_g_f[::-1][:, None, None]
            blk = lambda i: jnp.where(i == 0, nblk - 1, nblk - i)
        else:
            cum = (idx + 1.0)[None, :, None] * log_g_f[:, None, None]
            blk = lambda i: jnp.maximum(i - 1, 0)
        cum = jnp.broadcast_to(cum, (RET_HEADS, CHUNK, RET_DK))
        tab_spec = pl.BlockSpec((SCAN_T, RET_DK), lambda bi, h, i, blk=blk: (blk(i), 0))
        extra_specs = [(cos2, tab_spec), (sin2, tab_spec),
                       (cum, pl.BlockSpec((RET_HEADS, CHUNK, RET_DK), lambda bi, h, i: (0, 0, 0)))]
        merge_in = None if outs is None else (outs[0], outs[1], OD_COL["g"], ret_norm)
        outs = _scan_call(_ret_kernel, zc, zl, RET_HEADS, RET_DK, RET_DV, OD_COL["q"], OD_COL["k"],
                          OD_COL["v"], [], extra_specs, merge_in, reverse=reverse,
                          name="ret_bwd" if reverse else "ret_fwd_merge")
    return outs


ATT_QKV = (ATT_HEADS + 2 * ATT_KV_HEADS) * ATT_HD


def _att_prep_kernel(z_ref, qn_ref, kn_ref, *rest, latent):
    if latent:
        cos_ref, sin_ref, o_ref = rest
        cs, sn = cos_ref[...], sin_ref[...]
        lane = lax.broadcasted_iota(jnp.int32, cs.shape, 1)
        low = (lane & (ATT_HD // 2 - 1)) < (ATT_HD // 4)
    else:
        (o_ref,) = rest
    for h in range(ATT_HEADS + ATT_KV_HEADS):
        x = z_ref[0, :, h * ATT_HD:(h + 1) * ATT_HD]
        gain = qn_ref[...] if h < ATT_HEADS else kn_ref[...]
        y = x * lax.rsqrt(jnp.mean(x * x, axis=-1, keepdims=True) + EPS) * gain
        if latent:
            partner = jnp.where(low, pltpu.roll(y, ATT_HD - ATT_HD // 4, 1), pltpu.roll(y, ATT_HD // 4, 1))
            y = y * cs + partner * sn
        if h < ATT_HEADS:
            y = y * (ATT_HD ** -0.5)
        o_ref[0, :, h * ATT_HD:(h + 1) * ATT_HD] = y.astype(o_ref.dtype)
    v0 = (ATT_HEADS + ATT_KV_HEADS) * ATT_HD
    o_ref[0, :, v0:] = z_ref[0, :, v0:].astype(o_ref.dtype)


def att_prep(z, att_q_norm, att_k_norm, tables, *, tm):
    b, l, _ = z.shape
    latent = tables is not None
    row = lambda bi, i: (bi, i, 0)
    const = lambda bi, i: (0, 0)
    ins = [z, att_q_norm.reshape(1, ATT_HD), att_k_norm.reshape(1, ATT_HD)]
    specs = [pl.BlockSpec((1, tm, ATT_QKV), lambda bi, i: (bi, i, OD_COL["aq"] // ATT_QKV)),
             pl.BlockSpec((1, ATT_HD), const), pl.BlockSpec((1, ATT_HD), const)]
    if latent:
        ins += list(tables)
        specs += [pl.BlockSpec((tm, ATT_HD), lambda bi, i: (i, 0))] * 2
    return pl.pallas_call(
        functools.partial(_att_prep_kernel, latent=latent),
        grid=(b, l // tm),
        in_specs=specs,
        out_specs=pl.BlockSpec((1, tm, ATT_QKV), row),
        out_shape=jax.ShapeDtypeStruct((b, l, ATT_QKV), BF16),
        compiler_params=_cparams(("parallel", "parallel")),
        name="att_prep",
    )(*ins)


def _axial_tables(l):
    quarter = ATT_HD // 4
    inv = ROPE_BASE ** (-jnp.arange(quarter, dtype=F32) / quarter)
    t = jnp.arange(l)
    a_row = (t // GRID_W).astype(F32)[:, None] * inv[None, :]
    a_col = (t % GRID_W).astype(F32)[:, None] * inv[None, :]
    cos = jnp.concatenate([jnp.cos(a_row)] * 2 + [jnp.cos(a_col)] * 2, axis=-1)
    sin = jnp.concatenate([-jnp.sin(a_row), jnp.sin(a_row), -jnp.sin(a_col), jnp.sin(a_col)], axis=-1)
    return cos, sin


def _att_kernel(sink_ref, q_ref, kp, kc, kn, vp, vc, vn, ck, cv, o_ref, *, seq):
    kvh = pl.program_id(1)
    n = pl.program_id(2)
    w = ATT_BLOCK
    q = jnp.concatenate([q_ref[0, :, g * ATT_HD:(g + 1) * ATT_HD] for g in range(ATT_GROUP)], axis=0)
    kw = jnp.concatenate([kp[0], kc[0], kn[0]], axis=0)
    vw = jnp.concatenate([vp[0], vc[0], vn[0]], axis=0)
    s_w = lax.dot_general(q, kw, _NT, preferred_element_type=F32)
    s_c = lax.dot_general(q, ck[0], _NT, preferred_element_type=F32)
    r = lax.broadcasted_iota(jnp.int32, s_w.shape, 0) & (w - 1)
    c = lax.broadcasted_iota(jnp.int32, s_w.shape, 1) - w
    kpos = n * w + c
    valid = (jnp.abs(r - c) <= ATT_WINDOW) & (kpos >= 0) & (kpos < seq)
    s_w = jnp.where(valid, s_w, -jnp.inf)
    sink = jnp.concatenate([jnp.full((w, 1), sink_ref[kvh * ATT_GROUP + g], F32) for g in range(ATT_GROUP)],
                           axis=0)
    m = jnp.maximum(jnp.maximum(jnp.max(s_w, axis=-1, keepdims=True), jnp.max(s_c, axis=-1, keepdims=True)),
                    sink)
    p_w = jnp.exp(s_w - m)
    p_c = jnp.exp(s_c - m)
    denom = (jnp.sum(p_w, axis=-1, keepdims=True) + jnp.sum(p_c, axis=-1, keepdims=True)
             + jnp.exp(sink - m))
    o = (jnp.dot(p_w.astype(BF16), vw, preferred_element_type=F32)
         + jnp.dot(p_c.astype(BF16), cv[0], preferred_element_type=F32)) / denom
    for g in range(ATT_GROUP):
        o_ref[0, :, g * ATT_HD:(g + 1) * ATT_HD] = o[g * w:(g + 1) * w].astype(o_ref.dtype)


def window_attention(qkv_l, qkv_c, att_sink):
    b, l, _ = qkv_l.shape
    lc = qkv_c.shape[1]
    w = ATT_BLOCK
    nb = l // w
    gw = ATT_GROUP * ATT_HD
    kcol = ATT_HEADS
    vcol = ATT_HEADS + ATT_KV_HEADS

    def kv_spec(col, shift):
        return pl.BlockSpec((1, w, ATT_HD),
                            lambda bi, h, n, s: (bi, jnp.clip(n + shift, 0, nb - 1), col + h))

    grid_spec = pltpu.PrefetchScalarGridSpec(
        num_scalar_prefetch=1,
        grid=(b, ATT_KV_HEADS, nb),
        in_specs=[pl.BlockSpec((1, w, gw), lambda bi, h, n, s: (bi, n, h))]
        + [kv_spec(kcol, d) for d in (-1, 0, 1)] + [kv_spec(vcol, d) for d in (-1, 0, 1)]
        + [pl.BlockSpec((1, lc, ATT_HD), lambda bi, h, n, s: (bi, 0, kcol + h)),
           pl.BlockSpec((1, lc, ATT_HD), lambda bi, h, n, s: (bi, 0, vcol + h))],
        out_specs=pl.BlockSpec((1, w, gw), lambda bi, h, n, s: (bi, n, h)),
    )
    return pl.pallas_call(
        functools.partial(_att_kernel, seq=l),
        grid_spec=grid_spec,
        out_shape=jax.ShapeDtypeStruct((b, l, ATT_HEADS * ATT_HD), BF16),
        compiler_params=_cparams(("parallel", "parallel", "arbitrary")),
        name="window_attention",
    )(att_sink.astype(F32), qkv_l, qkv_l, qkv_l, qkv_l, qkv_l, qkv_l, qkv_l, qkv_c, qkv_c)


def moe_experts(hs, route, w1, w3, w2, layer):
    n = sum(h.shape[0] for h in hs)
    d = hs[0].shape[1]
    tm = MOE_TM
    flat_e = route[0:TOP_K].astype(jnp.int32).reshape(-1)
    w_tok = route[TOP_K:2 * TOP_K]
    onehot = (flat_e[:, None] == jnp.arange(N_EXPERTS)[None, :]).astype(jnp.int32)
    rank = jnp.sum((jnp.cumsum(onehot, axis=0) - onehot) * onehot, axis=-1)
    counts = jnp.sum(onehot, axis=0)
    padded = ((counts + tm - 1) // tm) * tm
    ends = jnp.cumsum(padded)
    offs = ends - padded
    pos = (offs[flat_e] + rank).astype(jnp.int32)
    n_tiles = (2 * n + tm - 1) // tm + N_EXPERTS
    p = n_tiles * tm
    used = ends[-1] // tm
    tile_ids = jnp.arange(n_tiles, dtype=jnp.int32)
    tile_valid = (tile_ids < used).astype(jnp.int32)
    tile_src = jnp.minimum(tile_ids, used - 1).astype(jnp.int32)
    tile_expert = jnp.minimum(
        jnp.searchsorted(ends, tile_src * tm, side="right"), N_EXPERTS - 1).astype(jnp.int32)
    n_tail = -(-(p - TOP_K * n) // ZERO_ROWS)
    fill_start = jnp.concatenate([(offs + counts) // SUBLANE * SUBLANE,
                                  ends[-1] + ZERO_ROWS * jnp.arange(n_tail, dtype=jnp.int32)])
    fill_start = jnp.minimum(fill_start, p - ZERO_ROWS).astype(jnp.int32)
    pos = pos.reshape(TOP_K, n)
    x_sorted = row_scatter(hs, pos, p, fill_start, step=MOVE_ROWS)
    y_sorted = moe_grouped(x_sorted, w1, w3, w2, layer, tile_expert, tile_src, tile_valid)
    return y_sorted, pos, w_tok


FFT_Q = 128
HY_COLS = (HY_ORDER + 1) * HY_CH


def _hy_conv_kernel(u_ref, prev_ref, next_ref, w_ref, b_ref, z_ref, *g_refs):
    i = pl.program_id(1)
    u = u_ref[0]
    t = u.shape[0]
    row = lax.broadcasted_iota(jnp.int32, (t, 1), 0)
    has_prev = (i > 0).astype(F32)
    has_next = (i < pl.num_programs(1) - 1).astype(F32)
    up = jnp.where(row == 0, prev_ref[0, SUBLANE - 1:SUBLANE, :] * has_prev, pltpu.roll(u, 1, 0))
    dn = jnp.where(row == t - 1, next_ref[0, 0:1, :] * has_next, pltpu.roll(u, t - 1, 0))
    y = up * w_ref[0:1, :] + u * w_ref[1:2, :] + dn * w_ref[2:3, :] + b_ref[...]
    for n, o_ref in enumerate((z_ref,) + g_refs):
        o_ref[0] = y[:, n * HY_CH:(n + 1) * HY_CH]


def hy_short_conv(z, w, bias, *, tm):
    b, l, _ = z.shape
    ucol = EV_COL["u"] // HY_COLS
    nh = l // SUBLANE
    hb = tm // SUBLANE
    row = lambda bi, i: (bi, i, 0)
    return pl.pallas_call(
        _hy_conv_kernel,
        grid=(b, l // tm),
        in_specs=[
            pl.BlockSpec((1, tm, HY_COLS), lambda bi, i: (bi, i, ucol)),
            pl.BlockSpec((1, SUBLANE, HY_COLS), lambda bi, i: (bi, jnp.maximum(i * hb - 1, 0), ucol)),
            pl.BlockSpec((1, SUBLANE, HY_COLS), lambda bi, i: (bi, jnp.minimum((i + 1) * hb, nh - 1), ucol)),
            pl.BlockSpec((3, HY_COLS), lambda bi, i: (0, 0)),
            pl.BlockSpec((1, HY_COLS), lambda bi, i: (0, 0)),
        ],
        out_specs=[pl.BlockSpec((1, tm, HY_CH), row)] * (HY_ORDER + 1),
        out_shape=[jax.ShapeDtypeStruct((b, l, HY_CH), F32)] * (HY_ORDER + 1),
        compiler_params=_cparams(("parallel", "parallel")),
        name="hy_short_conv",
    )(z, z, z, w, bias.reshape(1, HY_COLS))


def _hy_filter_kernel(feat_ref, w1_ref, b1_ref, w2_ref, b2_ref, w3_ref, delta_ref, kern_ref, den_ref):
    feat = feat_ref[...]
    h = jnp.sin(jnp.dot(feat, w1_ref[...], preferred_element_type=F32, precision=HIGHEST) + b1_ref[...])
    h = jnp.sin(jnp.dot(h, w2_ref[...], preferred_element_type=F32, precision=HIGHEST) + b2_ref[...])
    f = jnp.dot(h.astype(BF16), w3_ref[...].astype(BF16), preferred_element_type=F32)
    t = feat[:, 0:1]
    live = feat[:, LANE - 1:LANE]
    kern = f * (jnp.exp(-t * delta_ref[...]) + HY_SHIFT) * live
    kern_ref[...] = kern

    @pl.when((pl.program_id(0) == 0) & (pl.program_id(1) == 0))
    def _():
        den_ref[...] = jnp.zeros_like(den_ref)

    den_ref[...] += jnp.sum(jnp.abs(kern), axis=0, keepdims=True)


def hy_filters(l, w1, b1, w2, b2, w3, *, tr):
    t = jnp.linspace(0.0, 1.0, l, dtype=F32)[:, None]
    w = (2.0 * math.pi / l) * jnp.arange(l, dtype=F32)[:, None]
    bands = jnp.linspace(1e-4, HY_BANDS - 1.0, HY_BANDS, dtype=F32)[None, :]
    feats = jnp.concatenate([t, jnp.cos(bands * w), -jnp.sin(bands * w)], axis=-1)
    feats = jnp.pad(feats, ((0, 0), (0, LANE - 1 - HY_EMB)))
    feats = jnp.concatenate([feats, jnp.ones((l, 1), F32)], axis=-1)
    table = jnp.concatenate([feats, jnp.zeros((1, LANE), F32), feats[:0:-1]], axis=0)
    w1p = jnp.pad(w1.astype(F32), ((0, LANE - HY_EMB), (0, 0)))
    w3s = w3.astype(F32).reshape(HY_FFN, HY_ORDER, 2, HY_CH).transpose(0, 2, 1, 3).reshape(HY_FFN, -1)
    deltas = jnp.abs(jnp.linspace(HY_MIN_DECAY, HY_MAX_DECAY, HY_CH, dtype=F32))
    deltas = jnp.tile(deltas, HY_ORDER).reshape(1, -1)
    oc = HY_ORDER * HY_CH
    nb = l // tr
    const = lambda s, i: (0, 0)
    return pl.pallas_call(
        _hy_filter_kernel,
        grid=(2, nb),
        in_specs=[
            pl.BlockSpec((tr, LANE), lambda s, i: (s * nb + i, 0)),
            pl.BlockSpec((LANE, HY_FFN), const), pl.BlockSpec((1, HY_FFN), const),
            pl.BlockSpec((HY_FFN, HY_FFN), const), pl.BlockSpec((1, HY_FFN), const),
            pl.BlockSpec((HY_FFN, oc), lambda s, i: (0, s)),
            pl.BlockSpec((1, oc), const),
        ],
        out_specs=[pl.BlockSpec((tr, oc), lambda s, i: (s * nb + i, 0)), pl.BlockSpec((1, oc), const)],
        out_shape=[jax.ShapeDtypeStruct((2 * l, oc), F32), jax.ShapeDtypeStruct((1, oc), F32)],
        compiler_params=_cparams(("arbitrary", "arbitrary")),
        name="hy_filters",
    )(table, w1p, b1.reshape(1, -1).astype(F32), w2.astype(F32), b2.reshape(1, -1).astype(F32), w3s, deltas)


def _cis(num, den):
    ang = (-2.0 * math.pi / den) * (num % den).astype(F32)
    return jnp.cos(ang), jnp.sin(ang)


def _block_complex(re, im):
    return jnp.concatenate([jnp.concatenate([re, -im], axis=-1), jnp.concatenate([im, re], axis=-1)], axis=-2)


def _fft_tables(p):
    q = FFT_Q
    n = p * q
    i_p = jnp.arange(p, dtype=jnp.int32)
    i_q = jnp.arange(q, dtype=jnp.int32)
    fr, fi = _cis(i_p[:, None] * i_p[None, :], p)
    f_filt = jnp.concatenate([fr, fi], axis=0)
    f_in = _block_complex(fr[:, :p // 2], fi[:, :p // 2])
    f_out = _block_complex(fr[:p // 2, :], -fi[:p // 2, :]) * (1.0 / n)
    num = i_q[None, None, :] * i_p[:, None, None] + p * i_q[None, :, None] * i_q[None, None, :]
    gr, gi = _cis(num, n)
    g = _block_complex(gr, gi)
    h = _block_complex(jnp.swapaxes(gr, 1, 2), -jnp.swapaxes(gi, 1, 2))
    return tuple(a.astype(BF16) for a in (f_filt, f_in, g, h, f_out))


FFT_TC = 512


def _fft_in_kernel(f_ref, x_ref, o_ref, xs_ref):
    for s in range(x_ref.shape[1]):
        xs_ref[...] = x_ref[:, s, :]
        o_ref[:, s, :] = jnp.dot(f_ref[...], xs_ref[...].astype(BF16), preferred_element_type=F32)


def fft_in(f, x):
    k, q, c = x.shape
    tc = FFT_TC
    return pl.pallas_call(
        _fft_in_kernel,
        grid=(q // SUBLANE, c // tc),
        in_specs=[pl.BlockSpec(f.shape, lambda j, ci: (0, 0)),
                  pl.BlockSpec((k, SUBLANE, tc), lambda j, ci: (0, j, ci))],
        out_specs=pl.BlockSpec((f.shape[0], SUBLANE, tc), lambda j, ci: (0, j, ci)),
        out_shape=jax.ShapeDtypeStruct((f.shape[0], q, c), F32),
        scratch_shapes=[pltpu.VMEM((k, tc), F32)],
        compiler_params=_cparams(("parallel", "parallel")),
        name="fft_in",
    )(f, x)


def _cmul_rows(s, k):
    half = s.shape[0] // 2
    sr, si, kr, ki = s[:half], s[half:], k[:half], k[half:]
    return jnp.concatenate([sr * kr - si * ki, sr * ki + si * kr], axis=0)


def _fft_mid_kernel(g_ref, h_ref, a_ref, af_ref, inv_ref, o_ref):
    q2 = g_ref.shape[1]
    c = a_ref.shape[-1]
    a = a_ref[...].reshape(q2, c).astype(BF16)
    af = af_ref[...].reshape(q2, c).astype(BF16)
    y = jnp.dot(g_ref[0], a, preferred_element_type=F32)
    kf = jnp.dot(g_ref[0], af, preferred_element_type=F32) * inv_ref[...]
    z = _cmul_rows(y, kf).astype(BF16)
    o_ref[...] = jnp.dot(h_ref[0], z, preferred_element_type=F32).reshape(o_ref.shape)


def fft_mid(g, h, a, af, den, order):
    p = g.shape[0]
    c = HY_CH
    mat = pl.BlockSpec((1, 2 * FFT_Q, 2 * FFT_Q), lambda k: (k, 0, 0))
    sig = pl.BlockSpec((2, 1, FFT_Q, c), lambda k: (0, k, 0, 0))
    return pl.pallas_call(
        _fft_mid_kernel,
        grid=(p,),
        in_specs=[mat, mat, sig,
                  pl.BlockSpec((2, 1, FFT_Q, c), lambda k: (0, k, 0, order)),
                  pl.BlockSpec((1, c), lambda k: (0, order))],
        out_specs=sig,
        out_shape=jax.ShapeDtypeStruct((2, p, FFT_Q, c), F32),
        compiler_params=_cparams(("parallel",)),
        name="fft_mid",
    )(g, h, a.reshape(2, p, FFT_Q, c), af.reshape(2, p, FFT_Q, af.shape[-1]), 1.0 / den
      ).reshape(2 * p, FFT_Q, c)


def _fft_out_kernel(f_ref, c_ref, z_ref, gate_ref, bias_ref, o_ref, cs_ref, zs_ref, gs_ref):
    for s in range(c_ref.shape[1]):
        cs_ref[...] = c_ref[:, s, :]
        zs_ref[...] = z_ref[:, s, :]
        gs_ref[...] = gate_ref[:, s, :]
        y = jnp.dot(f_ref[...], cs_ref[...].astype(BF16), preferred_element_type=F32)
        o_ref[:, s, :] = gs_ref[...] * (y + zs_ref[...] * bias_ref[...])


def fft_out(f, cm, z, gate, bias):
    p2, q, c = cm.shape
    tc = FFT_TC
    rows = f.shape[0]
    half = pl.BlockSpec((rows, SUBLANE, tc), lambda j, ci: (0, j, ci))
    return pl.pallas_call(
        _fft_out_kernel,
        grid=(q // SUBLANE, c // tc),
        in_specs=[pl.BlockSpec(f.shape, lambda j, ci: (0, 0)),
                  pl.BlockSpec((p2, SUBLANE, tc), lambda j, ci: (0, j, ci)),
                  half, half,
                  pl.BlockSpec((1, tc), lambda j, ci: (0, ci))],
        out_specs=half,
        out_shape=jax.ShapeDtypeStruct(z.shape, F32),
        scratch_shapes=[pltpu.VMEM((p2, tc), F32), pltpu.VMEM((rows, tc), F32), pltpu.VMEM((rows, tc), F32)],
        compiler_params=_cparams(("parallel", "parallel")),
        name="fft_out",
    )(f, cm, z, gate, bias)


def _hy_ctx_kernel(ff_ref, fin_ref, fout_ref, kern_ref, inv_ref, z_ref, g1_ref, g2_ref, bias_ref, o_ref):
    b, l, c = z_ref.shape
    z = z_ref[...].reshape(b * l, c)
    gates = (g1_ref, g2_ref)
    for n in range(HY_ORDER):
        cols = slice(n * c, (n + 1) * c)
        kf = (jnp.dot(ff_ref[...], kern_ref[:, cols].astype(BF16), preferred_element_type=F32)
              * inv_ref[:, cols])
        s = jnp.dot(fin_ref[...], z.astype(BF16), preferred_element_type=F32)
        y = jnp.dot(fout_ref[...], _cmul_rows(s, kf).astype(BF16), preferred_element_type=F32)
        z = gates[n][...].reshape(b * l, c) * (y + z * bias_ref[n:n + 1, :])
    o_ref[...] = z.reshape(b, l, c).astype(o_ref.dtype)


def hy_ctx_conv(kern, den, z0, g1, g2, bias):
    b, l, c = z0.shape
    assert b == 2
    n = 2 * l
    i_n = jnp.arange(n, dtype=jnp.int32)
    fr, fi = _cis(i_n[:, None] * i_n[None, :], n)
    ff = jnp.concatenate([fr, fi], axis=0).astype(BF16)
    fin = _block_complex(fr[:, :l], fi[:, :l]).astype(BF16)
    fout = (_block_complex(fr[:l, :], -fi[:l, :]) * (1.0 / n)).astype(BF16)
    return pl.pallas_call(
        _hy_ctx_kernel,
        out_shape=jax.ShapeDtypeStruct((b, l, c), BF16),
        compiler_params=pltpu.CompilerParams(vmem_limit_bytes=VMEM_LIMIT),
        name="hy_ctx_conv",
    )(ff, fin, fout, kern, 1.0 / den, z0, g1, g2, bias)


def hyena_latent(z, conv_w, conv_b, w1, b1, w2, b2, w3, bias):
    b, l, _ = z.shape
    assert b == 2 and (2 * l) % (2 * FFT_Q) == 0
    p = 2 * l // FFT_Q
    z0, *gates = hy_short_conv(z, conv_w, conv_b, tm=256)
    kern, den = hy_filters(l, w1, b1, w2, b2, w3, tr=512)
    f_filt, f_in, g, h, f_out = _fft_tables(p)
    af = fft_in(f_filt, kern.reshape(p, FFT_Q, HY_ORDER * HY_CH))
    split = lambda t: t.reshape(p, FFT_Q, HY_CH)
    zz = split(z0)
    for n in range(HY_ORDER):
        cm = fft_mid(g, h, fft_in(f_in, zz), af, den, n)
        zz = fft_out(f_out, cm, zz, split(gates[n]), bias[n].astype(F32).reshape(1, HY_CH))
    return zz.reshape(b, l, HY_CH)


def hyena_ctx(z, conv_w, conv_b, w1, b1, w2, b2, w3, bias):
    b, l, _ = z.shape
    z0, g1, g2 = hy_short_conv(z, conv_w, conv_b, tm=l)
    kern, den = hy_filters(l, w1, b1, w2, b2, w3, tr=l)
    return hy_ctx_conv(kern, den, z0, g1, g2, bias.astype(F32))


def even_mixer_core(zc, zl, gla_w_gate, gla_b_gate, gla_norm, hy_conv_w, hy_conv_b,
                    hy_w1, hy_b1, hy_w2, hy_b2, hy_w3, hy_bias):
    gc, gl = gla_bidir(zc, zl, gla_w_gate, gla_b_gate, gla_norm)
    hy = (hy_conv_w, hy_conv_b, hy_w1, hy_b1, hy_w2, hy_b2, hy_w3, hy_bias)
    return (gc, hyena_ctx(zc, *hy)), (gl, hyena_latent(zl, *hy))


def odd_mixer_core(zc, zl, ret_norm, att_q_norm, att_k_norm, att_sink):
    _, rl = retention_bidir(zc, zl, ret_norm)
    qkv_l = att_prep(zl, att_q_norm, att_k_norm, _axial_tables(zl.shape[1]), tm=512)
    qkv_c = att_prep(zc, att_q_norm, att_k_norm, None, tm=zc.shape[1])
    return rl, window_attention(qkv_l, qkv_c, att_sink)


def kernel(x, c, ctx, c_ctx, ada_w, ada_b, norm_mix, norm_ffn, ev_w_in, ev_w_out, gla_w_gate, gla_b_gate, gla_norm, hy_conv_w, hy_conv_b, hy_w1, hy_b1, hy_w2, hy_b2, hy_w3, hy_bias, od_w_in, od_w_out, ret_norm, att_q_norm, att_k_norm, att_sink, router_w, router_b, moe_w1, moe_w3, moe_w2):
    assert DEPTH == 2
    bsz, seq, d = x.shape
    n_ctx = ctx.shape[1]
    c_pad = jnp.zeros((8, d), F32).at[:bsz].set(c).at[bsz].set(c_ctx)
    mods = ada_modulation(c_pad, ada_w, ada_b)
    slot = np.array([SUBLANE * (e % EXPERTS_PER_GROUP) + e // EXPERTS_PER_GROUP for e in range(N_EXPERTS)])
    rw_pad = jnp.zeros((d, LOGIT_PAD), F32).at[:, slot].set(router_w)
    rw_hi = rw_pad.astype(BF16)
    rw_pad = jnp.concatenate([rw_hi, (rw_pad - rw_hi.astype(F32)).astype(BF16)], axis=1)
    rb_pad = jnp.zeros((EXPERTS_PER_GROUP * SUBLANE, LANE), F32).at[slot].set(
        jnp.broadcast_to(router_b.astype(F32)[:, None], (N_EXPERTS, LANE)))

    xl, xc = x, ctx
    for layer in range(DEPTH):
        last = layer == DEPTH - 1
        ml = mods[layer, :bsz].reshape(bsz, 1, 6, d)
        mc = jnp.broadcast_to(mods[layer, bsz].reshape(1, 1, 6, d), (bsz, 1, 6, d))
        sh1, sc1, g1, sh2, sc2, g2 = [ml[:, :, i] for i in range(6)]
        csh1, csc1, cg1, csh2, csc2, cg2 = [mc[:, :, i] for i in range(6)]
        i = layer // 2
        if layer % 2 == 0:
            w_in, w_out = ev_w_in[i], ev_w_out[i]
            lr0 = sum(EVEN_SPLITS[:4])
            lr1 = lr0 + EVEN_SPLITS[4]
            w_in = jnp.concatenate([w_in[:, :lr0], w_in[:, lr1:], w_in[:, lr0:lr1],
                                    jnp.zeros((d, EVEN_COLS_PAD - w_in.shape[1]), F32)], axis=1)
            tn = EVEN_COLS_PAD // 7
        else:
            w_in, w_out = od_w_in[i], od_w_out[i]
            tn = w_in.shape[1] // 6
        w_in = cast_bf16(w_in, tr=256)
        zl = norm_mod_matmul(xl, norm_mix[layer], sc1, sh1, w_in, tm=1024, tn=tn, out_dtype=F32)
        zc = norm_mod_matmul(xc, norm_mix[layer], csc1, csh1, w_in, tm=n_ctx, tn=tn, out_dtype=F32)
        if layer % 2 == 0:
            mixc, mixl = even_mixer_core(zc, zl, gla_w_gate[i], gla_b_gate[i], gla_norm[i], hy_conv_w[i],
                                         hy_conv_b[i], hy_w1[i], hy_b1[i], hy_w2[i], hy_b2[i], hy_w3[i],
                                         hy_bias[i])
        else:
            mixl = odd_mixer_core(zc, zl, ret_norm[i], att_q_norm[i], att_k_norm[i], att_sink[i])
        lat0 = 0 if last else bsz * n_ctx
        flat = lambda rt: rt.transpose(1, 0, 2).reshape(ROUTE_ROWS, -1)
        hs, route = [], []
        if not last:
            xc, hc, rtc = outproj_residual_norm(*mixc, w_out, xc, cg1, norm_ffn[layer], csc2, csh2, rw_pad,
                                                rb_pad, tm=n_ctx)
            hs.append(hc)
            route.append(flat(rtc))
        xl, hl, rtl = outproj_residual_norm(*mixl, w_out, xl, g1, norm_ffn[layer], sc2, sh2, rw_pad, rb_pad,
                                            tm=512)
        hs.append(hl)
        route.append(flat(rtl))
        y_sorted, pos, w_tok = moe_experts(hs, jnp.concatenate(route, axis=1), moe_w1, moe_w3, moe_w2, layer)
        if not last:
            xc = moe_gather_combine(xc, cg2, y_sorted, pos, w_tok, 0, tm=n_ctx)
        xl = moe_gather_combine(xl, g2, y_sorted, pos, w_tok, lat0, tm=512)
    return xl
```

```python
import functools
import math

import jax
import jax.numpy as jnp
import numpy as np
from jax import lax
from jax.experimental import pallas as pl
from jax.experimental.pallas import tpu as pltpu

D_MODEL = 2048
BATCH = 2
SEQ = 8192
DEPTH = 2
GRID_W = 64
CTX_LEN = 256
MIX_WIDTH = D_MODEL
GROUP_WIDTH = MIX_WIDTH // 2
EPS = 1e-6
F32 = jnp.float32
BF16 = jnp.bfloat16

GLA_HEADS = 4
GLA_DK = GROUP_WIDTH // (2 * GLA_HEADS)
GLA_DV = GROUP_WIDTH // GLA_HEADS
GLA_LOWRANK = 16
GLA_GATE_NORMALIZER = 16.0
GLA_CHUNK = 64

HY_CH = GROUP_WIDTH
HY_ORDER = 2
HY_BANDS = 16
HY_EMB = 1 + 2 * HY_BANDS
HY_FFN = 64
HY_SHIFT = 0.05
HY_MIN_DECAY = math.log(1e-2) / 1.5
HY_MAX_DECAY = math.log(1e-2) / 0.3

RET_HEADS = 4
RET_DK = GROUP_WIDTH // (2 * RET_HEADS)
RET_DV = GROUP_WIDTH // RET_HEADS
RET_CHUNK = 64

ATT_HD = 128
ATT_HEADS = GROUP_WIDTH // ATT_HD
ATT_KV_HEADS = 2
ATT_GROUP = ATT_HEADS // ATT_KV_HEADS
ATT_WINDOW = 128
ATT_BLOCK = 128
ROPE_BASE = 10000.0

N_EXPERTS = 16
N_GROUPS = 4
EXPERTS_PER_GROUP = N_EXPERTS // N_GROUPS
TOP_K = 2
D_FF_EXPERT = D_MODEL // 2

EVEN_SPLITS = (GLA_HEADS * GLA_DK, GLA_HEADS * GLA_DK, GLA_HEADS * GLA_DV, GLA_HEADS * GLA_DV,
               2 * GLA_LOWRANK, (HY_ORDER + 1) * HY_CH)
ODD_SPLITS = (RET_HEADS * RET_DK, RET_HEADS * RET_DK, RET_HEADS * RET_DV, RET_HEADS * RET_DV,
              ATT_HEADS * ATT_HD, ATT_KV_HEADS * ATT_HD, ATT_KV_HEADS * ATT_HD)

LANE = 128
EVEN_COLS_PAD = -(-sum(EVEN_SPLITS) // (7 * LANE)) * (7 * LANE)
VMEM_LIMIT = 56 * 1024 * 1024
MOE_TM = 512
MOE_FF_SPLIT = 2
MOVE_ROWS = 512
LOGIT_PAD = LANE


def _cparams(sem):
    return pltpu.CompilerParams(dimension_semantics=sem, vmem_limit_bytes=VMEM_LIMIT)


def _ada_kernel(c_ref, w_ref, b_ref, o_ref):
    c = c_ref[...]
    s = (c * jax.nn.sigmoid(c)).astype(BF16)
    o_ref[0] = jnp.dot(s, w_ref[0].astype(BF16), preferred_element_type=F32) + b_ref[0]


def ada_modulation(c_pad, ada_w, ada_b):
    depth, d, n = ada_w.shape
    tn = 1024
    return pl.pallas_call(
        _ada_kernel,
        grid=(depth, n // tn),
        in_specs=[
            pl.BlockSpec((8, d), lambda l, j: (0, 0)),
            pl.BlockSpec((1, d, tn), lambda l, j: (l, 0, j)),
            pl.BlockSpec((1, 1, tn), lambda l, j: (l, 0, j)),
        ],
        out_specs=pl.BlockSpec((1, 8, tn), lambda l, j: (l, 0, j)),
        out_shape=jax.ShapeDtypeStruct((depth, 8, n), F32),
        compiler_params=_cparams(("parallel", "arbitrary")),
        name="ada_modulation",
    )(c_pad, ada_w, ada_b.reshape(depth, 1, n))


def _norm_mod(x, gain, sc, sh):
    ms = jnp.mean(x * x, axis=-1, keepdims=True)
    return (x * lax.rsqrt(ms + EPS) * gain) * (1.0 + sc) + sh


def _cast_kernel(x_ref, o_ref):
    o_ref[...] = x_ref[...].astype(o_ref.dtype)


def cast_bf16(w, *, tr):
    r, c = w.shape
    return pl.pallas_call(
        _cast_kernel,
        grid=(r // tr,),
        in_specs=[pl.BlockSpec((tr, c), lambda i: (i, 0))],
        out_specs=pl.BlockSpec((tr, c), lambda i: (i, 0)),
        out_shape=jax.ShapeDtypeStruct((r, c), BF16),
        compiler_params=_cparams(("parallel",)),
        name="cast_bf16",
    )(w)


def _inproj_kernel(x_ref, gain_ref, sc_ref, sh_ref, w_ref, o_ref, h_ref):
    @pl.when(pl.program_id(2) == 0)
    def _():
        h_ref[...] = _norm_mod(x_ref[0], gain_ref[...], sc_ref[0], sh_ref[0]).astype(BF16)

    o_ref[0] = jnp.dot(h_ref[...], w_ref[...].astype(BF16), preferred_element_type=F32).astype(o_ref.dtype)


def norm_mod_matmul(x, gain, sc, sh, w, *, tm, tn, out_dtype):
    b, l, d = x.shape
    n = w.shape[1]
    return pl.pallas_call(
        _inproj_kernel,
        grid=(b, l // tm, n // tn),
        in_specs=[
            pl.BlockSpec((1, tm, d), lambda bi, i, j: (bi, i, 0)),
            pl.BlockSpec((1, d), lambda bi, i, j: (0, 0)),
            pl.BlockSpec((1, 1, d), lambda bi, i, j: (bi, 0, 0)),
            pl.BlockSpec((1, 1, d), lambda bi, i, j: (bi, 0, 0)),
            pl.BlockSpec((d, tn), lambda bi, i, j: (0, j)),
        ],
        out_specs=pl.BlockSpec((1, tm, tn), lambda bi, i, j: (bi, i, j)),
        out_shape=jax.ShapeDtypeStruct((b, l, n), out_dtype),
        scratch_shapes=[pltpu.VMEM((tm, d), BF16)],
        compiler_params=_cparams(("parallel", "parallel", "arbitrary")),
        name="norm_mod_matmul",
    )(x, gain.reshape(1, d), sc, sh, w)


SUBLANE = 8
ROUTE_ROWS = SUBLANE


def _route_block(lg, rb_ref):
    tm = lg.shape[0]
    t = lg.T
    sub = lax.broadcasted_iota(jnp.int32, (SUBLANE, tm), 0)
    valid = sub < N_GROUPS
    members = range(EXPERTS_PER_GROUP)
    s = [jnp.where(valid, t[SUBLANE * j:SUBLANE * (j + 1), :], -jnp.inf) for j in members]
    mx = jnp.max(jnp.maximum(jnp.maximum(s[0], s[1]), jnp.maximum(s[2], s[3])), axis=0, keepdims=True)
    e = [jnp.exp(sj - mx) for sj in s]
    den = jnp.sum((e[0] + e[1]) + (e[2] + e[3]), axis=0, keepdims=True)
    sc = [ej / den for ej in e]
    sel = [sc[j] + jnp.concatenate([rb_ref[SUBLANE * j:SUBLANE * (j + 1), :]] * (tm // LANE), axis=1)
           for j in members]
    hi01, lo01 = jnp.maximum(sel[0], sel[1]), jnp.minimum(sel[0], sel[1])
    hi23, lo23 = jnp.maximum(sel[2], sel[3]), jnp.minimum(sel[2], sel[3])
    top1 = jnp.maximum(hi01, hi23)
    top2 = jnp.maximum(jnp.minimum(hi01, hi23), jnp.maximum(lo01, lo23))
    gscore = jnp.where(valid, top1 + top2, -jnp.inf)
    gmax = jnp.max(gscore, axis=0, keepdims=True)
    gidx = jnp.min(jnp.where(gscore == gmax, sub, SUBLANE), axis=0, keepdims=True)
    chosen = sub == gidx
    v = [jnp.sum(jnp.where(chosen, sel[j], 0.0), axis=0, keepdims=True) for j in members]
    p = [jnp.sum(jnp.where(chosen, sc[j], 0.0), axis=0, keepdims=True) for j in members]
    one = lambda cond: jnp.where(cond, 1.0, 0.0)
    picks = []
    for want in (0.0, 1.0):
        idx = jnp.zeros_like(v[0])
        wt = jnp.zeros_like(v[0])
        for j in members:
            rank = sum(one(v[k] > v[j]) for k in members if k != j) + sum(one(v[k] == v[j]) for k in range(j))
            hit = rank == want
            idx = idx + jnp.where(hit, float(j), 0.0)
            wt = wt + jnp.where(hit, p[j], 0.0)
        picks.append((idx, wt))
    (j1, p1), (j2, p2) = picks
    base = (EXPERTS_PER_GROUP * gidx).astype(F32)
    tot = p1 + p2
    rows = [base + j1, base + j2, p1 / tot, p2 / tot]
    return jnp.concatenate(rows + [jnp.zeros((ROUTE_ROWS - len(rows), tm), F32)], axis=0)


def _outproj_kernel(m1_ref, m2_ref, w_ref, x_ref, g_ref, gain_ref, sc_ref, sh_ref, rw_ref, rb_ref,
                    xo_ref, h_ref, rt_ref):
    k1 = m1_ref.shape[-1]
    o = (jnp.dot(m1_ref[0].astype(BF16), w_ref[:k1, :].astype(BF16), preferred_element_type=F32)
         + jnp.dot(m2_ref[0].astype(BF16), w_ref[k1:, :].astype(BF16), preferred_element_type=F32))
    xn = x_ref[0] + g_ref[0] * o
    xo_ref[0] = xn
    h = _norm_mod(xn, gain_ref[...], sc_ref[0], sh_ref[0])
    h_ref[...] = h
    h_hi = h.astype(BF16)
    h_lo = (h - h_hi.astype(F32)).astype(BF16)
    both = jnp.dot(h_hi, rw_ref[...], preferred_element_type=F32)
    lg = (both[:, :LOGIT_PAD] + both[:, LOGIT_PAD:]
          + jnp.dot(h_lo, rw_ref[:, :LOGIT_PAD], preferred_element_type=F32))
    rt_ref[0] = _route_block(lg, rb_ref)


def outproj_residual_norm(m1, m2, w, x, g, gain, sc, sh, rw_pad, rb_pad, *, tm):
    b, l, d = x.shape
    k1, k2 = m1.shape[-1], m2.shape[-1]
    nb = l // tm
    row = lambda bi, i: (bi, i, 0)
    mod = lambda bi, i: (bi, 0, 0)
    const = lambda bi, i: (0, 0)
    return pl.pallas_call(
        _outproj_kernel,
        grid=(b, nb),
        in_specs=[
            pl.BlockSpec((1, tm, k1), row),
            pl.BlockSpec((1, tm, k2), row),
            pl.BlockSpec((k1 + k2, d), const, pipeline_mode=pl.Buffered(1)),
            pl.BlockSpec((1, tm, d), row),
            pl.BlockSpec((1, 1, d), mod),
            pl.BlockSpec((1, d), const),
            pl.BlockSpec((1, 1, d), mod),
            pl.BlockSpec((1, 1, d), mod),
            pl.BlockSpec((d, 2 * LOGIT_PAD), const),
            pl.BlockSpec(rb_pad.shape, const),
        ],
        out_specs=[
            pl.BlockSpec((1, tm, d), row),
            pl.BlockSpec((tm, d), lambda bi, i: (bi * nb + i, 0)),
            pl.BlockSpec((1, ROUTE_ROWS, tm), lambda bi, i: (bi, 0, i)),
        ],
        out_shape=[
            jax.ShapeDtypeStruct((b, l, d), F32),
            jax.ShapeDtypeStruct((b * l, d), F32),
            jax.ShapeDtypeStruct((b, ROUTE_ROWS, l), F32),
        ],
        compiler_params=_cparams(("parallel", "parallel")),
        name="outproj_residual_norm",
    )(m1, m2, w, x, g, gain.reshape(1, d), sc, sh, rw_pad, rb_pad)


def _moe_kernel(te_ref, tsrc_ref, tvalid_ref, x_ref, w1_ref, w3_ref, w2_ref, o_ref):
    i = pl.program_id(0)

    @pl.when(tvalid_ref[i] > 0)
    def _():
        x = x_ref[...].astype(BF16)
        ff = w1_ref.shape[-1]
        fc = ff // MOE_FF_SPLIT
        y = None
        for c in range(MOE_FF_SPLIT):
            cols = slice(c * fc, (c + 1) * fc)
            a = jnp.dot(x, w1_ref[0, 0, :, cols].astype(BF16), preferred_element_type=F32)
            b = jnp.dot(x, w3_ref[0, 0, :, cols].astype(BF16), preferred_element_type=F32)
            mid = (a * jax.nn.sigmoid(a) * b).astype(BF16)
            part = jnp.dot(mid, w2_ref[0, 0, cols, :].astype(BF16), preferred_element_type=F32)
            y = part if y is None else y + part
        o_ref[...] = y

    @pl.when(tvalid_ref[i] == 0)
    def _():
        o_ref[...] = jnp.zeros_like(o_ref)


def moe_grouped(x_sorted, w1, w3, w2, layer, tile_expert, tile_src, tile_valid):
    p, d = x_sorted.shape
    ff = w1.shape[-1]
    tm = MOE_TM
    wmap = lambda i, te, ts, tv: (layer, te[i], 0, 0)
    grid_spec = pltpu.PrefetchScalarGridSpec(
        num_scalar_prefetch=3,
        grid=(p // tm,),
        in_specs=[
            pl.BlockSpec((tm, d), lambda i, te, ts, tv: (ts[i], 0)),
            pl.BlockSpec((1, 1, d, ff), wmap, pipeline_mode=pl.Buffered(1)),
            pl.BlockSpec((1, 1, d, ff), wmap, pipeline_mode=pl.Buffered(1)),
            pl.BlockSpec((1, 1, ff, d), wmap, pipeline_mode=pl.Buffered(1)),
        ],
        out_specs=pl.BlockSpec((tm, d), lambda i, te, ts, tv: (i, 0)),
    )
    return pl.pallas_call(
        _moe_kernel,
        grid_spec=grid_spec,
        out_shape=jax.ShapeDtypeStruct((p, d), F32),
        compiler_params=_cparams(("arbitrary",)),
        name="moe_grouped",
    )(tile_expert, tile_src, tile_valid, x_sorted, w1, w3, w2)


ZERO_ROWS = MOE_TM + SUBLANE


def _row_scatter_kernel(dst_ref, fill_ref, *rest, starts):
    h_refs, (out_ref, sem, zbuf) = rest[:-3], rest[-3:]
    j = pl.program_id(0)

    @pl.when((j == 0) & (pl.program_id(1) == 0))
    def _():
        zbuf[...] = jnp.zeros_like(zbuf)
        for e in range(fill_ref.shape[0]):
            fill = pltpu.make_async_copy(
                zbuf, out_ref.at[pl.ds(pl.multiple_of(fill_ref[e], SUBLANE), ZERO_ROWS)], sem)
            fill.start()
            fill.wait()

    for s, h_ref in enumerate(h_refs):
        @pl.when((j >= starts[s]) & (j < starts[s + 1]))
        def _(h_ref=h_ref):
            def issue(r, carry):
                pltpu.make_async_copy(h_ref.at[pl.ds(r, 1)], out_ref.at[pl.ds(dst_ref[0, 0, r], 1)], sem).start()
                return carry

            lax.fori_loop(0, h_ref.shape[0], issue, 0, unroll=4)
            pltpu.make_async_copy(h_ref, h_ref, sem).wait()


def row_scatter(hs, dst, out_rows, fill_start, *, step):
    d = hs[0].shape[1]
    nbs = [h.shape[0] // step for h in hs]
    assert all(nb * step == h.shape[0] for nb, h in zip(nbs, hs))
    starts = [0]
    for nb in nbs:
        starts.append(starts[-1] + nb)
    nb = starts[-1]
    dst = dst.reshape(TOP_K, nb, 1, step).transpose(1, 0, 2, 3).reshape(nb * TOP_K, 1, step)
    seg_spec = lambda s: pl.BlockSpec((step, d), lambda j, k: (jnp.clip(j - starts[s], 0, nbs[s] - 1), 0))
    return pl.pallas_call(
        functools.partial(_row_scatter_kernel, starts=tuple(starts)),
        grid=(nb, TOP_K),
        in_specs=[pl.BlockSpec((1, 1, step), lambda j, k: (j * TOP_K + k, 0, 0), memory_space=pltpu.SMEM),
                  pl.BlockSpec(memory_space=pltpu.SMEM)] + [seg_spec(s) for s in range(len(hs))],
        out_specs=pl.BlockSpec(memory_space=pl.ANY),
        out_shape=jax.ShapeDtypeStruct((out_rows, d), hs[0].dtype),
        scratch_shapes=[pltpu.SemaphoreType.DMA(()), pltpu.VMEM((ZERO_ROWS, d), hs[0].dtype)],
        compiler_params=_cparams(("arbitrary", "arbitrary")),
        name="row_scatter",
    )(dst, fill_start, *hs)


def _gather_combine_kernel(p0_ref, p1_ref, x_ref, g_ref, w0_ref, w1_ref, y_ref, o_ref, buf, sem):
    tm = x_ref.shape[1]

    def issue(r, carry):
        pltpu.make_async_copy(y_ref.at[pl.ds(p0_ref[0, 0, r], 1)], buf.at[0, pl.ds(r, 1)], sem).start()
        pltpu.make_async_copy(y_ref.at[pl.ds(p1_ref[0, 0, r], 1)], buf.at[1, pl.ds(r, 1)], sem).start()
        return carry

    lax.fori_loop(0, tm, issue, 0, unroll=2)
    pltpu.make_async_copy(buf, buf, sem).wait()
    y = w0_ref[...] * buf[0] + w1_ref[...] * buf[1]
    o_ref[0] = x_ref[0] + g_ref[0] * y


def moe_gather_combine(x, g, y_sorted, pos, w_tok, row0, *, tm):
    b, l, d = x.shape
    n = pos.shape[1]
    nb = l // tm
    assert row0 % tm == 0 and n % tm == 0
    tok = lambda bi, i: row0 // tm + bi * nb + i
    pos = pos.reshape(TOP_K * (n // tm), 1, tm)
    w_col = w_tok.reshape(TOP_K * n, 1)
    idx = lambda k: pl.BlockSpec((1, 1, tm), lambda bi, i: (k * (n // tm) + tok(bi, i), 0, 0),
                                 memory_space=pltpu.SMEM)
    wgt = lambda k: pl.BlockSpec((tm, 1), lambda bi, i: (k * (n // tm) + tok(bi, i), 0))
    return pl.pallas_call(
        _gather_combine_kernel,
        grid=(b, nb),
        in_specs=[idx(0), idx(1),
                  pl.BlockSpec((1, tm, d), lambda bi, i: (bi, i, 0)),
                  pl.BlockSpec((1, 1, d), lambda bi, i: (bi, 0, 0)),
                  wgt(0), wgt(1),
                  pl.BlockSpec(memory_space=pl.ANY)],
        out_specs=pl.BlockSpec((1, tm, d), lambda bi, i: (bi, i, 0)),
        out_shape=jax.ShapeDtypeStruct(x.shape, x.dtype),
        scratch_shapes=[pltpu.VMEM((TOP_K, tm, d), F32), pltpu.SemaphoreType.DMA(())],
        compiler_params=_cparams(("arbitrary", "arbitrary")),
        name="moe_gather_combine",
    )(pos, pos, x, g, w_col, w_col, y_sorted)


CHUNK = 64
SCAN_T = 256
_NT = (((1,), (1,)), ((), ()))
_TN = (((0,), (0,)), ((), ()))
HIGHEST = lax.Precision.HIGHEST


def _block_keep(t, reverse):
    r = lax.broadcasted_iota(jnp.int32, (t, t), 0)
    c = lax.broadcasted_iota(jnp.int32, (t, t), 1)
    shift = CHUNK.bit_length() - 1
    same = (r >> shift) == (c >> shift)
    return same & ((c >= r) if reverse else (c <= r))


def _chunk_order(n, reverse):
    return range(n - 1, -1, -1) if reverse else range(n)


def _block_scan(q, k, v, cum, st, keep, reverse):
    n = q.shape[0] // CHUNK
    ends = [cum[c * CHUNK:c * CHUNK + 1] if reverse else cum[(c + 1) * CHUNK - 1:(c + 1) * CHUNK]
            for c in range(n)]
    last = jnp.concatenate([jnp.broadcast_to(e, (CHUNK, e.shape[1])) for e in ends], axis=0)
    qe = (q * jnp.exp(cum)).astype(BF16)
    ke = (k * jnp.exp(-cum)).astype(BF16)
    kd = (k * jnp.exp(last - cum)).astype(BF16)
    vb = v.astype(BF16)
    att = lax.dot_general(qe, ke, _NT, preferred_element_type=F32)
    att = jnp.where(keep, att, 0.0).astype(BF16)
    intra = jnp.dot(att, vb, preferred_element_type=F32)
    outs = [None] * n
    for c in _chunk_order(n, reverse):
        sl = slice(c * CHUNK, (c + 1) * CHUNK)
        outs[c] = intra[sl] + lax.dot_general(qe[sl], st.astype(BF16), _NT, preferred_element_type=F32)
        st = jnp.exp(ends[c]) * st + lax.dot_general(vb[sl], kd[sl], _TN, preferred_element_type=F32)
    return jnp.concatenate(outs, axis=0), st


def _merge(o, ob, g, gain):
    tot = o + ob
    y = tot * lax.rsqrt(jnp.mean(tot * tot, axis=-1, keepdims=True) + EPS) * gain
    return y * (g * jax.nn.sigmoid(g))


def _scan_refs(rest, merge):
    if merge:
        return rest
    oc_ref, ol_ref, st_ref = rest
    return (None,) * 5 + (oc_ref, ol_ref, st_ref)


def _scan_heads(head_fn, dk, dv, v_ref, o_ref, ob_ref, g_ref, gn_ref, st_ref, keep, reverse):
    outs = []
    for h in range(st_ref.shape[0]):
        kcol = slice(h * dk, (h + 1) * dk)
        vcol = slice(h * dv, (h + 1) * dv)
        q, k, cum = head_fn(h, kcol)
        o, st = _block_scan(q, k, v_ref[0, :, vcol], cum, st_ref[h], keep, reverse)
        st_ref[h] = st
        outs.append(o if ob_ref is None else _merge(o, ob_ref[0, :, vcol], g_ref[0, :, vcol], gn_ref[...]))
    o_ref[0] = jnp.concatenate(outs, axis=1).astype(o_ref.dtype)


def _gla_kernel(qc, kc, vc, lrc, ql, kl, vl, lrl, wg, bg, *rest, reverse, merge):
    obc, obl, gc, gl, gn, oc_ref, ol_ref, st_ref = _scan_refs(rest, merge)

    def run(q_ref, k_ref, v_ref, lr_ref, o_ref, ob_ref, g_ref):
        keep = _block_keep(q_ref.shape[1], reverse)
        pre = jnp.dot(lr_ref[0], wg[...], preferred_element_type=F32, precision=HIGHEST) + bg[...]
        la = (jnp.minimum(pre, 0.0) - jnp.log(1.0 + jnp.exp(-jnp.abs(pre)))) * (1.0 / GLA_GATE_NORMALIZER)
        cum = jnp.dot(keep.astype(F32), la, preferred_element_type=F32, precision=HIGHEST)

        def head(h, kcol):
            return q_ref[0, :, kcol] * (GLA_DK ** -0.5), k_ref[0, :, kcol], cum[:, kcol]

        _scan_heads(head, GLA_DK, GLA_DV, v_ref, o_ref, ob_ref, g_ref, gn, st_ref, keep, reverse)

    @pl.when(pl.program_id(2) == 0)
    def _():
        st_ref[...] = jnp.zeros_like(st_ref)
        run(qc, kc, vc, lrc, oc_ref, obc, gc)

    @pl.when(pl.program_id(2) > 0)
    def _():
        run(ql, kl, vl, lrl, ol_ref, obl, gl)


def _ret_kernel(qc, kc, vc, ql, kl, vl, cos_ref, sin_ref, cum_ref, *rest, reverse, merge):
    obc, obl, gc, gl, gn, oc_ref, ol_ref, st_ref = _scan_refs(rest, merge)

    def run(q_ref, k_ref, v_ref, o_ref, ob_ref, g_ref, latent):
        t = q_ref.shape[1]
        keep = _block_keep(t, reverse)

        def head(h, kcol):
            q = q_ref[0, :, kcol]
            k = k_ref[0, :, kcol]
            if latent:
                cs = cos_ref[...]
                sn = sin_ref[...]
                q = q * cs + pltpu.roll(q, RET_DK // 2, 1) * sn
                k = k * cs + pltpu.roll(k, RET_DK // 2, 1) * sn
            return q, k * (RET_DK ** -0.5), jnp.concatenate([cum_ref[h]] * (t // CHUNK), axis=0)

        _scan_heads(head, RET_DK, RET_DV, v_ref, o_ref, ob_ref, g_ref, gn, st_ref, keep, reverse)

    @pl.when(pl.program_id(2) == 0)
    def _():
        st_ref[...] = jnp.zeros_like(st_ref)
        run(qc, kc, vc, oc_ref, obc, gc, False)

    @pl.when(pl.program_id(2) > 0)
    def _():
        run(ql, kl, vl, ol_ref, obl, gl, True)


def _scan_call(body, zc, zl, heads, dk, dv, col_q, col_k, col_v, shared_cols, extra_specs, merge_in,
               *, reverse, name):
    b, lc, _ = zc.shape
    l = zl.shape[1]
    t = SCAN_T
    nblk = l // t
    if reverse:
        blk = lambda i: jnp.where(i == 0, nblk - 1, nblk - i)
    else:
        blk = lambda i: jnp.maximum(i - 1, 0)

    def ctx_spec(w, col, per_head=True):
        return pl.BlockSpec((1, lc, w), lambda bi, h, i: (bi, 0, col // w + (h if per_head else 0)))

    def lat_spec(w, col, per_head=True):
        return pl.BlockSpec((1, t, w), lambda bi, h, i: (bi, blk(i), col // w + (h if per_head else 0)))

    hk, hv = heads * dk, heads * dv
    ins = [zc, zc, zc] + [zc] * len(shared_cols) + [zl, zl, zl] + [zl] * len(shared_cols)
    specs = [ctx_spec(hk, col_q), ctx_spec(hk, col_k), ctx_spec(hv, col_v)]
    specs += [ctx_spec(w, col, False) for w, col in shared_cols]
    specs += [lat_spec(hk, col_q), lat_spec(hk, col_k), lat_spec(hv, col_v)]
    specs += [lat_spec(w, col, False) for w, col in shared_cols]
    for a, s in extra_specs:
        ins.append(a)
        specs.append(s)
    if merge_in is not None:
        obc, obl, col_g, gain = merge_in
        ins += [obc, obl, zc, zl, gain.reshape(1, dv)]
        specs += [ctx_spec(hv, 0), lat_spec(hv, 0), ctx_spec(hv, col_g), lat_spec(hv, col_g),
                  pl.BlockSpec((1, dv), lambda bi, h, i: (0, 0))]
        out_dtype = BF16
    else:
        out_dtype = F32
    return pl.pallas_call(
        functools.partial(body, reverse=reverse, merge=merge_in is not None),
        grid=(b, 1, 1 + nblk),
        in_specs=specs,
        out_specs=[ctx_spec(hv, 0), lat_spec(hv, 0)],
        out_shape=[jax.ShapeDtypeStruct((b, lc, heads * dv), out_dtype),
                   jax.ShapeDtypeStruct((b, l, heads * dv), out_dtype)],
        scratch_shapes=[pltpu.VMEM((heads, dv, dk), F32)],
        compiler_params=_cparams(("parallel", "parallel", "arbitrary")),
        name=name,
    )(*ins)


EV_COL = dict(q=0, k=512, v=1024, g=2048, u=3072, lr=6144)
OD_COL = dict(q=0, k=512, v=1024, g=2048, aq=3072, ak=4096, av=4352)


def gla_bidir(zc, zl, gla_w_gate, gla_b_gate, gla_norm):
    hk = GLA_HEADS * GLA_DK
    outs = None
    for d, reverse in ((1, True), (0, False)):
        wg = jnp.zeros((LANE, hk), F32).at[d * GLA_LOWRANK:(d + 1) * GLA_LOWRANK].set(gla_w_gate[d])
        shared_cols = [(LANE, EV_COL["lr"])]
        extra_specs = [(wg, pl.BlockSpec((LANE, hk), lambda bi, h, i: (0, 0))),
                       (gla_b_gate[d].reshape(1, hk), pl.BlockSpec((1, hk), lambda bi, h, i: (0, 0)))]
        merge_in = None if outs is None else (outs[0], outs[1], EV_COL["g"], gla_norm)
        outs = _scan_call(_gla_kernel, zc, zl, GLA_HEADS, GLA_DK, GLA_DV, EV_COL["q"], EV_COL["k"],
                          EV_COL["v"], shared_cols, extra_specs, merge_in, reverse=reverse,
                          name="gla_bwd" if reverse else "gla_fwd_merge")
    return outs


def retention_bidir(zc, zl, ret_norm):
    l = zl.shape[1]
    inv = ROPE_BASE ** (-jnp.linspace(0.0, 1.0, RET_DK // 2, dtype=F32))
    ang = jnp.arange(l, dtype=F32)[:, None] * inv[None, :]
    cos2 = jnp.concatenate([jnp.cos(ang), jnp.cos(ang)], axis=-1)
    sin2 = jnp.concatenate([-jnp.sin(ang), jnp.sin(ang)], axis=-1)
    log_g_f = jnp.log(1.0 - 2.0 ** (-5.0 - jnp.arange(RET_HEADS, dtype=F32)))
    idx = jnp.arange(CHUNK, dtype=F32)
    nblk = l // SCAN_T
    outs = None
    for reverse in (True, False):
        if reverse:
            cum = (CHUNK - idx)[None, :, None] * log_g_f[::-1][:, None, None]
            blk = lambda i: jnp.where(i == 0, nblk - 1, nblk - i)
        else:
            cum = (idx + 1.0)[None, :, None] * log_g_f[:, None, None]
            blk = lambda i: jnp.maximum(i - 1, 0)
        cum = jnp.broadcast_to(cum, (RET_HEADS, CHUNK, RET_DK))
        tab_spec = pl.BlockSpec((SCAN_T, RET_DK), lambda bi, h, i, blk=blk: (blk(i), 0))
        extra_specs = [(cos2, tab_spec), (sin2, tab_spec),
                       (cum, pl.BlockSpec((RET_HEADS, CHUNK, RET_DK), lambda bi, h, i: (0, 0, 0)))]
        merge_in = None if outs is None else (outs[0], outs[1], OD_COL["g"], ret_norm)
        outs = _scan_call(_ret_kernel, zc, zl, RET_HEADS, RET_DK, RET_DV, OD_COL["q"], OD_COL["k"],
                          OD_COL["v"], [], extra_specs, merge_in, reverse=reverse,
                          name="ret_bwd" if reverse else "ret_fwd_merge")
    return outs


ATT_QKV = (ATT_HEADS + 2 * ATT_KV_HEADS) * ATT_HD


def _att_prep_kernel(z_ref, qn_ref, kn_ref, *rest, latent):
    if latent:
        cos_ref, sin_ref, o_ref = rest
        cs, sn = cos_ref[...], sin_ref[...]
        lane = lax.broadcasted_iota(jnp.int32, cs.shape, 1)
        low = (lane & (ATT_HD // 2 - 1)) < (ATT_HD // 4)
    else:
        (o_ref,) = rest
    for h in range(ATT_HEADS + ATT_KV_HEADS):
        x = z_ref[0, :, h * ATT_HD:(h + 1) * ATT_HD]
        gain = qn_ref[...] if h < ATT_HEADS else kn_ref[...]
        y = x * lax.rsqrt(jnp.mean(x * x, axis=-1, keepdims=True) + EPS) * gain
        if latent:
            partner = jnp.where(low, pltpu.roll(y, ATT_HD - ATT_HD // 4, 1), pltpu.roll(y, ATT_HD // 4, 1))
            y = y * cs + partner * sn
        if h < ATT_HEADS:
            y = y * (ATT_HD ** -0.5)
        o_ref[0, :, h * ATT_HD:(h + 1) * ATT_HD] = y.astype(o_ref.dtype)
    v0 = (ATT_HEADS + ATT_KV_HEADS) * ATT_HD
    o_ref[0, :, v0:] = z_ref[0, :, v0:].astype(o_ref.dtype)


def att_prep(z, att_q_norm, att_k_norm, tables, *, tm):
    b, l, _ = z.shape
    latent = tables is not None
    row = lambda bi, i: (bi, i, 0)
    const = lambda bi, i: (0, 0)
    ins = [z, att_q_norm.reshape(1, ATT_HD), att_k_norm.reshape(1, ATT_HD)]
    specs = [pl.BlockSpec((1, tm, ATT_QKV), lambda bi, i: (bi, i, OD_COL["aq"] // ATT_QKV)),
             pl.BlockSpec((1, ATT_HD), const), pl.BlockSpec((1, ATT_HD), const)]
    if latent:
        ins += list(tables)
        specs += [pl.BlockSpec((tm, ATT_HD), lambda bi, i: (i, 0))] * 2
    return pl.pallas_call(
        functools.partial(_att_prep_kernel, latent=latent),
        grid=(b, l // tm),
        in_specs=specs,
        out_specs=pl.BlockSpec((1, tm, ATT_QKV), row),
        out_shape=jax.ShapeDtypeStruct((b, l, ATT_QKV), BF16),
        compiler_params=_cparams(("parallel", "parallel")),
        name="att_prep",
    )(*ins)


def _axial_tables(l):
    quarter = ATT_HD // 4
    inv = ROPE_BASE ** (-jnp.arange(quarter, dtype=F32) / quarter)
    t = jnp.arange(l)
    a_row = (t // GRID_W).astype(F32)[:, None] * inv[None, :]
    a_col = (t % GRID_W).astype(F32)[:, None] * inv[None, :]
    cos = jnp.concatenate([jnp.cos(a_row)] * 2 + [jnp.cos(a_col)] * 2, axis=-1)
    sin = jnp.concatenate([-jnp.sin(a_row), jnp.sin(a_row), -jnp.sin(a_col), jnp.sin(a_col)], axis=-1)
    return cos, sin


def _att_kernel(sink_ref, q_ref, kp, kc, kn, vp, vc, vn, ck, cv, o_ref, *, seq):
    n = pl.program_id(1)
    w = ATT_BLOCK
    gw = ATT_GROUP * w
    r = lax.broadcasted_iota(jnp.int32, (gw, 3 * w), 0) & (w - 1)
    c = lax.broadcasted_iota(jnp.int32, (gw, 3 * w), 1) - w
    near = (r - c + ATT_WINDOW).astype(jnp.uint32) <= 2 * ATT_WINDOW
    inside = (n * w + c).astype(jnp.uint32) < seq
    outs = []
    for h in range(ATT_KV_HEADS):
        hd = slice(h * ATT_HD, (h + 1) * ATT_HD)
        q = jnp.concatenate([q_ref[0, :, (h * ATT_GROUP + g) * ATT_HD:(h * ATT_GROUP + g + 1) * ATT_HD]
                             for g in range(ATT_GROUP)], axis=0)
        kw = jnp.concatenate([kp[0, :, hd], kc[0, :, hd], kn[0, :, hd]], axis=0)
        vw = jnp.concatenate([vp[0, :, hd], vc[0, :, hd], vn[0, :, hd]], axis=0)
        s_w = lax.dot_general(q, kw, _NT, preferred_element_type=F32)
        s_c = lax.dot_general(q, ck[0, :, hd], _NT, preferred_element_type=F32)
        s_w = jnp.where(near, jnp.where(inside, s_w, -jnp.inf), -jnp.inf)
        sink = jnp.concatenate([jnp.full((w, 1), sink_ref[h * ATT_GROUP + g], F32) for g in range(ATT_GROUP)],
                               axis=0)
        m = jnp.maximum(jnp.maximum(jnp.max(s_w, axis=-1, keepdims=True),
                                    jnp.max(s_c, axis=-1, keepdims=True)), sink)
        p_w = jnp.exp(s_w - m)
        p_c = jnp.exp(s_c - m)
        denom = (jnp.sum(p_w, axis=-1, keepdims=True) + jnp.sum(p_c, axis=-1, keepdims=True)
                 + jnp.exp(sink - m))
        o = (jnp.dot(p_w.astype(BF16), vw, preferred_element_type=F32)
             + jnp.dot(p_c.astype(BF16), cv[0, :, hd], preferred_element_type=F32)) / denom
        outs += [o[g * w:(g + 1) * w] for g in range(ATT_GROUP)]
    o_ref[0] = jnp.concatenate(outs, axis=1).astype(o_ref.dtype)


def window_attention(qkv_l, qkv_c, att_sink):
    b, l, _ = qkv_l.shape
    lc = qkv_c.shape[1]
    w = ATT_BLOCK
    nb = l // w
    qw = ATT_HEADS * ATT_HD
    kvw = ATT_KV_HEADS * ATT_HD
    kcol = qw // kvw
    vcol = kcol + 1

    def kv_spec(col, shift):
        return pl.BlockSpec((1, w, kvw), lambda bi, n, s: (bi, jnp.clip(n + shift, 0, nb - 1), col))

    grid_spec = pltpu.PrefetchScalarGridSpec(
        num_scalar_prefetch=1,
        grid=(b, nb),
        in_specs=[pl.BlockSpec((1, w, qw), lambda bi, n, s: (bi, n, 0))]
        + [kv_spec(kcol, d) for d in (-1, 0, 1)] + [kv_spec(vcol, d) for d in (-1, 0, 1)]
        + [pl.BlockSpec((1, lc, kvw), lambda bi, n, s: (bi, 0, kcol)),
           pl.BlockSpec((1, lc, kvw), lambda bi, n, s: (bi, 0, vcol))],
        out_specs=pl.BlockSpec((1, w, qw), lambda bi, n, s: (bi, n, 0)),
    )
    return pl.pallas_call(
        functools.partial(_att_kernel, seq=l),
        grid_spec=grid_spec,
        out_shape=jax.ShapeDtypeStruct((b, l, qw), BF16),
        compiler_params=_cparams(("parallel", "arbitrary")),
        name="window_attention",
    )(att_sink.astype(F32), qkv_l, qkv_l, qkv_l, qkv_l, qkv_l, qkv_l, qkv_l, qkv_c, qkv_c)


PLAN_W = 512
PLAN_ROWS = dict(tile_expert=0, tile_src=1, tile_valid=2, fill_start=3)


def _plan_kernel(e_ref, pos_ref, meta_ref, rank_ref, *, n_rows, p):
    w = PLAN_W
    expert = lax.broadcasted_iota(jnp.int32, (N_EXPERTS, w), 0).astype(F32)
    upper = (lax.broadcasted_iota(jnp.int32, (w, w), 0) <= lax.broadcasted_iota(jnp.int32, (w, w), 1))
    upper = jnp.where(upper, 1.0, 0.0).astype(BF16)

    def count(i, run):
        onehot = jnp.where(expert == e_ref[pl.ds(i, 1), :], 1.0, 0.0)
        seen = jnp.dot(onehot.astype(BF16), upper, preferred_element_type=F32)
        rank_ref[pl.ds(i, 1), :] = jnp.sum(onehot * (run + seen - 1.0), axis=0, keepdims=True)
        return run + seen[:, w - 1:w]

    counts = lax.fori_loop(0, n_rows, count, jnp.zeros((N_EXPERTS, 1), F32)).astype(jnp.int32)
    shift = MOE_TM.bit_length() - 1
    padded = ((counts + (MOE_TM - 1)) >> shift) << shift
    sub = lax.broadcasted_iota(jnp.int32, (N_EXPERTS, N_EXPERTS), 0)
    lan = lax.broadcasted_iota(jnp.int32, (N_EXPERTS, N_EXPERTS), 1)
    on_lanes = jnp.sum(jnp.where(sub == lan, padded, 0), axis=0, keepdims=True)
    ends = jnp.sum(jnp.where(lan <= sub, on_lanes, 0), axis=1, keepdims=True)
    offs = ends - padded

    def place(i, carry):
        onehot = expert == e_ref[pl.ds(i, 1), :]
        base = jnp.sum(jnp.where(onehot, offs, 0), axis=0, keepdims=True)
        pos_ref[pl.ds(i, 1), :] = base + rank_ref[pl.ds(i, 1), :].astype(jnp.int32)
        return carry

    lax.fori_loop(0, n_rows, place, 0)

    lane = lax.broadcasted_iota(jnp.int32, (1, LANE), 1)
    total = ends[N_EXPERTS - 1:N_EXPERTS, :]
    used = total >> shift
    tile_src = jnp.minimum(lane, used - 1)
    tile_expert = jnp.minimum(jnp.sum(jnp.where(ends <= tile_src * MOE_TM, 1, 0), axis=0, keepdims=True),
                              N_EXPERTS - 1)
    tile_valid = jnp.where(lane < used, 1, 0)
    pad_start = ((offs + counts) >> 3) << 3
    lane16 = lax.broadcasted_iota(jnp.int32, (N_EXPERTS, LANE), 1)
    sub16 = lax.broadcasted_iota(jnp.int32, (N_EXPERTS, LANE), 0)
    pad_lanes = jnp.sum(jnp.where(sub16 == lane16, pad_start, 0), axis=0, keepdims=True)
    fill = jnp.where(lane < N_EXPERTS, pad_lanes, total + ZERO_ROWS * (lane - N_EXPERTS))
    fill = jnp.minimum(fill, p - ZERO_ROWS)
    rows = [tile_expert, tile_src, tile_valid, fill]
    meta_ref[...] = jnp.concatenate(rows + [jnp.zeros((SUBLANE - len(rows), LANE), jnp.int32)], axis=0)


def dispatch_plan(expert_ids, p):
    k, n = expert_ids.shape
    r = k * n // PLAN_W
    assert r * PLAN_W == k * n and p // MOE_TM <= LANE
    pos, meta = pl.pallas_call(
        functools.partial(_plan_kernel, n_rows=r, p=p),
        out_shape=[jax.ShapeDtypeStruct((r, PLAN_W), jnp.int32), jax.ShapeDtypeStruct((SUBLANE, LANE), jnp.int32)],
        scratch_shapes=[pltpu.VMEM((r, PLAN_W), F32)],
        compiler_params=pltpu.CompilerParams(vmem_limit_bytes=VMEM_LIMIT),
        name="dispatch_plan",
    )(expert_ids.reshape(r, PLAN_W))
    return pos.reshape(k, n), meta


def moe_experts(hs, route, w1, w3, w2, layer):
    n = sum(h.shape[0] for h in hs)
    n_tiles = (TOP_K * n + MOE_TM - 1) // MOE_TM + N_EXPERTS
    p = n_tiles * MOE_TM
    pos, meta = dispatch_plan(route[0:TOP_K], p)
    w_tok = route[TOP_K:2 * TOP_K]
    n_fill = N_EXPERTS + -(-(p - TOP_K * n) // ZERO_ROWS)
    x_sorted = row_scatter(hs, pos, p, meta[PLAN_ROWS["fill_start"], :n_fill], step=MOVE_ROWS)
    tiles = lambda name: meta[PLAN_ROWS[name], :n_tiles]
    y_sorted = moe_grouped(x_sorted, w1, w3, w2, layer, tiles("tile_expert"), tiles("tile_src"),
                           tiles("tile_valid"))
    return y_sorted, pos, w_tok


FFT_Q = 128
HY_COLS = (HY_ORDER + 1) * HY_CH


def _hy_conv_kernel(u_ref, prev_ref, next_ref, w_ref, b_ref, z_ref, *g_refs):
    i = pl.program_id(1)
    u = u_ref[0]
    t = u.shape[0]
    row = lax.broadcasted_iota(jnp.int32, (t, 1), 0)
    has_prev = (i > 0).astype(F32)
    has_next = (i < pl.num_programs(1) - 1).astype(F32)
    up = jnp.where(row == 0, prev_ref[0, SUBLANE - 1:SUBLANE, :] * has_prev, pltpu.roll(u, 1, 0))
    dn = jnp.where(row == t - 1, next_ref[0, 0:1, :] * has_next, pltpu.roll(u, t - 1, 0))
    y = up * w_ref[0:1, :] + u * w_ref[1:2, :] + dn * w_ref[2:3, :] + b_ref[...]
    for n, o_ref in enumerate((z_ref,) + g_refs):
        o_ref[0] = y[:, n * HY_CH:(n + 1) * HY_CH]


def hy_short_conv(z, w, bias, *, tm):
    b, l, _ = z.shape
    ucol = EV_COL["u"] // HY_COLS
    nh = l // SUBLANE
    hb = tm // SUBLANE
    row = lambda bi, i: (bi, i, 0)
    return pl.pallas_call(
        _hy_conv_kernel,
        grid=(b, l // tm),
        in_specs=[
            pl.BlockSpec((1, tm, HY_COLS), lambda bi, i: (bi, i, ucol)),
            pl.BlockSpec((1, SUBLANE, HY_COLS), lambda bi, i: (bi, jnp.maximum(i * hb - 1, 0), ucol)),
            pl.BlockSpec((1, SUBLANE, HY_COLS), lambda bi, i: (bi, jnp.minimum((i + 1) * hb, nh - 1), ucol)),
            pl.BlockSpec((3, HY_COLS), lambda bi, i: (0, 0)),
            pl.BlockSpec((1, HY_COLS), lambda bi, i: (0, 0)),
        ],
        out_specs=[pl.BlockSpec((1, tm, HY_CH), row)] * (HY_ORDER + 1),
        out_shape=[jax.ShapeDtypeStruct((b, l, HY_CH), F32)] * (HY_ORDER + 1),
        compiler_params=_cparams(("parallel", "parallel")),
        name="hy_short_conv",
    )(z, z, z, w, bias.reshape(1, HY_COLS))


def _hy_filter_kernel(feat_ref, w1_ref, b1_ref, w2_ref, b2_ref, w3_ref, delta_ref, kern_ref, den_ref):
    feat = feat_ref[...]
    h = jnp.sin(jnp.dot(feat, w1_ref[...], preferred_element_type=F32, precision=HIGHEST) + b1_ref[...])
    h = jnp.sin(jnp.dot(h, w2_ref[...], preferred_element_type=F32, precision=HIGHEST) + b2_ref[...])
    f = jnp.dot(h.astype(BF16), w3_ref[...].astype(BF16), preferred_element_type=F32)
    t = feat[:, 0:1]
    live = feat[:, LANE - 1:LANE]
    kern = f * (jnp.exp(-t * delta_ref[...]) + HY_SHIFT) * live
    kern_ref[...] = kern

    @pl.when((pl.program_id(0) == 0) & (pl.program_id(1) == 0))
    def _():
        den_ref[...] = jnp.zeros_like(den_ref)

    den_ref[...] += jnp.sum(jnp.abs(kern), axis=0, keepdims=True)


def hy_filters(l, w1, b1, w2, b2, w3, *, tr):
    t = jnp.linspace(0.0, 1.0, l, dtype=F32)[:, None]
    w = (2.0 * math.pi / l) * jnp.arange(l, dtype=F32)[:, None]
    bands = jnp.linspace(1e-4, HY_BANDS - 1.0, HY_BANDS, dtype=F32)[None, :]
    feats = jnp.concatenate([t, jnp.cos(bands * w), -jnp.sin(bands * w)], axis=-1)
    feats = jnp.pad(feats, ((0, 0), (0, LANE - 1 - HY_EMB)))
    feats = jnp.concatenate([feats, jnp.ones((l, 1), F32)], axis=-1)
    table = jnp.concatenate([feats, jnp.zeros((1, LANE), F32), feats[:0:-1]], axis=0)
    w1p = jnp.pad(w1.astype(F32), ((0, LANE - HY_EMB), (0, 0)))
    w3s = w3.astype(F32).reshape(HY_FFN, HY_ORDER, 2, HY_CH).transpose(0, 2, 1, 3).reshape(HY_FFN, -1)
    deltas = jnp.abs(jnp.linspace(HY_MIN_DECAY, HY_MAX_DECAY, HY_CH, dtype=F32))
    deltas = jnp.tile(deltas, HY_ORDER).reshape(1, -1)
    oc = HY_ORDER * HY_CH
    nb = l // tr
    const = lambda s, i: (0, 0)
    return pl.pallas_call(
        _hy_filter_kernel,
        grid=(2, nb),
        in_specs=[
            pl.BlockSpec((tr, LANE), lambda s, i: (s * nb + i, 0)),
            pl.BlockSpec((LANE, HY_FFN), const), pl.BlockSpec((1, HY_FFN), const),
            pl.BlockSpec((HY_FFN, HY_FFN), const), pl.BlockSpec((1, HY_FFN), const),
            pl.BlockSpec((HY_FFN, oc), lambda s, i: (0, s)),
            pl.BlockSpec((1, oc), const),
        ],
        out_specs=[pl.BlockSpec((tr, oc), lambda s, i: (s * nb + i, 0)), pl.BlockSpec((1, oc), const)],
        out_shape=[jax.ShapeDtypeStruct((2 * l, oc), F32), jax.ShapeDtypeStruct((1, oc), F32)],
        compiler_params=_cparams(("arbitrary", "arbitrary")),
        name="hy_filters",
    )(table, w1p, b1.reshape(1, -1).astype(F32), w2.astype(F32), b2.reshape(1, -1).astype(F32), w3s, deltas)


def _cis(num, den):
    ang = (-2.0 * math.pi / den) * (num % den).astype(F32)
    return jnp.cos(ang), jnp.sin(ang)


def _block_complex(re, im):
    return jnp.concatenate([jnp.concatenate([re, -im], axis=-1), jnp.concatenate([im, re], axis=-1)], axis=-2)


def _fft_tables(p):
    q = FFT_Q
    n = p * q
    i_p = jnp.arange(p, dtype=jnp.int32)
    i_q = jnp.arange(q, dtype=jnp.int32)
    fr, fi = _cis(i_p[:, None] * i_p[None, :], p)
    f_filt = jnp.concatenate([fr, fi], axis=0)
    f_in = _block_complex(fr[:, :p // 2], fi[:, :p // 2])
    f_out = _block_complex(fr[:p // 2, :], -fi[:p // 2, :]) * (1.0 / n)
    num = i_q[None, None, :] * i_p[:, None, None] + p * i_q[None, :, None] * i_q[None, None, :]
    gr, gi = _cis(num, n)
    g = _block_complex(gr, gi)
    h = _block_complex(jnp.swapaxes(gr, 1, 2), -jnp.swapaxes(gi, 1, 2))
    return tuple(a.astype(BF16) for a in (f_filt, f_in, g, h, f_out))


FFT_TC = 512


def _fft_in_kernel(f_ref, x_ref, o_ref, xs_ref):
    for s in range(x_ref.shape[1]):
        xs_ref[...] = x_ref[:, s, :]
        o_ref[:, s, :] = jnp.dot(f_ref[...], xs_ref[...].astype(BF16), preferred_element_type=F32)


def fft_in(f, x):
    k, q, c = x.shape
    tc = FFT_TC
    return pl.pallas_call(
        _fft_in_kernel,
        grid=(q // SUBLANE, c // tc),
        in_specs=[pl.BlockSpec(f.shape, lambda j, ci: (0, 0)),
                  pl.BlockSpec((k, SUBLANE, tc), lambda j, ci: (0, j, ci))],
        out_specs=pl.BlockSpec((f.shape[0], SUBLANE, tc), lambda j, ci: (0, j, ci)),
        out_shape=jax.ShapeDtypeStruct((f.shape[0], q, c), F32),
        scratch_shapes=[pltpu.VMEM((k, tc), F32)],
        compiler_params=_cparams(("parallel", "parallel")),
        name="fft_in",
    )(f, x)


def _cmul_rows(s, k):
    half = s.shape[0] // 2
    sr, si, kr, ki = s[:half], s[half:], k[:half], k[half:]
    return jnp.concatenate([sr * kr - si * ki, sr * ki + si * kr], axis=0)


def _fft_mid_kernel(g_ref, h_ref, a_ref, af_ref, inv_ref, o_ref):
    q2 = g_ref.shape[1]
    c = a_ref.shape[-1]
    a = a_ref[...].reshape(q2, c).astype(BF16)
    af = af_ref[...].reshape(q2, c).astype(BF16)
    y = jnp.dot(g_ref[0], a, preferred_element_type=F32)
    kf = jnp.dot(g_ref[0], af, preferred_element_type=F32) * inv_ref[...]
    z = _cmul_rows(y, kf).astype(BF16)
    o_ref[...] = jnp.dot(h_ref[0], z, preferred_element_type=F32).reshape(o_ref.shape)


def fft_mid(g, h, a, af, den, order):
    p = g.shape[0]
    c = HY_CH
    mat = pl.BlockSpec((1, 2 * FFT_Q, 2 * FFT_Q), lambda k: (k, 0, 0))
    sig = pl.BlockSpec((2, 1, FFT_Q, c), lambda k: (0, k, 0, 0))
    return pl.pallas_call(
        _fft_mid_kernel,
        grid=(p,),
        in_specs=[mat, mat, sig,
                  pl.BlockSpec((2, 1, FFT_Q, c), lambda k: (0, k, 0, order)),
                  pl.BlockSpec((1, c), lambda k: (0, order))],
        out_specs=sig,
        out_shape=jax.ShapeDtypeStruct((2, p, FFT_Q, c), F32),
        compiler_params=_cparams(("parallel",)),
        name="fft_mid",
    )(g, h, a.reshape(2, p, FFT_Q, c), af.reshape(2, p, FFT_Q, af.shape[-1]), 1.0 / den
      ).reshape(2 * p, FFT_Q, c)


def _fft_out_kernel(f_ref, c_ref, z_ref, gate_ref, bias_ref, o_ref, cs_ref, zs_ref, gs_ref):
    for s in range(c_ref.shape[1]):
        cs_ref[...] = c_ref[:, s, :]
        zs_ref[...] = z_ref[:, s, :]
        gs_ref[...] = gate_ref[:, s, :]
        y = jnp.dot(f_ref[...], cs_ref[...].astype(BF16), preferred_element_type=F32)
        o_ref[:, s, :] = gs_ref[...] * (y + zs_ref[...] * bias_ref[...])


def fft_out(f, cm, z, gate, bias):
    p2, q, c = cm.shape
    tc = FFT_TC
    rows = f.shape[0]
    half = pl.BlockSpec((rows, SUBLANE, tc), lambda j, ci: (0, j, ci))
    return pl.pallas_call(
        _fft_out_kernel,
        grid=(q // SUBLANE, c // tc),
        in_specs=[pl.BlockSpec(f.shape, lambda j, ci: (0, 0)),
                  pl.BlockSpec((p2, SUBLANE, tc), lambda j, ci: (0, j, ci)),
                  half, half,
                  pl.BlockSpec((1, tc), lambda j, ci: (0, ci))],
        out_specs=half,
        out_shape=jax.ShapeDtypeStruct(z.shape, F32),
        scratch_shapes=[pltpu.VMEM((p2, tc), F32), pltpu.VMEM((rows, tc), F32), pltpu.VMEM((rows, tc), F32)],
        compiler_params=_cparams(("parallel", "parallel")),
        name="fft_out",
    )(f, cm, z, gate, bias)


def _hy_ctx_kernel(ff_ref, fin_ref, fout_ref, kern_ref, inv_ref, z_ref, g1_ref, g2_ref, bias_ref, o_ref):
    b, l, c = z_ref.shape
    z = z_ref[...].reshape(b * l, c)
    gates = (g1_ref, g2_ref)
    for n in range(HY_ORDER):
        cols = slice(n * c, (n + 1) * c)
        kf = (jnp.dot(ff_ref[...], kern_ref[:, cols].astype(BF16), preferred_element_type=F32)
              * inv_ref[:, cols])
        s = jnp.dot(fin_ref[...], z.astype(BF16), preferred_element_type=F32)
        y = jnp.dot(fout_ref[...], _cmul_rows(s, kf).astype(BF16), preferred_element_type=F32)
        z = gates[n][...].reshape(b * l, c) * (y + z * bias_ref[n:n + 1, :])
    o_ref[...] = z.reshape(b, l, c).astype(o_ref.dtype)


def hy_ctx_conv(kern, den, z0, g1, g2, bias):
    b, l, c = z0.shape
    assert b == 2
    n = 2 * l
    i_n = jnp.arange(n, dtype=jnp.int32)
    fr, fi = _cis(i_n[:, None] * i_n[None, :], n)
    ff = jnp.concatenate([fr, fi], axis=0).astype(BF16)
    fin = _block_complex(fr[:, :l], fi[:, :l]).astype(BF16)
    fout = (_block_complex(fr[:l, :], -fi[:l, :]) * (1.0 / n)).astype(BF16)
    return pl.pallas_call(
        _hy_ctx_kernel,
        out_shape=jax.ShapeDtypeStruct((b, l, c), BF16),
        compiler_params=pltpu.CompilerParams(vmem_limit_bytes=VMEM_LIMIT),
        name="hy_ctx_conv",
    )(ff, fin, fout, kern, 1.0 / den, z0, g1, g2, bias)


def hyena_latent(z, conv_w, conv_b, w1, b1, w2, b2, w3, bias):
    b, l, _ = z.shape
    assert b == 2 and (2 * l) % (2 * FFT_Q) == 0
    p = 2 * l // FFT_Q
    z0, *gates = hy_short_conv(z, conv_w, conv_b, tm=256)
    kern, den = hy_filters(l, w1, b1, w2, b2, w3, tr=512)
    f_filt, f_in, g, h, f_out = _fft_tables(p)
    af = fft_in(f_filt, kern.reshape(p, FFT_Q, HY_ORDER * HY_CH))
    split = lambda t: t.reshape(p, FFT_Q, HY_CH)
    zz = split(z0)
    for n in range(HY_ORDER):
        cm = fft_mid(g, h, fft_in(f_in, zz), af, den, n)
        zz = fft_out(f_out, cm, zz, split(gates[n]), bias[n].astype(F32).reshape(1, HY_CH))
    return zz.reshape(b, l, HY_CH)


def hyena_ctx(z, conv_w, conv_b, w1, b1, w2, b2, w3, bias):
    b, l, _ = z.shape
    z0, g1, g2 = hy_short_conv(z, conv_w, conv_b, tm=l)
    kern, den = hy_filters(l, w1, b1, w2, b2, w3, tr=l)
    return hy_ctx_conv(kern, den, z0, g1, g2, bias.astype(F32))


def even_mixer_core(zc, zl, gla_w_gate, gla_b_gate, gla_norm, hy_conv_w, hy_conv_b,
                    hy_w1, hy_b1, hy_w2, hy_b2, hy_w3, hy_bias):
    gc, gl = gla_bidir(zc, zl, gla_w_gate, gla_b_gate, gla_norm)
    hy = (hy_conv_w, hy_conv_b, hy_w1, hy_b1, hy_w2, hy_b2, hy_w3, hy_bias)
    return (gc, hyena_ctx(zc, *hy)), (gl, hyena_latent(zl, *hy))


def odd_mixer_core(zc, zl, ret_norm, att_q_norm, att_k_norm, att_sink):
    _, rl = retention_bidir(zc, zl, ret_norm)
    qkv_l = att_prep(zl, att_q_norm, att_k_norm, _axial_tables(zl.shape[1]), tm=512)
    qkv_c = att_prep(zc, att_q_norm, att_k_norm, None, tm=zc.shape[1])
    return rl, window_attention(qkv_l, qkv_c, att_sink)


def kernel(x, c, ctx, c_ctx, ada_w, ada_b, norm_mix, norm_ffn, ev_w_in, ev_w_out, gla_w_gate, gla_b_gate, gla_norm, hy_conv_w, hy_conv_b, hy_w1, hy_b1, hy_w2, hy_b2, hy_w3, hy_bias, od_w_in, od_w_out, ret_norm, att_q_norm, att_k_norm, att_sink, router_w, router_b, moe_w1, moe_w3, moe_w2):
    assert DEPTH == 2
    bsz, seq, d = x.shape
    n_ctx = ctx.shape[1]
    c_pad = jnp.zeros((8, d), F32).at[:bsz].set(c).at[bsz].set(c_ctx)
    mods = ada_modulation(c_pad, ada_w, ada_b)
    slot = np.array([SUBLANE * (e % EXPERTS_PER_GROUP) + e // EXPERTS_PER_GROUP for e in range(N_EXPERTS)])
    rw_pad = jnp.zeros((d, LOGIT_PAD), F32).at[:, slot].set(router_w)
    rw_hi = rw_pad.astype(BF16)
    rw_pad = jnp.concatenate([rw_hi, (rw_pad - rw_hi.astype(F32)).astype(BF16)], axis=1)
    rb_pad = jnp.zeros((EXPERTS_PER_GROUP * SUBLANE, LANE), F32).at[slot].set(
        jnp.broadcast_to(router_b.astype(F32)[:, None], (N_EXPERTS, LANE)))

    xl, xc = x, ctx
    for layer in range(DEPTH):
        last = layer == DEPTH - 1
        ml = mods[layer, :bsz].reshape(bsz, 1, 6, d)
        mc = jnp.broadcast_to(mods[layer, bsz].reshape(1, 1, 6, d), (bsz, 1, 6, d))
        sh1, sc1, g1, sh2, sc2, g2 = [ml[:, :, i] for i in range(6)]
        csh1, csc1, cg1, csh2, csc2, cg2 = [mc[:, :, i] for i in range(6)]
        i = layer // 2
        if layer % 2 == 0:
            w_in, w_out = ev_w_in[i], ev_w_out[i]
            lr0 = sum(EVEN_SPLITS[:4])
            lr1 = lr0 + EVEN_SPLITS[4]
            w_in = jnp.concatenate([w_in[:, :lr0], w_in[:, lr1:], w_in[:, lr0:lr1],
                                    jnp.zeros((d, EVEN_COLS_PAD - w_in.shape[1]), F32)], axis=1)
            tn = EVEN_COLS_PAD // 7
        else:
            w_in, w_out = od_w_in[i], od_w_out[i]
            tn = w_in.shape[1] // 6
        w_in = cast_bf16(w_in, tr=256)
        zl = norm_mod_matmul(xl, norm_mix[layer], sc1, sh1, w_in, tm=1024, tn=tn, out_dtype=F32)
        zc = norm_mod_matmul(xc, norm_mix[layer], csc1, csh1, w_in, tm=n_ctx, tn=tn, out_dtype=F32)
        if layer % 2 == 0:
            mixc, mixl = even_mixer_core(zc, zl, gla_w_gate[i], gla_b_gate[i], gla_norm[i], hy_conv_w[i],
                                         hy_conv_b[i], hy_w1[i], hy_b1[i], hy_w2[i], hy_b2[i], hy_w3[i],
                                         hy_bias[i])
        else:
            mixl = odd_mixer_core(zc, zl, ret_norm[i], att_q_norm[i], att_k_norm[i], att_sink[i])
        lat0 = 0 if last else bsz * n_ctx
        flat = lambda rt: rt.transpose(1, 0, 2).reshape(ROUTE_ROWS, -1)
        hs, route = [], []
        if not last:
            xc, hc, rtc = outproj_residual_norm(*mixc, w_out, xc, cg1, norm_ffn[layer], csc2, csh2, rw_pad,
                                                rb_pad, tm=n_ctx)
            hs.append(hc)
            route.append(flat(rtc))
        xl, hl, rtl = outproj_residual_norm(*mixl, w_out, xl, g1, norm_ffn[layer], sc2, sh2, rw_pad, rb_pad,
                                            tm=512)
        hs.append(hl)
        route.append(flat(rtl))
        y_sorted, pos, w_tok = moe_experts(hs, jnp.concatenate(route, axis=1), moe_w1, moe_w3, moe_w2, layer)
        if not last:
            xc = moe_gather_combine(xc, cg2, y_sorted, pos, w_tok, 0, tm=n_ctx)
        xl = moe_gather_combine(xl, g2, y_sorted, pos, w_tok, lat0, tm=512)
    return xl
```

```python
import functools
import math

import jax
import jax.numpy as jnp
import numpy as np
from jax import lax
from jax.experimental import pallas as pl
from jax.experimental.pallas import tpu as pltpu

D_MODEL = 2048
BATCH = 2
SEQ = 8192
DEPTH = 2
GRID_W = 64
CTX_LEN = 256
MIX_WIDTH = D_MODEL
GROUP_WIDTH = MIX_WIDTH // 2
EPS = 1e-6
F32 = jnp.float32
BF16 = jnp.bfloat16

GLA_HEADS = 4
GLA_DK = GROUP_WIDTH // (2 * GLA_HEADS)
GLA_DV = GROUP_WIDTH // GLA_HEADS
GLA_LOWRANK = 16
GLA_GATE_NORMALIZER = 16.0
GLA_CHUNK = 64

HY_CH = GROUP_WIDTH
HY_ORDER = 2
HY_BANDS = 16
HY_EMB = 1 + 2 * HY_BANDS
HY_FFN = 64
HY_SHIFT = 0.05
HY_MIN_DECAY = math.log(1e-2) / 1.5
HY_MAX_DECAY = math.log(1e-2) / 0.3

RET_HEADS = 4
RET_DK = GROUP_WIDTH // (2 * RET_HEADS)
RET_DV = GROUP_WIDTH // RET_HEADS
RET_CHUNK = 64

ATT_HD = 128
ATT_HEADS = GROUP_WIDTH // ATT_HD
ATT_KV_HEADS = 2
ATT_GROUP = ATT_HEADS // ATT_KV_HEADS
ATT_WINDOW = 128
ATT_BLOCK = 128
ROPE_BASE = 10000.0

N_EXPERTS = 16
N_GROUPS = 4
EXPERTS_PER_GROUP = N_EXPERTS // N_GROUPS
TOP_K = 2
D_FF_EXPERT = D_MODEL // 2

EVEN_SPLITS = (GLA_HEADS * GLA_DK, GLA_HEADS * GLA_DK, GLA_HEADS * GLA_DV, GLA_HEADS * GLA_DV,
               2 * GLA_LOWRANK, (HY_ORDER + 1) * HY_CH)
ODD_SPLITS = (RET_HEADS * RET_DK, RET_HEADS * RET_DK, RET_HEADS * RET_DV, RET_HEADS * RET_DV,
              ATT_HEADS * ATT_HD, ATT_KV_HEADS * ATT_HD, ATT_KV_HEADS * ATT_HD)

LANE = 128
EVEN_COLS_PAD = -(-sum(EVEN_SPLITS) // (7 * LANE)) * (7 * LANE)
VMEM_LIMIT = 56 * 1024 * 1024
MOE_TM = 512
MOE_FF_SPLIT = 2
MOVE_ROWS = 1024
LOGIT_PAD = LANE


def _cparams(sem):
    return pltpu.CompilerParams(dimension_semantics=sem, vmem_limit_bytes=VMEM_LIMIT)


def _ada_kernel(c_ref, w_ref, b_ref, o_ref):
    c = c_ref[...]
    s = (c * jax.nn.sigmoid(c)).astype(BF16)
    o_ref[0] = jnp.dot(s, w_ref[0].astype(BF16), preferred_element_type=F32) + b_ref[0]


def ada_modulation(c_pad, ada_w, ada_b):
    depth, d, n = ada_w.shape
    tn = 1024
    return pl.pallas_call(
        _ada_kernel,
        grid=(depth, n // tn),
        in_specs=[
            pl.BlockSpec((8, d), lambda l, j: (0, 0)),
            pl.BlockSpec((1, d, tn), lambda l, j: (l, 0, j)),
            pl.BlockSpec((1, 1, tn), lambda l, j: (l, 0, j)),
        ],
        out_specs=pl.BlockSpec((1, 8, tn), lambda l, j: (l, 0, j)),
        out_shape=jax.ShapeDtypeStruct((depth, 8, n), F32),
        compiler_params=_cparams(("parallel", "arbitrary")),
        name="ada_modulation",
    )(c_pad, ada_w, ada_b.reshape(depth, 1, n))


def _norm_mod(x, gain, sc, sh):
    ms = jnp.mean(x * x, axis=-1, keepdims=True)
    return (x * lax.rsqrt(ms + EPS) * gain) * (1.0 + sc) + sh


def _cast_kernel(x_ref, o_ref):
    o_ref[...] = x_ref[...].astype(o_ref.dtype)


def cast_bf16(w, *, tr):
    r, c = w.shape
    return pl.pallas_call(
        _cast_kernel,
        grid=(r // tr,),
        in_specs=[pl.BlockSpec((tr, c), lambda i: (i, 0))],
        out_specs=pl.BlockSpec((tr, c), lambda i: (i, 0)),
        out_shape=jax.ShapeDtypeStruct((r, c), BF16),
        compiler_params=_cparams(("parallel",)),
        name="cast_bf16",
    )(w)


def _inproj_kernel(x_ref, gain_ref, sc_ref, sh_ref, w_ref, o_ref, h_ref):
    @pl.when(pl.program_id(2) == 0)
    def _():
        h_ref[...] = _norm_mod(x_ref[0], gain_ref[...], sc_ref[0], sh_ref[0]).astype(BF16)

    o_ref[0] = jnp.dot(h_ref[...], w_ref[...].astype(BF16), preferred_element_type=F32).astype(o_ref.dtype)


def norm_mod_matmul(x, gain, sc, sh, w, *, tm, tn, out_dtype):
    b, l, d = x.shape
    n = w.shape[1]
    return pl.pallas_call(
        _inproj_kernel,
        grid=(b, l // tm, n // tn),
        in_specs=[
            pl.BlockSpec((1, tm, d), lambda bi, i, j: (bi, i, 0)),
            pl.BlockSpec((1, d), lambda bi, i, j: (0, 0)),
            pl.BlockSpec((1, 1, d), lambda bi, i, j: (bi, 0, 0)),
            pl.BlockSpec((1, 1, d), lambda bi, i, j: (bi, 0, 0)),
            pl.BlockSpec((d, tn), lambda bi, i, j: (0, j)),
        ],
        out_specs=pl.BlockSpec((1, tm, tn), lambda bi, i, j: (bi, i, j)),
        out_shape=jax.ShapeDtypeStruct((b, l, n), out_dtype),
        scratch_shapes=[pltpu.VMEM((tm, d), BF16)],
        compiler_params=_cparams(("parallel", "parallel", "arbitrary")),
        name="norm_mod_matmul",
    )(x, gain.reshape(1, d), sc, sh, w)


SUBLANE = 8
ROUTE_ROWS = SUBLANE


def _route_block(lg, rb_ref):
    tm = lg.shape[0]
    t = lg.T
    sub = lax.broadcasted_iota(jnp.int32, (SUBLANE, tm), 0)
    valid = sub < N_GROUPS
    members = range(EXPERTS_PER_GROUP)
    s = [jnp.where(valid, t[SUBLANE * j:SUBLANE * (j + 1), :], -jnp.inf) for j in members]
    mx = jnp.max(jnp.maximum(jnp.maximum(s[0], s[1]), jnp.maximum(s[2], s[3])), axis=0, keepdims=True)
    e = [jnp.exp(sj - mx) for sj in s]
    den = jnp.sum((e[0] + e[1]) + (e[2] + e[3]), axis=0, keepdims=True)
    sc = [ej / den for ej in e]
    sel = [sc[j] + jnp.concatenate([rb_ref[SUBLANE * j:SUBLANE * (j + 1), :]] * (tm // LANE), axis=1)
           for j in members]
    hi01, lo01 = jnp.maximum(sel[0], sel[1]), jnp.minimum(sel[0], sel[1])
    hi23, lo23 = jnp.maximum(sel[2], sel[3]), jnp.minimum(sel[2], sel[3])
    top1 = jnp.maximum(hi01, hi23)
    top2 = jnp.maximum(jnp.minimum(hi01, hi23), jnp.maximum(lo01, lo23))
    gscore = jnp.where(valid, top1 + top2, -jnp.inf)
    gmax = jnp.max(gscore, axis=0, keepdims=True)
    gidx = jnp.min(jnp.where(gscore == gmax, sub, SUBLANE), axis=0, keepdims=True)
    chosen = sub == gidx
    v = [jnp.sum(jnp.where(chosen, sel[j], 0.0), axis=0, keepdims=True) for j in members]
    p = [jnp.sum(jnp.where(chosen, sc[j], 0.0), axis=0, keepdims=True) for j in members]
    one = lambda cond: jnp.where(cond, 1.0, 0.0)
    picks = []
    for want in (0.0, 1.0):
        idx = jnp.zeros_like(v[0])
        wt = jnp.zeros_like(v[0])
        for j in members:
            rank = sum(one(v[k] > v[j]) for k in members if k != j) + sum(one(v[k] == v[j]) for k in range(j))
            hit = rank == want
            idx = idx + jnp.where(hit, float(j), 0.0)
            wt = wt + jnp.where(hit, p[j], 0.0)
        picks.append((idx, wt))
    (j1, p1), (j2, p2) = picks
    base = (EXPERTS_PER_GROUP * gidx).astype(F32)
    tot = p1 + p2
    rows = [base + j1, base + j2, p1 / tot, p2 / tot]
    return jnp.concatenate(rows + [jnp.zeros((ROUTE_ROWS - len(rows), tm), F32)], axis=0)


def _outproj_kernel(m1_ref, m2_ref, w_ref, x_ref, g_ref, gain_ref, sc_ref, sh_ref, rw_ref, rb_ref,
                    xo_ref, h_ref, rt_ref):
    k1 = m1_ref.shape[-1]
    o = (jnp.dot(m1_ref[0].astype(BF16), w_ref[:k1, :].astype(BF16), preferred_element_type=F32)
         + jnp.dot(m2_ref[0].astype(BF16), w_ref[k1:, :].astype(BF16), preferred_element_type=F32))
    xn = x_ref[0] + g_ref[0] * o
    xo_ref[0] = xn
    h = _norm_mod(xn, gain_ref[...], sc_ref[0], sh_ref[0])
    h_ref[...] = h
    h_hi = h.astype(BF16)
    h_lo = (h - h_hi.astype(F32)).astype(BF16)
    both = jnp.dot(h_hi, rw_ref[...], preferred_element_type=F32)
    lg = (both[:, :LOGIT_PAD] + both[:, LOGIT_PAD:]
          + jnp.dot(h_lo, rw_ref[:, :LOGIT_PAD], preferred_element_type=F32))
    rt_ref[0] = _route_block(lg, rb_ref)


def outproj_residual_norm(m1, m2, w, x, g, gain, sc, sh, rw_pad, rb_pad, *, tm):
    b, l, d = x.shape
    k1, k2 = m1.shape[-1], m2.shape[-1]
    nb = l // tm
    row = lambda bi, i: (bi, i, 0)
    mod = lambda bi, i: (bi, 0, 0)
    const = lambda bi, i: (0, 0)
    return pl.pallas_call(
        _outproj_kernel,
        grid=(b, nb),
        in_specs=[
            pl.BlockSpec((1, tm, k1), row),
            pl.BlockSpec((1, tm, k2), row),
            pl.BlockSpec((k1 + k2, d), const, pipeline_mode=pl.Buffered(1)),
            pl.BlockSpec((1, tm, d), row),
            pl.BlockSpec((1, 1, d), mod),
            pl.BlockSpec((1, d), const),
            pl.BlockSpec((1, 1, d), mod),
            pl.BlockSpec((1, 1, d), mod),
            pl.BlockSpec((d, 2 * LOGIT_PAD), const),
            pl.BlockSpec(rb_pad.shape, const),
        ],
        out_specs=[
            pl.BlockSpec((1, tm, d), row),
            pl.BlockSpec((tm, d), lambda bi, i: (bi * nb + i, 0)),
            pl.BlockSpec((1, ROUTE_ROWS, tm), lambda bi, i: (bi, 0, i)),
        ],
        out_shape=[
            jax.ShapeDtypeStruct((b, l, d), F32),
            jax.ShapeDtypeStruct((b * l, d), F32),
            jax.ShapeDtypeStruct((b, ROUTE_ROWS, l), F32),
        ],
        compiler_params=_cparams(("parallel", "parallel")),
        name="outproj_residual_norm",
    )(m1, m2, w, x, g, gain.reshape(1, d), sc, sh, rw_pad, rb_pad)


def _moe_kernel(te_ref, tsrc_ref, tvalid_ref, x_ref, w1_ref, w3_ref, w2_ref, o_ref):
    i = pl.program_id(0)

    @pl.when(tvalid_ref[i] > 0)
    def _():
        x = x_ref[...].astype(BF16)
        ff = w1_ref.shape[-1]
        fc = ff // MOE_FF_SPLIT
        y = None
        for c in range(MOE_FF_SPLIT):
            cols = slice(c * fc, (c + 1) * fc)
            a = jnp.dot(x, w1_ref[0, 0, :, cols].astype(BF16), preferred_element_type=F32)
            b = jnp.dot(x, w3_ref[0, 0, :, cols].astype(BF16), preferred_element_type=F32)
            mid = (a * jax.nn.sigmoid(a) * b).astype(BF16)
            part = jnp.dot(mid, w2_ref[0, 0, cols, :].astype(BF16), preferred_element_type=F32)
            y = part if y is None else y + part
        o_ref[...] = y

    @pl.when(tvalid_ref[i] == 0)
    def _():
        o_ref[...] = jnp.zeros_like(o_ref)


def moe_grouped(x_sorted, w1, w3, w2, layer, tile_expert, tile_src, tile_valid):
    p, d = x_sorted.shape
    ff = w1.shape[-1]
    tm = MOE_TM
    wmap = lambda i, te, ts, tv: (layer, te[i], 0, 0)
    grid_spec = pltpu.PrefetchScalarGridSpec(
        num_scalar_prefetch=3,
        grid=(p // tm,),
        in_specs=[
            pl.BlockSpec((tm, d), lambda i, te, ts, tv: (ts[i], 0)),
            pl.BlockSpec((1, 1, d, ff), wmap, pipeline_mode=pl.Buffered(1)),
            pl.BlockSpec((1, 1, d, ff), wmap, pipeline_mode=pl.Buffered(1)),
            pl.BlockSpec((1, 1, ff, d), wmap, pipeline_mode=pl.Buffered(1)),
        ],
        out_specs=pl.BlockSpec((tm, d), lambda i, te, ts, tv: (i, 0)),
    )
    return pl.pallas_call(
        _moe_kernel,
        grid_spec=grid_spec,
        out_shape=jax.ShapeDtypeStruct((p, d), F32),
        compiler_params=_cparams(("arbitrary",)),
        name="moe_grouped",
    )(tile_expert, tile_src, tile_valid, x_sorted, w1, w3, w2)


ZERO_ROWS = MOE_TM + SUBLANE


def _row_scatter_kernel(dst_ref, fill_ref, *rest, starts):
    h_refs, (out_ref, sem, zbuf) = rest[:-3], rest[-3:]
    j = pl.program_id(0)

    @pl.when((j == 0) & (pl.program_id(1) == 0))
    def _():
        zbuf[...] = jnp.zeros_like(zbuf)
        for e in range(fill_ref.shape[0]):
            fill = pltpu.make_async_copy(
                zbuf, out_ref.at[pl.ds(pl.multiple_of(fill_ref[e], SUBLANE), ZERO_ROWS)], sem)
            fill.start()
            fill.wait()

    for s, h_ref in enumerate(h_refs):
        @pl.when((j >= starts[s]) & (j < starts[s + 1]))
        def _(h_ref=h_ref):
            def issue(r, carry):
                pltpu.make_async_copy(h_ref.at[pl.ds(r, 1)], out_ref.at[pl.ds(dst_ref[0, 0, r], 1)], sem).start()
                return carry

            lax.fori_loop(0, h_ref.shape[0], issue, 0, unroll=4)
            pltpu.make_async_copy(h_ref, h_ref, sem).wait()


def row_scatter(hs, dst, out_rows, fill_start, *, step):
    d = hs[0].shape[1]
    steps = [min(step, h.shape[0]) for h in hs]
    nbs = [h.shape[0] // st for h, st in zip(hs, steps)]
    assert all(nb * st == h.shape[0] for nb, st, h in zip(nbs, steps, hs))
    starts = [0]
    for nb in nbs:
        starts.append(starts[-1] + nb)
    nb = starts[-1]
    parts, r0 = [], 0
    for h, st, nbk in zip(hs, steps, nbs):
        seg = dst[:, r0:r0 + h.shape[0]].reshape(TOP_K, nbk, st).transpose(1, 0, 2)
        parts.append(jnp.pad(seg, ((0, 0), (0, 0), (0, step - st))))
        r0 += h.shape[0]
    dst = jnp.concatenate(parts, axis=0).reshape(nb * TOP_K, 1, step)
    seg_spec = lambda s: pl.BlockSpec((steps[s], d), lambda j, k: (jnp.clip(j - starts[s], 0, nbs[s] - 1), 0))
    return pl.pallas_call(
        functools.partial(_row_scatter_kernel, starts=tuple(starts)),
        grid=(nb, TOP_K),
        in_specs=[pl.BlockSpec((1, 1, step), lambda j, k: (j * TOP_K + k, 0, 0), memory_space=pltpu.SMEM),
                  pl.BlockSpec(memory_space=pltpu.SMEM)] + [seg_spec(s) for s in range(len(hs))],
        out_specs=pl.BlockSpec(memory_space=pl.ANY),
        out_shape=jax.ShapeDtypeStruct((out_rows, d), hs[0].dtype),
        scratch_shapes=[pltpu.SemaphoreType.DMA(()), pltpu.VMEM((ZERO_ROWS, d), hs[0].dtype)],
        compiler_params=_cparams(("arbitrary", "arbitrary")),
        name="row_scatter",
    )(dst, fill_start, *hs)


def _gather_combine_kernel(p0_ref, p1_ref, x_ref, g_ref, w0_ref, w1_ref, y_ref, o_ref, buf, sem):
    tm = x_ref.shape[1]

    def issue(r, carry):
        pltpu.make_async_copy(y_ref.at[pl.ds(p0_ref[0, 0, r], 1)], buf.at[0, pl.ds(r, 1)], sem).start()
        pltpu.make_async_copy(y_ref.at[pl.ds(p1_ref[0, 0, r], 1)], buf.at[1, pl.ds(r, 1)], sem).start()
        return carry

    lax.fori_loop(0, tm, issue, 0, unroll=2)
    pltpu.make_async_copy(buf, buf, sem).wait()
    y = w0_ref[...] * buf[0] + w1_ref[...] * buf[1]
    o_ref[0] = x_ref[0] + g_ref[0] * y


def moe_gather_combine(x, g, y_sorted, pos, w_tok, row0, *, tm):
    b, l, d = x.shape
    n = pos.shape[1]
    nb = l // tm
    assert row0 % tm == 0 and n % tm == 0
    tok = lambda bi, i: row0 // tm + bi * nb + i
    pos = pos.reshape(TOP_K * (n // tm), 1, tm)
    w_col = w_tok.reshape(TOP_K * n, 1)
    idx = lambda k: pl.BlockSpec((1, 1, tm), lambda bi, i: (k * (n // tm) + tok(bi, i), 0, 0),
                                 memory_space=pltpu.SMEM)
    wgt = lambda k: pl.BlockSpec((tm, 1), lambda bi, i: (k * (n // tm) + tok(bi, i), 0))
    return pl.pallas_call(
        _gather_combine_kernel,
        grid=(b, nb),
        in_specs=[idx(0), idx(1),
                  pl.BlockSpec((1, tm, d), lambda bi, i: (bi, i, 0)),
                  pl.BlockSpec((1, 1, d), lambda bi, i: (bi, 0, 0)),
                  wgt(0), wgt(1),
                  pl.BlockSpec(memory_space=pl.ANY)],
        out_specs=pl.BlockSpec((1, tm, d), lambda bi, i: (bi, i, 0)),
        out_shape=jax.ShapeDtypeStruct(x.shape, x.dtype),
        scratch_shapes=[pltpu.VMEM((TOP_K, tm, d), F32), pltpu.SemaphoreType.DMA(())],
        compiler_params=_cparams(("arbitrary", "arbitrary")),
        name="moe_gather_combine",
    )(pos, pos, x, g, w_col, w_col, y_sorted)


CHUNK = 64
SCAN_T = 256
_NT = (((1,), (1,)), ((), ()))
_TN = (((0,), (0,)), ((), ()))
HIGHEST = lax.Precision.HIGHEST


def _block_keep(t, reverse):
    r = lax.broadcasted_iota(jnp.int32, (t, t), 0)
    c = lax.broadcasted_iota(jnp.int32, (t, t), 1)
    shift = CHUNK.bit_length() - 1
    same = (r >> shift) == (c >> shift)
    return same & ((c >= r) if reverse else (c <= r))


def _chunk_order(n, reverse):
    return range(n - 1, -1, -1) if reverse else range(n)


def _block_scan(q, k, v, cum, st, keep, reverse):
    n = q.shape[0] // CHUNK
    ends = [cum[c * CHUNK:c * CHUNK + 1] if reverse else cum[(c + 1) * CHUNK - 1:(c + 1) * CHUNK]
            for c in range(n)]
    last = jnp.concatenate([jnp.broadcast_to(e, (CHUNK, e.shape[1])) for e in ends], axis=0)
    qe = (q * jnp.exp(cum)).astype(BF16)
    ke = (k * jnp.exp(-cum)).astype(BF16)
    kd = (k * jnp.exp(last - cum)).astype(BF16)
    vb = v.astype(BF16)
    att = lax.dot_general(qe, ke, _NT, preferred_element_type=F32)
    att = jnp.where(keep, att, 0.0).astype(BF16)
    intra = jnp.dot(att, vb, preferred_element_type=F32)
    outs = [None] * n
    for c in _chunk_order(n, reverse):
        sl = slice(c * CHUNK, (c + 1) * CHUNK)
        outs[c] = intra[sl] + lax.dot_general(qe[sl], st.astype(BF16), _NT, preferred_element_type=F32)
        st = jnp.exp(ends[c]) * st + lax.dot_general(vb[sl], kd[sl], _TN, preferred_element_type=F32)
    return jnp.concatenate(outs, axis=0), st


def _merge(o, ob, g, gain):
    tot = o + ob
    y = tot * lax.rsqrt(jnp.mean(tot * tot, axis=-1, keepdims=True) + EPS) * gain
    return y * (g * jax.nn.sigmoid(g))


def _scan_refs(rest, merge):
    if merge:
        return rest
    oc_ref, ol_ref, st_ref = rest
    return (None,) * 5 + (oc_ref, ol_ref, st_ref)


def _scan_heads(head_fn, dk, dv, v_ref, o_ref, ob_ref, g_ref, gn_ref, st_ref, keep, reverse):
    outs = []
    for h in range(st_ref.shape[0]):
        kcol = slice(h * dk, (h + 1) * dk)
        vcol = slice(h * dv, (h + 1) * dv)
        q, k, cum = head_fn(h, kcol)
        o, st = _block_scan(q, k, v_ref[0, :, vcol], cum, st_ref[h], keep, reverse)
        st_ref[h] = st
        outs.append(o if ob_ref is None
                    else _merge(o, ob_ref[0, :, vcol], g_ref[0, :, vcol].astype(F32), gn_ref[...]))
    o_ref[0] = jnp.concatenate(outs, axis=1).astype(o_ref.dtype)


def _gla_kernel(qc, kc, vc, lrc, ql, kl, vl, lrl, wg, bg, *rest, reverse, merge):
    obc, obl, gc, gl, gn, oc_ref, ol_ref, st_ref = _scan_refs(rest, merge)

    def run(q_ref, k_ref, v_ref, lr_ref, o_ref, ob_ref, g_ref):
        keep = _block_keep(q_ref.shape[1], reverse)
        pre = jnp.dot(lr_ref[0].astype(F32), wg[...], preferred_element_type=F32, precision=HIGHEST) + bg[...]
        la = (jnp.minimum(pre, 0.0) - jnp.log(1.0 + jnp.exp(-jnp.abs(pre)))) * (1.0 / GLA_GATE_NORMALIZER)
        cum = jnp.dot(keep.astype(F32), la, preferred_element_type=F32, precision=HIGHEST)

        def head(h, kcol):
            return (q_ref[0, :, kcol].astype(F32) * (GLA_DK ** -0.5), k_ref[0, :, kcol].astype(F32),
                    cum[:, kcol])

        _scan_heads(head, GLA_DK, GLA_DV, v_ref, o_ref, ob_ref, g_ref, gn, st_ref, keep, reverse)

    @pl.when(pl.program_id(2) == 0)
    def _():
        st_ref[...] = jnp.zeros_like(st_ref)
        run(qc, kc, vc, lrc, oc_ref, obc, gc)

    @pl.when(pl.program_id(2) > 0)
    def _():
        run(ql, kl, vl, lrl, ol_ref, obl, gl)


def _ret_kernel(qc, kc, vc, ql, kl, vl, cos_ref, sin_ref, cum_ref, *rest, reverse, merge):
    obc, obl, gc, gl, gn, oc_ref, ol_ref, st_ref = _scan_refs(rest, merge)

    def run(q_ref, k_ref, v_ref, o_ref, ob_ref, g_ref, latent):
        t = q_ref.shape[1]
        keep = _block_keep(t, reverse)

        def head(h, kcol):
            q = q_ref[0, :, kcol].astype(F32)
            k = k_ref[0, :, kcol].astype(F32)
            if latent:
                cs = cos_ref[...]
                sn = sin_ref[...]
                q = q * cs + pltpu.roll(q, RET_DK // 2, 1) * sn
                k = k * cs + pltpu.roll(k, RET_DK // 2, 1) * sn
            return q, k * (RET_DK ** -0.5), jnp.concatenate([cum_ref[h]] * (t // CHUNK), axis=0)

        _scan_heads(head, RET_DK, RET_DV, v_ref, o_ref, ob_ref, g_ref, gn, st_ref, keep, reverse)

    @pl.when(pl.program_id(2) == 0)
    def _():
        st_ref[...] = jnp.zeros_like(st_ref)
        run(qc, kc, vc, oc_ref, obc, gc, False)

    @pl.when(pl.program_id(2) > 0)
    def _():
        run(ql, kl, vl, ol_ref, obl, gl, True)


def _scan_call(body, zc, zl, heads, dk, dv, col_q, col_k, col_v, shared_cols, extra_specs, merge_in,
               *, reverse, name):
    b, lc, _ = zc.shape
    l = zl.shape[1]
    t = SCAN_T
    nblk = l // t
    if reverse:
        blk = lambda i: jnp.where(i == 0, nblk - 1, nblk - i)
    else:
        blk = lambda i: jnp.maximum(i - 1, 0)

    def ctx_spec(w, col, per_head=True):
        return pl.BlockSpec((1, lc, w), lambda bi, h, i: (bi, 0, col // w + (h if per_head else 0)))

    def lat_spec(w, col, per_head=True):
        return pl.BlockSpec((1, t, w), lambda bi, h, i: (bi, blk(i), col // w + (h if per_head else 0)))

    hk, hv = heads * dk, heads * dv
    ins = [zc, zc, zc] + [zc] * len(shared_cols) + [zl, zl, zl] + [zl] * len(shared_cols)
    specs = [ctx_spec(hk, col_q), ctx_spec(hk, col_k), ctx_spec(hv, col_v)]
    specs += [ctx_spec(w, col, False) for w, col in shared_cols]
    specs += [lat_spec(hk, col_q), lat_spec(hk, col_k), lat_spec(hv, col_v)]
    specs += [lat_spec(w, col, False) for w, col in shared_cols]
    for a, s in extra_specs:
        ins.append(a)
        specs.append(s)
    if merge_in is not None:
        obc, obl, col_g, gain = merge_in
        ins += [obc, obl, zc, zl, gain.reshape(1, dv)]
        specs += [ctx_spec(hv, 0), lat_spec(hv, 0), ctx_spec(hv, col_g), lat_spec(hv, col_g),
                  pl.BlockSpec((1, dv), lambda bi, h, i: (0, 0))]
        out_dtype = BF16
    else:
        out_dtype = F32
    return pl.pallas_call(
        functools.partial(body, reverse=reverse, merge=merge_in is not None),
        grid=(b, 1, 1 + nblk),
        in_specs=specs,
        out_specs=[ctx_spec(hv, 0), lat_spec(hv, 0)],
        out_shape=[jax.ShapeDtypeStruct((b, lc, heads * dv), out_dtype),
                   jax.ShapeDtypeStruct((b, l, heads * dv), out_dtype)],
        scratch_shapes=[pltpu.VMEM((heads, dv, dk), F32)],
        compiler_params=_cparams(("parallel", "parallel", "arbitrary")),
        name=name,
    )(*ins)


EV_COL = dict(q=0, k=512, v=1024, g=2048, u=3072, lr=6144)
OD_COL = dict(q=0, k=512, v=1024, g=2048, aq=3072, ak=4096, av=4352)


def gla_bidir(zc, zl, gla_w_gate, gla_b_gate, gla_norm):
    hk = GLA_HEADS * GLA_DK
    outs = None
    for d, reverse in ((1, True), (0, False)):
        wg = jnp.zeros((LANE, hk), F32).at[d * GLA_LOWRANK:(d + 1) * GLA_LOWRANK].set(gla_w_gate[d])
        shared_cols = [(LANE, EV_COL["lr"])]
        extra_specs = [(wg, pl.BlockSpec((LANE, hk), lambda bi, h, i: (0, 0))),
                       (gla_b_gate[d].reshape(1, hk), pl.BlockSpec((1, hk), lambda bi, h, i: (0, 0)))]
        merge_in = None if outs is None else (outs[0], outs[1], EV_COL["g"], gla_norm)
        outs = _scan_call(_gla_kernel, zc, zl, GLA_HEADS, GLA_DK, GLA_DV, EV_COL["q"], EV_COL["k"],
                          EV_COL["v"], shared_cols, extra_specs, merge_in, reverse=reverse,
                          name="gla_bwd" if reverse else "gla_fwd_merge")
    return outs


def retention_bidir(zc, zl, ret_norm):
    l = zl.shape[1]
    inv = ROPE_BASE ** (-jnp.linspace(0.0, 1.0, RET_DK // 2, dtype=F32))
    ang = jnp.arange(l, dtype=F32)[:, None] * inv[None, :]
    cos2 = jnp.concatenate([jnp.cos(ang), jnp.cos(ang)], axis=-1)
    sin2 = jnp.concatenate([-jnp.sin(ang), jnp.sin(ang)], axis=-1)
    log_g_f = jnp.log(1.0 - 2.0 ** (-5.0 - jnp.arange(RET_HEADS, dtype=F32)))
    idx = jnp.arange(CHUNK, dtype=F32)
    nblk = l // SCAN_T
    outs = None
    for reverse in (True, False):
        if reverse:
            cum = (CHUNK - idx)[None, :, None] * log_g_f[::-1][:, None, None]
            blk = lambda i: jnp.where(i == 0, nblk - 1, nblk - i)
        else:
            cum = (idx + 1.0)[None, :, None] * log_g_f[:, None, None]
            blk = lambda i: jnp.maximum(i - 1, 0)
        cum = jnp.broadcast_to(cum, (RET_HEADS, CHUNK, RET_DK))
        tab_spec = pl.BlockSpec((SCAN_T, RET_DK), lambda bi, h, i, blk=blk: (blk(i), 0))
        extra_specs = [(cos2, tab_spec), (sin2, tab_spec),
                       (cum, pl.BlockSpec((RET_HEADS, CHUNK, RET_DK), lambda bi, h, i: (0, 0, 0)))]
        merge_in = None if outs is None else (outs[0], outs[1], OD_COL["g"], ret_norm)
        outs = _scan_call(_ret_kernel, zc, zl, RET_HEADS, RET_DK, RET_DV, OD_COL["q"], OD_COL["k"],
                          OD_COL["v"], [], extra_specs, merge_in, reverse=reverse,
                          name="ret_bwd" if reverse else "ret_fwd_merge")
    return outs


ATT_QKV = (ATT_HEADS + 2 * ATT_KV_HEADS) * ATT_HD


def _att_prep_kernel(z_ref, qn_ref, kn_ref, *rest, latent):
    if latent:
        cos_ref, sin_ref, o_ref = rest
        cs, sn = cos_ref[...], sin_ref[...]
        lane = lax.broadcasted_iota(jnp.int32, cs.shape, 1)
        low = (lane & (ATT_HD // 2 - 1)) < (ATT_HD // 4)
    else:
        (o_ref,) = rest
    for h in range(ATT_HEADS + ATT_KV_HEADS):
        x = z_ref[0, :, h * ATT_HD:(h + 1) * ATT_HD].astype(F32)
        gain = qn_ref[...] if h < ATT_HEADS else kn_ref[...]
        y = x * lax.rsqrt(jnp.mean(x * x, axis=-1, keepdims=True) + EPS) * gain
        if latent:
            partner = jnp.where(low, pltpu.roll(y, ATT_HD - ATT_HD // 4, 1), pltpu.roll(y, ATT_HD // 4, 1))
            y = y * cs + partner * sn
        if h < ATT_HEADS:
            y = y * (ATT_HD ** -0.5)
        o_ref[0, :, h * ATT_HD:(h + 1) * ATT_HD] = y.astype(o_ref.dtype)
    v0 = (ATT_HEADS + ATT_KV_HEADS) * ATT_HD
    o_ref[0, :, v0:] = z_ref[0, :, v0:].astype(o_ref.dtype)


def att_prep(z, att_q_norm, att_k_norm, tables, *, tm):
    b, l, _ = z.shape
    latent = tables is not None
    row = lambda bi, i: (bi, i, 0)
    const = lambda bi, i: (0, 0)
    ins = [z, att_q_norm.reshape(1, ATT_HD), att_k_norm.reshape(1, ATT_HD)]
    specs = [pl.BlockSpec((1, tm, ATT_QKV), lambda bi, i: (bi, i, OD_COL["aq"] // ATT_QKV)),
             pl.BlockSpec((1, ATT_HD), const), pl.BlockSpec((1, ATT_HD), const)]
    if latent:
        ins += list(tables)
        specs += [pl.BlockSpec((tm, ATT_HD), lambda bi, i: (i, 0))] * 2
    return pl.pallas_call(
        functools.partial(_att_prep_kernel, latent=latent),
        grid=(b, l // tm),
        in_specs=specs,
        out_specs=pl.BlockSpec((1, tm, ATT_QKV), row),
        out_shape=jax.ShapeDtypeStruct((b, l, ATT_QKV), BF16),
        compiler_params=_cparams(("parallel", "parallel")),
        name="att_prep",
    )(*ins)


def _axial_tables(l):
    quarter = ATT_HD // 4
    inv = ROPE_BASE ** (-jnp.arange(quarter, dtype=F32) / quarter)
    t = jnp.arange(l)
    a_row = (t // GRID_W).astype(F32)[:, None] * inv[None, :]
    a_col = (t % GRID_W).astype(F32)[:, None] * inv[None, :]
    cos = jnp.concatenate([jnp.cos(a_row)] * 2 + [jnp.cos(a_col)] * 2, axis=-1)
    sin = jnp.concatenate([-jnp.sin(a_row), jnp.sin(a_row), -jnp.sin(a_col), jnp.sin(a_col)], axis=-1)
    return cos, sin


def _att_kernel(sink_ref, q_ref, kp, kc, kn, vp, vc, vn, ck, cv, o_ref, *, seq):
    n = pl.program_id(1)
    w = ATT_BLOCK
    gw = ATT_GROUP * w
    r = lax.broadcasted_iota(jnp.int32, (gw, 3 * w), 0) & (w - 1)
    c = lax.broadcasted_iota(jnp.int32, (gw, 3 * w), 1) - w
    near = (r - c + ATT_WINDOW).astype(jnp.uint32) <= 2 * ATT_WINDOW
    inside = (n * w + c).astype(jnp.uint32) < seq
    outs = []
    for h in range(ATT_KV_HEADS):
        hd = slice(h * ATT_HD, (h + 1) * ATT_HD)
        q = jnp.concatenate([q_ref[0, :, (h * ATT_GROUP + g) * ATT_HD:(h * ATT_GROUP + g + 1) * ATT_HD]
                             for g in range(ATT_GROUP)], axis=0)
        kw = jnp.concatenate([kp[0, :, hd], kc[0, :, hd], kn[0, :, hd]], axis=0)
        vw = jnp.concatenate([vp[0, :, hd], vc[0, :, hd], vn[0, :, hd]], axis=0)
        s_w = lax.dot_general(q, kw, _NT, preferred_element_type=F32)
        s_c = lax.dot_general(q, ck[0, :, hd], _NT, preferred_element_type=F32)
        s_w = jnp.where(near, jnp.where(inside, s_w, -jnp.inf), -jnp.inf)
        sink = jnp.concatenate([jnp.full((w, 1), sink_ref[h * ATT_GROUP + g], F32) for g in range(ATT_GROUP)],
                               axis=0)
        m = jnp.maximum(jnp.maximum(jnp.max(s_w, axis=-1, keepdims=True),
                                    jnp.max(s_c, axis=-1, keepdims=True)), sink)
        p_w = jnp.exp(s_w - m)
        p_c = jnp.exp(s_c - m)
        denom = (jnp.sum(p_w, axis=-1, keepdims=True) + jnp.sum(p_c, axis=-1, keepdims=True)
                 + jnp.exp(sink - m))
        o = (jnp.dot(p_w.astype(BF16), vw, preferred_element_type=F32)
             + jnp.dot(p_c.astype(BF16), cv[0, :, hd], preferred_element_type=F32)) / denom
        outs += [o[g * w:(g + 1) * w] for g in range(ATT_GROUP)]
    o_ref[0] = jnp.concatenate(outs, axis=1).astype(o_ref.dtype)


def window_attention(qkv_l, qkv_c, att_sink):
    b, l, _ = qkv_l.shape
    lc = qkv_c.shape[1]
    w = ATT_BLOCK
    nb = l // w
    qw = ATT_HEADS * ATT_HD
    kvw = ATT_KV_HEADS * ATT_HD
    kcol = qw // kvw
    vcol = kcol + 1

    def kv_spec(col, shift):
        return pl.BlockSpec((1, w, kvw), lambda bi, n, s: (bi, jnp.clip(n + shift, 0, nb - 1), col))

    grid_spec = pltpu.PrefetchScalarGridSpec(
        num_scalar_prefetch=1,
        grid=(b, nb),
        in_specs=[pl.BlockSpec((1, w, qw), lambda bi, n, s: (bi, n, 0))]
        + [kv_spec(kcol, d) for d in (-1, 0, 1)] + [kv_spec(vcol, d) for d in (-1, 0, 1)]
        + [pl.BlockSpec((1, lc, kvw), lambda bi, n, s: (bi, 0, kcol)),
           pl.BlockSpec((1, lc, kvw), lambda bi, n, s: (bi, 0, vcol))],
        out_specs=pl.BlockSpec((1, w, qw), lambda bi, n, s: (bi, n, 0)),
    )
    return pl.pallas_call(
        functools.partial(_att_kernel, seq=l),
        grid_spec=grid_spec,
        out_shape=jax.ShapeDtypeStruct((b, l, qw), BF16),
        compiler_params=_cparams(("parallel", "arbitrary")),
        name="window_attention",
    )(att_sink.astype(F32), qkv_l, qkv_l, qkv_l, qkv_l, qkv_l, qkv_l, qkv_l, qkv_c, qkv_c)


PLAN_W = 512
PLAN_ROWS = dict(tile_expert=0, tile_src=1, tile_valid=2, fill_start=3)


def _plan_kernel(e_ref, pos_ref, meta_ref, rank_ref, *, n_rows, p):
    w = PLAN_W
    expert = lax.broadcasted_iota(jnp.int32, (N_EXPERTS, w), 0).astype(F32)
    upper = (lax.broadcasted_iota(jnp.int32, (w, w), 0) <= lax.broadcasted_iota(jnp.int32, (w, w), 1))
    upper = jnp.where(upper, 1.0, 0.0).astype(BF16)

    def count(i, run):
        onehot = jnp.where(expert == e_ref[pl.ds(i, 1), :], 1.0, 0.0)
        seen = jnp.dot(onehot.astype(BF16), upper, preferred_element_type=F32)
        rank_ref[pl.ds(i, 1), :] = jnp.sum(onehot * (run + seen - 1.0), axis=0, keepdims=True)
        return run + seen[:, w - 1:w]

    counts = lax.fori_loop(0, n_rows, count, jnp.zeros((N_EXPERTS, 1), F32)).astype(jnp.int32)
    shift = MOE_TM.bit_length() - 1
    padded = ((counts + (MOE_TM - 1)) >> shift) << shift
    sub = lax.broadcasted_iota(jnp.int32, (N_EXPERTS, N_EXPERTS), 0)
    lan = lax.broadcasted_iota(jnp.int32, (N_EXPERTS, N_EXPERTS), 1)
    on_lanes = jnp.sum(jnp.where(sub == lan, padded, 0), axis=0, keepdims=True)
    ends = jnp.sum(jnp.where(lan <= sub, on_lanes, 0), axis=1, keepdims=True)
    offs = ends - padded

    def place(i, carry):
        onehot = expert == e_ref[pl.ds(i, 1), :]
        base = jnp.sum(jnp.where(onehot, offs, 0), axis=0, keepdims=True)
        pos_ref[pl.ds(i, 1), :] = base + rank_ref[pl.ds(i, 1), :].astype(jnp.int32)
        return carry

    lax.fori_loop(0, n_rows, place, 0)

    lane = lax.broadcasted_iota(jnp.int32, (1, LANE), 1)
    total = ends[N_EXPERTS - 1:N_EXPERTS, :]
    used = total >> shift
    tile_src = jnp.minimum(lane, used - 1)
    tile_expert = jnp.minimum(jnp.sum(jnp.where(ends <= tile_src * MOE_TM, 1, 0), axis=0, keepdims=True),
                              N_EXPERTS - 1)
    tile_valid = jnp.where(lane < used, 1, 0)
    pad_start = ((offs + counts) >> 3) << 3
    lane16 = lax.broadcasted_iota(jnp.int32, (N_EXPERTS, LANE), 1)
    sub16 = lax.broadcasted_iota(jnp.int32, (N_EXPERTS, LANE), 0)
    pad_lanes = jnp.sum(jnp.where(sub16 == lane16, pad_start, 0), axis=0, keepdims=True)
    fill = jnp.where(lane < N_EXPERTS, pad_lanes, total + ZERO_ROWS * (lane - N_EXPERTS))
    fill = jnp.minimum(fill, p - ZERO_ROWS)
    rows = [tile_expert, tile_src, tile_valid, fill]
    meta_ref[...] = jnp.concatenate(rows + [jnp.zeros((SUBLANE - len(rows), LANE), jnp.int32)], axis=0)


def dispatch_plan(expert_ids, p):
    k, n = expert_ids.shape
    r = k * n // PLAN_W
    assert r * PLAN_W == k * n and p // MOE_TM <= LANE
    pos, meta = pl.pallas_call(
        functools.partial(_plan_kernel, n_rows=r, p=p),
        out_shape=[jax.ShapeDtypeStruct((r, PLAN_W), jnp.int32), jax.ShapeDtypeStruct((SUBLANE, LANE), jnp.int32)],
        scratch_shapes=[pltpu.VMEM((r, PLAN_W), F32)],
        compiler_params=pltpu.CompilerParams(vmem_limit_bytes=VMEM_LIMIT),
        name="dispatch_plan",
    )(expert_ids.reshape(r, PLAN_W))
    return pos.reshape(k, n), meta


def moe_experts(hs, route, w1, w3, w2, layer):
    n = sum(h.shape[0] for h in hs)
    n_tiles = (TOP_K * n + MOE_TM - 1) // MOE_TM + N_EXPERTS
    p = n_tiles * MOE_TM
    pos, meta = dispatch_plan(route[0:TOP_K], p)
    w_tok = route[TOP_K:2 * TOP_K]
    n_fill = N_EXPERTS + -(-(p - TOP_K * n) // ZERO_ROWS)
    x_sorted = row_scatter(hs, pos, p, meta[PLAN_ROWS["fill_start"], :n_fill], step=MOVE_ROWS)
    tiles = lambda name: meta[PLAN_ROWS[name], :n_tiles]
    y_sorted = moe_grouped(x_sorted, w1, w3, w2, layer, tiles("tile_expert"), tiles("tile_src"),
                           tiles("tile_valid"))
    return y_sorted, pos, w_tok


FFT_Q = 128
HY_COLS = (HY_ORDER + 1) * HY_CH


def _hy_conv_kernel(u_ref, prev_ref, next_ref, w_ref, b_ref, z_ref, *g_refs):
    i = pl.program_id(1)
    u = u_ref[0].astype(F32)
    halo = prev_ref.shape[1]
    t = u.shape[0]
    row = lax.broadcasted_iota(jnp.int32, (t, 1), 0)
    has_prev = (i > 0).astype(F32)
    has_next = (i < pl.num_programs(1) - 1).astype(F32)
    up = jnp.where(row == 0, prev_ref[0, halo - 1:halo, :].astype(F32) * has_prev, pltpu.roll(u, 1, 0))
    dn = jnp.where(row == t - 1, next_ref[0, 0:1, :].astype(F32) * has_next, pltpu.roll(u, t - 1, 0))
    y = up * w_ref[0:1, :] + u * w_ref[1:2, :] + dn * w_ref[2:3, :] + b_ref[...]
    for n, o_ref in enumerate((z_ref,) + g_refs):
        o_ref[0] = y[:, n * HY_CH:(n + 1) * HY_CH]


def hy_short_conv(z, w, bias, *, tm):
    b, l, _ = z.shape
    ucol = EV_COL["u"] // HY_COLS
    halo = 2 * SUBLANE
    nh = l // halo
    hb = tm // halo
    row = lambda bi, i: (bi, i, 0)
    return pl.pallas_call(
        _hy_conv_kernel,
        grid=(b, l // tm),
        in_specs=[
            pl.BlockSpec((1, tm, HY_COLS), lambda bi, i: (bi, i, ucol)),
            pl.BlockSpec((1, halo, HY_COLS), lambda bi, i: (bi, jnp.maximum(i * hb - 1, 0), ucol)),
            pl.BlockSpec((1, halo, HY_COLS), lambda bi, i: (bi, jnp.minimum((i + 1) * hb, nh - 1), ucol)),
            pl.BlockSpec((3, HY_COLS), lambda bi, i: (0, 0)),
            pl.BlockSpec((1, HY_COLS), lambda bi, i: (0, 0)),
        ],
        out_specs=[pl.BlockSpec((1, tm, HY_CH), row)] * (HY_ORDER + 1),
        out_shape=[jax.ShapeDtypeStruct((b, l, HY_CH), F32)] * (HY_ORDER + 1),
        compiler_params=_cparams(("parallel", "parallel")),
        name="hy_short_conv",
    )(z, z, z, w, bias.reshape(1, HY_COLS))


def _hy_filter_kernel(feat_ref, w1_ref, b1_ref, w2_ref, b2_ref, w3_ref, delta_ref, kern_ref, den_ref):
    feat = feat_ref[...]
    h = jnp.sin(jnp.dot(feat, w1_ref[...], preferred_element_type=F32, precision=HIGHEST) + b1_ref[...])
    h = jnp.sin(jnp.dot(h, w2_ref[...], preferred_element_type=F32, precision=HIGHEST) + b2_ref[...])
    f = jnp.dot(h.astype(BF16), w3_ref[...].astype(BF16), preferred_element_type=F32)
    t = feat[:, 0:1]
    live = feat[:, LANE - 1:LANE]
    kern = f * (jnp.exp(-t * delta_ref[...]) + HY_SHIFT) * live
    kern_ref[...] = kern

    @pl.when((pl.program_id(0) == 0) & (pl.program_id(1) == 0))
    def _():
        den_ref[...] = jnp.zeros_like(den_ref)

    den_ref[...] += jnp.sum(jnp.abs(kern), axis=0, keepdims=True)


def hy_filters(l, w1, b1, w2, b2, w3, *, tr):
    t = jnp.linspace(0.0, 1.0, l, dtype=F32)[:, None]
    w = (2.0 * math.pi / l) * jnp.arange(l, dtype=F32)[:, None]
    bands = jnp.linspace(1e-4, HY_BANDS - 1.0, HY_BANDS, dtype=F32)[None, :]
    feats = jnp.concatenate([t, jnp.cos(bands * w), -jnp.sin(bands * w)], axis=-1)
    feats = jnp.pad(feats, ((0, 0), (0, LANE - 1 - HY_EMB)))
    feats = jnp.concatenate([feats, jnp.ones((l, 1), F32)], axis=-1)
    table = jnp.concatenate([feats, jnp.zeros((1, LANE), F32), feats[:0:-1]], axis=0)
    w1p = jnp.pad(w1.astype(F32), ((0, LANE - HY_EMB), (0, 0)))
    w3s = w3.astype(F32).reshape(HY_FFN, HY_ORDER, 2, HY_CH).transpose(0, 2, 1, 3).reshape(HY_FFN, -1)
    deltas = jnp.abs(jnp.linspace(HY_MIN_DECAY, HY_MAX_DECAY, HY_CH, dtype=F32))
    deltas = jnp.tile(deltas, HY_ORDER).reshape(1, -1)
    oc = HY_ORDER * HY_CH
    nb = l // tr
    const = lambda s, i: (0, 0)
    return pl.pallas_call(
        _hy_filter_kernel,
        grid=(2, nb),
        in_specs=[
            pl.BlockSpec((tr, LANE), lambda s, i: (s * nb + i, 0)),
            pl.BlockSpec((LANE, HY_FFN), const), pl.BlockSpec((1, HY_FFN), const),
            pl.BlockSpec((HY_FFN, HY_FFN), const), pl.BlockSpec((1, HY_FFN), const),
            pl.BlockSpec((HY_FFN, oc), lambda s, i: (0, s)),
            pl.BlockSpec((1, oc), const),
        ],
        out_specs=[pl.BlockSpec((tr, oc), lambda s, i: (s * nb + i, 0)), pl.BlockSpec((1, oc), const)],
        out_shape=[jax.ShapeDtypeStruct((2 * l, oc), F32), jax.ShapeDtypeStruct((1, oc), F32)],
        compiler_params=_cparams(("arbitrary", "arbitrary")),
        name="hy_filters",
    )(table, w1p, b1.reshape(1, -1).astype(F32), w2.astype(F32), b2.reshape(1, -1).astype(F32), w3s, deltas)


def _cis(num, den):
    ang = (-2.0 * math.pi / den) * (num % den).astype(F32)
    return jnp.cos(ang), jnp.sin(ang)


def _block_complex(re, im):
    return jnp.concatenate([jnp.concatenate([re, -im], axis=-1), jnp.concatenate([im, re], axis=-1)], axis=-2)


def _fft_tables(p):
    q = FFT_Q
    n = p * q
    i_p = jnp.arange(p, dtype=jnp.int32)
    i_q = jnp.arange(q, dtype=jnp.int32)
    fr, fi = _cis(i_p[:, None] * i_p[None, :], p)
    f_filt = jnp.concatenate([fr, fi], axis=0)
    f_in = _block_complex(fr[:, :p // 2], fi[:, :p // 2])
    f_out = _block_complex(fr[:p // 2, :], -fi[:p // 2, :]) * (1.0 / n)
    num = i_q[None, None, :] * i_p[:, None, None] + p * i_q[None, :, None] * i_q[None, None, :]
    gr, gi = _cis(num, n)
    g = _block_complex(gr, gi)
    h = _block_complex(jnp.swapaxes(gr, 1, 2), -jnp.swapaxes(gi, 1, 2))
    return tuple(a.astype(BF16) for a in (f_filt, f_in, g, h, f_out))


FFT_TC = 512


def _fft_in_kernel(f_ref, x_ref, o_ref, xs_ref):
    for s in range(x_ref.shape[1]):
        xs_ref[...] = x_ref[:, s, :]
        o_ref[:, s, :] = jnp.dot(f_ref[...], xs_ref[...].astype(BF16), preferred_element_type=F32)


def fft_in(f, x):
    k, q, c = x.shape
    tc = FFT_TC
    return pl.pallas_call(
        _fft_in_kernel,
        grid=(q // SUBLANE, c // tc),
        in_specs=[pl.BlockSpec(f.shape, lambda j, ci: (0, 0)),
                  pl.BlockSpec((k, SUBLANE, tc), lambda j, ci: (0, j, ci))],
        out_specs=pl.BlockSpec((f.shape[0], SUBLANE, tc), lambda j, ci: (0, j, ci)),
        out_shape=jax.ShapeDtypeStruct((f.shape[0], q, c), F32),
        scratch_shapes=[pltpu.VMEM((k, tc), F32)],
        compiler_params=_cparams(("parallel", "parallel")),
        name="fft_in",
    )(f, x)


def _cmul_rows(s, k):
    half = s.shape[0] // 2
    sr, si, kr, ki = s[:half], s[half:], k[:half], k[half:]
    return jnp.concatenate([sr * kr - si * ki, sr * ki + si * kr], axis=0)


def _fft_mid_kernel(g_ref, h_ref, a_ref, af_ref, inv_ref, o_ref):
    q2 = g_ref.shape[1]
    c = a_ref.shape[-1]
    a = a_ref[...].reshape(q2, c).astype(BF16)
    af = af_ref[...].reshape(q2, c).astype(BF16)
    y = jnp.dot(g_ref[0], a, preferred_element_type=F32)
    kf = jnp.dot(g_ref[0], af, preferred_element_type=F32) * inv_ref[...]
    z = _cmul_rows(y, kf).astype(BF16)
    o_ref[...] = jnp.dot(h_ref[0], z, preferred_element_type=F32).reshape(o_ref.shape)


def fft_mid(g, h, a, af, den, order):
    p = g.shape[0]
    c = HY_CH
    mat = pl.BlockSpec((1, 2 * FFT_Q, 2 * FFT_Q), lambda k: (k, 0, 0))
    sig = pl.BlockSpec((2, 1, FFT_Q, c), lambda k: (0, k, 0, 0))
    return pl.pallas_call(
        _fft_mid_kernel,
        grid=(p,),
        in_specs=[mat, mat, sig,
                  pl.BlockSpec((2, 1, FFT_Q, c), lambda k: (0, k, 0, order)),
                  pl.BlockSpec((1, c), lambda k: (0, order))],
        out_specs=sig,
        out_shape=jax.ShapeDtypeStruct((2, p, FFT_Q, c), F32),
        compiler_params=_cparams(("parallel",)),
        name="fft_mid",
    )(g, h, a.reshape(2, p, FFT_Q, c), af.reshape(2, p, FFT_Q, af.shape[-1]), 1.0 / den
      ).reshape(2 * p, FFT_Q, c)


def _fft_out_kernel(f_ref, c_ref, z_ref, gate_ref, bias_ref, o_ref, cs_ref, zs_ref, gs_ref):
    for s in range(c_ref.shape[1]):
        cs_ref[...] = c_ref[:, s, :]
        zs_ref[...] = z_ref[:, s, :]
        gs_ref[...] = gate_ref[:, s, :]
        y = jnp.dot(f_ref[...], cs_ref[...].astype(BF16), preferred_element_type=F32)
        o_ref[:, s, :] = gs_ref[...] * (y + zs_ref[...] * bias_ref[...])


def fft_out(f, cm, z, gate, bias):
    p2, q, c = cm.shape
    tc = FFT_TC
    rows = f.shape[0]
    half = pl.BlockSpec((rows, SUBLANE, tc), lambda j, ci: (0, j, ci))
    return pl.pallas_call(
        _fft_out_kernel,
        grid=(q // SUBLANE, c // tc),
        in_specs=[pl.BlockSpec(f.shape, lambda j, ci: (0, 0)),
                  pl.BlockSpec((p2, SUBLANE, tc), lambda j, ci: (0, j, ci)),
                  half, half,
                  pl.BlockSpec((1, tc), lambda j, ci: (0, ci))],
        out_specs=half,
        out_shape=jax.ShapeDtypeStruct(z.shape, F32),
        scratch_shapes=[pltpu.VMEM((p2, tc), F32), pltpu.VMEM((rows, tc), F32), pltpu.VMEM((rows, tc), F32)],
        compiler_params=_cparams(("parallel", "parallel")),
        name="fft_out",
    )(f, cm, z, gate, bias)


def _hy_ctx_kernel(ff_ref, fin_ref, fout_ref, kern_ref, inv_ref, z_ref, g1_ref, g2_ref, bias_ref, o_ref):
    b, l, c = z_ref.shape
    z = z_ref[...].reshape(b * l, c)
    gates = (g1_ref, g2_ref)
    for n in range(HY_ORDER):
        cols = slice(n * c, (n + 1) * c)
        kf = (jnp.dot(ff_ref[...], kern_ref[:, cols].astype(BF16), preferred_element_type=F32)
              * inv_ref[:, cols])
        s = jnp.dot(fin_ref[...], z.astype(BF16), preferred_element_type=F32)
        y = jnp.dot(fout_ref[...], _cmul_rows(s, kf).astype(BF16), preferred_element_type=F32)
        z = gates[n][...].reshape(b * l, c) * (y + z * bias_ref[n:n + 1, :])
    o_ref[...] = z.reshape(b, l, c).astype(o_ref.dtype)


def hy_ctx_conv(kern, den, z0, g1, g2, bias):
    b, l, c = z0.shape
    assert b == 2
    n = 2 * l
    i_n = jnp.arange(n, dtype=jnp.int32)
    fr, fi = _cis(i_n[:, None] * i_n[None, :], n)
    ff = jnp.concatenate([fr, fi], axis=0).astype(BF16)
    fin = _block_complex(fr[:, :l], fi[:, :l]).astype(BF16)
    fout = (_block_complex(fr[:l, :], -fi[:l, :]) * (1.0 / n)).astype(BF16)
    return pl.pallas_call(
        _hy_ctx_kernel,
        out_shape=jax.ShapeDtypeStruct((b, l, c), BF16),
        compiler_params=pltpu.CompilerParams(vmem_limit_bytes=VMEM_LIMIT),
        name="hy_ctx_conv",
    )(ff, fin, fout, kern, 1.0 / den, z0, g1, g2, bias)


def hyena_latent(z, conv_w, conv_b, w1, b1, w2, b2, w3, bias):
    b, l, _ = z.shape
    assert b == 2 and (2 * l) % (2 * FFT_Q) == 0
    p = 2 * l // FFT_Q
    z0, *gates = hy_short_conv(z, conv_w, conv_b, tm=256)
    kern, den = hy_filters(l, w1, b1, w2, b2, w3, tr=512)
    f_filt, f_in, g, h, f_out = _fft_tables(p)
    af = fft_in(f_filt, kern.reshape(p, FFT_Q, HY_ORDER * HY_CH))
    split = lambda t: t.reshape(p, FFT_Q, HY_CH)
    zz = split(z0)
    for n in range(HY_ORDER):
        cm = fft_mid(g, h, fft_in(f_in, zz), af, den, n)
        zz = fft_out(f_out, cm, zz, split(gates[n]), bias[n].astype(F32).reshape(1, HY_CH))
    return zz.reshape(b, l, HY_CH)


def hyena_ctx(z, conv_w, conv_b, w1, b1, w2, b2, w3, bias):
    b, l, _ = z.shape
    z0, g1, g2 = hy_short_conv(z, conv_w, conv_b, tm=l)
    kern, den = hy_filters(l, w1, b1, w2, b2, w3, tr=l)
    return hy_ctx_conv(kern, den, z0, g1, g2, bias.astype(F32))


def even_mixer_core(zc, zl, gla_w_gate, gla_b_gate, gla_norm, hy_conv_w, hy_conv_b,
                    hy_w1, hy_b1, hy_w2, hy_b2, hy_w3, hy_bias):
    gc, gl = gla_bidir(zc, zl, gla_w_gate, gla_b_gate, gla_norm)
    hy = (hy_conv_w, hy_conv_b, hy_w1, hy_b1, hy_w2, hy_b2, hy_w3, hy_bias)
    return (gc, hyena_ctx(zc, *hy)), (gl, hyena_latent(zl, *hy))


def odd_mixer_core(zc, zl, ret_norm, att_q_norm, att_k_norm, att_sink):
    _, rl = retention_bidir(zc, zl, ret_norm)
    qkv_l = att_prep(zl, att_q_norm, att_k_norm, _axial_tables(zl.shape[1]), tm=512)
    qkv_c = att_prep(zc, att_q_norm, att_k_norm, None, tm=zc.shape[1])
    return rl, window_attention(qkv_l, qkv_c, att_sink)


def kernel(x, c, ctx, c_ctx, ada_w, ada_b, norm_mix, norm_ffn, ev_w_in, ev_w_out, gla_w_gate, gla_b_gate, gla_norm, hy_conv_w, hy_conv_b, hy_w1, hy_b1, hy_w2, hy_b2, hy_w3, hy_bias, od_w_in, od_w_out, ret_norm, att_q_norm, att_k_norm, att_sink, router_w, router_b, moe_w1, moe_w3, moe_w2):
    assert DEPTH == 2
    bsz, seq, d = x.shape
    n_ctx = ctx.shape[1]
    c_pad = jnp.zeros((8, d), F32).at[:bsz].set(c).at[bsz].set(c_ctx)
    mods = ada_modulation(c_pad, ada_w, ada_b)
    slot = np.array([SUBLANE * (e % EXPERTS_PER_GROUP) + e // EXPERTS_PER_GROUP for e in range(N_EXPERTS)])
    rw_pad = jnp.zeros((d, LOGIT_PAD), F32).at[:, slot].set(router_w)
    rw_hi = rw_pad.astype(BF16)
    rw_pad = jnp.concatenate([rw_hi, (rw_pad - rw_hi.astype(F32)).astype(BF16)], axis=1)
    rb_pad = jnp.zeros((EXPERTS_PER_GROUP * SUBLANE, LANE), F32).at[slot].set(
        jnp.broadcast_to(router_b.astype(F32)[:, None], (N_EXPERTS, LANE)))

    xl, xc = x, ctx
    for layer in range(DEPTH):
        last = layer == DEPTH - 1
        ml = mods[layer, :bsz].reshape(bsz, 1, 6, d)
        mc = jnp.broadcast_to(mods[layer, bsz].reshape(1, 1, 6, d), (bsz, 1, 6, d))
        sh1, sc1, g1, sh2, sc2, g2 = [ml[:, :, i] for i in range(6)]
        csh1, csc1, cg1, csh2, csc2, cg2 = [mc[:, :, i] for i in range(6)]
        i = layer // 2
        if layer % 2 == 0:
            w_in, w_out = ev_w_in[i], ev_w_out[i]
            lr0 = sum(EVEN_SPLITS[:4])
            lr1 = lr0 + EVEN_SPLITS[4]
            w_in = jnp.concatenate([w_in[:, :lr0], w_in[:, lr1:], w_in[:, lr0:lr1],
                                    jnp.zeros((d, EVEN_COLS_PAD - w_in.shape[1]), F32)], axis=1)
            tn = EVEN_COLS_PAD // 7
        else:
            w_in, w_out = od_w_in[i], od_w_out[i]
            tn = w_in.shape[1] // 6
        w_in = cast_bf16(w_in, tr=256)
        zl = norm_mod_matmul(xl, norm_mix[layer], sc1, sh1, w_in, tm=1024, tn=tn, out_dtype=BF16)
        zc = norm_mod_matmul(xc, norm_mix[layer], csc1, csh1, w_in, tm=n_ctx, tn=tn, out_dtype=BF16)
        if layer % 2 == 0:
            mixc, mixl = even_mixer_core(zc, zl, gla_w_gate[i], gla_b_gate[i], gla_norm[i], hy_conv_w[i],
                                         hy_conv_b[i], hy_w1[i], hy_b1[i], hy_w2[i], hy_b2[i], hy_w3[i],
                                         hy_bias[i])
        else:
            mixl = odd_mixer_core(zc, zl, ret_norm[i], att_q_norm[i], att_k_norm[i], att_sink[i])
        lat0 = 0 if last else bsz * n_ctx
        flat = lambda rt: rt.transpose(1, 0, 2).reshape(ROUTE_ROWS, -1)
        hs, route = [], []
        if not last:
            xc, hc, rtc = outproj_residual_norm(*mixc, w_out, xc, cg1, norm_ffn[layer], csc2, csh2, rw_pad,
                                                rb_pad, tm=n_ctx)
            hs.append(hc)
            route.append(flat(rtc))
        xl, hl, rtl = outproj_residual_norm(*mixl, w_out, xl, g1, norm_ffn[layer], sc2, sh2, rw_pad, rb_pad,
                                            tm=512)
        hs.append(hl)
        route.append(flat(rtl))
        y_sorted, pos, w_tok = moe_experts(hs, jnp.concatenate(route, axis=1), moe_w1, moe_w3, moe_w2, layer)
        if not last:
            xc = moe_gather_combine(xc, cg2, y_sorted, pos, w_tok, 0, tm=n_ctx)
        xl = moe_gather_combine(xl, g2, y_sorted, pos, w_tok, lat0, tm=512)
    return xl
```
